```python
import math
import jax, jax.numpy as jnp
from jax import lax
import numpy as np

D_MODEL = 1024
BATCH = 4
SEQ = 4096
DEPTH = 1

NSA_HEADS = 8
NSA_GROUPS = 2
HEADS_PER_GROUP = NSA_HEADS // NSA_GROUPS
HEAD_DIM = 64
NSA_WIDTH = NSA_HEADS * HEAD_DIM
KV_WIDTH = NSA_GROUPS * HEAD_DIM
CMP_BLOCK = 32
CMP_STRIDE = 16
CMP_HIDDEN = 256
SEL_BLOCK = 64
SEL_TOPK = 16
WINDOW = 512
Q_BLOCK = 64
ROPE_THETA = 10000.0
FORCED_SCORE = 1.0e4
NEG = -1.0e30
S5_WIDTH = D_MODEL // 2
S5_GROUP = 16
S5_GROUPS = S5_WIDTH // S5_GROUP
S5_STATE = 64
DT_MIN = 1.0e-3
DT_MAX = 1.0e-1
N_BRANCHES = 2
RMS_EPS = 1.0e-6
D_IN = NSA_WIDTH + 6 * KV_WIDTH + 3 * NSA_HEADS + NSA_WIDTH + 2 * S5_WIDTH + N_BRANCHES * D_MODEL

kernel_name = "hybrid_nsa_s5_gated_block"


def _split_points():
    sizes = [NSA_WIDTH] + [KV_WIDTH] * 6 + [3 * NSA_HEADS, NSA_WIDTH, S5_WIDTH, S5_WIDTH]
    pts, acc = [], 0
    for s in sizes:
        acc += s
        pts.append(acc)
    return pts


def rmsnorm(x, g):
    xf = x.astype(jnp.float32)
    y = xf * lax.rsqrt(jnp.mean(xf * xf, axis=-1, keepdims=True) + RMS_EPS)
    return (y * g.astype(jnp.float32)).astype(x.dtype)


def rope(t, pos):
    half = t.shape[-1] // 2
    inv_freq = ROPE_THETA ** (-jnp.arange(half, dtype=jnp.float32) / half)
    ang = pos.astype(jnp.float32)[:, None] * inv_freq[None, :]
    shp = (1, t.shape[1]) + (1,) * (t.ndim - 3) + (half,)
    cos = jnp.cos(ang).reshape(shp).astype(t.dtype)
    sin = jnp.sin(ang).reshape(shp).astype(t.dtype)
    t1, t2 = t[..., :half], t[..., half:]
    return jnp.concatenate([t1 * cos - t2 * sin, t2 * cos + t1 * sin], axis=-1)


def masked_softmax(s, mask):
    s = jnp.where(mask, s.astype(jnp.float32), NEG)
    m = jnp.max(s, axis=-1, keepdims=True)
    e = jnp.exp(s - m) * mask
    return e / jnp.maximum(jnp.sum(e, axis=-1, keepdims=True), 1.0e-30)


def compress(kv, pos_emb, w1, w2):
    B, T, G, dh = kv.shape
    chunks = kv.reshape(B, T // CMP_STRIDE, CMP_STRIDE, G, dh)
    blocks = jnp.concatenate([chunks[:, :-1], chunks[:, 1:]], axis=2)
    blocks = blocks + pos_emb[None, None, :, None, :]
    flat = jnp.moveaxis(blocks, 3, 2).reshape(B, blocks.shape[1], G, CMP_BLOCK * dh)
    return jax.nn.gelu(flat @ w1) @ w2


def s5_scan(u, lam_re, lam_im, log_dt, b_re, b_im, c_re, c_im, d):
    B, T, _ = u.shape
    f32 = jnp.float32
    lam_re, lam_im = lam_re.astype(f32), lam_im.astype(f32)
    dt = jnp.exp(log_dt.astype(f32))[:, None]
    mag = jnp.exp(lam_re * dt)
    a_re, a_im = mag * jnp.cos(lam_im * dt), mag * jnp.sin(lam_im * dt)
    den = lam_re * lam_re + lam_im * lam_im
    z_re = ((a_re - 1.0) * lam_re + a_im * lam_im) / den
    z_im = (a_im * lam_re - (a_re - 1.0) * lam_im) / den
    b_re, b_im = b_re.astype(f32), b_im.astype(f32)
    bb_re = z_re[..., None] * b_re - z_im[..., None] * b_im
    bb_im = z_re[..., None] * b_im + z_im[..., None] * b_re
    ug = u.astype(f32).reshape(B, T, S5_GROUPS, S5_GROUP)
    bu_re = jnp.einsum('btgc,gpc->btgp', ug, bb_re)
    bu_im = jnp.einsum('btgc,gpc->btgp', ug, bb_im)
    ar = jnp.broadcast_to(a_re, bu_re.shape)
    ai = jnp.broadcast_to(a_im, bu_im.shape)

    def combine(e1, e2):
        a1r, a1i, b1r, b1i = e1
        a2r, a2i, b2r, b2i = e2
        return (a2r * a1r - a2i * a1i,
                a2r * a1i + a2i * a1r,
                a2r * b1r - a2i * b1i + b2r,
                a2r * b1i + a2i * b1r + b2i)

    _, _, xr, xi = lax.associative_scan(combine, (ar, ai, bu_re, bu_im), axis=1)
    y = jnp.einsum('btgp,gcp->btgc', xr, c_re.astype(f32)) - jnp.einsum('btgp,gcp->btgc', xi, c_im.astype(f32))
    y = y.reshape(B, T, S5_WIDTH) + d.astype(f32) * u.astype(f32)
    return y.astype(u.dtype)


def setup_inputs(seed: int = 0) -> dict:
    key = jax.random.key(seed)
    ks = jax.random.split(key, 24)
    nrm = jax.random.normal
    L = DEPTH
    n = jnp.arange(S5_STATE, dtype=jnp.float32)
    log_lo, log_hi = math.log(DT_MIN), math.log(DT_MAX)
    return {
        "x": nrm(ks[0], (BATCH, SEQ, D_MODEL), jnp.float32),
        "norm_g": 1.0 + 0.02 * nrm(ks[1], (L, D_MODEL), jnp.float32),
        "w_in": nrm(ks[2], (L, D_MODEL, D_IN), jnp.float32) * D_MODEL ** -0.5,
        "cmp_pos_k": 0.1 * nrm(ks[3], (L, CMP_BLOCK, HEAD_DIM), jnp.float32),
        "cmp_pos_v": 0.1 * nrm(ks[4], (L, CMP_BLOCK, HEAD_DIM), jnp.float32),
        "cmp_w1_k": nrm(ks[5], (L, CMP_BLOCK * HEAD_DIM, CMP_HIDDEN), jnp.float32) * (CMP_BLOCK * HEAD_DIM) ** -0.5,
        "cmp_w2_k": nrm(ks[6], (L, CMP_HIDDEN, HEAD_DIM), jnp.float32) * CMP_HIDDEN ** -0.5,
        "cmp_w1_v": nrm(ks[7], (L, CMP_BLOCK * HEAD_DIM, CMP_HIDDEN), jnp.float32) * (CMP_BLOCK * HEAD_DIM) ** -0.5,
        "cmp_w2_v": nrm(ks[8], (L, CMP_HIDDEN, HEAD_DIM), jnp.float32) * CMP_HIDDEN ** -0.5,
        "s5_lam_re": -0.5 + 0.01 * nrm(ks[9], (L, S5_GROUPS, S5_STATE), jnp.float32),
        "s5_lam_im": math.pi * n + 0.01 * nrm(ks[10], (L, S5_GROUPS, S5_STATE), jnp.float32),
        "s5_log_dt": jax.random.uniform(ks[11], (L, S5_GROUPS), jnp.float32, log_lo, log_hi),
        "s5_b_re": nrm(ks[12], (L, S5_GROUPS, S5_STATE, S5_GROUP), jnp.float32) * (2 * S5_GROUP) ** -0.5,
        "s5_b_im": nrm(ks[13], (L, S5_GROUPS, S5_STATE, S5_GROUP), jnp.float32) * (2 * S5_GROUP) ** -0.5,
        "s5_c_re": nrm(ks[14], (L, S5_GROUPS, S5_GROUP, S5_STATE), jnp.float32) * S5_STATE ** -0.5,
        "s5_c_im": nrm(ks[15], (L, S5_GROUPS, S5_GROUP, S5_STATE), jnp.float32) * S5_STATE ** -0.5,
        "s5_d": nrm(ks[16], (L, S5_WIDTH), jnp.float32),
        "w_glu": nrm(ks[17], (L, S5_WIDTH, S5_WIDTH), jnp.float32) * S5_WIDTH ** -0.5,
        "b_glu": 0.01 * nrm(ks[18], (L, S5_WIDTH), jnp.float32),
        "w_proj_nsa": nrm(ks[19], (L, NSA_WIDTH, D_MODEL), jnp.float32) * NSA_WIDTH ** -0.5,
        "w_proj_s5": nrm(ks[20], (L, S5_WIDTH, D_MODEL), jnp.float32) * S5_WIDTH ** -0.5,
        "w_out": nrm(ks[21], (L, D_MODEL, D_MODEL), jnp.float32) * D_MODEL ** -0.5,
        "final_g": 1.0 + 0.02 * nrm(ks[22], (D_MODEL,), jnp.float32),
    }


def reference(x, norm_g, w_in, cmp_pos_k, cmp_pos_v, cmp_w1_k, cmp_w2_k, cmp_w1_v, cmp_w2_v,
              s5_lam_re, s5_lam_im, s5_log_dt, s5_b_re, s5_b_im, s5_c_re, s5_c_im, s5_d,
              w_glu, b_glu, w_proj_nsa, w_proj_s5, w_out, final_g):
    B, T, _ = x.shape
    G, R, dh = NSA_GROUPS, HEADS_PER_GROUP, HEAD_DIM
    NC = T // CMP_STRIDE - 1
    NS = T // SEL_BLOCK
    NQB = T // Q_BLOCK
    n_top = min(SEL_TOPK, NS)
    scale = HEAD_DIM ** -0.5
    pos = jnp.arange(T)

    c_start = jnp.arange(NC) * CMP_STRIDE
    s_start = jnp.arange(NS) * SEL_BLOCK
    overlap = ((c_start[:, None] < s_start[None, :] + SEL_BLOCK) &
               (c_start[:, None] + CMP_BLOCK > s_start[None, :])).astype(jnp.float32)
    cmp_end = c_start + CMP_BLOCK - 1
    blk = jnp.arange(NS)
    b_ix = jnp.arange(B)[:, None, None, None]
    g_ix = jnp.arange(G)[None, :, None, None]

    for l in range(DEPTH):
        h = rmsnorm(x, norm_g[l])
        proj = h @ w_in[l]
        (q, kc, vc, ksl, vsl, kwi, vwi, gl, gate_nsa, u, gate_s5, mg) = jnp.split(proj, _split_points(), axis=-1)

        q = q.reshape(B, T, G, R, dh)
        qr = rope(q, pos)
        kc, vc = kc.reshape(B, T, G, dh), vc.reshape(B, T, G, dh)
        k_cmp = compress(kc, cmp_pos_k[l], cmp_w1_k[l], cmp_w2_k[l])
        v_cmp = compress(vc, cmp_pos_v[l], cmp_w1_v[l], cmp_w2_v[l])
        ksl = rope(ksl.reshape(B, T, G, dh), pos)
        k_sel = ksl.reshape(B, NS, SEL_BLOCK, G, dh).transpose(0, 3, 1, 2, 4)
        v_sel = vsl.reshape(B, NS, SEL_BLOCK, G, dh).transpose(0, 3, 1, 2, 4)
        kwi = rope(kwi.reshape(B, T, G, dh), pos)
        k_win = jnp.pad(kwi, ((0, 0), (WINDOW, 0), (0, 0), (0, 0)))
        v_win = jnp.pad(vwi.reshape(B, T, G, dh), ((0, 0), (WINDOW, 0), (0, 0), (0, 0)))

        def nsa_block(args):
            qb, q_blk, qr_blk = args
            t = qb * Q_BLOCK + jnp.arange(Q_BLOCK)
            s_c = jnp.einsum('bqgrd,bcgd->bgrqc', q_blk, k_cmp) * scale
            p_cmp = masked_softmax(s_c, cmp_end[None, :] <= t[:, None])
            o_cmp = jnp.einsum('bgrqc,bcgd->bqgrd', p_cmp.astype(v_cmp.dtype), v_cmp)
            imp = jnp.einsum('bgrqc,cn->bgqn', p_cmp, overlap)
            cur = t // SEL_BLOCK
            forced = (blk[None] == 0) | (blk[None] == cur[:, None]) | (blk[None] == cur[:, None] - 1)
            causal_blk = blk[None] * SEL_BLOCK <= t[:, None]
            imp = jnp.where(forced, FORCED_SCORE, jnp.where(causal_blk, imp, -1.0))
            _, idx = lax.top_k(imp, n_top)
            kg = k_sel[b_ix, g_ix, idx]
            vg = v_sel[b_ix, g_ix, idx]
            s_s = jnp.einsum('bqgrd,bgqnkd->bgrqnk', qr_blk, kg) * scale
            kpos = idx[..., None] * SEL_BLOCK + jnp.arange(SEL_BLOCK)
            m_s = (kpos <= t[:, None, None]).reshape(B, G, 1, Q_BLOCK, n_top * SEL_BLOCK)
            p_s = masked_softmax(s_s.reshape(B, G, R, Q_BLOCK, n_top * SEL_BLOCK), m_s)
            o_sel = jnp.einsum('bgrqm,bgqmd->bqgrd', p_s.astype(vg.dtype),
                               vg.reshape(B, G, Q_BLOCK, n_top * SEL_BLOCK, dh))
            start = qb * Q_BLOCK
            kw = lax.dynamic_slice_in_dim(k_win, start, Q_BLOCK + WINDOW, axis=1)
            vw = lax.dynamic_slice_in_dim(v_win, start, Q_BLOCK + WINDOW, axis=1)
            wpos = start - WINDOW + jnp.arange(Q_BLOCK + WINDOW)
            m_w = (wpos[None] <= t[:, None]) & (wpos[None] > t[:, None] - WINDOW) & (wpos[None] >= 0)
            s_w = jnp.einsum('bqgrd,bkgd->bgrqk', qr_blk, kw) * scale
            p_w = masked_softmax(s_w, m_w)
            o_win = jnp.einsum('bgrqk,bkgd->bqgrd', p_w.astype(vw.dtype), vw)
            return (o_cmp, o_sel, o_win)

        q_blocks = q.reshape(B, NQB, Q_BLOCK, G, R, dh).swapaxes(0, 1)
        qr_blocks = qr.reshape(B, NQB, Q_BLOCK, G, R, dh).swapaxes(0, 1)
        o_c, o_s, o_w = lax.map(nsa_block, (jnp.arange(NQB), q_blocks, qr_blocks))
        unblock = lambda o: o.swapaxes(0, 1).reshape(B, T, G, R, dh)
        gates = jax.nn.sigmoid(gl.reshape(B, T, G, R, 3))
        o_nsa = (gates[..., 0:1] * unblock(o_c) + gates[..., 1:2] * unblock(o_s)
                 + gates[..., 2:3] * unblock(o_w)).reshape(B, T, NSA_WIDTH)
        branch_a = (o_nsa * jax.nn.silu(gate_nsa)) @ w_proj_nsa[l]

        y = s5_scan(u, s5_lam_re[l], s5_lam_im[l], s5_log_dt[l], s5_b_re[l], s5_b_im[l],
                    s5_c_re[l], s5_c_im[l], s5_d[l])
        z = jax.nn.gelu(y)
        o_s5 = z * jax.nn.sigmoid(z @ w_glu[l] + b_glu[l])
        branch_b = (o_s5 * jax.nn.silu(gate_s5)) @ w_proj_s5[l]

        mg_a, mg_b = jnp.split(mg, N_BRANCHES, axis=-1)
        merged = jax.nn.sigmoid(mg_a) * branch_a + jax.nn.sigmoid(mg_b) * branch_b
        x = x + merged @ w_out[l]

    return rmsnorm(x, final_g)
```

```python
import functools
import math

import jax
import jax.numpy as jnp
from jax import lax
from jax.experimental import pallas as pl
from jax.experimental.pallas import tpu as pltpu

F32 = jnp.float32
BF16 = jnp.bfloat16

D_MODEL = 1024
NSA_HEADS = 8
NSA_GROUPS = 2
HEADS_PER_GROUP = 4
HEAD_DIM = 64
NSA_WIDTH = 512
KV_WIDTH = 128
CMP_BLOCK = 32
CMP_STRIDE = 16
CMP_HIDDEN = 256
SEL_BLOCK = 64
SEL_TOPK = 16
WINDOW = 512
ROPE_THETA = 10000.0
FORCED_SCORE = 1.0e4
NEG = -1.0e30
S5_WIDTH = 512
S5_GROUP = 16
S5_GROUPS = 32
S5_STATE = 64
RMS_EPS = 1.0e-6

LANES = 128
VMEM_LIMIT = 56 * 1024 * 1024

_OFF_Q = 0
_OFF_KV = 512
_OFF_GL = 1280
_OFF_GN = 1304
_OFF_U = 1816
_OFF_GS = 2328
_OFF_MG = 2840
_D_IN = 4888

TM_PROJ = 512
TQ = 128
TK = 512
TT = 128
S5_CW = 512


def _gelu_tanh(x):
    c = math.sqrt(2.0 / math.pi)
    return 0.5 * x * (1.0 + jnp.tanh(c * (x + 0.044715 * (x * x * x))))


def _sigmoid(x):
    return 1.0 / (1.0 + jnp.exp(-x))


def _rms_scale(xv):
    ms = jnp.mean(xv * xv, axis=-1, keepdims=True)
    return xv * lax.rsqrt(ms + RMS_EPS)


def _inproj_kernel(x_ref, g_ref, w_ref, cos_ref, sin_ref,
                   qq_ref, kc_ref, vc_ref, ks_ref, vs_ref, kw_ref, vw_ref, gl_ref, u_ref):
    h = (_rms_scale(x_ref[0]) * g_ref[...]).astype(BF16)
    cos2 = cos_ref[...]
    sin2 = sin_ref[...]
    lane = lax.broadcasted_iota(jnp.int32, cos2.shape, 1)
    first_half = (lane & (HEAD_DIM - 1)) < (HEAD_DIM // 2)
    low = lane < HEAD_DIM

    def proj(a, b):
        return jnp.dot(h, w_ref[:, a:b], preferred_element_type=F32)

    def rope(xs):
        partner = jnp.where(first_half, pltpu.roll(xs, 96, 1), pltpu.roll(xs, 32, 1))
        return xs * cos2 + partner * sin2

    scale = HEAD_DIM ** -0.5
    for i in range(NSA_HEADS // 2):
        xs = proj(_OFF_Q + LANES * i, _OFF_Q + LANES * (i + 1)) * scale
        xr = rope(xs)
        qq_ref[0, 2 * i] = jnp.where(low, xr, pltpu.roll(xs, 64, 1)).astype(BF16)
        qq_ref[0, 2 * i + 1] = jnp.where(low, pltpu.roll(xr, 64, 1), xs).astype(BF16)

    kc_ref[0] = proj(512, 640)
    vc_ref[0] = proj(640, 768)
    for (off, k_out, v_out) in ((768, ks_ref, vs_ref), (1024, kw_ref, vw_ref)):
        kr = rope(proj(off, off + LANES))
        k_out[0, 0] = jnp.where(low, kr, 0.0).astype(BF16)
        k_out[0, 1] = jnp.where(low, pltpu.roll(kr, 64, 1), 0.0).astype(BF16)
        vv = proj(off + LANES, off + 2 * LANES)
        v_out[0, 0] = jnp.where(low, vv, 0.0).astype(BF16)
        v_out[0, 1] = jnp.where(low, pltpu.roll(vv, 64, 1), 0.0).astype(BF16)
    gl_ref[0] = _sigmoid(proj(1280, 1408))
    u_ref[0] = proj(1408, 1920)


def _inproj(x, norm_g, w_a, cos2, sin2):
    B, T, D = x.shape
    tm = TM_PROJ
    grid = (B, T // tm)
    row_blk = lambda w: pl.BlockSpec((1, tm, w), lambda b, i: (b, i, 0))
    kv_blk = pl.BlockSpec((1, NSA_GROUPS, tm, LANES), lambda b, i: (b, 0, i, 0))
    kv_shape = jax.ShapeDtypeStruct((B, NSA_GROUPS, T, LANES), BF16)
    return pl.pallas_call(
        _inproj_kernel,
        grid=grid,
        in_specs=[
            row_blk(D),
            pl.BlockSpec((1, D), lambda b, i: (0, 0)),
            pl.BlockSpec(w_a.shape, lambda b, i: (0, 0)),
            pl.BlockSpec((tm, LANES), lambda b, i: (i, 0)),
            pl.BlockSpec((tm, LANES), lambda b, i: (i, 0)),
        ],
        out_specs=[
            pl.BlockSpec((1, NSA_HEADS, tm, LANES), lambda b, i: (b, 0, i, 0)),
            row_blk(LANES), row_blk(LANES),
            kv_blk, kv_blk, kv_blk, kv_blk,
            row_blk(LANES), row_blk(S5_WIDTH),
        ],
        out_shape=[
            jax.ShapeDtypeStruct((B, NSA_HEADS, T, LANES), BF16),
            jax.ShapeDtypeStruct((B, T, LANES), F32), jax.ShapeDtypeStruct((B, T, LANES), F32),
            kv_shape, kv_shape, kv_shape, kv_shape,
            jax.ShapeDtypeStruct((B, T, LANES), F32), jax.ShapeDtypeStruct((B, T, S5_WIDTH), F32),
        ],
        compiler_params=pltpu.CompilerParams(
            dimension_semantics=("parallel", "arbitrary"), vmem_limit_bytes=VMEM_LIMIT),
        name="inproj",
    )(x, norm_g, w_a, cos2, sin2)


def _compress_kernel(ck_ref, cv_ref, w1k_ref, w1v_ref, w2k_ref, w2v_ref, pk_ref, pv_ref, ko_ref, vo_ref):
    def one(c_ref, w1_ref, w2_ref, p_ref):
        a = jnp.dot(c_ref[0, 0].astype(BF16), w1_ref[...], preferred_element_type=F32)
        pb = jnp.dot(p_ref[...].astype(BF16), w1_ref[...], preferred_element_type=F32)
        pos_bias = pb[0:1, 0:CMP_HIDDEN] + pb[1:2, CMP_HIDDEN:]
        hid = a[:, 0:CMP_HIDDEN] + pltpu.roll(a[:, CMP_HIDDEN:], a.shape[0] - 1, 0) + pos_bias
        return jnp.dot(_gelu_tanh(hid).astype(BF16), w2_ref[...], preferred_element_type=F32)

    ko_ref[0, 0] = one(ck_ref, w1k_ref, w2k_ref, pk_ref).astype(BF16)
    vo_ref[0, 0] = one(cv_ref, w1v_ref, w2v_ref, pv_ref).astype(BF16)


def _compress(ck, cv, w1k, w1v, w2k, w2v, pk, pv):
    B, G, NCH, KW = ck.shape
    c_blk = pl.BlockSpec((1, 1, NCH, KW), lambda b, g: (b, g, 0, 0))
    full = lambda a: pl.BlockSpec(a.shape, lambda b, g: (0,) * a.ndim)
    o_blk = pl.BlockSpec((1, 1, NCH, LANES), lambda b, g: (b, g, 0, 0))
    o_shape = jax.ShapeDtypeStruct((B, G, NCH, LANES), BF16)
    return pl.pallas_call(
        _compress_kernel,
        grid=(B, G),
        in_specs=[c_blk, c_blk, full(w1k), full(w1v), full(w2k), full(w2v), full(pk), full(pv)],
        out_specs=[o_blk, o_blk],
        out_shape=[o_shape, o_shape],
        compiler_params=pltpu.CompilerParams(
            dimension_semantics=("parallel", "arbitrary"), vmem_limit_bytes=VMEM_LIMIT),
        name="compress",
    )(ck, cv, w1k, w1v, w2k, w2v, pk, pv)


def _nt_dot(a, b):
    return lax.dot_general(a, b, (((1,), (1,)), ((), ())), preferred_element_type=F32)


def _nsa_kernel(qq_ref, kc_ref, vc_ref, ks_ref, vs_ref, kw_ref, vw_ref, gl_ref, ovt_ref, e_ref, o_ref):
    t0 = pl.program_id(1) * TQ
    R = HEADS_PER_GROUP
    rows = R * TQ
    NCP = kc_ref.shape[2]
    NS = ovt_ref.shape[0]
    WK = WINDOW + TQ
    t_row = t0 + (lax.broadcasted_iota(jnp.int32, (rows, 1), 0) & (TQ - 1))
    gates = gl_ref[0]
    outs = []
    for g in range(NSA_GROUPS):
        q = qq_ref[0, R * g:R * (g + 1)].reshape(rows, LANES)

        s = _nt_dot(q, kc_ref[0, g])
        c_end = lax.broadcasted_iota(jnp.int32, (rows, NCP), 1) * CMP_STRIDE + (CMP_BLOCK - 1)
        valid = c_end <= t_row
        s = jnp.where(valid, s, NEG)
        m = jnp.max(s, axis=-1, keepdims=True)
        e = jnp.where(valid, jnp.exp(s - m), 0.0)
        p = e / jnp.maximum(jnp.sum(e, axis=-1, keepdims=True), 1.0e-30)
        o_cmp = jnp.dot(p.astype(BF16), vc_ref[0, g], preferred_element_type=F32)

        psum = p[0:TQ] + p[TQ:2 * TQ] + p[2 * TQ:3 * TQ] + p[3 * TQ:4 * TQ]
        p_hi = psum.astype(BF16)
        p_lo = (psum - p_hi.astype(F32)).astype(BF16)
        imp = _nt_dot(ovt_ref[...], p_hi) + _nt_dot(ovt_ref[...], p_lo)
        blk = lax.broadcasted_iota(jnp.int32, (NS, TQ), 0)
        t_l = t0 + lax.broadcasted_iota(jnp.int32, (NS, TQ), 1)
        cur = t_l >> 6
        imp = jnp.where(blk * SEL_BLOCK <= t_l, imp, -1.0)
        imp = jnp.where(blk == 0, FORCED_SCORE, imp)
        imp = jnp.where(blk == cur, FORCED_SCORE, imp)
        imp = jnp.where(blk == cur - 1, FORCED_SCORE, imp)
        rank = jnp.zeros((NS, TQ), F32)
        for mm in range(NS):
            row = imp[mm:mm + 1, :]
            tie = jnp.where(blk > mm, 1.0, 0.0)
            rank = rank + jnp.where(row > imp, 1.0, 0.0) + jnp.where(row == imp, tie, 0.0)
        pen = jnp.where(rank < float(SEL_TOPK), 0.0, NEG)
        pen = jnp.transpose(pen).astype(BF16)

        def sel_body(kt, carry):
            m_i, l_i, acc = carry
            k0 = pl.multiple_of(kt * TK, TK)
            sc = _nt_dot(q, ks_ref[0, g, pl.ds(k0, TK), :])
            pm = jnp.dot(pen, e_ref[:, pl.ds(k0, TK)], preferred_element_type=F32)
            sc = sc + jnp.concatenate([pm] * R, axis=0)
            kpos = k0 + lax.broadcasted_iota(jnp.int32, (rows, TK), 1)
            sc = jnp.where(kpos <= t_row, sc, NEG)
            m_new = jnp.maximum(m_i, jnp.max(sc, axis=-1, keepdims=True))
            alpha = jnp.exp(m_i - m_new)
            pp = jnp.exp(sc - m_new)
            l_new = alpha * l_i + jnp.sum(pp, axis=-1, keepdims=True)
            acc = acc * alpha + jnp.dot(pp.astype(BF16), vs_ref[0, g, pl.ds(k0, TK), :],
                                        preferred_element_type=F32)
            return m_new, l_new, acc

        n_kt = (t0 + TQ + TK - 1) // TK
        init = (jnp.full((rows, 1), NEG, F32), jnp.zeros((rows, 1), F32), jnp.zeros((rows, LANES), F32))
        _, l_s, acc_s = lax.fori_loop(0, n_kt, sel_body, init)
        o_sel = acc_s / l_s

        ks0 = pl.multiple_of(jnp.maximum(t0 - WINDOW, 0), LANES)
        sw = _nt_dot(q, kw_ref[0, g, pl.ds(ks0, WK), :])
        wpos = ks0 + lax.broadcasted_iota(jnp.int32, (rows, WK), 1)
        sw = jnp.where(wpos <= t_row, jnp.where(wpos > t_row - WINDOW, sw, NEG), NEG)
        mw = jnp.max(sw, axis=-1, keepdims=True)
        ew = jnp.exp(sw - mw)
        o_win = jnp.dot(ew.astype(BF16), vw_ref[0, g, pl.ds(ks0, WK), :], preferred_element_type=F32)
        o_win = o_win / jnp.sum(ew, axis=-1, keepdims=True)

        for r in range(R):
            hh = R * g + r
            sl = slice(r * TQ, (r + 1) * TQ)
            outs.append(gates[:, 3 * hh:3 * hh + 1] * o_cmp[sl]
                        + gates[:, 3 * hh + 1:3 * hh + 2] * o_sel[sl]
                        + gates[:, 3 * hh + 2:3 * hh + 3] * o_win[sl])
    for i in range(NSA_HEADS // 2):
        o_ref[0, :, LANES * i:LANES * (i + 1)] = outs[2 * i] + pltpu.roll(outs[2 * i + 1], 64, 1)


def _nsa(qq, kcmp, vcmp, ks, vs, kw, vw, gl, ovt, emat):
    B, H, T, _ = qq.shape
    G = NSA_GROUPS
    NCP = kcmp.shape[2]
    grid = (B, T // TQ)
    per_b = lambda n: pl.BlockSpec((1, G, n, LANES), lambda b, i: (b, 0, 0, 0))
    full = lambda a: pl.BlockSpec(a.shape, lambda b, i: (0,) * a.ndim)
    return pl.pallas_call(
        _nsa_kernel,
        grid=grid,
        in_specs=[
            pl.BlockSpec((1, H, TQ, LANES), lambda b, i: (b, 0, i, 0)),
            per_b(NCP), per_b(NCP), per_b(T), per_b(T), per_b(T), per_b(T),
            pl.BlockSpec((1, TQ, LANES), lambda b, i: (b, i, 0)),
            full(ovt), full(emat),
        ],
        out_specs=pl.BlockSpec((1, TQ, NSA_WIDTH), lambda b, i: (b, i, 0)),
        out_shape=jax.ShapeDtypeStruct((B, T, NSA_WIDTH), F32),
        compiler_params=pltpu.CompilerParams(
            dimension_semantics=("parallel", "arbitrary"), vmem_limit_bytes=VMEM_LIMIT),
        name="nsa",
    )(qq, kcmp, vcmp, ks, vs, kw, vw, gl, ovt, emat)


def _s5_prep_kernel(lre_ref, lim_ref, ldt_ref, bre_ref, bim_ref, are_ref, aim_ref, bbre_ref, bbim_ref):
    lre, lim = lre_ref[...], lim_ref[...]
    dt = jnp.exp(ldt_ref[...])
    mag = jnp.exp(lre * dt)
    a_re = mag * jnp.cos(lim * dt)
    a_im = mag * jnp.sin(lim * dt)
    den = lre * lre + lim * lim
    z_re = ((a_re - 1.0) * lre + a_im * lim) / den
    z_im = (a_im * lre - (a_re - 1.0) * lim) / den
    are_ref[...] = a_re
    aim_ref[...] = a_im
    bbre_ref[...] = z_re * bre_ref[...] - z_im * bim_ref[...]
    bbim_ref[...] = z_re * bim_ref[...] + z_im * bre_ref[...]


def _s5_prep(lre, lim, ldt, bre, bim):
    shp = jax.ShapeDtypeStruct(lre.shape, F32)
    return pl.pallas_call(_s5_prep_kernel, out_shape=[shp, shp, shp, shp], name="s5_prep")(lre, lim, ldt, bre, bim)


def _s5_kernel(u_ref, are_ref, aim_ref, bb_ref, cre_ref, cim_ref, d_ref, wg_ref, bg_ref, o_ref, x_scr, st_scr):
    nb = st_scr.shape[0]
    rows = u_ref.shape[0]
    HW = S5_WIDTH // 2
    HS = S5_GROUPS // 2 * S5_STATE

    @pl.when(pl.program_id(0) == 0)
    def _():
        st_scr[...] = jnp.zeros_like(st_scr)

    uv = u_ref[...]
    ub = uv.astype(BF16)
    for h in range(2):
        x_scr[:, 2 * HS * h:2 * HS * (h + 1)] = jnp.dot(ub[:, HW * h:HW * (h + 1)], bb_ref[h],
                                                          preferred_element_type=F32)

    for h in range(2):
        for j in range(HS // S5_CW):
            c_re = 2 * HS * h + S5_CW * j
            c_im = c_re + HS
            c_a = HS * h + S5_CW * j
            ar = are_ref[:, c_a:c_a + S5_CW]
            ai = aim_ref[:, c_a:c_a + S5_CW]

            def body(k, carry):
                sr, si = carry
                r0 = pl.multiple_of(k * 2 * nb, 2 * nb)
                br = x_scr[pl.ds(r0, 2 * nb), c_re:c_re + S5_CW]
                bi = x_scr[pl.ds(r0, 2 * nb), c_im:c_im + S5_CW]
                xr0 = ar * sr - ai * si + br[0:nb]
                xi0 = ar * si + ai * sr + bi[0:nb]
                xr1 = ar * xr0 - ai * xi0 + br[nb:]
                xi1 = ar * xi0 + ai * xr0 + bi[nb:]
                x_scr[pl.ds(r0, 2 * nb), c_re:c_re + S5_CW] = jnp.concatenate([xr0, xr1], axis=0)
                x_scr[pl.ds(r0, 2 * nb), c_im:c_im + S5_CW] = jnp.concatenate([xi0, xi1], axis=0)
                return xr1, xi1

            sr, si = lax.fori_loop(0, rows // (2 * nb), body,
                                   (st_scr[:, c_re:c_re + S5_CW], st_scr[:, c_im:c_im + S5_CW]))
            st_scr[:, c_re:c_re + S5_CW] = sr
            st_scr[:, c_im:c_im + S5_CW] = si

    ys = []
    for h in range(2):
        xr = x_scr[:, 2 * HS * h:2 * HS * h + HS].astype(BF16)
        xi = x_scr[:, 2 * HS * h + HS:2 * HS * (h + 1)].astype(BF16)
        ys.append(jnp.dot(xr, cre_ref[h], preferred_element_type=F32)
                  - jnp.dot(xi, cim_ref[h], preferred_element_type=F32))
    y = jnp.concatenate(ys, axis=1) + d_ref[...] * uv
    z = _gelu_tanh(y)
    gate = jnp.dot(z.astype(BF16), wg_ref[...], preferred_element_type=F32) + bg_ref[...]
    o_ref[...] = z * _sigmoid(gate)


def _s5(u_tb, a_re, a_im, bb, c_re, c_im, d, w_glu, b_glu, nb):
    rows_total = u_tb.shape[0]
    rows = TT * nb
    n_state_cols = bb.shape[2] * 2
    full = lambda a: pl.BlockSpec(a.shape, lambda i: (0,) * a.ndim)
    return pl.pallas_call(
        _s5_kernel,
        grid=(rows_total // rows,),
        in_specs=[pl.BlockSpec((rows, S5_WIDTH), lambda i: (i, 0)),
                  full(a_re), full(a_im), full(bb), full(c_re), full(c_im), full(d), full(w_glu), full(b_glu)],
        out_specs=pl.BlockSpec((rows, S5_WIDTH), lambda i: (i, 0)),
        out_shape=jax.ShapeDtypeStruct((rows_total, S5_WIDTH), F32),
        scratch_shapes=[pltpu.VMEM((rows, n_state_cols), F32), pltpu.VMEM((nb, n_state_cols), F32)],
        compiler_params=pltpu.CompilerParams(dimension_semantics=("arbitrary",), vmem_limit_bytes=VMEM_LIMIT),
        name="s5",
    )(u_tb, a_re, a_im, bb, c_re, c_im, d, w_glu, b_glu)


def _final_kernel(x_ref, g_ref, wb_ref, on_ref, os_ref, wpn_ref, wps_ref, wo_ref, fg_ref, o_ref):
    xv = x_ref[0]
    h = (_rms_scale(xv) * g_ref[...]).astype(BF16)

    def proj(a, b):
        return jnp.dot(h, wb_ref[:, a:b], preferred_element_type=F32)

    def silu(v):
        return v * _sigmoid(v)

    a_in = (on_ref[0] * silu(proj(0, NSA_WIDTH))).astype(BF16)
    b_in = (os_ref[0] * silu(proj(NSA_WIDTH, NSA_WIDTH + S5_WIDTH))).astype(BF16)
    branch_a = jnp.dot(a_in, wpn_ref[...], preferred_element_type=F32)
    branch_b = jnp.dot(b_in, wps_ref[...], preferred_element_type=F32)
    o1 = NSA_WIDTH + S5_WIDTH
    merged = (_sigmoid(proj(o1, o1 + D_MODEL)) * branch_a
              + _sigmoid(proj(o1 + D_MODEL, o1 + 2 * D_MODEL)) * branch_b)
    y = xv + jnp.dot(merged.astype(BF16), wo_ref[...], preferred_element_type=F32)
    o_ref[0] = _rms_scale(y) * fg_ref[...]


def _final(x, norm_g, w_b, o_nsa, o_s5, wpn, wps, wo, final_g):
    B, T, D = x.shape
    tm = TM_PROJ
    row_blk = lambda w: pl.BlockSpec((1, tm, w), lambda b, i: (b, i, 0))
    full = lambda a: pl.BlockSpec(a.shape, lambda b, i: (0,) * a.ndim)
    return pl.pallas_call(
        _final_kernel,
        grid=(B, T // tm),
        in_specs=[row_blk(D), full(norm_g), full(w_b), row_blk(NSA_WIDTH), row_blk(S5_WIDTH),
                  full(wpn), full(wps), full(wo), full(final_g)],
        out_specs=row_blk(D),
        out_shape=jax.ShapeDtypeStruct((B, T, D), F32),
        compiler_params=pltpu.CompilerParams(
            dimension_semantics=("parallel", "arbitrary"), vmem_limit_bytes=VMEM_LIMIT),
        name="final",
    )(x, norm_g, w_b, o_nsa, o_s5, wpn, wps, wo, final_g)


def _rope_tables(T):
    half = HEAD_DIM // 2
    inv_freq = ROPE_THETA ** (-jnp.arange(half, dtype=F32) / half)
    ang = jnp.arange(T).astype(F32)[:, None] * inv_freq[None, :]
    cos, sin = jnp.cos(ang), jnp.sin(ang)
    cos2 = jnp.concatenate([cos, cos, cos, cos], axis=1)
    sin2 = jnp.concatenate([-sin, sin, -sin, sin], axis=1)
    return cos2, sin2


def _block_diag_halves(m):
    g, a, b = m.shape
    gh = g // 2
    eye = jnp.eye(gh, dtype=m.dtype)
    out = m.reshape(2, gh, a, 1, b) * eye[None, :, None, :, None]
    return out.reshape(2, gh * a, gh * b)


def kernel(x, norm_g, w_in, cmp_pos_k, cmp_pos_v, cmp_w1_k, cmp_w2_k, cmp_w1_v, cmp_w2_v, s5_lam_re, s5_lam_im, s5_log_dt, s5_b_re, s5_b_im, s5_c_re, s5_c_im, s5_d, w_glu, b_glu, w_proj_nsa, w_proj_s5, w_out, final_g):
    B, T, D = x.shape
    assert w_in.shape[0] == 1, "single-layer block"
    G, dh = NSA_GROUPS, HEAD_DIM
    NCH = T // CMP_STRIDE
    NS = T // SEL_BLOCK

    w = w_in[0]
    w_a = jnp.concatenate([w[:, :_OFF_GL], jnp.pad(w[:, _OFF_GL:_OFF_GN], ((0, 0), (0, LANES - 24))),
                           w[:, _OFF_U:_OFF_GS]], axis=1).astype(BF16)
    w_b = jnp.concatenate([w[:, _OFF_GN:_OFF_U], w[:, _OFF_GS:]], axis=1).astype(BF16)
    g2 = norm_g[0][None, :]
    cos2, sin2 = _rope_tables(T)

    qq, kc, vc, ks, vs, kw, vw, gl, u = _inproj(x, g2, w_a, cos2, sin2)

    def chunks(a):
        return a.reshape(B, NCH, CMP_STRIDE, G, dh).transpose(0, 3, 1, 2, 4).reshape(B, G, NCH, CMP_STRIDE * dh)

    half_rows = CMP_STRIDE * dh
    w1cat = lambda w1: jnp.concatenate([w1[:half_rows], w1[half_rows:]], axis=1).astype(BF16)
    pos2 = lambda p: jnp.pad(p.reshape(2, half_rows), ((0, 6), (0, 0)))
    w2k = jnp.concatenate([jnp.zeros_like(cmp_w2_k[0]), cmp_w2_k[0]], axis=1).astype(BF16)
    w2v = jnp.concatenate([cmp_w2_v[0], jnp.zeros_like(cmp_w2_v[0])], axis=1).astype(BF16)
    kcmp, vcmp = _compress(chunks(kc), chunks(vc), w1cat(cmp_w1_k[0]), w1cat(cmp_w1_v[0]), w2k, w2v,
                           pos2(cmp_pos_k[0]), pos2(cmp_pos_v[0]))

    c_start = jnp.arange(NCH) * CMP_STRIDE
    s_start = jnp.arange(NS) * SEL_BLOCK
    ovt = ((c_start[None, :] < s_start[:, None] + SEL_BLOCK) & (c_start[None, :] + CMP_BLOCK > s_start[:, None])
           & (jnp.arange(NCH)[None, :] < NCH - 1)).astype(BF16)
    emat = (jnp.arange(T)[None, :] // SEL_BLOCK == jnp.arange(NS)[:, None]).astype(BF16)
    o_nsa = _nsa(qq, kcmp, vcmp, ks, vs, kw, vw, gl, ovt, emat)

    rep = lambda a: jnp.repeat(a, S5_GROUP, axis=0)
    tr = lambda b: b.transpose(0, 2, 1).reshape(S5_GROUPS * S5_GROUP, S5_STATE)
    a_re, a_im, bb_re, bb_im = _s5_prep(
        rep(s5_lam_re[0]), rep(s5_lam_im[0]),
        rep(jnp.broadcast_to(s5_log_dt[0][:, None], (S5_GROUPS, S5_STATE))),
        tr(s5_b_re[0]), tr(s5_b_im[0]))
    halves = lambda a: a[::S5_GROUP].reshape(1, 2, S5_GROUPS // 2 * S5_STATE)
    a_re_c = halves(a_re).reshape(1, -1)
    a_im_c = halves(a_im).reshape(1, -1)
    grp = lambda a: a.reshape(S5_GROUPS, S5_GROUP, S5_STATE)
    bb = jnp.concatenate([_block_diag_halves(grp(bb_re)), _block_diag_halves(grp(bb_im))], axis=2).astype(BF16)
    c_re = _block_diag_halves(s5_c_re[0].transpose(0, 2, 1)).astype(BF16)
    c_im = _block_diag_halves(s5_c_im[0].transpose(0, 2, 1)).astype(BF16)
    u_tb = u.transpose(1, 0, 2).reshape(T * B, S5_WIDTH)
    o_s5 = _s5(u_tb, a_re_c, a_im_c, bb, c_re, c_im, s5_d[0][None, :], w_glu[0].astype(BF16), b_glu[0][None, :], B)
    o_s5 = o_s5.reshape(T, B, S5_WIDTH).transpose(1, 0, 2)

    return _final(x, g2, w_b, o_nsa, o_s5, w_proj_nsa[0].astype(BF16), w_proj_s5[0].astype(BF16),
                  w_out[0].astype(BF16), final_g[None, :])
```

```python
import math

import jax
import jax.numpy as jnp
from jax import lax
from jax.experimental import pallas as pl
from jax.experimental.pallas import tpu as pltpu

F32 = jnp.float32
BF16 = jnp.bfloat16

D_MODEL = 1024
NSA_HEADS = 8
NSA_GROUPS = 2
HEADS_PER_GROUP = 4
HEAD_DIM = 64
NSA_WIDTH = 512
CMP_BLOCK = 32
CMP_STRIDE = 16
CMP_HIDDEN = 256
SEL_BLOCK = 64
SEL_TOPK = 16
WINDOW = 512
ROPE_THETA = 10000.0
FORCED_SCORE = 1.0e4
NEG = -1.0e30
S5_WIDTH = 512
S5_GROUP = 16
S5_GROUPS = 32
S5_STATE = 64
RMS_EPS = 1.0e-6

LANES = 128
SUBLANES = 8
VMEM_LIMIT = 56 * 1024 * 1024

_OFF_GL = 1280
_OFF_GN = 1304
_OFF_U = 1816
_OFF_GS = 2328

TM_PROJ = 512
TQ = 128
TK = 512
V_ROWS = 80
GATE_ROWS = 32
TT = 128
S5_CW = 512


def _gelu_tanh(x):
    c = math.sqrt(2.0 / math.pi)
    return 0.5 * x * (1.0 + jnp.tanh(c * (x + 0.044715 * (x * x * x))))


def _sigmoid(x):
    return 1.0 / (1.0 + jnp.exp(-x))


def _rms_scale(xv):
    ms = jnp.mean(xv * xv, axis=-1, keepdims=True)
    return xv * lax.rsqrt(ms + RMS_EPS)


def _nt_dot(a, b):
    return lax.dot_general(a, b, (((1,), (1,)), ((), ())), preferred_element_type=F32)


def _inproj_kernel(x_ref, g_ref, w_ref, cos_ref, sin_ref,
                   qq_ref, kc_ref, vc_ref, ks_ref, vs_ref, kw_ref, vw_ref, gl_ref, u_ref):
    h = (_rms_scale(x_ref[0]) * g_ref[...]).astype(BF16)
    cos2 = cos_ref[...]
    sin2 = sin_ref[...]
    lane = lax.broadcasted_iota(jnp.int32, cos2.shape, 1)
    first_half = (lane & (HEAD_DIM - 1)) < (HEAD_DIM // 2)
    low = lane < HEAD_DIM

    def proj(a, b):
        return jnp.dot(h, w_ref[:, a:b], preferred_element_type=F32)

    def rope(xs):
        partner = jnp.where(first_half, pltpu.roll(xs, 96, 1), pltpu.roll(xs, 32, 1))
        return xs * cos2 + partner * sin2

    scale = HEAD_DIM ** -0.5
    for i in range(NSA_HEADS // 2):
        xs = proj(LANES * i, LANES * (i + 1)) * scale
        xr = rope(xs)
        qq_ref[0, 2 * i] = jnp.where(low, xr, pltpu.roll(xs, 64, 1)).astype(BF16)
        qq_ref[0, 2 * i + 1] = jnp.where(low, pltpu.roll(xr, 64, 1), xs).astype(BF16)

    kc_ref[0] = proj(512, 640)
    vc_ref[0] = proj(640, 768)
    tm = cos2.shape[0]
    t_row = pl.program_id(1) * tm + lax.broadcasted_iota(jnp.int32, cos2.shape, 0)
    blk_onehot = jnp.where(lane - HEAD_DIM == t_row // SEL_BLOCK, 1.0, 0.0)
    ones_rows = jnp.where(lax.broadcasted_iota(jnp.int32, (V_ROWS - HEAD_DIM, tm), 0) == 0, 1.0, 0.0)
    for (off, k_out, v_out, k_pad) in ((768, ks_ref, vs_ref, blk_onehot), (1024, kw_ref, vw_ref, 0.0)):
        kr = rope(proj(off, off + LANES))
        k_out[0, 0] = jnp.where(low, kr, k_pad).astype(BF16)
        k_out[0, 1] = jnp.where(low, pltpu.roll(kr, 64, 1), k_pad).astype(BF16)
        vt = proj(off + LANES, off + 2 * LANES).T
        for g in range(NSA_GROUPS):
            v_out[0, g] = jnp.concatenate([vt[HEAD_DIM * g:HEAD_DIM * (g + 1)], ones_rows], axis=0).astype(BF16)
    gl_ref[0] = _sigmoid(proj(1280, 1408)).T[0:GATE_ROWS]
    u_ref[0] = proj(1408, 1920)


def _inproj(x, norm_g, w_a, cos2, sin2):
    B, T, D = x.shape
    tm = TM_PROJ
    grid = (B, T // tm)
    row_blk = lambda w: pl.BlockSpec((1, tm, w), lambda b, i: (b, i, 0))
    kv_blk = pl.BlockSpec((1, NSA_GROUPS, tm, LANES), lambda b, i: (b, 0, i, 0))
    kv_shape = jax.ShapeDtypeStruct((B, NSA_GROUPS, T, LANES), BF16)
    vt_blk = pl.BlockSpec((1, NSA_GROUPS, V_ROWS, tm), lambda b, i: (b, 0, 0, i))
    vt_shape = jax.ShapeDtypeStruct((B, NSA_GROUPS, V_ROWS, T), BF16)
    return pl.pallas_call(
        _inproj_kernel,
        grid=grid,
        in_specs=[
            row_blk(D),
            pl.BlockSpec((1, D), lambda b, i: (0, 0)),
            pl.BlockSpec(w_a.shape, lambda b, i: (0, 0)),
            pl.BlockSpec((tm, LANES), lambda b, i: (i, 0)),
            pl.BlockSpec((tm, LANES), lambda b, i: (i, 0)),
        ],
        out_specs=[
            pl.BlockSpec((1, NSA_HEADS, tm, LANES), lambda b, i: (b, 0, i, 0)),
            row_blk(LANES), row_blk(LANES),
            kv_blk, vt_blk, kv_blk, vt_blk,
            pl.BlockSpec((1, GATE_ROWS, tm), lambda b, i: (b, 0, i)), row_blk(S5_WIDTH),
        ],
        out_shape=[
            jax.ShapeDtypeStruct((B, NSA_HEADS, T, LANES), BF16),
            jax.ShapeDtypeStruct((B, T, LANES), F32), jax.ShapeDtypeStruct((B, T, LANES), F32),
            kv_shape, vt_shape, kv_shape, vt_shape,
            jax.ShapeDtypeStruct((B, GATE_ROWS, T), F32), jax.ShapeDtypeStruct((B, T, S5_WIDTH), F32),
        ],
        compiler_params=pltpu.CompilerParams(
            dimension_semantics=("parallel", "arbitrary"), vmem_limit_bytes=VMEM_LIMIT),
        name="inproj",
    )(x, norm_g, w_a, cos2, sin2)


def _compress_kernel(kc_ref, vc_ref, w1k_ref, w1v_ref, w2k_ref, w2vt_ref, pbk_ref, pbv_ref, ko_ref, vo_ref):
    nch = ko_ref.shape[2]
    H = CMP_HIDDEN

    def hidden(c_ref, w1_ref, pb_ref):
        a = jnp.zeros((nch, 4 * H), F32)
        for j in range(CMP_STRIDE):
            rows = c_ref[0, pl.ds(j, nch, stride=CMP_STRIDE), :].astype(BF16)
            a = a + jnp.dot(rows, w1_ref[j], preferred_element_type=F32)
        out = []
        for g in range(NSA_GROUPS):
            lo = a[:, 2 * H * g:2 * H * g + H]
            hi = a[:, 2 * H * g + H:2 * H * (g + 1)]
            out.append(_gelu_tanh(lo + pltpu.roll(hi, nch - 1, 0) + pb_ref[...]).astype(BF16))
        return out

    hk = hidden(kc_ref, w1k_ref, pbk_ref)
    hv = hidden(vc_ref, w1v_ref, pbv_ref)
    for g in range(NSA_GROUPS):
        ko_ref[0, g] = jnp.dot(hk[g], w2k_ref[...], preferred_element_type=F32).astype(BF16)
        vo_ref[0, g] = _nt_dot(w2vt_ref[...], hv[g]).astype(BF16)


def _pos_bias_kernel(p_ref, w1_ref, o_ref):
    o_ref[...] = jnp.dot(p_ref[...].astype(BF16), w1_ref[...].astype(BF16), preferred_element_type=F32)


def _pos_bias(pos, w1):
    p8 = jnp.broadcast_to(pos.reshape(1, -1), (SUBLANES, pos.size))
    return pl.pallas_call(_pos_bias_kernel, out_shape=jax.ShapeDtypeStruct((SUBLANES, CMP_HIDDEN), F32),
                          name="pos_bias")(p8, w1)[0:1]


def _compress(kc, vc, w1k, w1v, w2k, w2vt, pbk, pbv):
    B, T, _ = kc.shape
    G = NSA_GROUPS
    nch = T // CMP_STRIDE
    c_blk = pl.BlockSpec((1, T, LANES), lambda b: (b, 0, 0))
    full = lambda a: pl.BlockSpec(a.shape, lambda b: (0,) * a.ndim)
    return pl.pallas_call(
        _compress_kernel,
        grid=(B,),
        in_specs=[c_blk, c_blk, full(w1k), full(w1v), full(w2k), full(w2vt), full(pbk), full(pbv)],
        out_specs=[pl.BlockSpec((1, G, nch, LANES), lambda b: (b, 0, 0, 0)),
                   pl.BlockSpec((1, G, HEAD_DIM, nch), lambda b: (b, 0, 0, 0))],
        out_shape=[jax.ShapeDtypeStruct((B, G, nch, LANES), BF16),
                   jax.ShapeDtypeStruct((B, G, HEAD_DIM, nch), BF16)],
        compiler_params=pltpu.CompilerParams(dimension_semantics=("parallel",), vmem_limit_bytes=VMEM_LIMIT),
        name="compress",
    )(kc, vc, w1k, w1v, w2k, w2vt, pbk, pbv)


def _nsa_kernel(qq_ref, kc_ref, vct_ref, ksa_ref, vst_ref, kw_ref, vwt_ref, glt_ref, ovt_ref, o_ref,
                qsel_scr, acc_scr, m_scr):
    t0 = pl.program_id(1) * TQ
    R = HEADS_PER_GROUP
    cols = R * TQ
    NCP = kc_ref.shape[2]
    NS = ovt_ref.shape[0]
    WK = WINDOW + TQ
    t_lane = t0 + (lax.broadcasted_iota(jnp.int32, (1, cols), 1) & (TQ - 1))
    low = lax.broadcasted_iota(jnp.int32, (cols, LANES), 1) < HEAD_DIM
    sub8 = lax.broadcasted_iota(jnp.int32, (SUBLANES, TQ), 0)
    o_cmp, o_win = [], []

    for g in range(NSA_GROUPS):
        q = qq_ref[0, R * g:R * (g + 1)].reshape(cols, LANES)

        s = _nt_dot(kc_ref[0, g], q)
        c_end = lax.broadcasted_iota(jnp.int32, (NCP, cols), 0) * CMP_STRIDE + (CMP_BLOCK - 1)
        valid = c_end <= t_lane
        s = jnp.where(valid, s, NEG)
        m = jnp.max(s, axis=0, keepdims=True)
        e = jnp.where(valid, jnp.exp(s - m), 0.0)
        p = e * (1.0 / jnp.maximum(jnp.sum(e, axis=0, keepdims=True), 1.0e-30))
        o_cmp.append(jnp.dot(vct_ref[0, g], p.astype(BF16), preferred_element_type=F32))

        psum = p[:, 0:TQ] + p[:, TQ:2 * TQ] + p[:, 2 * TQ:3 * TQ] + p[:, 3 * TQ:4 * TQ]
        p_hi = psum.astype(BF16)
        p_lo = (psum - p_hi.astype(F32)).astype(BF16)
        imp = (jnp.dot(ovt_ref[...], p_hi, preferred_element_type=F32)
               + jnp.dot(ovt_ref[...], p_lo, preferred_element_type=F32))
        blk = lax.broadcasted_iota(jnp.int32, (NS, TQ), 0)
        t_l = t0 + lax.broadcasted_iota(jnp.int32, (NS, TQ), 1)
        cur = t_l // SEL_BLOCK
        imp = jnp.where(blk * SEL_BLOCK <= t_l, imp, -1.0)
        imp = jnp.where(blk == 0, FORCED_SCORE, imp)
        imp = jnp.where(blk == cur, FORCED_SCORE, imp)
        imp = jnp.where(blk == cur - 1, FORCED_SCORE, imp)
        nv = NS // SUBLANES
        imp8 = [imp[SUBLANES * j:SUBLANES * (j + 1)] for j in range(nv)]
        rank8 = [jnp.zeros((SUBLANES, TQ), F32) for _ in range(nv)]
        for mm in range(NS):
            row = imp[mm:mm + 1, :]
            jm = mm // SUBLANES
            for j in range(nv):
                if j < jm:
                    ahead = jnp.where(row > imp8[j], 1.0, 0.0)
                elif j > jm:
                    ahead = jnp.where(row >= imp8[j], 1.0, 0.0)
                else:
                    tie = jnp.where(sub8 > (mm % SUBLANES), 1.0, 0.0)
                    ahead = jnp.where(row > imp8[j], 1.0, 0.0) + jnp.where(row == imp8[j], tie, 0.0)
                rank8[j] = rank8[j] + ahead
        rank = jnp.concatenate(rank8, axis=0)
        pen = jnp.where(rank < float(SEL_TOPK), 0.0, NEG)
        pen_t = jnp.concatenate([jnp.zeros((LANES - NS, TQ), F32), pen], axis=0).T
        qsel_scr[g] = jnp.where(low, q, jnp.concatenate([pen_t.astype(BF16)] * R, axis=0))

        ks0 = pl.multiple_of(jnp.maximum(t0 - WINDOW, 0), LANES)
        sw = _nt_dot(kw_ref[0, g, pl.ds(ks0, WK), :], q)
        wpos = ks0 + lax.broadcasted_iota(jnp.int32, (WK, cols), 0)
        sw = jnp.where(wpos <= t_lane, jnp.where(wpos > t_lane - WINDOW, sw, NEG), NEG)
        ew = jnp.exp(sw - jnp.max(sw, axis=0, keepdims=True)).astype(BF16)
        ow = jnp.dot(vwt_ref[0, g, :, pl.ds(ks0, WK)], ew, preferred_element_type=F32)
        o_win.append(ow[0:HEAD_DIM] * (1.0 / ow[HEAD_DIM:HEAD_DIM + 1]))

    acc_scr[...] = jnp.zeros_like(acc_scr)
    m_scr[...] = jnp.full(m_scr.shape, NEG, F32)

    def sel_tile(kt, masked):
        k0 = pl.multiple_of(kt * TK, TK)
        for g in range(NSA_GROUPS):
            sc = _nt_dot(ksa_ref[0, g, pl.ds(k0, TK), :], qsel_scr[g])
            if masked:
                kpos = k0 + lax.broadcasted_iota(jnp.int32, (TK, cols), 0)
                sc = jnp.where(kpos <= t_lane, sc, NEG)
            m_old = m_scr[g]
            m_new = jnp.maximum(m_old, jnp.max(sc, axis=0, keepdims=True))
            pp = jnp.exp(sc - m_new).astype(BF16)
            acc_scr[g] = acc_scr[g] * jnp.exp(m_old - m_new) + jnp.dot(
                vst_ref[0, g, :, pl.ds(k0, TK)], pp, preferred_element_type=F32)
            m_scr[g] = m_new

    n_full = t0 // TK

    def full_tile(kt, carry):
        sel_tile(kt, False)
        return carry

    lax.fori_loop(0, n_full, full_tile, 0)
    sel_tile(n_full, True)

    glt = glt_ref[0]
    heads = []
    for g in range(NSA_GROUPS):
        acc = acc_scr[g]
        o_sel = acc[0:HEAD_DIM] * (1.0 / acc[HEAD_DIM:HEAD_DIM + 1])
        for r in range(R):
            hh = R * g + r
            sl = slice(r * TQ, (r + 1) * TQ)
            heads.append(glt[3 * hh:3 * hh + 1] * o_cmp[g][:, sl]
                         + glt[3 * hh + 1:3 * hh + 2] * o_sel[:, sl]
                         + glt[3 * hh + 2:3 * hh + 3] * o_win[g][:, sl])
    o_ref[0] = jnp.concatenate(heads, axis=0).T


def _nsa(qq, kcmp, vcmpt, ksa, vst, kw, vwt, glt, ovt):
    B, H, T, _ = qq.shape
    G = NSA_GROUPS
    NCP = kcmp.shape[2]
    grid = (B, T // TQ)
    k_blk = lambda n: pl.BlockSpec((1, G, n, LANES), lambda b, i: (b, 0, 0, 0))
    vt_blk = lambda r, n: pl.BlockSpec((1, G, r, n), lambda b, i: (b, 0, 0, 0))
    return pl.pallas_call(
        _nsa_kernel,
        grid=grid,
        in_specs=[
            pl.BlockSpec((1, H, TQ, LANES), lambda b, i: (b, 0, i, 0)),
            k_blk(NCP), vt_blk(HEAD_DIM, NCP), k_blk(T), vt_blk(V_ROWS, T), k_blk(T), vt_blk(V_ROWS, T),
            pl.BlockSpec((1, GATE_ROWS, TQ), lambda b, i: (b, 0, i)),
            pl.BlockSpec(ovt.shape, lambda b, i: (0, 0)),
        ],
        out_specs=pl.BlockSpec((1, TQ, NSA_WIDTH), lambda b, i: (b, i, 0)),
        out_shape=jax.ShapeDtypeStruct((B, T, NSA_WIDTH), F32),
        scratch_shapes=[pltpu.VMEM((G, HEADS_PER_GROUP * TQ, LANES), BF16),
                        pltpu.VMEM((G, V_ROWS, HEADS_PER_GROUP * TQ), F32),
                        pltpu.VMEM((G, 1, HEADS_PER_GROUP * TQ), F32)],
        compiler_params=pltpu.CompilerParams(
            dimension_semantics=("parallel", "arbitrary"), vmem_limit_bytes=VMEM_LIMIT),
        name="nsa",
    )(qq, kcmp, vcmpt, ksa, vst, kw, vwt, glt, ovt)


def _s5_prep_kernel(lre_ref, lim_ref, ldt_ref, bre_ref, bim_ref, are_ref, aim_ref, bbre_ref, bbim_ref):
    lre, lim = lre_ref[...], lim_ref[...]
    dt = jnp.exp(ldt_ref[...])
    mag = jnp.exp(lre * dt)
    a_re = mag * jnp.cos(lim * dt)
    a_im = mag * jnp.sin(lim * dt)
    den = lre * lre + lim * lim
    z_re = ((a_re - 1.0) * lre + a_im * lim) / den
    z_im = (a_im * lre - (a_re - 1.0) * lim) / den
    are_ref[...] = a_re
    aim_ref[...] = a_im
    bbre_ref[...] = z_re * bre_ref[...] - z_im * bim_ref[...]
    bbim_ref[...] = z_re * bim_ref[...] + z_im * bre_ref[...]


def _s5_prep(lre, lim, ldt, bre, bim):
    shp = jax.ShapeDtypeStruct(lre.shape, F32)
    return pl.pallas_call(_s5_prep_kernel, out_shape=[shp, shp, shp, shp], name="s5_prep")(lre, lim, ldt, bre, bim)


def _s5_kernel(u_ref, are_ref, aim_ref, bb_ref, cre_ref, cim_ref, d_ref, wg_ref, bg_ref, o_ref, x_scr, st_scr):
    nb = st_scr.shape[0]
    rows = u_ref.shape[0]
    HW = S5_WIDTH // 2
    HS = S5_GROUPS // 2 * S5_STATE

    @pl.when(pl.program_id(0) == 0)
    def _():
        st_scr[...] = jnp.zeros_like(st_scr)

    uv = u_ref[...]
    ub = uv.astype(BF16)
    for h in range(2):
        x_scr[:, 2 * HS * h:2 * HS * (h + 1)] = jnp.dot(ub[:, HW * h:HW * (h + 1)], bb_ref[h],
                                                          preferred_element_type=F32)

    for h in range(2):
        for j in range(HS // S5_CW):
            c_re = 2 * HS * h + S5_CW * j
            c_im = c_re + HS
            c_a = HS * h + S5_CW * j
            ar = are_ref[:, c_a:c_a + S5_CW]
            ai = aim_ref[:, c_a:c_a + S5_CW]

            def body(k, carry):
                sr, si = carry
                r0 = pl.multiple_of(k * 2 * nb, 2 * nb)
                br = x_scr[pl.ds(r0, 2 * nb), c_re:c_re + S5_CW]
                bi = x_scr[pl.ds(r0, 2 * nb), c_im:c_im + S5_CW]
                xr0 = ar * sr - ai * si + br[0:nb]
                xi0 = ar * si + ai * sr + bi[0:nb]
                xr1 = ar * xr0 - ai * xi0 + br[nb:]
                xi1 = ar * xi0 + ai * xr0 + bi[nb:]
                x_scr[pl.ds(r0, 2 * nb), c_re:c_re + S5_CW] = jnp.concatenate([xr0, xr1], axis=0)
                x_scr[pl.ds(r0, 2 * nb), c_im:c_im + S5_CW] = jnp.concatenate([xi0, xi1], axis=0)
                return xr1, xi1

            sr, si = lax.fori_loop(0, rows // (2 * nb), body,
                                   (st_scr[:, c_re:c_re + S5_CW], st_scr[:, c_im:c_im + S5_CW]))
            st_scr[:, c_re:c_re + S5_CW] = sr
            st_scr[:, c_im:c_im + S5_CW] = si

    ys = []
    for h in range(2):
        xr = x_scr[:, 2 * HS * h:2 * HS * h + HS].astype(BF16)
        xi = x_scr[:, 2 * HS * h + HS:2 * HS * (h + 1)].astype(BF16)
        ys.append(jnp.dot(xr, cre_ref[h], preferred_element_type=F32)
                  - jnp.dot(xi, cim_ref[h], preferred_element_type=F32))
    y = jnp.concatenate(ys, axis=1) + d_ref[...] * uv
    z = _gelu_tanh(y)
    gate = jnp.dot(z.astype(BF16), wg_ref[...], preferred_element_type=F32) + bg_ref[...]
    o_ref[...] = z * _sigmoid(gate)


def _s5(u_tb, a_re, a_im, bb, c_re, c_im, d, w_glu, b_glu, nb):
    rows_total = u_tb.shape[0]
    rows = TT * nb
    n_state_cols = bb.shape[2] * 2
    full = lambda a: pl.BlockSpec(a.shape, lambda i: (0,) * a.ndim)
    return pl.pallas_call(
        _s5_kernel,
        grid=(rows_total // rows,),
        in_specs=[pl.BlockSpec((rows, S5_WIDTH), lambda i: (i, 0)),
                  full(a_re), full(a_im), full(bb), full(c_re), full(c_im), full(d), full(w_glu), full(b_glu)],
        out_specs=pl.BlockSpec((rows, S5_WIDTH), lambda i: (i, 0)),
        out_shape=jax.ShapeDtypeStruct((rows_total, S5_WIDTH), F32),
        scratch_shapes=[pltpu.VMEM((rows, n_state_cols), F32), pltpu.VMEM((nb, n_state_cols), F32)],
        compiler_params=pltpu.CompilerParams(dimension_semantics=("arbitrary",), vmem_limit_bytes=VMEM_LIMIT),
        name="s5",
    )(u_tb, a_re, a_im, bb, c_re, c_im, d, w_glu, b_glu)


def _final_kernel(x_ref, g_ref, wb_ref, on_ref, os_ref, wpn_ref, wps_ref, wo_ref, fg_ref, o_ref):
    xv = x_ref[0]
    h = (_rms_scale(xv) * g_ref[...]).astype(BF16)

    def proj(a, b):
        return jnp.dot(h, wb_ref[:, a:b], preferred_element_type=F32)

    def silu(v):
        return v * _sigmoid(v)

    a_in = (on_ref[0] * silu(proj(0, NSA_WIDTH))).astype(BF16)
    b_in = (os_ref[0] * silu(proj(NSA_WIDTH, NSA_WIDTH + S5_WIDTH))).astype(BF16)
    branch_a = jnp.dot(a_in, wpn_ref[...], preferred_element_type=F32)
    branch_b = jnp.dot(b_in, wps_ref[...], preferred_element_type=F32)
    o1 = NSA_WIDTH + S5_WIDTH
    merged = (_sigmoid(proj(o1, o1 + D_MODEL)) * branch_a
              + _sigmoid(proj(o1 + D_MODEL, o1 + 2 * D_MODEL)) * branch_b)
    y = xv + jnp.dot(merged.astype(BF16), wo_ref[...], preferred_element_type=F32)
    o_ref[0] = _rms_scale(y) * fg_ref[...]


def _final(x, norm_g, w_b, o_nsa, o_s5, wpn, wps, wo, final_g):
    B, T, D = x.shape
    tm = TM_PROJ
    row_blk = lambda w: pl.BlockSpec((1, tm, w), lambda b, i: (b, i, 0))
    full = lambda a: pl.BlockSpec(a.shape, lambda b, i: (0,) * a.ndim)
    return pl.pallas_call(
        _final_kernel,
        grid=(B, T // tm),
        in_specs=[row_blk(D), full(norm_g), full(w_b), row_blk(NSA_WIDTH), row_blk(S5_WIDTH),
                  full(wpn), full(wps), full(wo), full(final_g)],
        out_specs=row_blk(D),
        out_shape=jax.ShapeDtypeStruct((B, T, D), F32),
        compiler_params=pltpu.CompilerParams(
            dimension_semantics=("parallel", "arbitrary"), vmem_limit_bytes=VMEM_LIMIT),
        name="final",
    )(x, norm_g, w_b, o_nsa, o_s5, wpn, wps, wo, final_g)


def _rope_tables(T):
    half = HEAD_DIM // 2
    inv_freq = ROPE_THETA ** (-jnp.arange(half, dtype=F32) / half)
    ang = jnp.arange(T).astype(F32)[:, None] * inv_freq[None, :]
    cos, sin = jnp.cos(ang), jnp.sin(ang)
    cos2 = jnp.concatenate([cos, cos, cos, cos], axis=1)
    sin2 = jnp.concatenate([-sin, sin, -sin, sin], axis=1)
    return cos2, sin2


def _block_diag_halves(m):
    g, a, b = m.shape
    gh = g // 2
    eye = jnp.eye(gh, dtype=m.dtype)
    out = m.reshape(2, gh, a, 1, b) * eye[None, :, None, :, None]
    return out.reshape(2, gh * a, gh * b)


def _compress_w1(w1):
    w1r = w1.reshape(2, CMP_STRIDE, HEAD_DIM, CMP_HIDDEN)
    eye = jnp.eye(NSA_GROUPS, dtype=w1.dtype)
    out = jnp.einsum('hjdn,gk->jgdkhn', w1r, eye)
    return out.reshape(CMP_STRIDE, NSA_GROUPS * HEAD_DIM, NSA_GROUPS * 2 * CMP_HIDDEN).astype(BF16)


def kernel(x, norm_g, w_in, cmp_pos_k, cmp_pos_v, cmp_w1_k, cmp_w2_k, cmp_w1_v, cmp_w2_v, s5_lam_re, s5_lam_im, s5_log_dt, s5_b_re, s5_b_im, s5_c_re, s5_c_im, s5_d, w_glu, b_glu, w_proj_nsa, w_proj_s5, w_out, final_g):
    B, T, D = x.shape
    assert w_in.shape[0] == 1, "single-layer block"
    NCH = T // CMP_STRIDE
    NS = T // SEL_BLOCK

    w = w_in[0]
    w_a = jnp.concatenate([w[:, :_OFF_GL], jnp.pad(w[:, _OFF_GL:_OFF_GN], ((0, 0), (0, LANES - 24))),
                           w[:, _OFF_U:_OFF_GS]], axis=1).astype(BF16)
    w_b = jnp.concatenate([w[:, _OFF_GN:_OFF_U], w[:, _OFF_GS:]], axis=1).astype(BF16)
    g2 = norm_g[0][None, :]
    cos2, sin2 = _rope_tables(T)

    qq, kc, vc, ksa, vst, kw, vwt, glt, u = _inproj(x, g2, w_a, cos2, sin2)

    w2k = jnp.concatenate([jnp.zeros_like(cmp_w2_k[0]), cmp_w2_k[0]], axis=1).astype(BF16)
    w2vt = cmp_w2_v[0].T.astype(BF16)
    kcmp, vcmpt = _compress(kc, vc, _compress_w1(cmp_w1_k[0]), _compress_w1(cmp_w1_v[0]), w2k, w2vt,
                            _pos_bias(cmp_pos_k[0], cmp_w1_k[0]), _pos_bias(cmp_pos_v[0], cmp_w1_v[0]))

    c_start = jnp.arange(NCH) * CMP_STRIDE
    s_start = jnp.arange(NS) * SEL_BLOCK
    ovt = ((c_start[None, :] < s_start[:, None] + SEL_BLOCK) & (c_start[None, :] + CMP_BLOCK > s_start[:, None])
           & (jnp.arange(NCH)[None, :] < NCH - 1)).astype(BF16)
    o_nsa = _nsa(qq, kcmp, vcmpt, ksa, vst, kw, vwt, glt, ovt)

    rep = lambda a: jnp.repeat(a, S5_GROUP, axis=0)
    tr = lambda b: b.transpose(0, 2, 1).reshape(S5_GROUPS * S5_GROUP, S5_STATE)
    a_re, a_im, bb_re, bb_im = _s5_prep(
        rep(s5_lam_re[0]), rep(s5_lam_im[0]),
        rep(jnp.broadcast_to(s5_log_dt[0][:, None], (S5_GROUPS, S5_STATE))),
        tr(s5_b_re[0]), tr(s5_b_im[0]))
    halves = lambda a: a[::S5_GROUP].reshape(1, -1)
    grp = lambda a: a.reshape(S5_GROUPS, S5_GROUP, S5_STATE)
    bb = jnp.concatenate([_block_diag_halves(grp(bb_re)), _block_diag_halves(grp(bb_im))], axis=2).astype(BF16)
    c_re = _block_diag_halves(s5_c_re[0].transpose(0, 2, 1)).astype(BF16)
    c_im = _block_diag_halves(s5_c_im[0].transpose(0, 2, 1)).astype(BF16)
    u_tb = u.transpose(1, 0, 2).reshape(T * B, S5_WIDTH)
    o_s5 = _s5(u_tb, halves(a_re), halves(a_im), bb, c_re, c_im, s5_d[0][None, :], w_glu[0].astype(BF16),
               b_glu[0][None, :], B)
    o_s5 = o_s5.reshape(T, B, S5_WIDTH).transpose(1, 0, 2)

    return _final(x, g2, w_b, o_nsa, o_s5, w_proj_nsa[0].astype(BF16), w_proj_s5[0].astype(BF16),
                  w_out[0].astype(BF16), final_g[None, :])
```

```python
import math

import jax
import jax.numpy as jnp
from jax import lax
from jax.experimental import pallas as pl
from jax.experimental.pallas import tpu as pltpu

F32 = jnp.float32
BF16 = jnp.bfloat16

D_MODEL = 1024
NSA_HEADS = 8
NSA_GROUPS = 2
HEADS_PER_GROUP = 4
HEAD_DIM = 64
NSA_WIDTH = 512
CMP_BLOCK = 32
CMP_STRIDE = 16
CMP_HIDDEN = 256
SEL_BLOCK = 64
SEL_TOPK = 16
WINDOW = 512
ROPE_THETA = 10000.0
FORCED_SCORE = 1.0e4
NEG = -1.0e30
S5_WIDTH = 512
S5_GROUP = 16
S5_GROUPS = 32
S5_STATE = 64
RMS_EPS = 1.0e-6

LANES = 128
SUBLANES = 8
VMEM_LIMIT = 56 * 1024 * 1024

_OFF_GL = 1280
_OFF_GN = 1304
_OFF_U = 1816
_OFF_GS = 2328

TM_PROJ = 512
TQ = 128
TK = 512
V_ROWS = 80
GATE_ROWS = 32
TT = 128
S5_CW = 512


def _gelu_tanh(x):
    c = math.sqrt(2.0 / math.pi)
    return 0.5 * x * (1.0 + jnp.tanh(c * (x + 0.044715 * (x * x * x))))


def _sigmoid(x):
    return 1.0 / (1.0 + jnp.exp(-x))


def _rms_scale(xv):
    ms = jnp.mean(xv * xv, axis=-1, keepdims=True)
    return xv * lax.rsqrt(ms + RMS_EPS)


def _nt_dot(a, b):
    return lax.dot_general(a, b, (((1,), (1,)), ((), ())), preferred_element_type=F32)


def _inproj_kernel(x_ref, g_ref, w_ref, cos_ref, sin_ref,
                   qq_ref, kc_ref, vc_ref, ks_ref, vs_ref, kw_ref, vw_ref, gl_ref, u_ref):
    h = (_rms_scale(x_ref[0]) * g_ref[...]).astype(BF16)
    cos2 = cos_ref[...]
    sin2 = sin_ref[...]
    lane = lax.broadcasted_iota(jnp.int32, cos2.shape, 1)
    first_half = (lane & (HEAD_DIM - 1)) < (HEAD_DIM // 2)
    low = lane < HEAD_DIM

    def proj(a, b):
        return jnp.dot(h, w_ref[:, a:b], preferred_element_type=F32)

    def rope(xs):
        partner = jnp.where(first_half, pltpu.roll(xs, 96, 1), pltpu.roll(xs, 32, 1))
        return xs * cos2 + partner * sin2

    scale = HEAD_DIM ** -0.5 * math.log2(math.e)
    for i in range(NSA_HEADS // 2):
        xs = proj(LANES * i, LANES * (i + 1)) * scale
        xr = rope(xs)
        qq_ref[0, 2 * i] = jnp.where(low, xr, pltpu.roll(xs, 64, 1)).astype(BF16)
        qq_ref[0, 2 * i + 1] = jnp.where(low, pltpu.roll(xr, 64, 1), xs).astype(BF16)

    kc_ref[0] = proj(512, 640)
    vc_ref[0] = proj(640, 768)
    tm = cos2.shape[0]
    t_row = pl.program_id(1) * tm + lax.broadcasted_iota(jnp.int32, cos2.shape, 0)
    blk_onehot = jnp.where(lane - HEAD_DIM == t_row // SEL_BLOCK, 1.0, 0.0)
    ones_rows = jnp.where(lax.broadcasted_iota(jnp.int32, (V_ROWS - HEAD_DIM, tm), 0) == 0, 1.0, 0.0)
    for (off, k_out, v_out, k_pad) in ((768, ks_ref, vs_ref, blk_onehot), (1024, kw_ref, vw_ref, 0.0)):
        kr = rope(proj(off, off + LANES))
        k_out[0, 0] = jnp.where(low, kr, k_pad).astype(BF16)
        k_out[0, 1] = jnp.where(low, pltpu.roll(kr, 64, 1), k_pad).astype(BF16)
        vt = proj(off + LANES, off + 2 * LANES).T
        for g in range(NSA_GROUPS):
            v_out[0, g] = jnp.concatenate([vt[HEAD_DIM * g:HEAD_DIM * (g + 1)], ones_rows], axis=0).astype(BF16)
    gl_ref[0] = _sigmoid(proj(1280, 1408)).T[0:GATE_ROWS]
    u_ref[0] = proj(1408, 1920)


def _inproj(x, norm_g, w_a, cos2, sin2):
    B, T, D = x.shape
    tm = TM_PROJ
    grid = (B, T // tm)
    row_blk = lambda w: pl.BlockSpec((1, tm, w), lambda b, i: (b, i, 0))
    kv_blk = pl.BlockSpec((1, NSA_GROUPS, tm, LANES), lambda b, i: (b, 0, i, 0))
    kv_shape = jax.ShapeDtypeStruct((B, NSA_GROUPS, T, LANES), BF16)
    vt_blk = pl.BlockSpec((1, NSA_GROUPS, V_ROWS, tm), lambda b, i: (b, 0, 0, i))
    vt_shape = jax.ShapeDtypeStruct((B, NSA_GROUPS, V_ROWS, T), BF16)
    return pl.pallas_call(
        _inproj_kernel,
        grid=grid,
        in_specs=[
            row_blk(D),
            pl.BlockSpec((1, D), lambda b, i: (0, 0)),
            pl.BlockSpec(w_a.shape, lambda b, i: (0, 0)),
            pl.BlockSpec((tm, LANES), lambda b, i: (i, 0)),
            pl.BlockSpec((tm, LANES), lambda b, i: (i, 0)),
        ],
        out_specs=[
            pl.BlockSpec((1, NSA_HEADS, tm, LANES), lambda b, i: (b, 0, i, 0)),
            row_blk(LANES), row_blk(LANES),
            kv_blk, vt_blk, kv_blk, vt_blk,
            pl.BlockSpec((1, GATE_ROWS, tm), lambda b, i: (b, 0, i)), row_blk(S5_WIDTH),
        ],
        out_shape=[
            jax.ShapeDtypeStruct((B, NSA_HEADS, T, LANES), BF16),
            jax.ShapeDtypeStruct((B, T, LANES), F32), jax.ShapeDtypeStruct((B, T, LANES), F32),
            kv_shape, vt_shape, kv_shape, vt_shape,
            jax.ShapeDtypeStruct((B, GATE_ROWS, T), F32), jax.ShapeDtypeStruct((B, T, S5_WIDTH), F32),
        ],
        compiler_params=pltpu.CompilerParams(
            dimension_semantics=("parallel", "arbitrary"), vmem_limit_bytes=VMEM_LIMIT),
        name="inproj",
    )(x, norm_g, w_a, cos2, sin2)


def _compress_kernel(kc_ref, vc_ref, w1k_ref, w1v_ref, w2k_ref, w2vt_ref, pbk_ref, pbv_ref, ko_ref, vo_ref):
    nch = ko_ref.shape[2]
    H = CMP_HIDDEN

    def hidden(c_ref, w1_ref, pb_ref):
        a = jnp.zeros((nch, 4 * H), F32)
        for j in range(CMP_STRIDE):
            rows = c_ref[0, pl.ds(j, nch, stride=CMP_STRIDE), :].astype(BF16)
            a = a + jnp.dot(rows, w1_ref[j], preferred_element_type=F32)
        out = []
        for g in range(NSA_GROUPS):
            lo = a[:, 2 * H * g:2 * H * g + H]
            hi = a[:, 2 * H * g + H:2 * H * (g + 1)]
            out.append(_gelu_tanh(lo + pltpu.roll(hi, nch - 1, 0) + pb_ref[...]).astype(BF16))
        return out

    hk = hidden(kc_ref, w1k_ref, pbk_ref)
    hv = hidden(vc_ref, w1v_ref, pbv_ref)
    for g in range(NSA_GROUPS):
        ko_ref[0, g] = jnp.dot(hk[g], w2k_ref[...], preferred_element_type=F32).astype(BF16)
        vo_ref[0, g] = _nt_dot(w2vt_ref[...], hv[g]).astype(BF16)


def _pos_bias_kernel(p_ref, w1_ref, o_ref):
    o_ref[...] = jnp.dot(p_ref[...].astype(BF16), w1_ref[...].astype(BF16), preferred_element_type=F32)


def _pos_bias(pos, w1):
    p8 = jnp.broadcast_to(pos.reshape(1, -1), (SUBLANES, pos.size))
    return pl.pallas_call(_pos_bias_kernel, out_shape=jax.ShapeDtypeStruct((SUBLANES, CMP_HIDDEN), F32),
                          name="pos_bias")(p8, w1)[0:1]


def _compress(kc, vc, w1k, w1v, w2k, w2vt, pbk, pbv):
    B, T, _ = kc.shape
    G = NSA_GROUPS
    nch = T // CMP_STRIDE
    c_blk = pl.BlockSpec((1, T, LANES), lambda b: (b, 0, 0))
    full = lambda a: pl.BlockSpec(a.shape, lambda b: (0,) * a.ndim)
    return pl.pallas_call(
        _compress_kernel,
        grid=(B,),
        in_specs=[c_blk, c_blk, full(w1k), full(w1v), full(w2k), full(w2vt), full(pbk), full(pbv)],
        out_specs=[pl.BlockSpec((1, G, nch, LANES), lambda b: (b, 0, 0, 0)),
                   pl.BlockSpec((1, G, HEAD_DIM, nch), lambda b: (b, 0, 0, 0))],
        out_shape=[jax.ShapeDtypeStruct((B, G, nch, LANES), BF16),
                   jax.ShapeDtypeStruct((B, G, HEAD_DIM, nch), BF16)],
        compiler_params=pltpu.CompilerParams(dimension_semantics=("parallel",), vmem_limit_bytes=VMEM_LIMIT),
        name="compress",
    )(kc, vc, w1k, w1v, w2k, w2vt, pbk, pbv)


def _nsa_kernel(qq_ref, kc_ref, vct_ref, ksa_ref, vst_ref, kw_ref, vwt_ref, glt_ref, ovt_ref, o_ref,
                qsel_scr, acc_scr, m_scr):
    t0 = pl.program_id(1) * TQ
    R = HEADS_PER_GROUP
    cols = R * TQ
    NCP = kc_ref.shape[2]
    NS = ovt_ref.shape[0]
    WK = WINDOW + TQ
    t_lane = t0 + (lax.broadcasted_iota(jnp.int32, (1, cols), 1) & (TQ - 1))
    low = lax.broadcasted_iota(jnp.int32, (cols, LANES), 1) < HEAD_DIM
    sub8 = lax.broadcasted_iota(jnp.int32, (SUBLANES, TQ), 0)
    c_end = lax.broadcasted_iota(jnp.int32, (NCP, cols), 0) * CMP_STRIDE + (CMP_BLOCK - 1)
    cmp_valid = c_end <= t_lane
    ks0 = pl.multiple_of(jnp.maximum(t0 - WINDOW, 0), LANES)
    wpos = ks0 + lax.broadcasted_iota(jnp.int32, (WK, cols), 0)

    def group_q(g):
        return qq_ref[0, R * g:R * (g + 1)].reshape(cols, LANES)

    def cmp_scores(g):
        return _nt_dot(kc_ref[0, g], group_q(g))

    def cmp_probs(s):
        s = jnp.where(cmp_valid, s, NEG)
        e = jnp.where(cmp_valid, jnp.exp2(s - jnp.max(s, axis=0, keepdims=True)), 0.0)
        return e * (1.0 / jnp.maximum(jnp.sum(e, axis=0, keepdims=True), 1.0e-30))

    def win_scores(g):
        return _nt_dot(kw_ref[0, g, pl.ds(ks0, WK), :], group_q(g))

    def win_probs(sw):
        sw = jnp.where(wpos <= t_lane, jnp.where(wpos > t_lane - WINDOW, sw, NEG), NEG)
        return jnp.exp2((sw - jnp.max(sw, axis=0, keepdims=True)).astype(BF16))

    def win_out(g, ew):
        ow = jnp.dot(vwt_ref[0, g, :, pl.ds(ks0, WK)], ew, preferred_element_type=F32)
        return ow[0:HEAD_DIM] * (1.0 / ow[HEAD_DIM:HEAD_DIM + 1])

    def select_blocks(g, p):
        psum = p[:, 0:TQ] + p[:, TQ:2 * TQ] + p[:, 2 * TQ:3 * TQ] + p[:, 3 * TQ:4 * TQ]
        p_hi = psum.astype(BF16)
        p_lo = (psum - p_hi.astype(F32)).astype(BF16)
        imp = (jnp.dot(ovt_ref[...], p_hi, preferred_element_type=F32)
               + jnp.dot(ovt_ref[...], p_lo, preferred_element_type=F32))
        blk = lax.broadcasted_iota(jnp.int32, (NS, TQ), 0)
        t_l = t0 + lax.broadcasted_iota(jnp.int32, (NS, TQ), 1)
        cur = t_l // SEL_BLOCK
        imp = jnp.where(blk * SEL_BLOCK <= t_l, imp, -1.0)
        imp = jnp.where(blk == 0, FORCED_SCORE, imp)
        imp = jnp.where(blk == cur, FORCED_SCORE, imp)
        imp = jnp.where(blk == cur - 1, FORCED_SCORE, imp)
        nv = NS // SUBLANES
        imp8 = [imp[SUBLANES * j:SUBLANES * (j + 1)] for j in range(nv)]
        rank8 = [jnp.zeros((SUBLANES, TQ), F32) for _ in range(nv)]
        for mm in range(NS):
            row = imp[mm:mm + 1, :]
            jm = mm // SUBLANES
            for j in range(nv):
                if j < jm:
                    ahead = jnp.where(row > imp8[j], 1.0, 0.0)
                elif j > jm:
                    ahead = jnp.where(row >= imp8[j], 1.0, 0.0)
                else:
                    tie = jnp.where(sub8 > (mm % SUBLANES), 1.0, 0.0)
                    ahead = jnp.where(row > imp8[j], 1.0, 0.0) + jnp.where(row == imp8[j], tie, 0.0)
                rank8[j] = rank8[j] + ahead
        rank = jnp.concatenate(rank8, axis=0)
        pen = jnp.where(rank < float(SEL_TOPK), 0.0, NEG)
        pen_t = jnp.concatenate([jnp.zeros((LANES - NS, TQ), F32), pen], axis=0).T
        qsel_scr[g] = jnp.where(low, group_q(g), jnp.concatenate([pen_t.astype(BF16)] * R, axis=0))

    s_c0 = cmp_scores(0)
    s_c1 = cmp_scores(1)
    p_c0 = cmp_probs(s_c0)
    s_w0 = win_scores(0)
    o_cmp = [jnp.dot(vct_ref[0, 0], p_c0.astype(BF16), preferred_element_type=F32)]
    p_c1 = cmp_probs(s_c1)
    s_w1 = win_scores(1)
    o_cmp.append(jnp.dot(vct_ref[0, 1], p_c1.astype(BF16), preferred_element_type=F32))
    select_blocks(0, p_c0)
    e_w0 = win_probs(s_w0)
    select_blocks(1, p_c1)
    o_win = [win_out(0, e_w0)]
    o_win.append(win_out(1, win_probs(s_w1)))

    acc_scr[...] = jnp.zeros_like(acc_scr)
    m_scr[...] = jnp.full(m_scr.shape, NEG, F32)

    def sel_jobs(tiles):
        jobs = [(kt, g, masked) for (kt, masked) in tiles for g in range(NSA_GROUPS)]

        def scores(job):
            kt, g, _ = job
            return _nt_dot(ksa_ref[0, g, pl.ds(pl.multiple_of(kt * TK, TK), TK), :], qsel_scr[g])

        def probs(job, sc):
            kt, g, masked = job
            if masked:
                kpos = kt * TK + lax.broadcasted_iota(jnp.int32, (TK, cols), 0)
                sc = jnp.where(kpos <= t_lane, sc, NEG)
            m_old = m_scr[g]
            m_new = jnp.maximum(m_old, jnp.max(sc, axis=0, keepdims=True))
            m_scr[g] = m_new
            return jnp.exp2((sc - m_new).astype(BF16)), jnp.exp2(m_old - m_new)

        def accumulate(job, pp, alpha):
            kt, g, _ = job
            acc_scr[g] = acc_scr[g] * alpha + jnp.dot(
                vst_ref[0, g, :, pl.ds(pl.multiple_of(kt * TK, TK), TK)], pp, preferred_element_type=F32)

        pending = {i: scores(jobs[i]) for i in range(min(2, len(jobs)))}
        for i, job in enumerate(jobs):
            pp, alpha = probs(job, pending.pop(i))
            if i + 2 < len(jobs):
                pending[i + 2] = scores(jobs[i + 2])
            accumulate(job, pp, alpha)

    n_full = t0 // TK

    def tile_pair(j, carry):
        sel_jobs([(2 * j, False), (2 * j + 1, False)])
        return carry

    lax.fori_loop(0, n_full // 2, tile_pair, 0)

    @pl.when(n_full % 2 == 1)
    def _():
        sel_jobs([(n_full - 1, False), (n_full, True)])

    @pl.when(n_full % 2 == 0)
    def _():
        sel_jobs([(n_full, True)])

    glt = glt_ref[0]
    heads = []
    for g in range(NSA_GROUPS):
        acc = acc_scr[g]
        o_sel = acc[0:HEAD_DIM] * (1.0 / acc[HEAD_DIM:HEAD_DIM + 1])
        for r in range(R):
            hh = R * g + r
            sl = slice(r * TQ, (r + 1) * TQ)
            heads.append(glt[3 * hh:3 * hh + 1] * o_cmp[g][:, sl]
                         + glt[3 * hh + 1:3 * hh + 2] * o_sel[:, sl]
                         + glt[3 * hh + 2:3 * hh + 3] * o_win[g][:, sl])
    o_ref[0] = jnp.concatenate(heads, axis=0).T


def _nsa(qq, kcmp, vcmpt, ksa, vst, kw, vwt, glt, ovt):
    B, H, T, _ = qq.shape
    G = NSA_GROUPS
    NCP = kcmp.shape[2]
    grid = (B, T // TQ)
    k_blk = lambda n: pl.BlockSpec((1, G, n, LANES), lambda b, i: (b, 0, 0, 0))
    vt_blk = lambda r, n: pl.BlockSpec((1, G, r, n), lambda b, i: (b, 0, 0, 0))
    return pl.pallas_call(
        _nsa_kernel,
        grid=grid,
        in_specs=[
            pl.BlockSpec((1, H, TQ, LANES), lambda b, i: (b, 0, i, 0)),
            k_blk(NCP), vt_blk(HEAD_DIM, NCP), k_blk(T), vt_blk(V_ROWS, T), k_blk(T), vt_blk(V_ROWS, T),
            pl.BlockSpec((1, GATE_ROWS, TQ), lambda b, i: (b, 0, i)),
            pl.BlockSpec(ovt.shape, lambda b, i: (0, 0)),
        ],
        out_specs=pl.BlockSpec((1, TQ, NSA_WIDTH), lambda b, i: (b, i, 0)),
        out_shape=jax.ShapeDtypeStruct((B, T, NSA_WIDTH), F32),
        scratch_shapes=[pltpu.VMEM((G, HEADS_PER_GROUP * TQ, LANES), BF16),
                        pltpu.VMEM((G, V_ROWS, HEADS_PER_GROUP * TQ), F32),
                        pltpu.VMEM((G, 1, HEADS_PER_GROUP * TQ), F32)],
        compiler_params=pltpu.CompilerParams(
            dimension_semantics=("parallel", "arbitrary"), vmem_limit_bytes=VMEM_LIMIT),
        name="nsa",
    )(qq, kcmp, vcmpt, ksa, vst, kw, vwt, glt, ovt)


def _s5_prep_kernel(lre_ref, lim_ref, ldt_ref, bre_ref, bim_ref, are_ref, aim_ref, bbre_ref, bbim_ref):
    lre, lim = lre_ref[...], lim_ref[...]
    dt = jnp.exp(ldt_ref[...])
    mag = jnp.exp(lre * dt)
    a_re = mag * jnp.cos(lim * dt)
    a_im = mag * jnp.sin(lim * dt)
    den = lre * lre + lim * lim
    z_re = ((a_re - 1.0) * lre + a_im * lim) / den
    z_im = (a_im * lre - (a_re - 1.0) * lim) / den
    are_ref[...] = a_re
    aim_ref[...] = a_im
    bbre_ref[...] = z_re * bre_ref[...] - z_im * bim_ref[...]
    bbim_ref[...] = z_re * bim_ref[...] + z_im * bre_ref[...]


def _s5_prep(lre, lim, ldt, bre, bim):
    shp = jax.ShapeDtypeStruct(lre.shape, F32)
    return pl.pallas_call(_s5_prep_kernel, out_shape=[shp, shp, shp, shp], name="s5_prep")(lre, lim, ldt, bre, bim)


def _s5_kernel(u_ref, are_ref, aim_ref, bb_ref, cre_ref, cim_ref, d_ref, wg_ref, bg_ref, o_ref, x_scr, st_scr):
    nb = st_scr.shape[0]
    rows = u_ref.shape[0]
    HW = S5_WIDTH // 2
    HS = S5_GROUPS // 2 * S5_STATE

    @pl.when(pl.program_id(0) == 0)
    def _():
        st_scr[...] = jnp.zeros_like(st_scr)

    uv = u_ref[...]
    ub = uv.astype(BF16)
    for h in range(2):
        x_scr[:, 2 * HS * h:2 * HS * (h + 1)] = jnp.dot(ub[:, HW * h:HW * (h + 1)], bb_ref[h],
                                                          preferred_element_type=F32)

    for h in range(2):
        for j in range(HS // S5_CW):
            c_re = 2 * HS * h + S5_CW * j
            c_im = c_re + HS
            c_a = HS * h + S5_CW * j
            ar = are_ref[:, c_a:c_a + S5_CW]
            ai = aim_ref[:, c_a:c_a + S5_CW]

            def body(k, carry):
                sr, si = carry
                r0 = pl.multiple_of(k * 2 * nb, 2 * nb)
                br = x_scr[pl.ds(r0, 2 * nb), c_re:c_re + S5_CW]
                bi = x_scr[pl.ds(r0, 2 * nb), c_im:c_im + S5_CW]
                xr0 = ar * sr - ai * si + br[0:nb]
                xi0 = ar * si + ai * sr + bi[0:nb]
                xr1 = ar * xr0 - ai * xi0 + br[nb:]
                xi1 = ar * xi0 + ai * xr0 + bi[nb:]
                x_scr[pl.ds(r0, 2 * nb), c_re:c_re + S5_CW] = jnp.concatenate([xr0, xr1], axis=0)
                x_scr[pl.ds(r0, 2 * nb), c_im:c_im + S5_CW] = jnp.concatenate([xi0, xi1], axis=0)
                return xr1, xi1

            sr, si = lax.fori_loop(0, rows // (2 * nb), body,
                                   (st_scr[:, c_re:c_re + S5_CW], st_scr[:, c_im:c_im + S5_CW]))
            st_scr[:, c_re:c_re + S5_CW] = sr
            st_scr[:, c_im:c_im + S5_CW] = si

    ys = []
    for h in range(2):
        xr = x_scr[:, 2 * HS * h:2 * HS * h + HS].astype(BF16)
        xi = x_scr[:, 2 * HS * h + HS:2 * HS * (h + 1)].astype(BF16)
        ys.append(jnp.dot(xr, cre_ref[h], preferred_element_type=F32)
                  - jnp.dot(xi, cim_ref[h], preferred_element_type=F32))
    y = jnp.concatenate(ys, axis=1) + d_ref[...] * uv
    z = _gelu_tanh(y)
    gate = jnp.dot(z.astype(BF16), wg_ref[...], preferred_element_type=F32) + bg_ref[...]
    o_ref[...] = z * _sigmoid(gate)


def _s5(u_tb, a_re, a_im, bb, c_re, c_im, d, w_glu, b_glu, nb):
    rows_total = u_tb.shape[0]
    rows = TT * nb
    n_state_cols = bb.shape[2] * 2
    full = lambda a: pl.BlockSpec(a.shape, lambda i: (0,) * a.ndim)
    return pl.pallas_call(
        _s5_kernel,
        grid=(rows_total // rows,),
        in_specs=[pl.BlockSpec((rows, S5_WIDTH), lambda i: (i, 0)),
                  full(a_re), full(a_im), full(bb), full(c_re), full(c_im), full(d), full(w_glu), full(b_glu)],
        out_specs=pl.BlockSpec((rows, S5_WIDTH), lambda i: (i, 0)),
        out_shape=jax.ShapeDtypeStruct((rows_total, S5_WIDTH), F32),
        scratch_shapes=[pltpu.VMEM((rows, n_state_cols), F32), pltpu.VMEM((nb, n_state_cols), F32)],
        compiler_params=pltpu.CompilerParams(dimension_semantics=("arbitrary",), vmem_limit_bytes=VMEM_LIMIT),
        name="s5",
    )(u_tb, a_re, a_im, bb, c_re, c_im, d, w_glu, b_glu)


def _final_kernel(x_ref, g_ref, wb_ref, on_ref, os_ref, wpn_ref, wps_ref, wo_ref, fg_ref, o_ref):
    xv = x_ref[0]
    h = (_rms_scale(xv) * g_ref[...]).astype(BF16)

    def proj(a, b):
        return jnp.dot(h, wb_ref[:, a:b], preferred_element_type=F32)

    def silu(v):
        return v * _sigmoid(v)

    a_in = (on_ref[0] * silu(proj(0, NSA_WIDTH))).astype(BF16)
    b_in = (os_ref[0] * silu(proj(NSA_WIDTH, NSA_WIDTH + S5_WIDTH))).astype(BF16)
    branch_a = jnp.dot(a_in, wpn_ref[...], preferred_element_type=F32)
    branch_b = jnp.dot(b_in, wps_ref[...], preferred_element_type=F32)
    o1 = NSA_WIDTH + S5_WIDTH
    merged = (_sigmoid(proj(o1, o1 + D_MODEL)) * branch_a
              + _sigmoid(proj(o1 + D_MODEL, o1 + 2 * D_MODEL)) * branch_b)
    y = xv + jnp.dot(merged.astype(BF16), wo_ref[...], preferred_element_type=F32)
    o_ref[0] = _rms_scale(y) * fg_ref[...]


def _final(x, norm_g, w_b, o_nsa, o_s5, wpn, wps, wo, final_g):
    B, T, D = x.shape
    tm = TM_PROJ
    row_blk = lambda w: pl.BlockSpec((1, tm, w), lambda b, i: (b, i, 0))
    full = lambda a: pl.BlockSpec(a.shape, lambda b, i: (0,) * a.ndim)
    return pl.pallas_call(
        _final_kernel,
        grid=(B, T // tm),
        in_specs=[row_blk(D), full(norm_g), full(w_b), row_blk(NSA_WIDTH), row_blk(S5_WIDTH),
                  full(wpn), full(wps), full(wo), full(final_g)],
        out_specs=row_blk(D),
        out_shape=jax.ShapeDtypeStruct((B, T, D), F32),
        compiler_params=pltpu.CompilerParams(
            dimension_semantics=("parallel", "arbitrary"), vmem_limit_bytes=VMEM_LIMIT),
        name="final",
    )(x, norm_g, w_b, o_nsa, o_s5, wpn, wps, wo, final_g)


def _rope_tables(T):
    half = HEAD_DIM // 2
    inv_freq = ROPE_THETA ** (-jnp.arange(half, dtype=F32) / half)
    ang = jnp.arange(T).astype(F32)[:, None] * inv_freq[None, :]
    cos, sin = jnp.cos(ang), jnp.sin(ang)
    cos2 = jnp.concatenate([cos, cos, cos, cos], axis=1)
    sin2 = jnp.concatenate([-sin, sin, -sin, sin], axis=1)
    return cos2, sin2


def _block_diag_halves(m):
    g, a, b = m.shape
    gh = g // 2
    eye = jnp.eye(gh, dtype=m.dtype)
    out = m.reshape(2, gh, a, 1, b) * eye[None, :, None, :, None]
    return out.reshape(2, gh * a, gh * b)


def _compress_w1(w1):
    w1r = w1.reshape(2, CMP_STRIDE, HEAD_DIM, CMP_HIDDEN)
    eye = jnp.eye(NSA_GROUPS, dtype=w1.dtype)
    out = jnp.einsum('hjdn,gk->jgdkhn', w1r, eye)
    return out.reshape(CMP_STRIDE, NSA_GROUPS * HEAD_DIM, NSA_GROUPS * 2 * CMP_HIDDEN).astype(BF16)


def kernel(x, norm_g, w_in, cmp_pos_k, cmp_pos_v, cmp_w1_k, cmp_w2_k, cmp_w1_v, cmp_w2_v, s5_lam_re, s5_lam_im, s5_log_dt, s5_b_re, s5_b_im, s5_c_re, s5_c_im, s5_d, w_glu, b_glu, w_proj_nsa, w_proj_s5, w_out, final_g):
    B, T, D = x.shape
    assert w_in.shape[0] == 1, "single-layer block"
    NCH = T // CMP_STRIDE
    NS = T // SEL_BLOCK

    w = w_in[0]
    w_a = jnp.concatenate([w[:, :_OFF_GL], jnp.pad(w[:, _OFF_GL:_OFF_GN], ((0, 0), (0, LANES - 24))),
                           w[:, _OFF_U:_OFF_GS]], axis=1).astype(BF16)
    w_b = jnp.concatenate([w[:, _OFF_GN:_OFF_U], w[:, _OFF_GS:]], axis=1).astype(BF16)
    g2 = norm_g[0][None, :]
    cos2, sin2 = _rope_tables(T)

    qq, kc, vc, ksa, vst, kw, vwt, glt, u = _inproj(x, g2, w_a, cos2, sin2)

    w2k = jnp.concatenate([jnp.zeros_like(cmp_w2_k[0]), cmp_w2_k[0]], axis=1).astype(BF16)
    w2vt = cmp_w2_v[0].T.astype(BF16)
    kcmp, vcmpt = _compress(kc, vc, _compress_w1(cmp_w1_k[0]), _compress_w1(cmp_w1_v[0]), w2k, w2vt,
                            _pos_bias(cmp_pos_k[0], cmp_w1_k[0]), _pos_bias(cmp_pos_v[0], cmp_w1_v[0]))

    c_start = jnp.arange(NCH) * CMP_STRIDE
    s_start = jnp.arange(NS) * SEL_BLOCK
    ovt = ((c_start[None, :] < s_start[:, None] + SEL_BLOCK) & (c_start[None, :] + CMP_BLOCK > s_start[:, None])
           & (jnp.arange(NCH)[None, :] < NCH - 1)).astype(BF16)
    o_nsa = _nsa(qq, kcmp, vcmpt, ksa, vst, kw, vwt, glt, ovt)

    rep = lambda a: jnp.repeat(a, S5_GROUP, axis=0)
    tr = lambda b: b.transpose(0, 2, 1).reshape(S5_GROUPS * S5_GROUP, S5_STATE)
    a_re, a_im, bb_re, bb_im = _s5_prep(
        rep(s5_lam_re[0]), rep(s5_lam_im[0]),
        rep(jnp.broadcast_to(s5_log_dt[0][:, None], (S5_GROUPS, S5_STATE))),
        tr(s5_b_re[0]), tr(s5_b_im[0]))
    halves = lambda a: a[::S5_GROUP].reshape(1, -1)
    grp = lambda a: a.reshape(S5_GROUPS, S5_GROUP, S5_STATE)
    bb = jnp.concatenate([_block_diag_halves(grp(bb_re)), _block_diag_halves(grp(bb_im))], axis=2).astype(BF16)
    c_re = _block_diag_halves(s5_c_re[0].transpose(0, 2, 1)).astype(BF16)
    c_im = _block_diag_halves(s5_c_im[0].transpose(0, 2, 1)).astype(BF16)
    u_tb = u.transpose(1, 0, 2).reshape(T * B, S5_WIDTH)
    o_s5 = _s5(u_tb, halves(a_re), halves(a_im), bb, c_re, c_im, s5_d[0][None, :], w_glu[0].astype(BF16),
               b_glu[0][None, :], B)
    o_s5 = o_s5.reshape(T, B, S5_WIDTH).transpose(1, 0, 2)

    return _final(x, g2, w_b, o_nsa, o_s5, w_proj_nsa[0].astype(BF16), w_proj_s5[0].astype(BF16),
                  w_out[0].astype(BF16), final_g[None, :])
```

```python
import math

import jax
import jax.numpy as jnp
from jax import lax
from jax.experimental import pallas as pl
from jax.experimental.pallas import tpu as pltpu

F32 = jnp.float32
BF16 = jnp.bfloat16

D_MODEL = 1024
NSA_HEADS = 8
NSA_GROUPS = 2
HEADS_PER_GROUP = 4
HEAD_DIM = 64
NSA_WIDTH = 512
CMP_BLOCK = 32
CMP_STRIDE = 16
CMP_HIDDEN = 256
SEL_BLOCK = 64
SEL_TOPK = 16
WINDOW = 512
ROPE_THETA = 10000.0
FORCED_SCORE = 1.0e4
NEG = -1.0e30
S5_WIDTH = 512
S5_GROUP = 16
S5_GROUPS = 32
S5_STATE = 64
RMS_EPS = 1.0e-6

LANES = 128
SUBLANES = 8
VMEM_LIMIT = 56 * 1024 * 1024

_OFF_GL = 1280
_OFF_GN = 1304
_OFF_U = 1816
_OFF_GS = 2328

TM_PROJ = 512
TQ = 128
TK = 512
SEL_LOOKAHEAD = 2
V_ROWS = 80
GATE_ROWS = 32
S5_L = 4
S5_CT = 128
S5_SCAN_SLABS = 4


def _gelu_tanh(x):
    c = math.sqrt(2.0 / math.pi)
    return 0.5 * x * (1.0 + jnp.tanh(c * (x + 0.044715 * (x * x * x))))


def _sigmoid(x):
    return 1.0 / (1.0 + jnp.exp(-x))


def _rms_scale(xv):
    ms = jnp.mean(xv * xv, axis=-1, keepdims=True)
    return xv * lax.rsqrt(ms + RMS_EPS)


def _nt_dot(a, b):
    return lax.dot_general(a, b, (((1,), (1,)), ((), ())), preferred_element_type=F32)


def _inproj_kernel(x_ref, g_ref, w_ref, cos_ref, sin_ref,
                   qq_ref, kc_ref, vc_ref, ks_ref, vs_ref, kw_ref, vw_ref, gl_ref, u_ref, us_scr):
    h = (_rms_scale(x_ref[0]) * g_ref[...]).astype(BF16)
    cos2 = cos_ref[...]
    sin2 = sin_ref[...]
    lane = lax.broadcasted_iota(jnp.int32, cos2.shape, 1)
    first_half = (lane & (HEAD_DIM - 1)) < (HEAD_DIM // 2)
    low = lane < HEAD_DIM

    def proj(a, b):
        return jnp.dot(h, w_ref[:, a:b], preferred_element_type=F32)

    def rope(xs):
        partner = jnp.where(first_half, pltpu.roll(xs, 96, 1), pltpu.roll(xs, 32, 1))
        return xs * cos2 + partner * sin2

    scale = HEAD_DIM ** -0.5 * math.log2(math.e)
    for i in range(NSA_HEADS // 2):
        xs = proj(LANES * i, LANES * (i + 1)) * scale
        xr = rope(xs)
        qq_ref[0, 2 * i] = jnp.where(low, xr, pltpu.roll(xs, 64, 1)).astype(BF16)
        qq_ref[0, 2 * i + 1] = jnp.where(low, pltpu.roll(xr, 64, 1), xs).astype(BF16)

    kc_ref[0] = proj(512, 640)
    vc_ref[0] = proj(640, 768)
    tm = cos2.shape[0]
    t_row = pl.program_id(1) * tm + lax.broadcasted_iota(jnp.int32, cos2.shape, 0)
    blk_onehot = jnp.where(lane - HEAD_DIM == t_row // SEL_BLOCK, 1.0, 0.0)
    ones_rows = jnp.where(lax.broadcasted_iota(jnp.int32, (V_ROWS - HEAD_DIM, tm), 0) == 0, 1.0, 0.0)
    for (off, k_out, v_out, k_pad) in ((768, ks_ref, vs_ref, blk_onehot), (1024, kw_ref, vw_ref, 0.0)):
        kr = rope(proj(off, off + LANES))
        k_out[0, 0] = jnp.where(low, kr, k_pad).astype(BF16)
        k_out[0, 1] = jnp.where(low, pltpu.roll(kr, 64, 1), k_pad).astype(BF16)
        vt = proj(off + LANES, off + 2 * LANES).T
        for g in range(NSA_GROUPS):
            v_out[0, g] = jnp.concatenate([vt[HEAD_DIM * g:HEAD_DIM * (g + 1)], ones_rows], axis=0).astype(BF16)
    gl_ref[0] = _sigmoid(proj(1280, 1408)).T[0:GATE_ROWS]
    uv = proj(1408, 1920)
    for s in range(S5_WIDTH // LANES):
        us_scr[s] = uv[:, LANES * s:LANES * (s + 1)]
    for i in range(S5_L):
        for s in range(S5_WIDTH // LANES):
            c0 = i * S5_WIDTH + LANES * s
            u_ref[0, :, c0:c0 + LANES] = us_scr[s, pl.ds(i, tm // S5_L, stride=S5_L), :]


def _inproj(x, norm_g, w_a, cos2, sin2):
    B, T, D = x.shape
    tm = TM_PROJ
    grid = (B, T // tm)
    row_blk = lambda w: pl.BlockSpec((1, tm, w), lambda b, i: (b, i, 0))
    kv_blk = pl.BlockSpec((1, NSA_GROUPS, tm, LANES), lambda b, i: (b, 0, i, 0))
    kv_shape = jax.ShapeDtypeStruct((B, NSA_GROUPS, T, LANES), BF16)
    vt_blk = pl.BlockSpec((1, NSA_GROUPS, V_ROWS, tm), lambda b, i: (b, 0, 0, i))
    vt_shape = jax.ShapeDtypeStruct((B, NSA_GROUPS, V_ROWS, T), BF16)
    return pl.pallas_call(
        _inproj_kernel,
        grid=grid,
        in_specs=[
            row_blk(D),
            pl.BlockSpec((1, D), lambda b, i: (0, 0)),
            pl.BlockSpec(w_a.shape, lambda b, i: (0, 0)),
            pl.BlockSpec((tm, LANES), lambda b, i: (i, 0)),
            pl.BlockSpec((tm, LANES), lambda b, i: (i, 0)),
        ],
        out_specs=[
            pl.BlockSpec((1, NSA_HEADS, tm, LANES), lambda b, i: (b, 0, i, 0)),
            row_blk(LANES), row_blk(LANES),
            kv_blk, vt_blk, kv_blk, vt_blk,
            pl.BlockSpec((1, GATE_ROWS, tm), lambda b, i: (b, 0, i)),
            pl.BlockSpec((1, tm // S5_L, S5_L * S5_WIDTH), lambda b, i: (b, i, 0)),
        ],
        out_shape=[
            jax.ShapeDtypeStruct((B, NSA_HEADS, T, LANES), BF16),
            jax.ShapeDtypeStruct((B, T, LANES), F32), jax.ShapeDtypeStruct((B, T, LANES), F32),
            kv_shape, vt_shape, kv_shape, vt_shape,
            jax.ShapeDtypeStruct((B, GATE_ROWS, T), F32),
            jax.ShapeDtypeStruct((B, T // S5_L, S5_L * S5_WIDTH), F32),
        ],
        scratch_shapes=[pltpu.VMEM((S5_WIDTH // LANES, tm, LANES), F32)],
        compiler_params=pltpu.CompilerParams(
            dimension_semantics=("parallel", "arbitrary"), vmem_limit_bytes=VMEM_LIMIT),
        name="inproj",
    )(x, norm_g, w_a, cos2, sin2)


def _compress_kernel(kc_ref, vc_ref, w1k_ref, w1v_ref, w2k_ref, w2vt_ref, pbk_ref, pbv_ref, ko_ref, vo_ref):
    nch = ko_ref.shape[2]
    H = CMP_HIDDEN

    def hidden(c_ref, w1_ref, pb_ref):
        a = jnp.zeros((nch, 4 * H), F32)
        for j in range(CMP_STRIDE):
            rows = c_ref[0, pl.ds(j, nch, stride=CMP_STRIDE), :].astype(BF16)
            a = a + jnp.dot(rows, w1_ref[j], preferred_element_type=F32)
        out = []
        for g in range(NSA_GROUPS):
            lo = a[:, 2 * H * g:2 * H * g + H]
            hi = a[:, 2 * H * g + H:2 * H * (g + 1)]
            out.append(_gelu_tanh(lo + pltpu.roll(hi, nch - 1, 0) + pb_ref[...]).astype(BF16))
        return out

    hk = hidden(kc_ref, w1k_ref, pbk_ref)
    hv = hidden(vc_ref, w1v_ref, pbv_ref)
    for g in range(NSA_GROUPS):
        ko_ref[0, g] = jnp.dot(hk[g], w2k_ref[...], preferred_element_type=F32).astype(BF16)
        vo_ref[0, g] = _nt_dot(w2vt_ref[...], hv[g]).astype(BF16)


def _pos_bias_kernel(p_ref, w1_ref, o_ref):
    o_ref[...] = jnp.dot(p_ref[...].astype(BF16), w1_ref[...].astype(BF16), preferred_element_type=F32)


def _pos_bias(pos, w1):
    p8 = jnp.broadcast_to(pos.reshape(1, -1), (SUBLANES, pos.size))
    return pl.pallas_call(_pos_bias_kernel, out_shape=jax.ShapeDtypeStruct((SUBLANES, CMP_HIDDEN), F32),
                          name="pos_bias")(p8, w1)[0:1]


def _compress(kc, vc, w1k, w1v, w2k, w2vt, pbk, pbv):
    B, T, _ = kc.shape
    G = NSA_GROUPS
    nch = T // CMP_STRIDE
    c_blk = pl.BlockSpec((1, T, LANES), lambda b: (b, 0, 0))
    full = lambda a: pl.BlockSpec(a.shape, lambda b: (0,) * a.ndim)
    return pl.pallas_call(
        _compress_kernel,
        grid=(B,),
        in_specs=[c_blk, c_blk, full(w1k), full(w1v), full(w2k), full(w2vt), full(pbk), full(pbv)],
        out_specs=[pl.BlockSpec((1, G, nch, LANES), lambda b: (b, 0, 0, 0)),
                   pl.BlockSpec((1, G, HEAD_DIM, nch), lambda b: (b, 0, 0, 0))],
        out_shape=[jax.ShapeDtypeStruct((B, G, nch, LANES), BF16),
                   jax.ShapeDtypeStruct((B, G, HEAD_DIM, nch), BF16)],
        compiler_params=pltpu.CompilerParams(dimension_semantics=("parallel",), vmem_limit_bytes=VMEM_LIMIT),
        name="compress",
    )(kc, vc, w1k, w1v, w2k, w2vt, pbk, pbv)


def _nsa_kernel(qq_ref, kc_ref, vct_ref, ksa_ref, vst_ref, kw_ref, vwt_ref, glt_ref, ovt_ref, o_ref,
                qsel_scr, acc_scr, m_scr):
    t0 = pl.program_id(1) * TQ
    R = HEADS_PER_GROUP
    cols = R * TQ
    NCP = kc_ref.shape[2]
    NS = ovt_ref.shape[0]
    WK = WINDOW + TQ
    t_lane = t0 + (lax.broadcasted_iota(jnp.int32, (1, cols), 1) & (TQ - 1))
    low = lax.broadcasted_iota(jnp.int32, (cols, LANES), 1) < HEAD_DIM
    sub8 = lax.broadcasted_iota(jnp.int32, (SUBLANES, TQ), 0)
    c_end = lax.broadcasted_iota(jnp.int32, (NCP, cols), 0) * CMP_STRIDE + (CMP_BLOCK - 1)
    cmp_valid = c_end <= t_lane
    ks0 = pl.multiple_of(jnp.maximum(t0 - WINDOW, 0), LANES)
    wpos = ks0 + lax.broadcasted_iota(jnp.int32, (WK, cols), 0)

    def group_q(g):
        return qq_ref[0, R * g:R * (g + 1)].reshape(cols, LANES)

    def cmp_scores(g):
        return _nt_dot(kc_ref[0, g], group_q(g))

    def cmp_probs(s):
        s = jnp.where(cmp_valid, s, NEG)
        e = jnp.where(cmp_valid, jnp.exp2(s - jnp.max(s, axis=0, keepdims=True)), 0.0)
        return e * (1.0 / jnp.maximum(jnp.sum(e, axis=0, keepdims=True), 1.0e-30))

    def win_scores(g):
        return _nt_dot(kw_ref[0, g, pl.ds(ks0, WK), :], group_q(g))

    def win_probs(sw):
        sw = jnp.where(wpos <= t_lane, jnp.where(wpos > t_lane - WINDOW, sw, NEG), NEG)
        return jnp.exp2((sw - jnp.max(sw, axis=0, keepdims=True)).astype(BF16))

    def win_out(g, ew):
        ow = jnp.dot(vwt_ref[0, g, :, pl.ds(ks0, WK)], ew, preferred_element_type=F32)
        return ow[0:HEAD_DIM] * (1.0 / ow[HEAD_DIM:HEAD_DIM + 1])

    def select_blocks(g, p):
        psum = p[:, 0:TQ] + p[:, TQ:2 * TQ] + p[:, 2 * TQ:3 * TQ] + p[:, 3 * TQ:4 * TQ]
        p_hi = psum.astype(BF16)
        p_lo = (psum - p_hi.astype(F32)).astype(BF16)
        imp = (jnp.dot(ovt_ref[...], p_hi, preferred_element_type=F32)
               + jnp.dot(ovt_ref[...], p_lo, preferred_element_type=F32))
        blk = lax.broadcasted_iota(jnp.int32, (NS, TQ), 0)
        t_l = t0 + lax.broadcasted_iota(jnp.int32, (NS, TQ), 1)
        cur = t_l // SEL_BLOCK
        imp = jnp.where(blk * SEL_BLOCK <= t_l, imp, -1.0)
        imp = jnp.where(blk == 0, FORCED_SCORE, imp)
        imp = jnp.where(blk == cur, FORCED_SCORE, imp)
        imp = jnp.where(blk == cur - 1, FORCED_SCORE, imp)
        nv = NS // SUBLANES
        imp8 = [imp[SUBLANES * j:SUBLANES * (j + 1)] for j in range(nv)]
        rank8 = [jnp.zeros((SUBLANES, TQ), F32) for _ in range(nv)]
        for mm in range(NS):
            row = imp[mm:mm + 1, :]
            jm = mm // SUBLANES
            for j in range(nv):
                if j < jm:
                    ahead = jnp.where(row > imp8[j], 1.0, 0.0)
                elif j > jm:
                    ahead = jnp.where(row >= imp8[j], 1.0, 0.0)
                else:
                    tie = jnp.where(sub8 > (mm % SUBLANES), 1.0, 0.0)
                    ahead = jnp.where(row > imp8[j], 1.0, 0.0) + jnp.where(row == imp8[j], tie, 0.0)
                rank8[j] = rank8[j] + ahead
        rank = jnp.concatenate(rank8, axis=0)
        pen = jnp.where(rank < float(SEL_TOPK), 0.0, NEG)
        pen_t = jnp.concatenate([jnp.zeros((LANES - NS, TQ), F32), pen], axis=0).T
        qsel_scr[g] = jnp.where(low, group_q(g), jnp.concatenate([pen_t.astype(BF16)] * R, axis=0))

    s_c0 = cmp_scores(0)
    s_c1 = cmp_scores(1)
    p_c0 = cmp_probs(s_c0)
    s_w0 = win_scores(0)
    o_cmp = [jnp.dot(vct_ref[0, 0], p_c0.astype(BF16), preferred_element_type=F32)]
    p_c1 = cmp_probs(s_c1)
    s_w1 = win_scores(1)
    o_cmp.append(jnp.dot(vct_ref[0, 1], p_c1.astype(BF16), preferred_element_type=F32))
    select_blocks(0, p_c0)
    e_w0 = win_probs(s_w0)
    select_blocks(1, p_c1)
    o_win = [win_out(0, e_w0)]
    o_win.append(win_out(1, win_probs(s_w1)))

    acc_scr[...] = jnp.zeros_like(acc_scr)
    m_scr[...] = jnp.full(m_scr.shape, NEG, F32)

    def sel_jobs(tiles):
        jobs = [(kt, g, masked) for (kt, masked) in tiles for g in range(NSA_GROUPS)]

        def scores(job):
            kt, g, _ = job
            return _nt_dot(ksa_ref[0, g, pl.ds(pl.multiple_of(kt * TK, TK), TK), :], qsel_scr[g])

        def stats(job, sc):
            kt, g, masked = job
            if masked:
                kpos = kt * TK + lax.broadcasted_iota(jnp.int32, (TK, cols), 0)
                sc = jnp.where(kpos <= t_lane, sc, NEG)
            m_old = m_scr[g]
            m_new = jnp.maximum(m_old, jnp.max(sc, axis=0, keepdims=True))
            m_scr[g] = m_new
            return sc, m_new, jnp.exp2(m_old - m_new)

        def accumulate(job, pp, alpha):
            kt, g, _ = job
            acc_scr[g] = acc_scr[g] * alpha + jnp.dot(
                vst_ref[0, g, :, pl.ds(pl.multiple_of(kt * TK, TK), TK)], pp, preferred_element_type=F32)

        n = len(jobs)
        pending = {i: scores(jobs[i]) for i in range(min(SEL_LOOKAHEAD, n))}
        st = {0: stats(jobs[0], pending.pop(0))}
        for i, job in enumerate(jobs):
            if i + 1 < n:
                st[i + 1] = stats(jobs[i + 1], pending.pop(i + 1))
            sc, m_new, alpha = st.pop(i)
            pp = jnp.exp2((sc - m_new).astype(BF16))
            if i + SEL_LOOKAHEAD < n:
                pending[i + SEL_LOOKAHEAD] = scores(jobs[i + SEL_LOOKAHEAD])
            accumulate(job, pp, alpha)

    n_full = t0 // TK

    def tile_pair(j, carry):
        sel_jobs([(2 * j, False), (2 * j + 1, False)])
        return carry

    lax.fori_loop(0, n_full // 2, tile_pair, 0)

    @pl.when(n_full % 2 == 1)
    def _():
        sel_jobs([(n_full - 1, False), (n_full, True)])

    @pl.when(n_full % 2 == 0)
    def _():
        sel_jobs([(n_full, True)])

    glt = glt_ref[0]
    heads = []
    for g in range(NSA_GROUPS):
        acc = acc_scr[g]
        o_sel = acc[0:HEAD_DIM] * (1.0 / acc[HEAD_DIM:HEAD_DIM + 1])
        for r in range(R):
            hh = R * g + r
            sl = slice(r * TQ, (r + 1) * TQ)
            heads.append(glt[3 * hh:3 * hh + 1] * o_cmp[g][:, sl]
                         + glt[3 * hh + 1:3 * hh + 2] * o_sel[:, sl]
                         + glt[3 * hh + 2:3 * hh + 3] * o_win[g][:, sl])
    o_ref[0] = jnp.concatenate(heads, axis=0).T


def _nsa(qq, kcmp, vcmpt, ksa, vst, kw, vwt, glt, ovt):
    B, H, T, _ = qq.shape
    G = NSA_GROUPS
    NCP = kcmp.shape[2]
    grid = (B, T // TQ)
    k_blk = lambda n: pl.BlockSpec((1, G, n, LANES), lambda b, i: (b, 0, 0, 0))
    vt_blk = lambda r, n: pl.BlockSpec((1, G, r, n), lambda b, i: (b, 0, 0, 0))
    return pl.pallas_call(
        _nsa_kernel,
        grid=grid,
        in_specs=[
            pl.BlockSpec((1, H, TQ, LANES), lambda b, i: (b, 0, i, 0)),
            k_blk(NCP), vt_blk(HEAD_DIM, NCP), k_blk(T), vt_blk(V_ROWS, T), k_blk(T), vt_blk(V_ROWS, T),
            pl.BlockSpec((1, GATE_ROWS, TQ), lambda b, i: (b, 0, i)),
            pl.BlockSpec(ovt.shape, lambda b, i: (0, 0)),
        ],
        out_specs=pl.BlockSpec((1, TQ, NSA_WIDTH), lambda b, i: (b, i, 0)),
        out_shape=jax.ShapeDtypeStruct((B, T, NSA_WIDTH), F32),
        scratch_shapes=[pltpu.VMEM((G, HEADS_PER_GROUP * TQ, LANES), BF16),
                        pltpu.VMEM((G, V_ROWS, HEADS_PER_GROUP * TQ), F32),
                        pltpu.VMEM((G, 1, HEADS_PER_GROUP * TQ), F32)],
        compiler_params=pltpu.CompilerParams(
            dimension_semantics=("parallel", "arbitrary"), vmem_limit_bytes=VMEM_LIMIT),
        name="nsa",
    )(qq, kcmp, vcmpt, ksa, vst, kw, vwt, glt, ovt)


def _s5_prep_kernel(lre_ref, lim_ref, ldt_ref, bre_ref, bim_ref, are_ref, aim_ref, bbre_ref, bbim_ref):
    lre, lim = lre_ref[...], lim_ref[...]
    dt = jnp.exp(ldt_ref[...])
    mag = jnp.exp(lre * dt)
    a_re = mag * jnp.cos(lim * dt)
    a_im = mag * jnp.sin(lim * dt)
    den = lre * lre + lim * lim
    z_re = ((a_re - 1.0) * lre + a_im * lim) / den
    z_im = (a_im * lre - (a_re - 1.0) * lim) / den
    are_ref[...] = a_re
    aim_ref[...] = a_im
    bbre_ref[...] = z_re * bre_ref[...] - z_im * bim_ref[...]
    bbim_ref[...] = z_re * bim_ref[...] + z_im * bre_ref[...]


def _s5_prep(lre, lim, ldt, bre, bim):
    shp = jax.ShapeDtypeStruct(lre.shape, F32)
    return pl.pallas_call(_s5_prep_kernel, out_shape=[shp, shp, shp, shp], name="s5_prep")(lre, lim, ldt, bre, bim)


def _s5_weights_kernel(bbre_ref, bbim_ref, are_ref, aim_ref, crt_ref, cit_ref,
                       bbp_ref, cpt_ref, tp_ref, alre_ref, alim_ref):
    bbre, bbim = bbre_ref[0], bbim_ref[0]
    crt, cit = crt_ref[0], cit_ref[0]
    are, aim = are_ref[0], aim_ref[0]
    pre, pim = jnp.ones_like(are), jnp.zeros_like(are)
    for k in range(S5_L):
        bpr = bbre * pre - bbim * pim
        bpi = bbre * pim + bbim * pre
        bbp_ref[0, k] = jnp.concatenate([bpr, bpi], axis=1).astype(BF16)
        tp_ref[0, k] = (_nt_dot(bpr.astype(BF16), crt.astype(BF16))
                        - _nt_dot(bpi.astype(BF16), cit.astype(BF16))).astype(BF16)
        pre, pim = pre * are - pim * aim, pre * aim + pim * are
        cpt_ref[0, k] = jnp.concatenate([crt * pre - cit * pim, -(crt * pim + cit * pre)], axis=1).astype(BF16)
    alre_ref[0] = pre
    alim_ref[0] = pim


def _s5_weights(bbre, bbim, are, aim, crt, cit):
    nh, hw, hs = bbre.shape
    blk = lambda r, c: pl.BlockSpec((1, r, c), lambda h: (h, 0, 0))
    blk4 = lambda r, c: pl.BlockSpec((1, S5_L, r, c), lambda h: (h, 0, 0, 0))
    return pl.pallas_call(
        _s5_weights_kernel,
        grid=(nh,),
        in_specs=[blk(hw, hs), blk(hw, hs), blk(1, hs), blk(1, hs), blk(hw, hs), blk(hw, hs)],
        out_specs=[blk4(hw, 2 * hs), blk4(hw, 2 * hs), blk4(hw, hw), blk(1, hs), blk(1, hs)],
        out_shape=[jax.ShapeDtypeStruct((nh, S5_L, hw, 2 * hs), BF16),
                   jax.ShapeDtypeStruct((nh, S5_L, hw, 2 * hs), BF16),
                   jax.ShapeDtypeStruct((nh, S5_L, hw, hw), BF16),
                   jax.ShapeDtypeStruct((nh, 1, hs), F32), jax.ShapeDtypeStruct((nh, 1, hs), F32)],
        compiler_params=pltpu.CompilerParams(dimension_semantics=("parallel",), vmem_limit_bytes=VMEM_LIMIT),
        name="s5_weights",
    )(bbre, bbim, are, aim, crt, cit)


def _s5_kernel(u_ref, bbp_ref, cpt_ref, tp_ref, alre_ref, alim_ref, d_ref, wg_ref, bg_ref, o_ref, e_scr, st_scr):
    nb, ct, _ = u_ref.shape
    L = S5_L
    HW = S5_WIDTH // 2
    HS = S5_GROUPS // 2 * S5_STATE
    nsl = HS // LANES
    W = S5_SCAN_SLABS * LANES

    @pl.when(pl.program_id(0) == 0)
    def _():
        st_scr[...] = jnp.zeros_like(st_scr)

    uf = u_ref[...].reshape(nb * ct, L * S5_WIDTH)
    ub = uf.astype(BF16)

    def u_part(i, h):
        c0 = i * S5_WIDTH + h * HW
        return ub[:, c0:c0 + HW]

    for h in range(2):
        e = jnp.dot(u_part(0, h), bbp_ref[h, L - 1], preferred_element_type=F32)
        for i in range(1, L):
            e = e + jnp.dot(u_part(i, h), bbp_ref[h, L - 1 - i], preferred_element_type=F32)
        for s in range(2 * nsl):
            for b in range(nb):
                e_scr[2 * nsl * h + s, pl.ds(b, ct, stride=nb), :] = e[b * ct:(b + 1) * ct, LANES * s:LANES * (s + 1)]

    for h in range(2):
        for j in range(nsl // S5_SCAN_SLABS):
            sl_re = [2 * nsl * h + S5_SCAN_SLABS * j + q for q in range(S5_SCAN_SLABS)]
            sl_im = [s + nsl for s in sl_re]
            ar = alre_ref[h, :, W * j:W * (j + 1)]
            ai = alim_ref[h, :, W * j:W * (j + 1)]
            c_re = 2 * HS * h + W * j
            c_im = c_re + HS

            def body(k, carry):
                sr, si = carry
                r0 = pl.multiple_of(k * 2 * nb, 2 * nb)
                er = jnp.concatenate([e_scr[s, pl.ds(r0, 2 * nb), :] for s in sl_re], axis=1)
                ei = jnp.concatenate([e_scr[s, pl.ds(r0, 2 * nb), :] for s in sl_im], axis=1)
                tr = ar * sr - ai * si + er[0:nb]
                ti = ar * si + ai * sr + ei[0:nb]
                xr = jnp.concatenate([sr, tr], axis=0)
                xi = jnp.concatenate([si, ti], axis=0)
                for q in range(S5_SCAN_SLABS):
                    e_scr[sl_re[q], pl.ds(r0, 2 * nb), :] = xr[:, LANES * q:LANES * (q + 1)]
                    e_scr[sl_im[q], pl.ds(r0, 2 * nb), :] = xi[:, LANES * q:LANES * (q + 1)]
                return ar * tr - ai * ti + er[nb:], ar * ti + ai * tr + ei[nb:]

            sr, si = lax.fori_loop(0, ct // 2, body, (st_scr[:, c_re:c_re + W], st_scr[:, c_im:c_im + W]))
            st_scr[:, c_re:c_re + W] = sr
            st_scr[:, c_im:c_im + W] = si

    xs = []
    for h in range(2):
        per_b = [jnp.concatenate([e_scr[2 * nsl * h + s, pl.ds(b, ct, stride=nb), :] for s in range(2 * nsl)], axis=1)
                 for b in range(nb)]
        xs.append(jnp.concatenate(per_b, axis=0).astype(BF16))
    outs = []
    for j in range(L):
        ys = []
        for h in range(2):
            y = _nt_dot(xs[h], cpt_ref[h, j])
            for i in range(j + 1):
                y = y + jnp.dot(u_part(i, h), tp_ref[h, j - i], preferred_element_type=F32)
            ys.append(y)
        yj = jnp.concatenate(ys, axis=1) + d_ref[...] * uf[:, j * S5_WIDTH:(j + 1) * S5_WIDTH]
        z = _gelu_tanh(yj)
        gate = jnp.dot(z.astype(BF16), wg_ref[...], preferred_element_type=F32) + bg_ref[...]
        outs.append(z * _sigmoid(gate))
    o_ref[...] = jnp.concatenate(outs, axis=1).reshape(nb, ct, L * S5_WIDTH)


def _s5(u4, bbp, cpt, tp, alre, alim, d, w_glu, b_glu):
    nb, nchunks, w4 = u4.shape
    ct = S5_CT
    n_slabs = 2 * bbp.shape[3] // LANES
    resident = lambda a: pl.BlockSpec(a.shape, lambda i: (0,) * a.ndim, pipeline_mode=pl.Buffered(1))
    u_blk = pl.BlockSpec((nb, ct, w4), lambda i: (0, i, 0))
    return pl.pallas_call(
        _s5_kernel,
        grid=(nchunks // ct,),
        in_specs=[u_blk, resident(bbp), resident(cpt), resident(tp), resident(alre), resident(alim),
                  resident(d), resident(w_glu), resident(b_glu)],
        out_specs=u_blk,
        out_shape=jax.ShapeDtypeStruct(u4.shape, F32),
        scratch_shapes=[pltpu.VMEM((n_slabs, nb * ct, LANES), F32), pltpu.VMEM((nb, n_slabs * LANES), F32)],
        compiler_params=pltpu.CompilerParams(dimension_semantics=("arbitrary",), vmem_limit_bytes=VMEM_LIMIT),
        name="s5",
    )(u4, bbp, cpt, tp, alre, alim, d, w_glu, b_glu)


def _final_kernel(x_ref, g_ref, wb_ref, on_ref, os_ref, wpn_ref, wps_ref, wo_ref, fg_ref, o_ref, os_scr):
    xv = x_ref[0]
    h = (_rms_scale(xv) * g_ref[...]).astype(BF16)
    tm = xv.shape[0]
    o4 = os_ref[0]
    for i in range(S5_L):
        for s in range(S5_WIDTH // LANES):
            c0 = i * S5_WIDTH + LANES * s
            os_scr[s, pl.ds(i, tm // S5_L, stride=S5_L), :] = o4[:, c0:c0 + LANES]
    o_s5 = jnp.concatenate([os_scr[s] for s in range(S5_WIDTH // LANES)], axis=1)

    def proj(a, b):
        return jnp.dot(h, wb_ref[:, a:b], preferred_element_type=F32)

    def silu(v):
        return v * _sigmoid(v)

    a_in = (on_ref[0] * silu(proj(0, NSA_WIDTH))).astype(BF16)
    b_in = (o_s5 * silu(proj(NSA_WIDTH, NSA_WIDTH + S5_WIDTH))).astype(BF16)
    branch_a = jnp.dot(a_in, wpn_ref[...], preferred_element_type=F32)
    branch_b = jnp.dot(b_in, wps_ref[...], preferred_element_type=F32)
    o1 = NSA_WIDTH + S5_WIDTH
    merged = (_sigmoid(proj(o1, o1 + D_MODEL)) * branch_a
              + _sigmoid(proj(o1 + D_MODEL, o1 + 2 * D_MODEL)) * branch_b)
    y = xv + jnp.dot(merged.astype(BF16), wo_ref[...], preferred_element_type=F32)
    o_ref[0] = _rms_scale(y) * fg_ref[...]


def _final(x, norm_g, w_b, o_nsa, o_s5, wpn, wps, wo, final_g):
    B, T, D = x.shape
    tm = TM_PROJ
    row_blk = lambda w: pl.BlockSpec((1, tm, w), lambda b, i: (b, i, 0))
    full = lambda a: pl.BlockSpec(a.shape, lambda b, i: (0,) * a.ndim)
    return pl.pallas_call(
        _final_kernel,
        grid=(B, T // tm),
        in_specs=[row_blk(D), full(norm_g), full(w_b), row_blk(NSA_WIDTH),
                  pl.BlockSpec((1, tm // S5_L, S5_L * S5_WIDTH), lambda b, i: (b, i, 0)),
                  full(wpn), full(wps), full(wo), full(final_g)],
        out_specs=row_blk(D),
        out_shape=jax.ShapeDtypeStruct((B, T, D), F32),
        scratch_shapes=[pltpu.VMEM((S5_WIDTH // LANES, tm, LANES), F32)],
        compiler_params=pltpu.CompilerParams(
            dimension_semantics=("parallel", "arbitrary"), vmem_limit_bytes=VMEM_LIMIT),
        name="final",
    )(x, norm_g, w_b, o_nsa, o_s5, wpn, wps, wo, final_g)


def _rope_tables(T):
    half = HEAD_DIM // 2
    inv_freq = ROPE_THETA ** (-jnp.arange(half, dtype=F32) / half)
    ang = jnp.arange(T).astype(F32)[:, None] * inv_freq[None, :]
    cos, sin = jnp.cos(ang), jnp.sin(ang)
    cos2 = jnp.concatenate([cos, cos, cos, cos], axis=1)
    sin2 = jnp.concatenate([-sin, sin, -sin, sin], axis=1)
    return cos2, sin2


def _block_diag_halves(m):
    g, a, b = m.shape
    gh = g // 2
    eye = jnp.eye(gh, dtype=m.dtype)
    out = m.reshape(2, gh, a, 1, b) * eye[None, :, None, :, None]
    return out.reshape(2, gh * a, gh * b)


def _compress_w1(w1):
    w1r = w1.reshape(2, CMP_STRIDE, HEAD_DIM, CMP_HIDDEN)
    eye = jnp.eye(NSA_GROUPS, dtype=w1.dtype)
    out = jnp.einsum('hjdn,gk->jgdkhn', w1r, eye)
    return out.reshape(CMP_STRIDE, NSA_GROUPS * HEAD_DIM, NSA_GROUPS * 2 * CMP_HIDDEN).astype(BF16)


def kernel(x, norm_g, w_in, cmp_pos_k, cmp_pos_v, cmp_w1_k, cmp_w2_k, cmp_w1_v, cmp_w2_v, s5_lam_re, s5_lam_im, s5_log_dt, s5_b_re, s5_b_im, s5_c_re, s5_c_im, s5_d, w_glu, b_glu, w_proj_nsa, w_proj_s5, w_out, final_g):
    B, T, D = x.shape
    assert w_in.shape[0] == 1, "single-layer block"
    NCH = T // CMP_STRIDE
    NS = T // SEL_BLOCK

    w = w_in[0]
    w_a = jnp.concatenate([w[:, :_OFF_GL], jnp.pad(w[:, _OFF_GL:_OFF_GN], ((0, 0), (0, LANES - 24))),
                           w[:, _OFF_U:_OFF_GS]], axis=1).astype(BF16)
    w_b = jnp.concatenate([w[:, _OFF_GN:_OFF_U], w[:, _OFF_GS:]], axis=1).astype(BF16)
    g2 = norm_g[0][None, :]
    cos2, sin2 = _rope_tables(T)

    qq, kc, vc, ksa, vst, kw, vwt, glt, u4 = _inproj(x, g2, w_a, cos2, sin2)

    w2k = jnp.concatenate([jnp.zeros_like(cmp_w2_k[0]), cmp_w2_k[0]], axis=1).astype(BF16)
    w2vt = cmp_w2_v[0].T.astype(BF16)
    kcmp, vcmpt = _compress(kc, vc, _compress_w1(cmp_w1_k[0]), _compress_w1(cmp_w1_v[0]), w2k, w2vt,
                            _pos_bias(cmp_pos_k[0], cmp_w1_k[0]), _pos_bias(cmp_pos_v[0], cmp_w1_v[0]))

    c_start = jnp.arange(NCH) * CMP_STRIDE
    s_start = jnp.arange(NS) * SEL_BLOCK
    ovt = ((c_start[None, :] < s_start[:, None] + SEL_BLOCK) & (c_start[None, :] + CMP_BLOCK > s_start[:, None])
           & (jnp.arange(NCH)[None, :] < NCH - 1)).astype(BF16)
    o_nsa = _nsa(qq, kcmp, vcmpt, ksa, vst, kw, vwt, glt, ovt)

    rep = lambda a: jnp.repeat(a, S5_GROUP, axis=0)
    tr = lambda b: b.transpose(0, 2, 1).reshape(S5_GROUPS * S5_GROUP, S5_STATE)
    a_re, a_im, bb_re, bb_im = _s5_prep(
        rep(s5_lam_re[0]), rep(s5_lam_im[0]),
        rep(jnp.broadcast_to(s5_log_dt[0][:, None], (S5_GROUPS, S5_STATE))),
        tr(s5_b_re[0]), tr(s5_b_im[0]))
    halves = lambda a: a[::S5_GROUP].reshape(2, 1, -1)
    grp = lambda a: a.reshape(S5_GROUPS, S5_GROUP, S5_STATE)
    bbp, cpt, tp, alre, alim = _s5_weights(
        _block_diag_halves(grp(bb_re)), _block_diag_halves(grp(bb_im)), halves(a_re), halves(a_im),
        _block_diag_halves(s5_c_re[0]), _block_diag_halves(s5_c_im[0]))
    o_s5 = _s5(u4, bbp, cpt, tp, alre, alim, s5_d[0][None, :], w_glu[0].astype(BF16), b_glu[0][None, :])

    return _final(x, g2, w_b, o_nsa, o_s5, w_proj_nsa[0].astype(BF16), w_proj_s5[0].astype(BF16),
                  w_out[0].astype(BF16), final_g[None, :])
```

```python
import math

import jax
import jax.numpy as jnp
from jax import lax
from jax.experimental import pallas as pl
from jax.experimental.pallas import tpu as pltpu

F32 = jnp.float32
BF16 = jnp.bfloat16

D_MODEL = 1024
NSA_HEADS = 8
NSA_GROUPS = 2
HEADS_PER_GROUP = 4
HEAD_DIM = 64
NSA_WIDTH = 512
CMP_BLOCK = 32
CMP_STRIDE = 16
CMP_HIDDEN = 256
SEL_BLOCK = 64
SEL_TOPK = 16
WINDOW = 512
ROPE_THETA = 10000.0
FORCED_SCORE = 1.0e4
NEG = -1.0e30
S5_WIDTH = 512
S5_GROUP = 16
S5_GROUPS = 32
S5_STATE = 64
RMS_EPS = 1.0e-6

LANES = 128
SUBLANES = 8
VMEM_LIMIT = 56 * 1024 * 1024

_OFF_GL = 1280
_OFF_GN = 1304
_OFF_U = 1816
_OFF_GS = 2328

TM_PROJ = 512
TQ = 128
TK = 512
SEL_COL_SPLIT = 1
SEL_LOOKAHEAD = 2
V_ROWS = 80
GATE_ROWS = 32
S5_L = 4
S5_CT = 128
S5_SCAN_SLABS = 4


def _gelu_tanh(x):
    c = math.sqrt(2.0 / math.pi)
    return 0.5 * x * (1.0 + jnp.tanh(c * (x + 0.044715 * (x * x * x))))


def _sigmoid(x):
    return 1.0 / (1.0 + jnp.exp(-x))


def _rms_scale(xv):
    ms = jnp.mean(xv * xv, axis=-1, keepdims=True)
    return xv * lax.rsqrt(ms + RMS_EPS)


def _nt_dot(a, b):
    return lax.dot_general(a, b, (((1,), (1,)), ((), ())), preferred_element_type=F32)


def _inproj_kernel(x_ref, g_ref, w_ref, cos_ref, sin_ref,
                   qq_ref, kc_ref, vc_ref, ks_ref, vs_ref, kw_ref, vw_ref, gl_ref, u_ref, us_scr):
    h = (_rms_scale(x_ref[0]) * g_ref[...]).astype(BF16)
    cos2 = cos_ref[...]
    sin2 = sin_ref[...]
    lane = lax.broadcasted_iota(jnp.int32, cos2.shape, 1)
    first_half = (lane & (HEAD_DIM - 1)) < (HEAD_DIM // 2)
    low = lane < HEAD_DIM

    wide = {}

    def proj(a, b):
        for (s0, s1) in ((0, 512), (512, 1280), (1280, 1920)):
            if s0 <= a and b <= s1:
                if s0 not in wide:
                    wide[s0] = jnp.dot(h, w_ref[:, s0:s1], preferred_element_type=F32)
                return wide[s0][:, a - s0:b - s0]
        raise ValueError((a, b))

    def rope(xs):
        partner = jnp.where(first_half, pltpu.roll(xs, 96, 1), pltpu.roll(xs, 32, 1))
        return xs * cos2 + partner * sin2

    scale = HEAD_DIM ** -0.5 * math.log2(math.e)
    for i in range(NSA_HEADS // 2):
        xs = proj(LANES * i, LANES * (i + 1)) * scale
        xr = rope(xs)
        qq_ref[0, 2 * i] = jnp.where(low, xr, pltpu.roll(xs, 64, 1)).astype(BF16)
        qq_ref[0, 2 * i + 1] = jnp.where(low, pltpu.roll(xr, 64, 1), xs).astype(BF16)

    kc_ref[0] = proj(512, 640)
    vc_ref[0] = proj(640, 768)
    tm = cos2.shape[0]
    t_row = pl.program_id(1) * tm + lax.broadcasted_iota(jnp.int32, cos2.shape, 0)
    blk_onehot = jnp.where(lane - HEAD_DIM == t_row // SEL_BLOCK, 1.0, 0.0)
    ones_rows = jnp.where(lax.broadcasted_iota(jnp.int32, (V_ROWS - HEAD_DIM, tm), 0) == 0, 1.0, 0.0)
    for (off, k_out, v_out, k_pad) in ((768, ks_ref, vs_ref, blk_onehot), (1024, kw_ref, vw_ref, 0.0)):
        kr = rope(proj(off, off + LANES))
        k_out[0, 0] = jnp.where(low, kr, k_pad).astype(BF16)
        k_out[0, 1] = jnp.where(low, pltpu.roll(kr, 64, 1), k_pad).astype(BF16)
        vt = proj(off + LANES, off + 2 * LANES).T
        for g in range(NSA_GROUPS):
            v_out[0, g] = jnp.concatenate([vt[HEAD_DIM * g:HEAD_DIM * (g + 1)], ones_rows], axis=0).astype(BF16)
    gl_ref[0] = _sigmoid(proj(1280, 1408)).T[0:GATE_ROWS]
    uv = proj(1408, 1920)
    for s in range(S5_WIDTH // LANES):
        us_scr[s] = uv[:, LANES * s:LANES * (s + 1)]
    for i in range(S5_L):
        for s in range(S5_WIDTH // LANES):
            c0 = i * S5_WIDTH + LANES * s
            u_ref[0, :, c0:c0 + LANES] = us_scr[s, pl.ds(i, tm // S5_L, stride=S5_L), :]


def _inproj(x, norm_g, w_a, cos2, sin2):
    B, T, D = x.shape
    tm = TM_PROJ
    grid = (B, T // tm)
    row_blk = lambda w: pl.BlockSpec((1, tm, w), lambda b, i: (b, i, 0))
    kv_blk = pl.BlockSpec((1, NSA_GROUPS, tm, LANES), lambda b, i: (b, 0, i, 0))
    kv_shape = jax.ShapeDtypeStruct((B, NSA_GROUPS, T, LANES), BF16)
    vt_blk = pl.BlockSpec((1, NSA_GROUPS, V_ROWS, tm), lambda b, i: (b, 0, 0, i))
    vt_shape = jax.ShapeDtypeStruct((B, NSA_GROUPS, V_ROWS, T), BF16)
    return pl.pallas_call(
        _inproj_kernel,
        grid=grid,
        in_specs=[
            row_blk(D),
            pl.BlockSpec((1, D), lambda b, i: (0, 0)),
            pl.BlockSpec(w_a.shape, lambda b, i: (0, 0)),
            pl.BlockSpec((tm, LANES), lambda b, i: (i, 0)),
            pl.BlockSpec((tm, LANES), lambda b, i: (i, 0)),
        ],
        out_specs=[
            pl.BlockSpec((1, NSA_HEADS, tm, LANES), lambda b, i: (b, 0, i, 0)),
            row_blk(LANES), row_blk(LANES),
            kv_blk, vt_blk, kv_blk, vt_blk,
            pl.BlockSpec((1, GATE_ROWS, tm), lambda b, i: (b, 0, i)),
            pl.BlockSpec((1, tm // S5_L, S5_L * S5_WIDTH), lambda b, i: (b, i, 0)),
        ],
        out_shape=[
            jax.ShapeDtypeStruct((B, NSA_HEADS, T, LANES), BF16),
            jax.ShapeDtypeStruct((B, T, LANES), F32), jax.ShapeDtypeStruct((B, T, LANES), F32),
            kv_shape, vt_shape, kv_shape, vt_shape,
            jax.ShapeDtypeStruct((B, GATE_ROWS, T), F32),
            jax.ShapeDtypeStruct((B, T // S5_L, S5_L * S5_WIDTH), F32),
        ],
        scratch_shapes=[pltpu.VMEM((S5_WIDTH // LANES, tm, LANES), F32)],
        compiler_params=pltpu.CompilerParams(
            dimension_semantics=("parallel", "arbitrary"), vmem_limit_bytes=VMEM_LIMIT),
        name="inproj",
    )(x, norm_g, w_a, cos2, sin2)


def _compress_kernel(kc_ref, vc_ref, w1k_ref, w1v_ref, w2k_ref, w2vt_ref, pbk_ref, pbv_ref, ko_ref, vo_ref):
    nch = ko_ref.shape[2]
    H = CMP_HIDDEN

    def hidden(c_ref, w1_ref, pb_ref):
        a = jnp.zeros((nch, 4 * H), F32)
        for j in range(CMP_STRIDE):
            rows = c_ref[0, pl.ds(j, nch, stride=CMP_STRIDE), :].astype(BF16)
            a = a + jnp.dot(rows, w1_ref[j], preferred_element_type=F32)
        out = []
        for g in range(NSA_GROUPS):
            lo = a[:, 2 * H * g:2 * H * g + H]
            hi = a[:, 2 * H * g + H:2 * H * (g + 1)]
            out.append(_gelu_tanh(lo + pltpu.roll(hi, nch - 1, 0) + pb_ref[...]).astype(BF16))
        return out

    hk = hidden(kc_ref, w1k_ref, pbk_ref)
    hv = hidden(vc_ref, w1v_ref, pbv_ref)
    for g in range(NSA_GROUPS):
        ko_ref[0, g] = jnp.dot(hk[g], w2k_ref[...], preferred_element_type=F32).astype(BF16)
        vo_ref[0, g] = _nt_dot(w2vt_ref[...], hv[g]).astype(BF16)


def _pos_bias_kernel(p_ref, w1_ref, o_ref):
    o_ref[...] = jnp.dot(p_ref[...].astype(BF16), w1_ref[...].astype(BF16), preferred_element_type=F32)


def _pos_bias(pos, w1):
    p8 = jnp.broadcast_to(pos.reshape(1, -1), (SUBLANES, pos.size))
    return pl.pallas_call(_pos_bias_kernel, out_shape=jax.ShapeDtypeStruct((SUBLANES, CMP_HIDDEN), F32),
                          name="pos_bias")(p8, w1)[0:1]


def _compress(kc, vc, w1k, w1v, w2k, w2vt, pbk, pbv):
    B, T, _ = kc.shape
    G = NSA_GROUPS
    nch = T // CMP_STRIDE
    c_blk = pl.BlockSpec((1, T, LANES), lambda b: (b, 0, 0))
    full = lambda a: pl.BlockSpec(a.shape, lambda b: (0,) * a.ndim)
    return pl.pallas_call(
        _compress_kernel,
        grid=(B,),
        in_specs=[c_blk, c_blk, full(w1k), full(w1v), full(w2k), full(w2vt), full(pbk), full(pbv)],
        out_specs=[pl.BlockSpec((1, G, nch, LANES), lambda b: (b, 0, 0, 0)),
                   pl.BlockSpec((1, G, HEAD_DIM, nch), lambda b: (b, 0, 0, 0))],
        out_shape=[jax.ShapeDtypeStruct((B, G, nch, LANES), BF16),
                   jax.ShapeDtypeStruct((B, G, HEAD_DIM, nch), BF16)],
        compiler_params=pltpu.CompilerParams(dimension_semantics=("parallel",), vmem_limit_bytes=VMEM_LIMIT),
        name="compress",
    )(kc, vc, w1k, w1v, w2k, w2vt, pbk, pbv)


def _nsa_kernel(qq_ref, kc_ref, vct_ref, ksa_ref, vst_ref, kw_ref, vwt_ref, glt_ref, ovt_ref, o_ref,
                qsel_scr, acc_scr, m_scr):
    t0 = pl.program_id(1) * TQ
    R = HEADS_PER_GROUP
    cols = R * TQ
    NCP = kc_ref.shape[2]
    NS = ovt_ref.shape[0]
    WK = WINDOW + TQ
    t_lane = t0 + (lax.broadcasted_iota(jnp.int32, (1, cols), 1) & (TQ - 1))
    low = lax.broadcasted_iota(jnp.int32, (cols, LANES), 1) < HEAD_DIM
    sub8 = lax.broadcasted_iota(jnp.int32, (SUBLANES, TQ), 0)
    c_end = lax.broadcasted_iota(jnp.int32, (NCP, cols), 0) * CMP_STRIDE + (CMP_BLOCK - 1)
    cmp_valid = c_end <= t_lane
    ks0 = pl.multiple_of(jnp.maximum(t0 - WINDOW, 0), LANES)
    wpos = ks0 + lax.broadcasted_iota(jnp.int32, (WK, cols), 0)

    def group_q(g):
        return qq_ref[0, R * g:R * (g + 1)].reshape(cols, LANES)

    def cmp_scores(g):
        return _nt_dot(kc_ref[0, g], group_q(g))

    def cmp_probs(s):
        s = jnp.where(cmp_valid, s, NEG)
        e = jnp.exp2(s - jnp.max(s, axis=0, keepdims=True))
        inv = 1.0 / jnp.maximum(jnp.sum(e, axis=0, keepdims=True), 1.0e-30)
        return e * jnp.where(t_lane >= CMP_BLOCK - 1, inv, 0.0)

    def win_scores(g):
        return _nt_dot(kw_ref[0, g, pl.ds(ks0, WK), :], group_q(g))

    def win_probs(sw):
        sw = jnp.where(wpos <= t_lane, jnp.where(wpos > t_lane - WINDOW, sw, NEG), NEG).astype(BF16)
        return jnp.exp2(sw - jnp.max(sw, axis=0, keepdims=True))

    def win_out(g, ew):
        ow = jnp.dot(vwt_ref[0, g, :, pl.ds(ks0, WK)], ew, preferred_element_type=F32)
        return ow[0:HEAD_DIM] * (1.0 / ow[HEAD_DIM:HEAD_DIM + 1])

    def select_blocks(g, p):
        psum = p[:, 0:TQ] + p[:, TQ:2 * TQ] + p[:, 2 * TQ:3 * TQ] + p[:, 3 * TQ:4 * TQ]
        p_hi = psum.astype(BF16)
        p_lo = (psum - p_hi.astype(F32)).astype(BF16)
        imp = (jnp.dot(ovt_ref[...], p_hi, preferred_element_type=F32)
               + jnp.dot(ovt_ref[...], p_lo, preferred_element_type=F32))
        blk = lax.broadcasted_iota(jnp.int32, (NS, TQ), 0)
        t_l = t0 + lax.broadcasted_iota(jnp.int32, (NS, TQ), 1)
        cur = t_l // SEL_BLOCK
        imp = jnp.where(blk * SEL_BLOCK <= t_l, imp, -1.0)
        imp = jnp.where(blk == 0, FORCED_SCORE, imp)
        imp = jnp.where(blk == cur, FORCED_SCORE, imp)
        imp = jnp.where(blk == cur - 1, FORCED_SCORE, imp)
        nv = NS // SUBLANES
        imp8 = [imp[SUBLANES * j:SUBLANES * (j + 1)] for j in range(nv)]
        rank8 = [jnp.zeros((SUBLANES, TQ), F32) for _ in range(nv)]
        for mm in range(NS):
            row = imp[mm:mm + 1, :]
            jm = mm // SUBLANES
            for j in range(nv):
                if j < jm:
                    ahead = jnp.where(row > imp8[j], 1.0, 0.0)
                elif j > jm:
                    ahead = jnp.where(row >= imp8[j], 1.0, 0.0)
                else:
                    tie = jnp.where(sub8 > (mm % SUBLANES), 1.0, 0.0)
                    ahead = jnp.where(row > imp8[j], 1.0, 0.0) + jnp.where(row == imp8[j], tie, 0.0)
                rank8[j] = rank8[j] + ahead
        rank = jnp.concatenate(rank8, axis=0)
        pen = jnp.where(rank < float(SEL_TOPK), 0.0, NEG)
        pen_t = jnp.concatenate([jnp.zeros((LANES - NS, TQ), F32), pen], axis=0).T
        qsel_scr[g] = jnp.where(low, group_q(g), jnp.concatenate([pen_t.astype(BF16)] * R, axis=0))

    s_c0 = cmp_scores(0)
    s_c1 = cmp_scores(1)
    p_c0 = cmp_probs(s_c0)
    s_w0 = win_scores(0)
    o_cmp = [jnp.dot(vct_ref[0, 0], p_c0.astype(BF16), preferred_element_type=F32)]
    p_c1 = cmp_probs(s_c1)
    s_w1 = win_scores(1)
    o_cmp.append(jnp.dot(vct_ref[0, 1], p_c1.astype(BF16), preferred_element_type=F32))
    select_blocks(0, p_c0)
    e_w0 = win_probs(s_w0)
    select_blocks(1, p_c1)
    o_win = [win_out(0, e_w0)]
    o_win.append(win_out(1, win_probs(s_w1)))

    acc_scr[...] = jnp.zeros_like(acc_scr)
    m_scr[...] = jnp.full(m_scr.shape, NEG, F32)

    jc = cols // SEL_COL_SPLIT

    def sel_jobs(tiles):
        jobs = [(kt, g, c, masked) for (kt, masked) in tiles for g in range(NSA_GROUPS)
                for c in range(SEL_COL_SPLIT)]

        def scores(job):
            kt, g, c, _ = job
            return _nt_dot(ksa_ref[0, g, pl.ds(pl.multiple_of(kt * TK, TK), TK), :],
                           qsel_scr[g, jc * c:jc * (c + 1), :])

        def stats(job, sc):
            kt, g, c, masked = job
            if masked:
                kpos = kt * TK + lax.broadcasted_iota(jnp.int32, (TK, jc), 0)
                sc = jnp.where(kpos <= t_lane[:, jc * c:jc * (c + 1)], sc, NEG)
            sc = sc.astype(BF16)
            m_old = m_scr[g, :, jc * c:jc * (c + 1)]
            m_new = jnp.maximum(m_old, jnp.max(sc, axis=0, keepdims=True).astype(F32))
            m_scr[g, :, jc * c:jc * (c + 1)] = m_new
            return sc, m_new.astype(BF16), jnp.exp2(m_old - m_new)

        def accumulate(job, pp, alpha):
            kt, g, c, _ = job
            acc_scr[g, :, jc * c:jc * (c + 1)] = acc_scr[g, :, jc * c:jc * (c + 1)] * alpha + jnp.dot(
                vst_ref[0, g, :, pl.ds(pl.multiple_of(kt * TK, TK), TK)], pp, preferred_element_type=F32)

        n = len(jobs)
        pending = {i: scores(jobs[i]) for i in range(min(SEL_LOOKAHEAD, n))}
        st = {0: stats(jobs[0], pending.pop(0))}
        for i, job in enumerate(jobs):
            if i + 1 < n:
                st[i + 1] = stats(jobs[i + 1], pending.pop(i + 1))
            sc, m_new, alpha = st.pop(i)
            pp = jnp.exp2(sc - m_new)
            if i + SEL_LOOKAHEAD < n:
                pending[i + SEL_LOOKAHEAD] = scores(jobs[i + SEL_LOOKAHEAD])
            accumulate(job, pp, alpha)

    n_full = t0 // TK

    def tile_pair(j, carry):
        sel_jobs([(2 * j, False), (2 * j + 1, False)])
        return carry

    lax.fori_loop(0, n_full // 2, tile_pair, 0)

    @pl.when(n_full % 2 == 1)
    def _():
        sel_jobs([(n_full - 1, False), (n_full, True)])

    @pl.when(n_full % 2 == 0)
    def _():
        sel_jobs([(n_full, True)])

    glt = glt_ref[0]
    heads = []
    for g in range(NSA_GROUPS):
        acc = acc_scr[g]
        o_sel = acc[0:HEAD_DIM] * (1.0 / acc[HEAD_DIM:HEAD_DIM + 1])
        for r in range(R):
            hh = R * g + r
            sl = slice(r * TQ, (r + 1) * TQ)
            heads.append(glt[3 * hh:3 * hh + 1] * o_cmp[g][:, sl]
                         + glt[3 * hh + 1:3 * hh + 2] * o_sel[:, sl]
                         + glt[3 * hh + 2:3 * hh + 3] * o_win[g][:, sl])
    o_ref[0] = jnp.concatenate(heads, axis=0).T


def _nsa(qq, kcmp, vcmpt, ksa, vst, kw, vwt, glt, ovt):
    B, H, T, _ = qq.shape
    G = NSA_GROUPS
    NCP = kcmp.shape[2]
    grid = (B, T // TQ)
    k_blk = lambda n: pl.BlockSpec((1, G, n, LANES), lambda b, i: (b, 0, 0, 0))
    vt_blk = lambda r, n: pl.BlockSpec((1, G, r, n), lambda b, i: (b, 0, 0, 0))
    return pl.pallas_call(
        _nsa_kernel,
        grid=grid,
        in_specs=[
            pl.BlockSpec((1, H, TQ, LANES), lambda b, i: (b, 0, i, 0)),
            k_blk(NCP), vt_blk(HEAD_DIM, NCP), k_blk(T), vt_blk(V_ROWS, T), k_blk(T), vt_blk(V_ROWS, T),
            pl.BlockSpec((1, GATE_ROWS, TQ), lambda b, i: (b, 0, i)),
            pl.BlockSpec(ovt.shape, lambda b, i: (0, 0)),
        ],
        out_specs=pl.BlockSpec((1, TQ, NSA_WIDTH), lambda b, i: (b, i, 0)),
        out_shape=jax.ShapeDtypeStruct((B, T, NSA_WIDTH), F32),
        scratch_shapes=[pltpu.VMEM((G, HEADS_PER_GROUP * TQ, LANES), BF16),
                        pltpu.VMEM((G, V_ROWS, HEADS_PER_GROUP * TQ), F32),
                        pltpu.VMEM((G, 1, HEADS_PER_GROUP * TQ), F32)],
        compiler_params=pltpu.CompilerParams(
            dimension_semantics=("parallel", "arbitrary"), vmem_limit_bytes=VMEM_LIMIT),
        name="nsa",
    )(qq, kcmp, vcmpt, ksa, vst, kw, vwt, glt, ovt)


def _s5_prep_kernel(lre_ref, lim_ref, ldt_ref, bre_ref, bim_ref, are_ref, aim_ref, bbre_ref, bbim_ref):
    lre, lim = lre_ref[...], lim_ref[...]
    dt = jnp.exp(ldt_ref[...])
    mag = jnp.exp(lre * dt)
    a_re = mag * jnp.cos(lim * dt)
    a_im = mag * jnp.sin(lim * dt)
    den = lre * lre + lim * lim
    z_re = ((a_re - 1.0) * lre + a_im * lim) / den
    z_im = (a_im * lre - (a_re - 1.0) * lim) / den
    are_ref[...] = a_re
    aim_ref[...] = a_im
    bbre_ref[...] = z_re * bre_ref[...] - z_im * bim_ref[...]
    bbim_ref[...] = z_re * bim_ref[...] + z_im * bre_ref[...]


def _s5_prep(lre, lim, ldt, bre, bim):
    shp = jax.ShapeDtypeStruct(lre.shape, F32)
    return pl.pallas_call(_s5_prep_kernel, out_shape=[shp, shp, shp, shp], name="s5_prep")(lre, lim, ldt, bre, bim)


def _s5_weights_kernel(bbre_ref, bbim_ref, are_ref, aim_ref, crt_ref, cit_ref,
                       bbp_ref, cpt_ref, tp_ref, alre_ref, alim_ref):
    bbre, bbim = bbre_ref[0], bbim_ref[0]
    crt, cit = crt_ref[0], cit_ref[0]
    are, aim = are_ref[0], aim_ref[0]
    pre, pim = jnp.ones_like(are), jnp.zeros_like(are)
    for k in range(S5_L):
        bpr = bbre * pre - bbim * pim
        bpi = bbre * pim + bbim * pre
        bbp_ref[0, k] = jnp.concatenate([bpr, bpi], axis=1).astype(BF16)
        tp_ref[0, k] = (_nt_dot(bpr.astype(BF16), crt.astype(BF16))
                        - _nt_dot(bpi.astype(BF16), cit.astype(BF16))).astype(BF16)
        pre, pim = pre * are - pim * aim, pre * aim + pim * are
        cpt_ref[0, k] = jnp.concatenate([crt * pre - cit * pim, -(crt * pim + cit * pre)], axis=1).astype(BF16)
    alre_ref[0] = pre
    alim_ref[0] = pim


def _s5_weights(bbre, bbim, are, aim, crt, cit):
    nh, hw, hs = bbre.shape
    blk = lambda r, c: pl.BlockSpec((1, r, c), lambda h: (h, 0, 0))
    blk4 = lambda r, c: pl.BlockSpec((1, S5_L, r, c), lambda h: (h, 0, 0, 0))
    return pl.pallas_call(
        _s5_weights_kernel,
        grid=(nh,),
        in_specs=[blk(hw, hs), blk(hw, hs), blk(1, hs), blk(1, hs), blk(hw, hs), blk(hw, hs)],
        out_specs=[blk4(hw, 2 * hs), blk4(hw, 2 * hs), blk4(hw, hw), blk(1, hs), blk(1, hs)],
        out_shape=[jax.ShapeDtypeStruct((nh, S5_L, hw, 2 * hs), BF16),
                   jax.ShapeDtypeStruct((nh, S5_L, hw, 2 * hs), BF16),
                   jax.ShapeDtypeStruct((nh, S5_L, hw, hw), BF16),
                   jax.ShapeDtypeStruct((nh, 1, hs), F32), jax.ShapeDtypeStruct((nh, 1, hs), F32)],
        compiler_params=pltpu.CompilerParams(dimension_semantics=("parallel",), vmem_limit_bytes=VMEM_LIMIT),
        name="s5_weights",
    )(bbre, bbim, are, aim, crt, cit)


def _s5_kernel(u_ref, bbp_ref, cpt_ref, tp_ref, alre_ref, alim_ref, d_ref, wg_ref, bg_ref, o_ref, e_scr, st_scr):
    nb, ct, _ = u_ref.shape
    L = S5_L
    HW = S5_WIDTH // 2
    HS = S5_GROUPS // 2 * S5_STATE
    nsl = HS // LANES
    W = S5_SCAN_SLABS * LANES

    @pl.when(pl.program_id(0) == 0)
    def _():
        st_scr[...] = jnp.zeros_like(st_scr)

    uf = u_ref[...].reshape(nb * ct, L * S5_WIDTH)
    ub = uf.astype(BF16)

    def u_part(i, h):
        c0 = i * S5_WIDTH + h * HW
        return ub[:, c0:c0 + HW]

    for h in range(2):
        e = jnp.dot(u_part(0, h), bbp_ref[h, L - 1], preferred_element_type=F32)
        for i in range(1, L):
            e = e + jnp.dot(u_part(i, h), bbp_ref[h, L - 1 - i], preferred_element_type=F32)
        for s in range(2 * nsl):
            for b in range(nb):
                e_scr[2 * nsl * h + s, pl.ds(b, ct, stride=nb), :] = e[b * ct:(b + 1) * ct, LANES * s:LANES * (s + 1)]

    for h in range(2):
        for j in range(nsl // S5_SCAN_SLABS):
            sl_re = [2 * nsl * h + S5_SCAN_SLABS * j + q for q in range(S5_SCAN_SLABS)]
            sl_im = [s + nsl for s in sl_re]
            ar = alre_ref[h, :, W * j:W * (j + 1)]
            ai = alim_ref[h, :, W * j:W * (j + 1)]
            c_re = 2 * HS * h + W * j
            c_im = c_re + HS

            def body(k, carry):
                sr, si = carry
                r0 = pl.multiple_of(k * 2 * nb, 2 * nb)
                er = jnp.concatenate([e_scr[s, pl.ds(r0, 2 * nb), :] for s in sl_re], axis=1)
                ei = jnp.concatenate([e_scr[s, pl.ds(r0, 2 * nb), :] for s in sl_im], axis=1)
                tr = ar * sr - ai * si + er[0:nb]
                ti = ar * si + ai * sr + ei[0:nb]
                xr = jnp.concatenate([sr, tr], axis=0)
                xi = jnp.concatenate([si, ti], axis=0)
                for q in range(S5_SCAN_SLABS):
                    e_scr[sl_re[q], pl.ds(r0, 2 * nb), :] = xr[:, LANES * q:LANES * (q + 1)]
                    e_scr[sl_im[q], pl.ds(r0, 2 * nb), :] = xi[:, LANES * q:LANES * (q + 1)]
                return ar * tr - ai * ti + er[nb:], ar * ti + ai * tr + ei[nb:]

            sr, si = lax.fori_loop(0, ct // 2, body, (st_scr[:, c_re:c_re + W], st_scr[:, c_im:c_im + W]))
            st_scr[:, c_re:c_re + W] = sr
            st_scr[:, c_im:c_im + W] = si

    xs = []
    for h in range(2):
        per_b = [jnp.concatenate([e_scr[2 * nsl * h + s, pl.ds(b, ct, stride=nb), :] for s in range(2 * nsl)], axis=1)
                 for b in range(nb)]
        xs.append(jnp.concatenate(per_b, axis=0).astype(BF16))
    outs = []
    for j in range(L):
        ys = []
        for h in range(2):
            y = _nt_dot(xs[h], cpt_ref[h, j])
            for i in range(j + 1):
                y = y + jnp.dot(u_part(i, h), tp_ref[h, j - i], preferred_element_type=F32)
            ys.append(y)
        yj = jnp.concatenate(ys, axis=1) + d_ref[...] * uf[:, j * S5_WIDTH:(j + 1) * S5_WIDTH]
        z = _gelu_tanh(yj)
        gate = jnp.dot(z.astype(BF16), wg_ref[...], preferred_element_type=F32) + bg_ref[...]
        outs.append(z * _sigmoid(gate))
    o_ref[...] = jnp.concatenate(outs, axis=1).reshape(nb, ct, L * S5_WIDTH)


def _s5(u4, bbp, cpt, tp, alre, alim, d, w_glu, b_glu):
    nb, nchunks, w4 = u4.shape
    ct = S5_CT
    n_slabs = 2 * bbp.shape[3] // LANES
    resident = lambda a: pl.BlockSpec(a.shape, lambda i: (0,) * a.ndim, pipeline_mode=pl.Buffered(1))
    u_blk = pl.BlockSpec((nb, ct, w4), lambda i: (0, i, 0))
    return pl.pallas_call(
        _s5_kernel,
        grid=(nchunks // ct,),
        in_specs=[u_blk, resident(bbp), resident(cpt), resident(tp), resident(alre), resident(alim),
                  resident(d), resident(w_glu), resident(b_glu)],
        out_specs=u_blk,
        out_shape=jax.ShapeDtypeStruct(u4.shape, F32),
        scratch_shapes=[pltpu.VMEM((n_slabs, nb * ct, LANES), F32), pltpu.VMEM((nb, n_slabs * LANES), F32)],
        compiler_params=pltpu.CompilerParams(dimension_semantics=("arbitrary",), vmem_limit_bytes=VMEM_LIMIT),
        name="s5",
    )(u4, bbp, cpt, tp, alre, alim, d, w_glu, b_glu)


def _final_kernel(x_ref, g_ref, wb_ref, on_ref, os_ref, wpn_ref, wps_ref, wo_ref, fg_ref, o_ref, os_scr):
    xv = x_ref[0]
    h = (_rms_scale(xv) * g_ref[...]).astype(BF16)
    tm = xv.shape[0]
    o4 = os_ref[0]
    for i in range(S5_L):
        for s in range(S5_WIDTH // LANES):
            c0 = i * S5_WIDTH + LANES * s
            os_scr[s, pl.ds(i, tm // S5_L, stride=S5_L), :] = o4[:, c0:c0 + LANES]
    o_s5 = jnp.concatenate([os_scr[s] for s in range(S5_WIDTH // LANES)], axis=1)

    def proj(a, b):
        return jnp.dot(h, wb_ref[:, a:b], preferred_element_type=F32)

    def silu(v):
        return v * _sigmoid(v)

    a_in = (on_ref[0] * silu(proj(0, NSA_WIDTH))).astype(BF16)
    b_in = (o_s5 * silu(proj(NSA_WIDTH, NSA_WIDTH + S5_WIDTH))).astype(BF16)
    branch_a = jnp.dot(a_in, wpn_ref[...], preferred_element_type=F32)
    branch_b = jnp.dot(b_in, wps_ref[...], preferred_element_type=F32)
    o1 = NSA_WIDTH + S5_WIDTH
    merged = (_sigmoid(proj(o1, o1 + D_MODEL)) * branch_a
              + _sigmoid(proj(o1 + D_MODEL, o1 + 2 * D_MODEL)) * branch_b)
    y = xv + jnp.dot(merged.astype(BF16), wo_ref[...], preferred_element_type=F32)
    o_ref[0] = _rms_scale(y) * fg_ref[...]


def _final(x, norm_g, w_b, o_nsa, o_s5, wpn, wps, wo, final_g):
    B, T, D = x.shape
    tm = TM_PROJ
    row_blk = lambda w: pl.BlockSpec((1, tm, w), lambda b, i: (b, i, 0))
    full = lambda a: pl.BlockSpec(a.shape, lambda b, i: (0,) * a.ndim)
    return pl.pallas_call(
        _final_kernel,
        grid=(B, T // tm),
        in_specs=[row_blk(D), full(norm_g), full(w_b), row_blk(NSA_WIDTH),
                  pl.BlockSpec((1, tm // S5_L, S5_L * S5_WIDTH), lambda b, i: (b, i, 0)),
                  full(wpn), full(wps), full(wo), full(final_g)],
        out_specs=row_blk(D),
        out_shape=jax.ShapeDtypeStruct((B, T, D), F32),
        scratch_shapes=[pltpu.VMEM((S5_WIDTH // LANES, tm, LANES), F32)],
        compiler_params=pltpu.CompilerParams(
            dimension_semantics=("parallel", "arbitrary"), vmem_limit_bytes=VMEM_LIMIT),
        name="final",
    )(x, norm_g, w_b, o_nsa, o_s5, wpn, wps, wo, final_g)


def _rope_tables(T):
    half = HEAD_DIM // 2
    inv_freq = ROPE_THETA ** (-jnp.arange(half, dtype=F32) / half)
    ang = jnp.arange(T).astype(F32)[:, None] * inv_freq[None, :]
    cos, sin = jnp.cos(ang), jnp.sin(ang)
    cos2 = jnp.concatenate([cos, cos, cos, cos], axis=1)
    sin2 = jnp.concatenate([-sin, sin, -sin, sin], axis=1)
    return cos2, sin2


def _block_diag_halves(m):
    g, a, b = m.shape
    gh = g // 2
    eye = jnp.eye(gh, dtype=m.dtype)
    out = m.reshape(2, gh, a, 1, b) * eye[None, :, None, :, None]
    return out.reshape(2, gh * a, gh * b)


def _compress_w1(w1):
    w1r = w1.reshape(2, CMP_STRIDE, HEAD_DIM, CMP_HIDDEN)
    eye = jnp.eye(NSA_GROUPS, dtype=w1.dtype)
    out = jnp.einsum('hjdn,gk->jgdkhn', w1r, eye)
    return out.reshape(CMP_STRIDE, NSA_GROUPS * HEAD_DIM, NSA_GROUPS * 2 * CMP_HIDDEN).astype(BF16)


def kernel(x, norm_g, w_in, cmp_pos_k, cmp_pos_v, cmp_w1_k, cmp_w2_k, cmp_w1_v, cmp_w2_v, s5_lam_re, s5_lam_im, s5_log_dt, s5_b_re, s5_b_im, s5_c_re, s5_c_im, s5_d, w_glu, b_glu, w_proj_nsa, w_proj_s5, w_out, final_g):
    B, T, D = x.shape
    assert w_in.shape[0] == 1, "single-layer block"
    NCH = T // CMP_STRIDE
    NS = T // SEL_BLOCK

    w = w_in[0]
    w_a = jnp.concatenate([w[:, :_OFF_GL], jnp.pad(w[:, _OFF_GL:_OFF_GN], ((0, 0), (0, LANES - 24))),
                           w[:, _OFF_U:_OFF_GS]], axis=1).astype(BF16)
    w_b = jnp.concatenate([w[:, _OFF_GN:_OFF_U], w[:, _OFF_GS:]], axis=1).astype(BF16)
    g2 = norm_g[0][None, :]
    cos2, sin2 = _rope_tables(T)

    qq, kc, vc, ksa, vst, kw, vwt, glt, u4 = _inproj(x, g2, w_a, cos2, sin2)

    w2k = jnp.concatenate([jnp.zeros_like(cmp_w2_k[0]), cmp_w2_k[0]], axis=1).astype(BF16)
    w2vt = cmp_w2_v[0].T.astype(BF16)
    kcmp, vcmpt = _compress(kc, vc, _compress_w1(cmp_w1_k[0]), _compress_w1(cmp_w1_v[0]), w2k, w2vt,
                            _pos_bias(cmp_pos_k[0], cmp_w1_k[0]), _pos_bias(cmp_pos_v[0], cmp_w1_v[0]))

    c_start = jnp.arange(NCH) * CMP_STRIDE
    s_start = jnp.arange(NS) * SEL_BLOCK
    ovt = ((c_start[None, :] < s_start[:, None] + SEL_BLOCK) & (c_start[None, :] + CMP_BLOCK > s_start[:, None])
           & (jnp.arange(NCH)[None, :] < NCH - 1)).astype(BF16)
    o_nsa = _nsa(qq, kcmp, vcmpt, ksa, vst, kw, vwt, glt, ovt)

    rep = lambda a: jnp.repeat(a, S5_GROUP, axis=0)
    tr = lambda b: b.transpose(0, 2, 1).reshape(S5_GROUPS * S5_GROUP, S5_STATE)
    a_re, a_im, bb_re, bb_im = _s5_prep(
        rep(s5_lam_re[0]), rep(s5_lam_im[0]),
        rep(jnp.broadcast_to(s5_log_dt[0][:, None], (S5_GROUPS, S5_STATE))),
        tr(s5_b_re[0]), tr(s5_b_im[0]))
    halves = lambda a: a[::S5_GROUP].reshape(2, 1, -1)
    grp = lambda a: a.reshape(S5_GROUPS, S5_GROUP, S5_STATE)
    bbp, cpt, tp, alre, alim = _s5_weights(
        _block_diag_halves(grp(bb_re)), _block_diag_halves(grp(bb_im)), halves(a_re), halves(a_im),
        _block_diag_halves(s5_c_re[0]), _block_diag_halves(s5_c_im[0]))
    o_s5 = _s5(u4, bbp, cpt, tp, alre, alim, s5_d[0][None, :], w_glu[0].astype(BF16), b_glu[0][None, :])

    return _final(x, g2, w_b, o_nsa, o_s5, w_proj_nsa[0].astype(BF16), w_proj_s5[0].astype(BF16),
                  w_out[0].astype(BF16), final_g[None, :])
```

```python
import math

import jax
import jax.numpy as jnp
from jax import lax
from jax.experimental import pallas as pl
from jax.experimental.pallas import tpu as pltpu

F32 = jnp.float32
BF16 = jnp.bfloat16

D_MODEL = 1024
NSA_HEADS = 8
NSA_GROUPS = 2
HEADS_PER_GROUP = 4
HEAD_DIM = 64
NSA_WIDTH = 512
CMP_BLOCK = 32
CMP_STRIDE = 16
CMP_HIDDEN = 256
SEL_BLOCK = 64
SEL_TOPK = 16
WINDOW = 512
ROPE_THETA = 10000.0
FORCED_SCORE = 1.0e4
NEG = -1.0e30
S5_WIDTH = 512
S5_GROUP = 16
S5_GROUPS = 32
S5_STATE = 64
RMS_EPS = 1.0e-6

LANES = 128
SUBLANES = 8
VMEM_LIMIT = 56 * 1024 * 1024

_OFF_GL = 1280
_OFF_GN = 1304
_OFF_U = 1816
_OFF_GS = 2328

TM_PROJ = 512
TQ = 128
TK = 512
NSA_NB = 2
SEL_LOOKAHEAD = 2
V_ROWS = 80
GATE_ROWS = 32
S5_L = 4
S5_CT = 128
S5_SCAN_SLABS = 4


def _gelu_tanh(x):
    c = math.sqrt(2.0 / math.pi)
    return 0.5 * x * (1.0 + jnp.tanh(c * (x + 0.044715 * (x * x * x))))


def _sigmoid(x):
    return 1.0 / (1.0 + jnp.exp(-x))


def _rms_scale(xv):
    ms = jnp.mean(xv * xv, axis=-1, keepdims=True)
    return xv * lax.rsqrt(ms + RMS_EPS)


def _nt_dot(a, b):
    return lax.dot_general(a, b, (((1,), (1,)), ((), ())), preferred_element_type=F32)


def _inproj_kernel(x_ref, g_ref, w_ref, cos_ref, sin_ref,
                   qq_ref, kc_ref, vc_ref, ks_ref, vs_ref, kw_ref, vw_ref, gl_ref, u_ref, us_scr):
    h = (_rms_scale(x_ref[0]) * g_ref[...]).astype(BF16)
    cos2 = cos_ref[...]
    sin2 = sin_ref[...]
    lane = lax.broadcasted_iota(jnp.int32, cos2.shape, 1)
    first_half = (lane & (HEAD_DIM - 1)) < (HEAD_DIM // 2)
    low = lane < HEAD_DIM

    wide = {}

    def proj(a, b):
        for (s0, s1) in ((0, 512), (512, 1280), (1280, 1920)):
            if s0 <= a and b <= s1:
                if s0 not in wide:
                    wide[s0] = jnp.dot(h, w_ref[:, s0:s1], preferred_element_type=F32)
                return wide[s0][:, a - s0:b - s0]
        raise ValueError((a, b))

    def rope(xs):
        partner = jnp.where(first_half, pltpu.roll(xs, 96, 1), pltpu.roll(xs, 32, 1))
        return xs * cos2 + partner * sin2

    scale = HEAD_DIM ** -0.5 * math.log2(math.e)
    for i in range(NSA_HEADS // 2):
        xs = proj(LANES * i, LANES * (i + 1)) * scale
        xr = rope(xs)
        qq_ref[0, 2 * i] = jnp.where(low, xr, pltpu.roll(xs, 64, 1)).astype(BF16)
        qq_ref[0, 2 * i + 1] = jnp.where(low, pltpu.roll(xr, 64, 1), xs).astype(BF16)

    kc_ref[0] = proj(512, 640)
    vc_ref[0] = proj(640, 768)
    tm = cos2.shape[0]
    t_row = pl.program_id(1) * tm + lax.broadcasted_iota(jnp.int32, cos2.shape, 0)
    blk_onehot = jnp.where(lane - HEAD_DIM == t_row // SEL_BLOCK, 1.0, 0.0)
    ones_rows = jnp.where(lax.broadcasted_iota(jnp.int32, (V_ROWS - HEAD_DIM, tm), 0) == 0, 1.0, 0.0)
    for (off, k_out, v_out, k_pad) in ((768, ks_ref, vs_ref, blk_onehot), (1024, kw_ref, vw_ref, 0.0)):
        kr = rope(proj(off, off + LANES))
        k_out[0, 0] = jnp.where(low, kr, k_pad).astype(BF16)
        k_out[0, 1] = jnp.where(low, pltpu.roll(kr, 64, 1), k_pad).astype(BF16)
        vt = proj(off + LANES, off + 2 * LANES).T
        for g in range(NSA_GROUPS):
            v_out[0, g] = jnp.concatenate([vt[HEAD_DIM * g:HEAD_DIM * (g + 1)], ones_rows], axis=0).astype(BF16)
    gl_ref[0] = _sigmoid(proj(1280, 1408)).T[0:GATE_ROWS]
    uv = proj(1408, 1920)
    for s in range(S5_WIDTH // LANES):
        us_scr[s] = uv[:, LANES * s:LANES * (s + 1)]
    for i in range(S5_L):
        for s in range(S5_WIDTH // LANES):
            c0 = i * S5_WIDTH + LANES * s
            u_ref[0, :, c0:c0 + LANES] = us_scr[s, pl.ds(i, tm // S5_L, stride=S5_L), :]


def _inproj(x, norm_g, w_a, cos2, sin2):
    B, T, D = x.shape
    tm = TM_PROJ
    grid = (B, T // tm)
    row_blk = lambda w: pl.BlockSpec((1, tm, w), lambda b, i: (b, i, 0))
    kv_blk = pl.BlockSpec((1, NSA_GROUPS, tm, LANES), lambda b, i: (b, 0, i, 0))
    kv_shape = jax.ShapeDtypeStruct((B, NSA_GROUPS, T, LANES), BF16)
    vt_blk = pl.BlockSpec((1, NSA_GROUPS, V_ROWS, tm), lambda b, i: (b, 0, 0, i))
    vt_shape = jax.ShapeDtypeStruct((B, NSA_GROUPS, V_ROWS, T), BF16)
    return pl.pallas_call(
        _inproj_kernel,
        grid=grid,
        in_specs=[
            row_blk(D),
            pl.BlockSpec((1, D), lambda b, i: (0, 0)),
            pl.BlockSpec(w_a.shape, lambda b, i: (0, 0)),
            pl.BlockSpec((tm, LANES), lambda b, i: (i, 0)),
            pl.BlockSpec((tm, LANES), lambda b, i: (i, 0)),
        ],
        out_specs=[
            pl.BlockSpec((1, NSA_HEADS, tm, LANES), lambda b, i: (b, 0, i, 0)),
            row_blk(LANES), row_blk(LANES),
            kv_blk, vt_blk, kv_blk, vt_blk,
            pl.BlockSpec((1, GATE_ROWS, tm), lambda b, i: (b, 0, i)),
            pl.BlockSpec((1, tm // S5_L, S5_L * S5_WIDTH), lambda b, i: (b, i, 0)),
        ],
        out_shape=[
            jax.ShapeDtypeStruct((B, NSA_HEADS, T, LANES), BF16),
            jax.ShapeDtypeStruct((B, T, LANES), F32), jax.ShapeDtypeStruct((B, T, LANES), F32),
            kv_shape, vt_shape, kv_shape, vt_shape,
            jax.ShapeDtypeStruct((B, GATE_ROWS, T), F32),
            jax.ShapeDtypeStruct((B, T // S5_L, S5_L * S5_WIDTH), F32),
        ],
        scratch_shapes=[pltpu.VMEM((S5_WIDTH // LANES, tm, LANES), F32)],
        compiler_params=pltpu.CompilerParams(
            dimension_semantics=("parallel", "arbitrary"), vmem_limit_bytes=VMEM_LIMIT),
        name="inproj",
    )(x, norm_g, w_a, cos2, sin2)


def _compress_kernel(kc_ref, vc_ref, w1k_ref, w1v_ref, w2k_ref, w2vt_ref, pbk_ref, pbv_ref, ko_ref, vo_ref):
    nch = ko_ref.shape[2]
    H = CMP_HIDDEN

    def hidden(c_ref, w1_ref, pb_ref):
        a = jnp.zeros((nch, 4 * H), F32)
        for j in range(CMP_STRIDE):
            rows = c_ref[0, pl.ds(j, nch, stride=CMP_STRIDE), :].astype(BF16)
            a = a + jnp.dot(rows, w1_ref[j], preferred_element_type=F32)
        out = []
        for g in range(NSA_GROUPS):
            lo = a[:, 2 * H * g:2 * H * g + H]
            hi = a[:, 2 * H * g + H:2 * H * (g + 1)]
            out.append(_gelu_tanh(lo + pltpu.roll(hi, nch - 1, 0) + pb_ref[...]).astype(BF16))
        return out

    hk = hidden(kc_ref, w1k_ref, pbk_ref)
    hv = hidden(vc_ref, w1v_ref, pbv_ref)
    for g in range(NSA_GROUPS):
        ko_ref[0, g] = jnp.dot(hk[g], w2k_ref[...], preferred_element_type=F32).astype(BF16)
        vo_ref[0, g] = _nt_dot(w2vt_ref[...], hv[g]).astype(BF16)


def _pos_bias_kernel(p_ref, w1_ref, o_ref):
    o_ref[...] = jnp.dot(p_ref[...].astype(BF16), w1_ref[...].astype(BF16), preferred_element_type=F32)


def _pos_bias(pos, w1):
    p8 = jnp.broadcast_to(pos.reshape(1, -1), (SUBLANES, pos.size))
    return pl.pallas_call(_pos_bias_kernel, out_shape=jax.ShapeDtypeStruct((SUBLANES, CMP_HIDDEN), F32),
                          name="pos_bias")(p8, w1)[0:1]


def _compress(kc, vc, w1k, w1v, w2k, w2vt, pbk, pbv):
    B, T, _ = kc.shape
    G = NSA_GROUPS
    nch = T // CMP_STRIDE
    c_blk = pl.BlockSpec((1, T, LANES), lambda b: (b, 0, 0))
    full = lambda a: pl.BlockSpec(a.shape, lambda b: (0,) * a.ndim)
    return pl.pallas_call(
        _compress_kernel,
        grid=(B,),
        in_specs=[c_blk, c_blk, full(w1k), full(w1v), full(w2k), full(w2vt), full(pbk), full(pbv)],
        out_specs=[pl.BlockSpec((1, G, nch, LANES), lambda b: (b, 0, 0, 0)),
                   pl.BlockSpec((1, G, HEAD_DIM, nch), lambda b: (b, 0, 0, 0))],
        out_shape=[jax.ShapeDtypeStruct((B, G, nch, LANES), BF16),
                   jax.ShapeDtypeStruct((B, G, HEAD_DIM, nch), BF16)],
        compiler_params=pltpu.CompilerParams(dimension_semantics=("parallel",), vmem_limit_bytes=VMEM_LIMIT),
        name="compress",
    )(kc, vc, w1k, w1v, w2k, w2vt, pbk, pbv)


def _nsa_kernel(qq_ref, kc_ref, vct_ref, ksa_ref, vst_ref, kw_ref, vwt_ref, glt_ref, ovt_ref, o_ref,
                qsel_scr, acc_scr, m_scr):
    units = [(bb, g) for bb in range(qq_ref.shape[0]) for g in range(NSA_GROUPS)]
    uidx = {u: i for i, u in enumerate(units)}
    t0 = pl.program_id(1) * TQ
    R = HEADS_PER_GROUP
    cols = R * TQ
    NCP = kc_ref.shape[2]
    NS = ovt_ref.shape[0]
    WK = WINDOW + TQ
    t_lane = t0 + (lax.broadcasted_iota(jnp.int32, (1, cols), 1) & (TQ - 1))
    low = lax.broadcasted_iota(jnp.int32, (cols, LANES), 1) < HEAD_DIM
    sub8 = lax.broadcasted_iota(jnp.int32, (SUBLANES, TQ), 0)
    c_end = lax.broadcasted_iota(jnp.int32, (NCP, cols), 0) * CMP_STRIDE + (CMP_BLOCK - 1)
    cmp_valid = c_end <= t_lane
    ks0 = pl.multiple_of(jnp.maximum(t0 - WINDOW, 0), LANES)
    wpos = ks0 + lax.broadcasted_iota(jnp.int32, (WK, cols), 0)

    def group_q(u):
        bb, g = u
        return qq_ref[bb, R * g:R * (g + 1)].reshape(cols, LANES)

    def cmp_scores(u):
        return _nt_dot(kc_ref[u[0], u[1]], group_q(u))

    def cmp_probs(s):
        s = jnp.where(cmp_valid, s, NEG)
        e = jnp.exp2(s - jnp.max(s, axis=0, keepdims=True))
        inv = 1.0 / jnp.maximum(jnp.sum(e, axis=0, keepdims=True), 1.0e-30)
        return e * jnp.where(t_lane >= CMP_BLOCK - 1, inv, 0.0)

    def win_scores(u):
        return _nt_dot(kw_ref[u[0], u[1], pl.ds(ks0, WK), :], group_q(u))

    def win_probs(sw):
        sw = jnp.where(wpos <= t_lane, jnp.where(wpos > t_lane - WINDOW, sw, NEG), NEG).astype(BF16)
        return jnp.exp2(sw - jnp.max(sw, axis=0, keepdims=True))

    def win_out(u, ew):
        ow = jnp.dot(vwt_ref[u[0], u[1], :, pl.ds(ks0, WK)], ew, preferred_element_type=F32)
        return ow[0:HEAD_DIM] * (1.0 / ow[HEAD_DIM:HEAD_DIM + 1])

    def select_blocks(u, p):
        psum = p[:, 0:TQ] + p[:, TQ:2 * TQ] + p[:, 2 * TQ:3 * TQ] + p[:, 3 * TQ:4 * TQ]
        p_hi = psum.astype(BF16)
        p_lo = (psum - p_hi.astype(F32)).astype(BF16)
        imp = (jnp.dot(ovt_ref[...], p_hi, preferred_element_type=F32)
               + jnp.dot(ovt_ref[...], p_lo, preferred_element_type=F32))
        blk = lax.broadcasted_iota(jnp.int32, (NS, TQ), 0)
        t_l = t0 + lax.broadcasted_iota(jnp.int32, (NS, TQ), 1)
        cur = t_l // SEL_BLOCK
        imp = jnp.where(blk * SEL_BLOCK <= t_l, imp, -1.0)
        imp = jnp.where(blk == 0, FORCED_SCORE, imp)
        imp = jnp.where(blk == cur, FORCED_SCORE, imp)
        imp = jnp.where(blk == cur - 1, FORCED_SCORE, imp)
        nv = NS // SUBLANES
        imp8 = [imp[SUBLANES * j:SUBLANES * (j + 1)] for j in range(nv)]
        rank8 = [jnp.zeros((SUBLANES, TQ), F32) for _ in range(nv)]
        for mm in range(NS):
            row = imp[mm:mm + 1, :]
            jm = mm // SUBLANES
            for j in range(nv):
                if j < jm:
                    ahead = jnp.where(row > imp8[j], 1.0, 0.0)
                elif j > jm:
                    ahead = jnp.where(row >= imp8[j], 1.0, 0.0)
                else:
                    tie = jnp.where(sub8 > (mm % SUBLANES), 1.0, 0.0)
                    ahead = jnp.where(row > imp8[j], 1.0, 0.0) + jnp.where(row == imp8[j], tie, 0.0)
                rank8[j] = rank8[j] + ahead
        rank = jnp.concatenate(rank8, axis=0)
        pen = jnp.where(rank < float(SEL_TOPK), 0.0, NEG)
        pen_t = jnp.concatenate([jnp.zeros((LANES - NS, TQ), F32), pen], axis=0).T
        qsel_scr[uidx[u]] = jnp.where(low, group_q(u), jnp.concatenate([pen_t.astype(BF16)] * R, axis=0))

    nu = len(units)
    s_c = {0: cmp_scores(units[0]), 1: cmp_scores(units[1])}
    p_c, o_cmp, s_w = [], [], []
    for i, u in enumerate(units):
        p_c.append(cmp_probs(s_c.pop(i)))
        if i + 2 < nu:
            s_c[i + 2] = cmp_scores(units[i + 2])
        else:
            s_w.append(win_scores(units[i + 2 - nu]))
        o_cmp.append(jnp.dot(vct_ref[u[0], u[1]], p_c[i].astype(BF16), preferred_element_type=F32))
    o_win = []
    for i, u in enumerate(units):
        select_blocks(u, p_c[i])
        e_w = win_probs(s_w[i])
        if i + 2 < nu:
            s_w.append(win_scores(units[i + 2]))
        o_win.append(win_out(u, e_w))

    acc_scr[...] = jnp.zeros_like(acc_scr)
    m_scr[...] = jnp.full(m_scr.shape, NEG, F32)

    def sel_jobs(tiles):
        jobs = [(kt, u, masked) for (kt, masked) in tiles for u in units]

        def scores(job):
            kt, u, _ = job
            return _nt_dot(ksa_ref[u[0], u[1], pl.ds(pl.multiple_of(kt * TK, TK), TK), :], qsel_scr[uidx[u]])

        def stats(job, sc):
            kt, u, masked = job
            if masked:
                kpos = kt * TK + lax.broadcasted_iota(jnp.int32, (TK, cols), 0)
                sc = jnp.where(kpos <= t_lane, sc, NEG)
            sc = sc.astype(BF16)
            m_old = m_scr[uidx[u]]
            m_new = jnp.maximum(m_old, jnp.max(sc, axis=0, keepdims=True).astype(F32))
            m_scr[uidx[u]] = m_new
            return sc, m_new.astype(BF16), jnp.exp2(m_old - m_new)

        def accumulate(job, pp, alpha):
            kt, u, _ = job
            acc_scr[uidx[u]] = acc_scr[uidx[u]] * alpha + jnp.dot(
                vst_ref[u[0], u[1], :, pl.ds(pl.multiple_of(kt * TK, TK), TK)], pp, preferred_element_type=F32)

        n = len(jobs)
        pending = {i: scores(jobs[i]) for i in range(min(SEL_LOOKAHEAD, n))}
        st = {0: stats(jobs[0], pending.pop(0))}
        for i, job in enumerate(jobs):
            if i + 1 < n:
                st[i + 1] = stats(jobs[i + 1], pending.pop(i + 1))
            sc, m_new, alpha = st.pop(i)
            pp = jnp.exp2(sc - m_new)
            if i + SEL_LOOKAHEAD < n:
                pending[i + SEL_LOOKAHEAD] = scores(jobs[i + SEL_LOOKAHEAD])
            accumulate(job, pp, alpha)

    n_full = t0 // TK

    sel_jobs([(n_full, True)])

    def tile_pair(j, carry):
        sel_jobs([(2 * j, False), (2 * j + 1, False)])
        return carry

    lax.fori_loop(0, n_full // 2, tile_pair, 0)

    @pl.when(n_full % 2 == 1)
    def _():
        sel_jobs([(n_full - 1, False)])

    for bb in range(qq_ref.shape[0]):
        glt = glt_ref[bb]
        heads = []
        for g in range(NSA_GROUPS):
            ui = uidx[(bb, g)]
            acc = acc_scr[ui]
            o_sel = acc[0:HEAD_DIM] * (1.0 / acc[HEAD_DIM:HEAD_DIM + 1])
            for r in range(R):
                hh = R * g + r
                sl = slice(r * TQ, (r + 1) * TQ)
                heads.append(glt[3 * hh:3 * hh + 1] * o_cmp[ui][:, sl]
                             + glt[3 * hh + 1:3 * hh + 2] * o_sel[:, sl]
                             + glt[3 * hh + 2:3 * hh + 3] * o_win[ui][:, sl])
        o_ref[bb] = jnp.concatenate(heads, axis=0).T


def _nsa(qq, kcmp, vcmpt, ksa, vst, kw, vwt, glt, ovt):
    B, H, T, _ = qq.shape
    G = NSA_GROUPS
    NB = NSA_NB
    NCP = kcmp.shape[2]
    grid = (B // NB, T // TQ)
    k_blk = lambda n: pl.BlockSpec((NB, G, n, LANES), lambda b, i: (b, 0, 0, 0))
    vt_blk = lambda r, n: pl.BlockSpec((NB, G, r, n), lambda b, i: (b, 0, 0, 0))
    return pl.pallas_call(
        _nsa_kernel,
        grid=grid,
        in_specs=[
            pl.BlockSpec((NB, H, TQ, LANES), lambda b, i: (b, 0, i, 0)),
            k_blk(NCP), vt_blk(HEAD_DIM, NCP), k_blk(T), vt_blk(V_ROWS, T), k_blk(T), vt_blk(V_ROWS, T),
            pl.BlockSpec((NB, GATE_ROWS, TQ), lambda b, i: (b, 0, i)),
            pl.BlockSpec(ovt.shape, lambda b, i: (0, 0)),
        ],
        out_specs=pl.BlockSpec((NB, TQ, NSA_WIDTH), lambda b, i: (b, i, 0)),
        out_shape=jax.ShapeDtypeStruct((B, T, NSA_WIDTH), F32),
        scratch_shapes=[pltpu.VMEM((NB * G, HEADS_PER_GROUP * TQ, LANES), BF16),
                        pltpu.VMEM((NB * G, V_ROWS, HEADS_PER_GROUP * TQ), F32),
                        pltpu.VMEM((NB * G, 1, HEADS_PER_GROUP * TQ), F32)],
        compiler_params=pltpu.CompilerParams(
            dimension_semantics=("parallel", "arbitrary"), vmem_limit_bytes=VMEM_LIMIT),
        name="nsa",
    )(qq, kcmp, vcmpt, ksa, vst, kw, vwt, glt, ovt)


def _s5_prep_kernel(lre_ref, lim_ref, ldt_ref, bre_ref, bim_ref, are_ref, aim_ref, bbre_ref, bbim_ref):
    lre, lim = lre_ref[...], lim_ref[...]
    dt = jnp.exp(ldt_ref[...])
    mag = jnp.exp(lre * dt)
    a_re = mag * jnp.cos(lim * dt)
    a_im = mag * jnp.sin(lim * dt)
    den = lre * lre + lim * lim
    z_re = ((a_re - 1.0) * lre + a_im * lim) / den
    z_im = (a_im * lre - (a_re - 1.0) * lim) / den
    are_ref[...] = a_re
    aim_ref[...] = a_im
    bbre_ref[...] = z_re * bre_ref[...] - z_im * bim_ref[...]
    bbim_ref[...] = z_re * bim_ref[...] + z_im * bre_ref[...]


def _s5_prep(lre, lim, ldt, bre, bim):
    shp = jax.ShapeDtypeStruct(lre.shape, F32)
    return pl.pallas_call(_s5_prep_kernel, out_shape=[shp, shp, shp, shp], name="s5_prep")(lre, lim, ldt, bre, bim)


def _s5_weights_kernel(bbre_ref, bbim_ref, are_ref, aim_ref, crt_ref, cit_ref,
                       bbp_ref, cpt_ref, tp_ref, alre_ref, alim_ref):
    bbre, bbim = bbre_ref[0], bbim_ref[0]
    crt, cit = crt_ref[0], cit_ref[0]
    are, aim = are_ref[0], aim_ref[0]
    pre, pim = jnp.ones_like(are), jnp.zeros_like(are)
    for k in range(S5_L):
        bpr = bbre * pre - bbim * pim
        bpi = bbre * pim + bbim * pre
        bbp_ref[0, k] = jnp.concatenate([bpr, bpi], axis=1).astype(BF16)
        tp_ref[0, k] = (_nt_dot(bpr.astype(BF16), crt.astype(BF16))
                        - _nt_dot(bpi.astype(BF16), cit.astype(BF16))).astype(BF16)
        pre, pim = pre * are - pim * aim, pre * aim + pim * are
        cpt_ref[0, k] = jnp.concatenate([crt * pre - cit * pim, -(crt * pim + cit * pre)], axis=1).astype(BF16)
    alre_ref[0] = pre
    alim_ref[0] = pim


def _s5_weights(bbre, bbim, are, aim, crt, cit):
    nh, hw, hs = bbre.shape
    blk = lambda r, c: pl.BlockSpec((1, r, c), lambda h: (h, 0, 0))
    blk4 = lambda r, c: pl.BlockSpec((1, S5_L, r, c), lambda h: (h, 0, 0, 0))
    return pl.pallas_call(
        _s5_weights_kernel,
        grid=(nh,),
        in_specs=[blk(hw, hs), blk(hw, hs), blk(1, hs), blk(1, hs), blk(hw, hs), blk(hw, hs)],
        out_specs=[blk4(hw, 2 * hs), blk4(hw, 2 * hs), blk4(hw, hw), blk(1, hs), blk(1, hs)],
        out_shape=[jax.ShapeDtypeStruct((nh, S5_L, hw, 2 * hs), BF16),
                   jax.ShapeDtypeStruct((nh, S5_L, hw, 2 * hs), BF16),
                   jax.ShapeDtypeStruct((nh, S5_L, hw, hw), BF16),
                   jax.ShapeDtypeStruct((nh, 1, hs), F32), jax.ShapeDtypeStruct((nh, 1, hs), F32)],
        compiler_params=pltpu.CompilerParams(dimension_semantics=("parallel",), vmem_limit_bytes=VMEM_LIMIT),
        name="s5_weights",
    )(bbre, bbim, are, aim, crt, cit)


def _s5_kernel(u_ref, bbp_ref, cpt_ref, tp_ref, alre_ref, alim_ref, d_ref, wg_ref, bg_ref, o_ref, e_scr, st_scr):
    nb, ct, _ = u_ref.shape
    L = S5_L
    HW = S5_WIDTH // 2
    HS = S5_GROUPS // 2 * S5_STATE
    nsl = HS // LANES
    W = S5_SCAN_SLABS * LANES

    @pl.when(pl.program_id(0) == 0)
    def _():
        st_scr[...] = jnp.zeros_like(st_scr)

    uf = u_ref[...].reshape(nb * ct, L * S5_WIDTH)
    ub = uf.astype(BF16)

    def u_part(i, h):
        c0 = i * S5_WIDTH + h * HW
        return ub[:, c0:c0 + HW]

    for h in range(2):
        e = jnp.dot(u_part(0, h), bbp_ref[h, L - 1], preferred_element_type=F32)
        for i in range(1, L):
            e = e + jnp.dot(u_part(i, h), bbp_ref[h, L - 1 - i], preferred_element_type=F32)
        for s in range(2 * nsl):
            for b in range(nb):
                e_scr[2 * nsl * h + s, pl.ds(b, ct, stride=nb), :] = e[b * ct:(b + 1) * ct, LANES * s:LANES * (s + 1)]

    for h in range(2):
        for j in range(nsl // S5_SCAN_SLABS):
            sl_re = [2 * nsl * h + S5_SCAN_SLABS * j + q for q in range(S5_SCAN_SLABS)]
            sl_im = [s + nsl for s in sl_re]
            ar = alre_ref[h, :, W * j:W * (j + 1)]
            ai = alim_ref[h, :, W * j:W * (j + 1)]
            c_re = 2 * HS * h + W * j
            c_im = c_re + HS

            def body(k, carry):
                sr, si = carry
                r0 = pl.multiple_of(k * 2 * nb, 2 * nb)
                er = jnp.concatenate([e_scr[s, pl.ds(r0, 2 * nb), :] for s in sl_re], axis=1)
                ei = jnp.concatenate([e_scr[s, pl.ds(r0, 2 * nb), :] for s in sl_im], axis=1)
                tr = ar * sr - ai * si + er[0:nb]
                ti = ar * si + ai * sr + ei[0:nb]
                xr = jnp.concatenate([sr, tr], axis=0)
                xi = jnp.concatenate([si, ti], axis=0)
                for q in range(S5_SCAN_SLABS):
                    e_scr[sl_re[q], pl.ds(r0, 2 * nb), :] = xr[:, LANES * q:LANES * (q + 1)]
                    e_scr[sl_im[q], pl.ds(r0, 2 * nb), :] = xi[:, LANES * q:LANES * (q + 1)]
                return ar * tr - ai * ti + er[nb:], ar * ti + ai * tr + ei[nb:]

            sr, si = lax.fori_loop(0, ct // 2, body, (st_scr[:, c_re:c_re + W], st_scr[:, c_im:c_im + W]))
            st_scr[:, c_re:c_re + W] = sr
            st_scr[:, c_im:c_im + W] = si

    xs = []
    for h in range(2):
        per_b = [jnp.concatenate([e_scr[2 * nsl * h + s, pl.ds(b, ct, stride=nb), :] for s in range(2 * nsl)], axis=1)
                 for b in range(nb)]
        xs.append(jnp.concatenate(per_b, axis=0).astype(BF16))
    outs = []
    for j in range(L):
        ys = []
        for h in range(2):
            y = _nt_dot(xs[h], cpt_ref[h, j])
            for i in range(j + 1):
                y = y + jnp.dot(u_part(i, h), tp_ref[h, j - i], preferred_element_type=F32)
            ys.append(y)
        yj = jnp.concatenate(ys, axis=1) + d_ref[...] * uf[:, j * S5_WIDTH:(j + 1) * S5_WIDTH]
        z = _gelu_tanh(yj)
        gate = jnp.dot(z.astype(BF16), wg_ref[...], preferred_element_type=F32) + bg_ref[...]
        outs.append(z * _sigmoid(gate))
    o_ref[...] = jnp.concatenate(outs, axis=1).reshape(nb, ct, L * S5_WIDTH)


def _s5(u4, bbp, cpt, tp, alre, alim, d, w_glu, b_glu):
    nb, nchunks, w4 = u4.shape
    ct = S5_CT
    n_slabs = 2 * bbp.shape[3] // LANES
    resident = lambda a: pl.BlockSpec(a.shape, lambda i: (0,) * a.ndim, pipeline_mode=pl.Buffered(1))
    u_blk = pl.BlockSpec((nb, ct, w4), lambda i: (0, i, 0))
    return pl.pallas_call(
        _s5_kernel,
        grid=(nchunks // ct,),
        in_specs=[u_blk, resident(bbp), resident(cpt), resident(tp), resident(alre), resident(alim),
                  resident(d), resident(w_glu), resident(b_glu)],
        out_specs=u_blk,
        out_shape=jax.ShapeDtypeStruct(u4.shape, F32),
        scratch_shapes=[pltpu.VMEM((n_slabs, nb * ct, LANES), F32), pltpu.VMEM((nb, n_slabs * LANES), F32)],
        compiler_params=pltpu.CompilerParams(dimension_semantics=("arbitrary",), vmem_limit_bytes=VMEM_LIMIT),
        name="s5",
    )(u4, bbp, cpt, tp, alre, alim, d, w_glu, b_glu)


def _final_kernel(x_ref, g_ref, wb_ref, on_ref, os_ref, wpn_ref, wps_ref, wo_ref, fg_ref, o_ref, os_scr):
    xv = x_ref[0]
    h = (_rms_scale(xv) * g_ref[...]).astype(BF16)
    tm = xv.shape[0]
    o4 = os_ref[0]
    for i in range(S5_L):
        for s in range(S5_WIDTH // LANES):
            c0 = i * S5_WIDTH + LANES * s
            os_scr[s, pl.ds(i, tm // S5_L, stride=S5_L), :] = o4[:, c0:c0 + LANES]
    o_s5 = jnp.concatenate([os_scr[s] for s in range(S5_WIDTH // LANES)], axis=1)

    def proj(a, b):
        return jnp.dot(h, wb_ref[:, a:b], preferred_element_type=F32)

    def silu(v):
        return v * _sigmoid(v)

    a_in = (on_ref[0] * silu(proj(0, NSA_WIDTH))).astype(BF16)
    b_in = (o_s5 * silu(proj(NSA_WIDTH, NSA_WIDTH + S5_WIDTH))).astype(BF16)
    branch_a = jnp.dot(a_in, wpn_ref[...], preferred_element_type=F32)
    branch_b = jnp.dot(b_in, wps_ref[...], preferred_element_type=F32)
    o1 = NSA_WIDTH + S5_WIDTH
    merged = (_sigmoid(proj(o1, o1 + D_MODEL)) * branch_a
              + _sigmoid(proj(o1 + D_MODEL, o1 + 2 * D_MODEL)) * branch_b)
    y = xv + jnp.dot(merged.astype(BF16), wo_ref[...], preferred_element_type=F32)
    o_ref[0] = _rms_scale(y) * fg_ref[...]


def _final(x, norm_g, w_b, o_nsa, o_s5, wpn, wps, wo, final_g):
    B, T, D = x.shape
    tm = TM_PROJ
    row_blk = lambda w: pl.BlockSpec((1, tm, w), lambda b, i: (b, i, 0))
    full = lambda a: pl.BlockSpec(a.shape, lambda b, i: (0,) * a.ndim)
    return pl.pallas_call(
        _final_kernel,
        grid=(B, T // tm),
        in_specs=[row_blk(D), full(norm_g), full(w_b), row_blk(NSA_WIDTH),
                  pl.BlockSpec((1, tm // S5_L, S5_L * S5_WIDTH), lambda b, i: (b, i, 0)),
                  full(wpn), full(wps), full(wo), full(final_g)],
        out_specs=row_blk(D),
        out_shape=jax.ShapeDtypeStruct((B, T, D), F32),
        scratch_shapes=[pltpu.VMEM((S5_WIDTH // LANES, tm, LANES), F32)],
        compiler_params=pltpu.CompilerParams(
            dimension_semantics=("parallel", "arbitrary"), vmem_limit_bytes=VMEM_LIMIT),
        name="final",
    )(x, norm_g, w_b, o_nsa, o_s5, wpn, wps, wo, final_g)


def _rope_tables(T):
    half = HEAD_DIM // 2
    inv_freq = ROPE_THETA ** (-jnp.arange(half, dtype=F32) / half)
    ang = jnp.arange(T).astype(F32)[:, None] * inv_freq[None, :]
    cos, sin = jnp.cos(ang), jnp.sin(ang)
    cos2 = jnp.concatenate([cos, cos, cos, cos], axis=1)
    sin2 = jnp.concatenate([-sin, sin, -sin, sin], axis=1)
    return cos2, sin2


def _block_diag_halves(m):
    g, a, b = m.shape
    gh = g // 2
    eye = jnp.eye(gh, dtype=m.dtype)
    out = m.reshape(2, gh, a, 1, b) * eye[None, :, None, :, None]
    return out.reshape(2, gh * a, gh * b)


def _compress_w1(w1):
    w1r = w1.reshape(2, CMP_STRIDE, HEAD_DIM, CMP_HIDDEN)
    eye = jnp.eye(NSA_GROUPS, dtype=w1.dtype)
    out = jnp.einsum('hjdn,gk->jgdkhn', w1r, eye)
    return out.reshape(CMP_STRIDE, NSA_GROUPS * HEAD_DIM, NSA_GROUPS * 2 * CMP_HIDDEN).astype(BF16)


def kernel(x, norm_g, w_in, cmp_pos_k, cmp_pos_v, cmp_w1_k, cmp_w2_k, cmp_w1_v, cmp_w2_v, s5_lam_re, s5_lam_im, s5_log_dt, s5_b_re, s5_b_im, s5_c_re, s5_c_im, s5_d, w_glu, b_glu, w_proj_nsa, w_proj_s5, w_out, final_g):
    B, T, D = x.shape
    assert w_in.shape[0] == 1, "single-layer block"
    NCH = T // CMP_STRIDE
    NS = T // SEL_BLOCK

    w = w_in[0]
    w_a = jnp.concatenate([w[:, :_OFF_GL], jnp.pad(w[:, _OFF_GL:_OFF_GN], ((0, 0), (0, LANES - 24))),
                           w[:, _OFF_U:_OFF_GS]], axis=1).astype(BF16)
    w_b = jnp.concatenate([w[:, _OFF_GN:_OFF_U], w[:, _OFF_GS:]], axis=1).astype(BF16)
    g2 = norm_g[0][None, :]
    cos2, sin2 = _rope_tables(T)

    qq, kc, vc, ksa, vst, kw, vwt, glt, u4 = _inproj(x, g2, w_a, cos2, sin2)

    w2k = jnp.concatenate([jnp.zeros_like(cmp_w2_k[0]), cmp_w2_k[0]], axis=1).astype(BF16)
    w2vt = cmp_w2_v[0].T.astype(BF16)
    kcmp, vcmpt = _compress(kc, vc, _compress_w1(cmp_w1_k[0]), _compress_w1(cmp_w1_v[0]), w2k, w2vt,
                            _pos_bias(cmp_pos_k[0], cmp_w1_k[0]), _pos_bias(cmp_pos_v[0], cmp_w1_v[0]))

    c_start = jnp.arange(NCH) * CMP_STRIDE
    s_start = jnp.arange(NS) * SEL_BLOCK
    ovt = ((c_start[None, :] < s_start[:, None] + SEL_BLOCK) & (c_start[None, :] + CMP_BLOCK > s_start[:, None])
           & (jnp.arange(NCH)[None, :] < NCH - 1)).astype(BF16)
    o_nsa = _nsa(qq, kcmp, vcmpt, ksa, vst, kw, vwt, glt, ovt)

    rep = lambda a: jnp.repeat(a, S5_GROUP, axis=0)
    tr = lambda b: b.transpose(0, 2, 1).reshape(S5_GROUPS * S5_GROUP, S5_STATE)
    a_re, a_im, bb_re, bb_im = _s5_prep(
        rep(s5_lam_re[0]), rep(s5_lam_im[0]),
        rep(jnp.broadcast_to(s5_log_dt[0][:, None], (S5_GROUPS, S5_STATE))),
        tr(s5_b_re[0]), tr(s5_b_im[0]))
    halves = lambda a: a[::S5_GROUP].reshape(2, 1, -1)
    grp = lambda a: a.reshape(S5_GROUPS, S5_GROUP, S5_STATE)
    bbp, cpt, tp, alre, alim = _s5_weights(
        _block_diag_halves(grp(bb_re)), _block_diag_halves(grp(bb_im)), halves(a_re), halves(a_im),
        _block_diag_halves(s5_c_re[0]), _block_diag_halves(s5_c_im[0]))
    o_s5 = _s5(u4, bbp, cpt, tp, alre, alim, s5_d[0][None, :], w_glu[0].astype(BF16), b_glu[0][None, :])

    return _final(x, g2, w_b, o_nsa, o_s5, w_proj_nsa[0].astype(BF16), w_proj_s5[0].astype(BF16),
                  w_out[0].astype(BF16), final_g[None, :])
```

```python
import math

import jax
import jax.numpy as jnp
import numpy as np
from jax import lax
from jax.experimental import pallas as pl
from jax.experimental.pallas import tpu as pltpu

F32 = jnp.float32
BF16 = jnp.bfloat16

D_MODEL = 1024
NSA_HEADS = 8
NSA_GROUPS = 2
HEADS_PER_GROUP = 4
HEAD_DIM = 64
NSA_WIDTH = 512
CMP_BLOCK = 32
CMP_STRIDE = 16
CMP_HIDDEN = 256
SEL_BLOCK = 64
SEL_TOPK = 16
WINDOW = 512
ROPE_THETA = 10000.0
FORCED_SCORE = 1.0e4
NEG = -1.0e30
S5_WIDTH = 512
S5_GROUP = 16
S5_GROUPS = 32
S5_STATE = 64
RMS_EPS = 1.0e-6

LANES = 128
SUBLANES = 8
VMEM_LIMIT = 56 * 1024 * 1024

_OFF_GL = 1280
_OFF_GN = 1304
_OFF_U = 1816
_OFF_GS = 2328

TM_PROJ = 512
TQ = 128
TK = 512
NSA_NB = 2
SEL_LOOKAHEAD = 2
V_ROWS = 80
GATE_ROWS = 32
S5_L = 4
S5_CT = 128
S5_SCAN_SLABS = 4


def _gelu_tanh(x):
    c = math.sqrt(2.0 / math.pi)
    return 0.5 * x * (1.0 + jnp.tanh(c * (x + 0.044715 * (x * x * x))))


def _sigmoid(x):
    return 1.0 / (1.0 + jnp.exp(-x))


def _rms_scale(xv):
    ms = jnp.mean(xv * xv, axis=-1, keepdims=True)
    return xv * lax.rsqrt(ms + RMS_EPS)


def _nt_dot(a, b):
    return lax.dot_general(a, b, (((1,), (1,)), ((), ())), preferred_element_type=F32)


def _inproj_kernel(x_ref, g_ref, w_ref, cos_ref, sin_ref,
                   qq_ref, kc_ref, vc_ref, ks_ref, vs_ref, kw_ref, vw_ref, gl_ref, u_ref, us_scr):
    h = (_rms_scale(x_ref[0]) * g_ref[...]).astype(BF16)
    cos2 = cos_ref[...]
    sin2 = sin_ref[...]
    lane = lax.broadcasted_iota(jnp.int32, cos2.shape, 1)
    first_half = (lane & (HEAD_DIM - 1)) < (HEAD_DIM // 2)
    low = lane < HEAD_DIM

    wide = {}

    def proj(a, b):
        for (s0, s1) in ((0, 512), (512, 1280), (1280, 1920)):
            if s0 <= a and b <= s1:
                if s0 not in wide:
                    wide[s0] = jnp.dot(h, w_ref[:, s0:s1], preferred_element_type=F32)
                return wide[s0][:, a - s0:b - s0]
        raise ValueError((a, b))

    def rope(xs):
        partner = jnp.where(first_half, pltpu.roll(xs, 96, 1), pltpu.roll(xs, 32, 1))
        return xs * cos2 + partner * sin2

    scale = HEAD_DIM ** -0.5 * math.log2(math.e)
    for i in range(NSA_HEADS // 2):
        xs = proj(LANES * i, LANES * (i + 1)) * scale
        xr = rope(xs)
        qq_ref[0, 2 * i] = jnp.where(low, xr, pltpu.roll(xs, 64, 1)).astype(BF16)
        qq_ref[0, 2 * i + 1] = jnp.where(low, pltpu.roll(xr, 64, 1), xs).astype(BF16)

    kc_ref[0] = proj(512, 640)
    vc_ref[0] = proj(640, 768)
    tm = cos2.shape[0]
    t_row = pl.program_id(1) * tm + lax.broadcasted_iota(jnp.int32, cos2.shape, 0)
    blk_onehot = jnp.where(lane - HEAD_DIM == t_row // SEL_BLOCK, 1.0, 0.0)
    ones_rows = jnp.where(lax.broadcasted_iota(jnp.int32, (V_ROWS - HEAD_DIM, tm), 0) == 0, 1.0, 0.0)
    for (off, k_out, v_out, k_pad) in ((768, ks_ref, vs_ref, blk_onehot), (1024, kw_ref, vw_ref, 0.0)):
        kr = rope(proj(off, off + LANES))
        k_out[0, 0] = jnp.where(low, kr, k_pad).astype(BF16)
        k_out[0, 1] = jnp.where(low, pltpu.roll(kr, 64, 1), k_pad).astype(BF16)
        vt = proj(off + LANES, off + 2 * LANES).T
        for g in range(NSA_GROUPS):
            v_out[0, g] = jnp.concatenate([vt[HEAD_DIM * g:HEAD_DIM * (g + 1)], ones_rows], axis=0).astype(BF16)
    gl_ref[0] = _sigmoid(proj(1280, 1408)).T[0:GATE_ROWS]
    uv = proj(1408, 1920)
    for s in range(S5_WIDTH // LANES):
        us_scr[s] = uv[:, LANES * s:LANES * (s + 1)]
    for i in range(S5_L):
        for s in range(S5_WIDTH // LANES):
            c0 = i * S5_WIDTH + LANES * s
            u_ref[0, :, c0:c0 + LANES] = us_scr[s, pl.ds(i, tm // S5_L, stride=S5_L), :]


def _inproj(x, norm_g, w_a, cos2, sin2):
    B, T, D = x.shape
    tm = TM_PROJ
    grid = (B, T // tm)
    row_blk = lambda w: pl.BlockSpec((1, tm, w), lambda b, i: (b, i, 0))
    kv_blk = pl.BlockSpec((1, NSA_GROUPS, tm, LANES), lambda b, i: (b, 0, i, 0))
    kv_shape = jax.ShapeDtypeStruct((B, NSA_GROUPS, T, LANES), BF16)
    vt_blk = pl.BlockSpec((1, NSA_GROUPS, V_ROWS, tm), lambda b, i: (b, 0, 0, i))
    vt_shape = jax.ShapeDtypeStruct((B, NSA_GROUPS, V_ROWS, T), BF16)
    return pl.pallas_call(
        _inproj_kernel,
        grid=grid,
        in_specs=[
            row_blk(D),
            pl.BlockSpec((1, D), lambda b, i: (0, 0)),
            pl.BlockSpec(w_a.shape, lambda b, i: (0, 0)),
            pl.BlockSpec((tm, LANES), lambda b, i: (i, 0)),
            pl.BlockSpec((tm, LANES), lambda b, i: (i, 0)),
        ],
        out_specs=[
            pl.BlockSpec((1, NSA_HEADS, tm, LANES), lambda b, i: (b, 0, i, 0)),
            row_blk(LANES), row_blk(LANES),
            kv_blk, vt_blk, kv_blk, vt_blk,
            pl.BlockSpec((1, GATE_ROWS, tm), lambda b, i: (b, 0, i)),
            pl.BlockSpec((1, tm // S5_L, S5_L * S5_WIDTH), lambda b, i: (b, i, 0)),
        ],
        out_shape=[
            jax.ShapeDtypeStruct((B, NSA_HEADS, T, LANES), BF16),
            jax.ShapeDtypeStruct((B, T, LANES), F32), jax.ShapeDtypeStruct((B, T, LANES), F32),
            kv_shape, vt_shape, kv_shape, vt_shape,
            jax.ShapeDtypeStruct((B, GATE_ROWS, T), F32),
            jax.ShapeDtypeStruct((B, T // S5_L, S5_L * S5_WIDTH), F32),
        ],
        scratch_shapes=[pltpu.VMEM((S5_WIDTH // LANES, tm, LANES), F32)],
        compiler_params=pltpu.CompilerParams(
            dimension_semantics=("parallel", "arbitrary"), vmem_limit_bytes=VMEM_LIMIT),
        name="inproj",
    )(x, norm_g, w_a, cos2, sin2)


def _compress_kernel(kc_ref, vc_ref, w1k_ref, w1v_ref, w2k_ref, w2vt_ref, pbk_ref, pbv_ref, ko_ref, vo_ref):
    nch = ko_ref.shape[2]
    H = CMP_HIDDEN

    def hidden(c_ref, w1_ref, pb_ref):
        acc = [jnp.zeros((nch, 2 * H), F32) for _ in range(NSA_GROUPS)]
        for j in range(CMP_STRIDE):
            rows = c_ref[0, pl.ds(j, nch, stride=CMP_STRIDE), :].astype(BF16)
            wj = w1_ref[HEAD_DIM * j:HEAD_DIM * (j + 1), :]
            for g in range(NSA_GROUPS):
                acc[g] = acc[g] + jnp.dot(rows[:, HEAD_DIM * g:HEAD_DIM * (g + 1)], wj, preferred_element_type=F32)
        return [_gelu_tanh(a[:, 0:H] + pltpu.roll(a[:, H:], nch - 1, 0) + pb_ref[...]).astype(BF16) for a in acc]

    hk = hidden(kc_ref, w1k_ref, pbk_ref)
    hv = hidden(vc_ref, w1v_ref, pbv_ref)
    for g in range(NSA_GROUPS):
        ko_ref[0, g] = jnp.dot(hk[g], w2k_ref[...], preferred_element_type=F32).astype(BF16)
        vo_ref[0, g] = _nt_dot(w2vt_ref[...], hv[g]).astype(BF16)


def _pos_bias_kernel(p_ref, w1_ref, o_ref):
    o_ref[...] = jnp.dot(p_ref[...].astype(BF16), w1_ref[...].astype(BF16), preferred_element_type=F32)


def _pos_bias(pos, w1):
    p8 = jnp.broadcast_to(pos.reshape(1, -1), (SUBLANES, pos.size))
    return pl.pallas_call(_pos_bias_kernel, out_shape=jax.ShapeDtypeStruct((SUBLANES, CMP_HIDDEN), F32),
                          name="pos_bias")(p8, w1)[0:1]


def _compress(kc, vc, w1k, w1v, w2k, w2vt, pbk, pbv):
    B, T, _ = kc.shape
    G = NSA_GROUPS
    nch = T // CMP_STRIDE
    c_blk = pl.BlockSpec((1, T, LANES), lambda b: (b, 0, 0))
    full = lambda a: pl.BlockSpec(a.shape, lambda b: (0,) * a.ndim)
    return pl.pallas_call(
        _compress_kernel,
        grid=(B,),
        in_specs=[c_blk, c_blk, full(w1k), full(w1v), full(w2k), full(w2vt), full(pbk), full(pbv)],
        out_specs=[pl.BlockSpec((1, G, nch, LANES), lambda b: (b, 0, 0, 0)),
                   pl.BlockSpec((1, G, HEAD_DIM, nch), lambda b: (b, 0, 0, 0))],
        out_shape=[jax.ShapeDtypeStruct((B, G, nch, LANES), BF16),
                   jax.ShapeDtypeStruct((B, G, HEAD_DIM, nch), BF16)],
        compiler_params=pltpu.CompilerParams(dimension_semantics=("parallel",), vmem_limit_bytes=VMEM_LIMIT),
        name="compress",
    )(kc, vc, w1k, w1v, w2k, w2vt, pbk, pbv)


def _nsa_kernel(qq_ref, kc_ref, vct_ref, ksa_ref, vst_ref, kw_ref, vwt_ref, glt_ref, ovt_ref, o_ref,
                qsel_scr, acc_scr, m_scr):
    units = [(bb, g) for bb in range(qq_ref.shape[0]) for g in range(NSA_GROUPS)]
    uidx = {u: i for i, u in enumerate(units)}
    t0 = pl.program_id(1) * TQ
    R = HEADS_PER_GROUP
    cols = R * TQ
    NCP = kc_ref.shape[2]
    NS = ovt_ref.shape[0]
    WK = WINDOW + TQ
    t_lane = t0 + (lax.broadcasted_iota(jnp.int32, (1, cols), 1) & (TQ - 1))
    low = lax.broadcasted_iota(jnp.int32, (cols, LANES), 1) < HEAD_DIM
    sub8 = lax.broadcasted_iota(jnp.int32, (SUBLANES, TQ), 0)
    c_end = lax.broadcasted_iota(jnp.int32, (NCP, cols), 0) * CMP_STRIDE + (CMP_BLOCK - 1)
    cmp_valid = c_end <= t_lane
    ks0 = pl.multiple_of(jnp.maximum(t0 - WINDOW, 0), LANES)
    wpos = ks0 + lax.broadcasted_iota(jnp.int32, (WK, cols), 0)

    def group_q(u):
        bb, g = u
        return qq_ref[bb, R * g:R * (g + 1)].reshape(cols, LANES)

    def cmp_scores(u):
        return _nt_dot(kc_ref[u[0], u[1]], group_q(u))

    def cmp_probs(s):
        s = jnp.where(cmp_valid, s, NEG)
        e = jnp.exp2(s - jnp.max(s, axis=0, keepdims=True))
        inv = 1.0 / jnp.maximum(jnp.sum(e, axis=0, keepdims=True), 1.0e-30)
        return e * jnp.where(t_lane >= CMP_BLOCK - 1, inv, 0.0)

    def win_scores(u):
        return _nt_dot(kw_ref[u[0], u[1], pl.ds(ks0, WK), :], group_q(u))

    def win_probs(sw):
        sw = jnp.where(wpos <= t_lane, jnp.where(wpos > t_lane - WINDOW, sw, NEG), NEG).astype(BF16)
        return jnp.exp2(sw - jnp.max(sw, axis=0, keepdims=True))

    def win_out(u, ew):
        ow = jnp.dot(vwt_ref[u[0], u[1], :, pl.ds(ks0, WK)], ew, preferred_element_type=F32)
        return ow[0:HEAD_DIM] * (1.0 / ow[HEAD_DIM:HEAD_DIM + 1])

    def select_blocks(u, p):
        psum = p[:, 0:TQ] + p[:, TQ:2 * TQ] + p[:, 2 * TQ:3 * TQ] + p[:, 3 * TQ:4 * TQ]
        p_hi = psum.astype(BF16)
        p_lo = (psum - p_hi.astype(F32)).astype(BF16)
        imp = (jnp.dot(ovt_ref[...], p_hi, preferred_element_type=F32)
               + jnp.dot(ovt_ref[...], p_lo, preferred_element_type=F32))
        blk = lax.broadcasted_iota(jnp.int32, (NS, TQ), 0)
        t_l = t0 + lax.broadcasted_iota(jnp.int32, (NS, TQ), 1)
        cur = t_l // SEL_BLOCK
        imp = jnp.where(blk * SEL_BLOCK <= t_l, imp, -1.0)
        imp = jnp.where(blk == 0, FORCED_SCORE, imp)
        imp = jnp.where(blk == cur, FORCED_SCORE, imp)
        imp = jnp.where(blk == cur - 1, FORCED_SCORE, imp)
        nv = NS // SUBLANES
        imp8 = [imp[SUBLANES * j:SUBLANES * (j + 1)] for j in range(nv)]
        rank8 = [jnp.zeros((SUBLANES, TQ), F32) for _ in range(nv)]
        for mm in range(NS):
            row = imp[mm:mm + 1, :]
            jm = mm // SUBLANES
            for j in range(nv):
                if j < jm:
                    ahead = jnp.where(row > imp8[j], 1.0, 0.0)
                elif j > jm:
                    ahead = jnp.where(row >= imp8[j], 1.0, 0.0)
                else:
                    tie = jnp.where(sub8 > (mm % SUBLANES), 1.0, 0.0)
                    ahead = jnp.where(row > imp8[j], 1.0, 0.0) + jnp.where(row == imp8[j], tie, 0.0)
                rank8[j] = rank8[j] + ahead
        rank = jnp.concatenate(rank8, axis=0)
        pen = jnp.where(rank < float(SEL_TOPK), 0.0, NEG)
        pen_t = jnp.concatenate([jnp.zeros((LANES - NS, TQ), F32), pen], axis=0).T
        qsel_scr[uidx[u]] = jnp.where(low, group_q(u), jnp.concatenate([pen_t.astype(BF16)] * R, axis=0))

    nu = len(units)
    s_c = {0: cmp_scores(units[0]), 1: cmp_scores(units[1])}
    p_c, o_cmp, s_w = [], [], []
    for i, u in enumerate(units):
        p_c.append(cmp_probs(s_c.pop(i)))
        if i + 2 < nu:
            s_c[i + 2] = cmp_scores(units[i + 2])
        else:
            s_w.append(win_scores(units[i + 2 - nu]))
        o_cmp.append(jnp.dot(vct_ref[u[0], u[1]], p_c[i].astype(BF16), preferred_element_type=F32))
    o_win = []
    for i, u in enumerate(units):
        select_blocks(u, p_c[i])
        e_w = win_probs(s_w[i])
        if i + 2 < nu:
            s_w.append(win_scores(units[i + 2]))
        o_win.append(win_out(u, e_w))

    acc_scr[...] = jnp.zeros_like(acc_scr)
    m_scr[...] = jnp.full(m_scr.shape, NEG, F32)

    def sel_jobs(tiles):
        jobs = [(kt, u, masked) for (kt, masked) in tiles for u in units]

        def scores(job):
            kt, u, _ = job
            return _nt_dot(ksa_ref[u[0], u[1], pl.ds(pl.multiple_of(kt * TK, TK), TK), :], qsel_scr[uidx[u]])

        def stats(job, sc):
            kt, u, masked = job
            if masked:
                kpos = kt * TK + lax.broadcasted_iota(jnp.int32, (TK, cols), 0)
                sc = jnp.where(kpos <= t_lane, sc, NEG)
            sc = sc.astype(BF16)
            m_old = m_scr[uidx[u]]
            m_new = jnp.maximum(m_old, jnp.max(sc, axis=0, keepdims=True).astype(F32))
            m_scr[uidx[u]] = m_new
            return sc, m_new.astype(BF16), jnp.exp2(m_old - m_new)

        def accumulate(job, pp, alpha):
            kt, u, _ = job
            acc_scr[uidx[u]] = acc_scr[uidx[u]] * alpha + jnp.dot(
                vst_ref[u[0], u[1], :, pl.ds(pl.multiple_of(kt * TK, TK), TK)], pp, preferred_element_type=F32)

        n = len(jobs)
        pending = {i: scores(jobs[i]) for i in range(min(SEL_LOOKAHEAD, n))}
        st = {0: stats(jobs[0], pending.pop(0))}
        for i, job in enumerate(jobs):
            if i + 1 < n:
                st[i + 1] = stats(jobs[i + 1], pending.pop(i + 1))
            sc, m_new, alpha = st.pop(i)
            pp = jnp.exp2(sc - m_new)
            if i + SEL_LOOKAHEAD < n:
                pending[i + SEL_LOOKAHEAD] = scores(jobs[i + SEL_LOOKAHEAD])
            accumulate(job, pp, alpha)

    n_full = t0 // TK

    sel_jobs([(n_full, True)])

    def tile_pair(j, carry):
        sel_jobs([(2 * j, False), (2 * j + 1, False)])
        return carry

    lax.fori_loop(0, n_full // 2, tile_pair, 0)

    @pl.when(n_full % 2 == 1)
    def _():
        sel_jobs([(n_full - 1, False)])

    for bb in range(qq_ref.shape[0]):
        glt = glt_ref[bb]
        heads = []
        for g in range(NSA_GROUPS):
            ui = uidx[(bb, g)]
            acc = acc_scr[ui]
            o_sel = acc[0:HEAD_DIM] * (1.0 / acc[HEAD_DIM:HEAD_DIM + 1])
            for r in range(R):
                hh = R * g + r
                sl = slice(r * TQ, (r + 1) * TQ)
                heads.append(glt[3 * hh:3 * hh + 1] * o_cmp[ui][:, sl]
                             + glt[3 * hh + 1:3 * hh + 2] * o_sel[:, sl]
                             + glt[3 * hh + 2:3 * hh + 3] * o_win[ui][:, sl])
        o_ref[bb] = jnp.concatenate(heads, axis=0).T


def _nsa(qq, kcmp, vcmpt, ksa, vst, kw, vwt, glt, ovt):
    B, H, T, _ = qq.shape
    G = NSA_GROUPS
    NB = NSA_NB
    NCP = kcmp.shape[2]
    grid = (B // NB, T // TQ)
    k_blk = lambda n: pl.BlockSpec((NB, G, n, LANES), lambda b, i: (b, 0, 0, 0))
    vt_blk = lambda r, n: pl.BlockSpec((NB, G, r, n), lambda b, i: (b, 0, 0, 0))
    return pl.pallas_call(
        _nsa_kernel,
        grid=grid,
        in_specs=[
            pl.BlockSpec((NB, H, TQ, LANES), lambda b, i: (b, 0, i, 0)),
            k_blk(NCP), vt_blk(HEAD_DIM, NCP), k_blk(T), vt_blk(V_ROWS, T), k_blk(T), vt_blk(V_ROWS, T),
            pl.BlockSpec((NB, GATE_ROWS, TQ), lambda b, i: (b, 0, i)),
            pl.BlockSpec(ovt.shape, lambda b, i: (0, 0)),
        ],
        out_specs=pl.BlockSpec((NB, TQ, NSA_WIDTH), lambda b, i: (b, i, 0)),
        out_shape=jax.ShapeDtypeStruct((B, T, NSA_WIDTH), F32),
        scratch_shapes=[pltpu.VMEM((NB * G, HEADS_PER_GROUP * TQ, LANES), BF16),
                        pltpu.VMEM((NB * G, V_ROWS, HEADS_PER_GROUP * TQ), F32),
                        pltpu.VMEM((NB * G, 1, HEADS_PER_GROUP * TQ), F32)],
        compiler_params=pltpu.CompilerParams(
            dimension_semantics=("parallel", "arbitrary"), vmem_limit_bytes=VMEM_LIMIT),
        name="nsa",
    )(qq, kcmp, vcmpt, ksa, vst, kw, vwt, glt, ovt)


def _s5_prep_kernel(lre_ref, lim_ref, ldt_ref, bre_ref, bim_ref, are_ref, aim_ref, bbre_ref, bbim_ref):
    lre, lim = lre_ref[...], lim_ref[...]
    dt = jnp.exp(ldt_ref[...])
    mag = jnp.exp(lre * dt)
    a_re = mag * jnp.cos(lim * dt)
    a_im = mag * jnp.sin(lim * dt)
    den = lre * lre + lim * lim
    z_re = ((a_re - 1.0) * lre + a_im * lim) / den
    z_im = (a_im * lre - (a_re - 1.0) * lim) / den
    are_ref[...] = a_re
    aim_ref[...] = a_im
    bbre_ref[...] = z_re * bre_ref[...] - z_im * bim_ref[...]
    bbim_ref[...] = z_re * bim_ref[...] + z_im * bre_ref[...]


def _s5_prep(lre, lim, ldt, bre, bim):
    shp = jax.ShapeDtypeStruct(lre.shape, F32)
    return pl.pallas_call(_s5_prep_kernel, out_shape=[shp, shp, shp, shp], name="s5_prep")(lre, lim, ldt, bre, bim)


def _s5_weights_kernel(bbre_ref, bbim_ref, are_ref, aim_ref, crt_ref, cit_ref,
                       bbp_ref, cpt_ref, tp_ref, alre_ref, alim_ref, bd_scr, arow_scr):
    ngl = S5_GROUPS // 2
    bd_scr[...] = jnp.zeros_like(bd_scr)
    for q, ref in enumerate((bbre_ref, bbim_ref, crt_ref, cit_ref)):
        for gl in range(ngl):
            bd_scr[q, S5_GROUP * gl:S5_GROUP * (gl + 1), S5_STATE * gl:S5_STATE * (gl + 1)] = (
                ref[S5_GROUP * gl:S5_GROUP * (gl + 1), :])
    for q, ref in enumerate((are_ref, aim_ref)):
        for gl in range(ngl):
            arow_scr[q, :, S5_STATE * gl:S5_STATE * (gl + 1)] = ref[S5_GROUP * gl:S5_GROUP * gl + 1, :]
    bbre, bbim, crt, cit = bd_scr[0], bd_scr[1], bd_scr[2], bd_scr[3]
    are, aim = arow_scr[0], arow_scr[1]
    pre, pim = jnp.ones_like(are), jnp.zeros_like(are)
    for k in range(S5_L):
        bpr = bbre * pre - bbim * pim
        bpi = bbre * pim + bbim * pre
        bbp_ref[0, k] = jnp.concatenate([bpr, bpi], axis=1).astype(BF16)
        tp_ref[0, k] = (_nt_dot(bpr.astype(BF16), crt.astype(BF16))
                        - _nt_dot(bpi.astype(BF16), cit.astype(BF16))).astype(BF16)
        pre, pim = pre * are - pim * aim, pre * aim + pim * are
        cpt_ref[0, k] = jnp.concatenate([crt * pre - cit * pim, -(crt * pim + cit * pre)], axis=1).astype(BF16)
    alre_ref[0] = pre
    alim_ref[0] = pim


def _s5_weights(bbre, bbim, are, aim, crt, cit):
    nh = 2
    hw = S5_WIDTH // nh
    hs = S5_GROUPS // nh * S5_STATE
    half = pl.BlockSpec((hw, S5_STATE), lambda h: (h, 0))
    blk = lambda r, c: pl.BlockSpec((1, r, c), lambda h: (h, 0, 0))
    blk4 = lambda r, c: pl.BlockSpec((1, S5_L, r, c), lambda h: (h, 0, 0, 0))
    return pl.pallas_call(
        _s5_weights_kernel,
        grid=(nh,),
        in_specs=[half] * 6,
        out_specs=[blk4(hw, 2 * hs), blk4(hw, 2 * hs), blk4(hw, hw), blk(1, hs), blk(1, hs)],
        scratch_shapes=[pltpu.VMEM((4, hw, hs), F32), pltpu.VMEM((2, 1, hs), F32)],
        out_shape=[jax.ShapeDtypeStruct((nh, S5_L, hw, 2 * hs), BF16),
                   jax.ShapeDtypeStruct((nh, S5_L, hw, 2 * hs), BF16),
                   jax.ShapeDtypeStruct((nh, S5_L, hw, hw), BF16),
                   jax.ShapeDtypeStruct((nh, 1, hs), F32), jax.ShapeDtypeStruct((nh, 1, hs), F32)],
        compiler_params=pltpu.CompilerParams(dimension_semantics=("parallel",), vmem_limit_bytes=VMEM_LIMIT),
        name="s5_weights",
    )(bbre, bbim, are, aim, crt, cit)


def _s5_kernel(u_ref, bbp_ref, cpt_ref, tp_ref, alre_ref, alim_ref, d_ref, wg_ref, bg_ref, o_ref, e_scr, st_scr):
    nb, ct, _ = u_ref.shape
    L = S5_L
    HW = S5_WIDTH // 2
    HS = S5_GROUPS // 2 * S5_STATE
    nsl = HS // LANES
    W = S5_SCAN_SLABS * LANES

    @pl.when(pl.program_id(0) == 0)
    def _():
        st_scr[...] = jnp.zeros_like(st_scr)

    uf = u_ref[...].reshape(nb * ct, L * S5_WIDTH)
    ub = uf.astype(BF16)

    def u_part(i, h):
        c0 = i * S5_WIDTH + h * HW
        return ub[:, c0:c0 + HW]

    for h in range(2):
        e = jnp.dot(u_part(0, h), bbp_ref[h, L - 1], preferred_element_type=F32)
        for i in range(1, L):
            e = e + jnp.dot(u_part(i, h), bbp_ref[h, L - 1 - i], preferred_element_type=F32)
        for s in range(2 * nsl):
            for b in range(nb):
                e_scr[2 * nsl * h + s, pl.ds(b, ct, stride=nb), :] = e[b * ct:(b + 1) * ct, LANES * s:LANES * (s + 1)]

    for h in range(2):
        for j in range(nsl // S5_SCAN_SLABS):
            sl_re = [2 * nsl * h + S5_SCAN_SLABS * j + q for q in range(S5_SCAN_SLABS)]
            sl_im = [s + nsl for s in sl_re]
            ar = alre_ref[h, :, W * j:W * (j + 1)]
            ai = alim_ref[h, :, W * j:W * (j + 1)]
            c_re = 2 * HS * h + W * j
            c_im = c_re + HS

            def body(k, carry):
                sr, si = carry
                r0 = k * 2 * nb
                er = jnp.concatenate([e_scr[s, pl.ds(r0, 2 * nb), :] for s in sl_re], axis=1)
                ei = jnp.concatenate([e_scr[s, pl.ds(r0, 2 * nb), :] for s in sl_im], axis=1)
                tr = ar * sr - ai * si + er[0:nb]
                ti = ar * si + ai * sr + ei[0:nb]
                xr = jnp.concatenate([sr, tr], axis=0)
                xi = jnp.concatenate([si, ti], axis=0)
                for q in range(S5_SCAN_SLABS):
                    e_scr[sl_re[q], pl.ds(r0, 2 * nb), :] = xr[:, LANES * q:LANES * (q + 1)]
                    e_scr[sl_im[q], pl.ds(r0, 2 * nb), :] = xi[:, LANES * q:LANES * (q + 1)]
                return ar * tr - ai * ti + er[nb:], ar * ti + ai * tr + ei[nb:]

            sr, si = st_scr[:, c_re:c_re + W], st_scr[:, c_im:c_im + W]
            for k in range(ct // 2):
                sr, si = body(k, (sr, si))
            st_scr[:, c_re:c_re + W] = sr
            st_scr[:, c_im:c_im + W] = si

    xs = []
    for h in range(2):
        per_b = [jnp.concatenate([e_scr[2 * nsl * h + s, pl.ds(b, ct, stride=nb), :] for s in range(2 * nsl)], axis=1)
                 for b in range(nb)]
        xs.append(jnp.concatenate(per_b, axis=0).astype(BF16))
    outs = []
    for j in range(L):
        ys = []
        for h in range(2):
            y = _nt_dot(xs[h], cpt_ref[h, j])
            for i in range(j + 1):
                y = y + jnp.dot(u_part(i, h), tp_ref[h, j - i], preferred_element_type=F32)
            ys.append(y)
        yj = jnp.concatenate(ys, axis=1) + d_ref[...] * uf[:, j * S5_WIDTH:(j + 1) * S5_WIDTH]
        z = _gelu_tanh(yj)
        gate = jnp.dot(z.astype(BF16), wg_ref[...], preferred_element_type=F32) + bg_ref[...]
        outs.append(z * _sigmoid(gate))
    o_ref[...] = jnp.concatenate(outs, axis=1).reshape(nb, ct, L * S5_WIDTH)


def _s5(u4, bbp, cpt, tp, alre, alim, d, w_glu, b_glu):
    nb, nchunks, w4 = u4.shape
    ct = S5_CT
    n_slabs = 2 * bbp.shape[3] // LANES
    resident = lambda a: pl.BlockSpec(a.shape, lambda i: (0,) * a.ndim, pipeline_mode=pl.Buffered(1))
    u_blk = pl.BlockSpec((nb, ct, w4), lambda i: (0, i, 0))
    return pl.pallas_call(
        _s5_kernel,
        grid=(nchunks // ct,),
        in_specs=[u_blk, resident(bbp), resident(cpt), resident(tp), resident(alre), resident(alim),
                  resident(d), resident(w_glu), resident(b_glu)],
        out_specs=u_blk,
        out_shape=jax.ShapeDtypeStruct(u4.shape, F32),
        scratch_shapes=[pltpu.VMEM((n_slabs, nb * ct, LANES), F32), pltpu.VMEM((nb, n_slabs * LANES), F32)],
        compiler_params=pltpu.CompilerParams(dimension_semantics=("arbitrary",), vmem_limit_bytes=VMEM_LIMIT),
        name="s5",
    )(u4, bbp, cpt, tp, alre, alim, d, w_glu, b_glu)


def _final_kernel(x_ref, g_ref, wb_ref, on_ref, os_ref, wpn_ref, wps_ref, wo_ref, fg_ref, o_ref, os_scr):
    xv = x_ref[0]
    h = (_rms_scale(xv) * g_ref[...]).astype(BF16)
    tm = xv.shape[0]
    o4 = os_ref[0]
    for i in range(S5_L):
        for s in range(S5_WIDTH // LANES):
            c0 = i * S5_WIDTH + LANES * s
            os_scr[s, pl.ds(i, tm // S5_L, stride=S5_L), :] = o4[:, c0:c0 + LANES]
    o_s5 = jnp.concatenate([os_scr[s] for s in range(S5_WIDTH // LANES)], axis=1)

    def proj(a, b):
        return jnp.dot(h, wb_ref[:, a:b], preferred_element_type=F32)

    def silu(v):
        return v * _sigmoid(v)

    a_in = (on_ref[0] * silu(proj(0, NSA_WIDTH))).astype(BF16)
    b_in = (o_s5 * silu(proj(NSA_WIDTH, NSA_WIDTH + S5_WIDTH))).astype(BF16)
    branch_a = jnp.dot(a_in, wpn_ref[...], preferred_element_type=F32)
    branch_b = jnp.dot(b_in, wps_ref[...], preferred_element_type=F32)
    o1 = NSA_WIDTH + S5_WIDTH
    merged = (_sigmoid(proj(o1, o1 + D_MODEL)) * branch_a
              + _sigmoid(proj(o1 + D_MODEL, o1 + 2 * D_MODEL)) * branch_b)
    y = xv + jnp.dot(merged.astype(BF16), wo_ref[...], preferred_element_type=F32)
    o_ref[0] = _rms_scale(y) * fg_ref[...]


def _final(x, norm_g, w_b, o_nsa, o_s5, wpn, wps, wo, final_g):
    B, T, D = x.shape
    tm = TM_PROJ
    row_blk = lambda w: pl.BlockSpec((1, tm, w), lambda b, i: (b, i, 0))
    full = lambda a: pl.BlockSpec(a.shape, lambda b, i: (0,) * a.ndim)
    return pl.pallas_call(
        _final_kernel,
        grid=(B, T // tm),
        in_specs=[row_blk(D), full(norm_g), full(w_b), row_blk(NSA_WIDTH),
                  pl.BlockSpec((1, tm // S5_L, S5_L * S5_WIDTH), lambda b, i: (b, i, 0)),
                  full(wpn), full(wps), full(wo), full(final_g)],
        out_specs=row_blk(D),
        out_shape=jax.ShapeDtypeStruct((B, T, D), F32),
        scratch_shapes=[pltpu.VMEM((S5_WIDTH // LANES, tm, LANES), F32)],
        compiler_params=pltpu.CompilerParams(
            dimension_semantics=("parallel", "arbitrary"), vmem_limit_bytes=VMEM_LIMIT),
        name="final",
    )(x, norm_g, w_b, o_nsa, o_s5, wpn, wps, wo, final_g)


def _rope_tables(T):
    half = HEAD_DIM // 2
    inv_freq = np.float32(ROPE_THETA) ** (-np.arange(half, dtype=np.float32) / np.float32(half))
    ang = np.arange(T, dtype=np.float32)[:, None] * inv_freq[None, :].astype(np.float32)
    cos, sin = np.cos(ang).astype(np.float32), np.sin(ang).astype(np.float32)
    cos2 = np.concatenate([cos, cos, cos, cos], axis=1)
    sin2 = np.concatenate([-sin, sin, -sin, sin], axis=1)
    return jnp.asarray(cos2), jnp.asarray(sin2)


def _compress_w1(w1):
    half_rows = CMP_STRIDE * HEAD_DIM
    return jnp.concatenate([w1[:half_rows], w1[half_rows:]], axis=1).astype(BF16)


def kernel(x, norm_g, w_in, cmp_pos_k, cmp_pos_v, cmp_w1_k, cmp_w2_k, cmp_w1_v, cmp_w2_v, s5_lam_re, s5_lam_im, s5_log_dt, s5_b_re, s5_b_im, s5_c_re, s5_c_im, s5_d, w_glu, b_glu, w_proj_nsa, w_proj_s5, w_out, final_g):
    B, T, D = x.shape
    assert w_in.shape[0] == 1, "single-layer block"
    NCH = T // CMP_STRIDE
    NS = T // SEL_BLOCK

    w = w_in[0]
    w_a = jnp.concatenate([w[:, :_OFF_GL], jnp.pad(w[:, _OFF_GL:_OFF_GN], ((0, 0), (0, LANES - 24))),
                           w[:, _OFF_U:_OFF_GS]], axis=1).astype(BF16)
    w_b = jnp.concatenate([w[:, _OFF_GN:_OFF_U], w[:, _OFF_GS:]], axis=1).astype(BF16)
    g2 = norm_g[0][None, :]
    cos2, sin2 = _rope_tables(T)

    qq, kc, vc, ksa, vst, kw, vwt, glt, u4 = _inproj(x, g2, w_a, cos2, sin2)

    w2k = jnp.concatenate([jnp.zeros_like(cmp_w2_k[0]), cmp_w2_k[0]], axis=1).astype(BF16)
    w2vt = cmp_w2_v[0].T.astype(BF16)
    kcmp, vcmpt = _compress(kc, vc, _compress_w1(cmp_w1_k[0]), _compress_w1(cmp_w1_v[0]), w2k, w2vt,
                            _pos_bias(cmp_pos_k[0], cmp_w1_k[0]), _pos_bias(cmp_pos_v[0], cmp_w1_v[0]))

    c_start = jnp.arange(NCH) * CMP_STRIDE
    s_start = jnp.arange(NS) * SEL_BLOCK
    ovt = ((c_start[None, :] < s_start[:, None] + SEL_BLOCK) & (c_start[None, :] + CMP_BLOCK > s_start[:, None])
           & (jnp.arange(NCH)[None, :] < NCH - 1)).astype(BF16)
    o_nsa = _nsa(qq, kcmp, vcmpt, ksa, vst, kw, vwt, glt, ovt)

    rep = lambda a: jnp.repeat(a, S5_GROUP, axis=0)
    tr = lambda b: b.transpose(0, 2, 1).reshape(S5_GROUPS * S5_GROUP, S5_STATE)
    a_re, a_im, bb_re, bb_im = _s5_prep(
        rep(s5_lam_re[0]), rep(s5_lam_im[0]),
        rep(jnp.broadcast_to(s5_log_dt[0][:, None], (S5_GROUPS, S5_STATE))),
        tr(s5_b_re[0]), tr(s5_b_im[0]))
    flat = lambda c: c.reshape(S5_GROUPS * S5_GROUP, S5_STATE)
    bbp, cpt, tp, alre, alim = _s5_weights(bb_re, bb_im, a_re, a_im, flat(s5_c_re[0]), flat(s5_c_im[0]))
    o_s5 = _s5(u4, bbp, cpt, tp, alre, alim, s5_d[0][None, :], w_glu[0].astype(BF16), b_glu[0][None, :])

    return _final(x, g2, w_b, o_nsa, o_s5, w_proj_nsa[0].astype(BF16), w_proj_s5[0].astype(BF16),
                  w_out[0].astype(BF16), final_g[None, :])
```

```python
import math

import jax
import jax.numpy as jnp
import numpy as np
from jax import lax
from jax.experimental import pallas as pl
from jax.experimental.pallas import tpu as pltpu

F32 = jnp.float32
BF16 = jnp.bfloat16

D_MODEL = 1024
NSA_HEADS = 8
NSA_GROUPS = 2
HEADS_PER_GROUP = 4
HEAD_DIM = 64
NSA_WIDTH = 512
CMP_BLOCK = 32
CMP_STRIDE = 16
CMP_HIDDEN = 256
SEL_BLOCK = 64
SEL_TOPK = 16
WINDOW = 512
ROPE_THETA = 10000.0
FORCED_SCORE = 1.0e4
NEG = -1.0e30
S5_WIDTH = 512
S5_GROUP = 16
S5_GROUPS = 32
S5_STATE = 64
RMS_EPS = 1.0e-6

LANES = 128
SUBLANES = 8
VMEM_LIMIT = 56 * 1024 * 1024

_OFF_GL = 1280
_OFF_GN = 1304
_OFF_U = 1816
_OFF_GS = 2328

TM_PROJ = 512
TQ = 128
TK = 512
NSA_NB = 2
SEL_LOOKAHEAD = 2
V_ROWS = 80
GATE_ROWS = 32
S5_L = 8
S5_CPB = 256 // S5_L
S5_CT = 64
S5_SCAN_SLABS = 4


def _gelu_tanh(x):
    c = math.sqrt(2.0 / math.pi)
    return 0.5 * x * (1.0 + jnp.tanh(c * (x + 0.044715 * (x * x * x))))


def _sigmoid(x):
    return 1.0 / (1.0 + jnp.exp(-x))


def _rms_scale(xv):
    ms = jnp.mean(xv * xv, axis=-1, keepdims=True)
    return xv * lax.rsqrt(ms + RMS_EPS)


def _nt_dot(a, b):
    return lax.dot_general(a, b, (((1,), (1,)), ((), ())), preferred_element_type=F32)


def _inproj_kernel(x_ref, g_ref, w_ref, cos_ref, sin_ref,
                   qq_ref, kc_ref, vc_ref, ks_ref, vs_ref, kw_ref, vw_ref, gl_ref, u_ref, us_scr):
    h = (_rms_scale(x_ref[0]) * g_ref[...]).astype(BF16)
    cos2 = cos_ref[...]
    sin2 = sin_ref[...]
    lane = lax.broadcasted_iota(jnp.int32, cos2.shape, 1)
    first_half = (lane & (HEAD_DIM - 1)) < (HEAD_DIM // 2)
    low = lane < HEAD_DIM

    wide = {}

    def proj(a, b):
        for (s0, s1) in ((0, 512), (512, 1280), (1280, 1920)):
            if s0 <= a and b <= s1:
                if s0 not in wide:
                    wide[s0] = jnp.dot(h, w_ref[:, s0:s1], preferred_element_type=F32)
                return wide[s0][:, a - s0:b - s0]
        raise ValueError((a, b))

    def rope(xs):
        partner = jnp.where(first_half, pltpu.roll(xs, 96, 1), pltpu.roll(xs, 32, 1))
        return xs * cos2 + partner * sin2

    scale = HEAD_DIM ** -0.5 * math.log2(math.e)
    for i in range(NSA_HEADS // 2):
        xs = proj(LANES * i, LANES * (i + 1)) * scale
        xr = rope(xs)
        qq_ref[0, 2 * i] = jnp.where(low, xr, pltpu.roll(xs, 64, 1)).astype(BF16)
        qq_ref[0, 2 * i + 1] = jnp.where(low, pltpu.roll(xr, 64, 1), xs).astype(BF16)

    kc_ref[0] = proj(512, 640)
    vc_ref[0] = proj(640, 768)
    tm = cos2.shape[0]
    t_row = pl.program_id(1) * tm + lax.broadcasted_iota(jnp.int32, cos2.shape, 0)
    blk_onehot = jnp.where(lane - HEAD_DIM == t_row // SEL_BLOCK, 1.0, 0.0)
    ones_rows = jnp.where(lax.broadcasted_iota(jnp.int32, (V_ROWS - HEAD_DIM, tm), 0) == 0, 1.0, 0.0)
    for (off, k_out, v_out, k_pad) in ((768, ks_ref, vs_ref, blk_onehot), (1024, kw_ref, vw_ref, 0.0)):
        kr = rope(proj(off, off + LANES))
        k_out[0, 0] = jnp.where(low, kr, k_pad).astype(BF16)
        k_out[0, 1] = jnp.where(low, pltpu.roll(kr, 64, 1), k_pad).astype(BF16)
        vt = proj(off + LANES, off + 2 * LANES).T
        for g in range(NSA_GROUPS):
            v_out[0, g] = jnp.concatenate([vt[HEAD_DIM * g:HEAD_DIM * (g + 1)], ones_rows], axis=0).astype(BF16)
    gl_ref[0] = _sigmoid(proj(1280, 1408)).T[0:GATE_ROWS]
    uv = proj(1408, 1920)
    for s in range(S5_WIDTH // LANES):
        us_scr[s] = uv[:, LANES * s:LANES * (s + 1)]
    for i in range(S5_L):
        for s in range(S5_WIDTH // LANES):
            c0 = i * S5_WIDTH + LANES * s
            u_ref[0, :, c0:c0 + LANES] = us_scr[s, pl.ds(i, tm // S5_L, stride=S5_L), :]


def _inproj(x, norm_g, w_a, cos2, sin2):
    B, T, D = x.shape
    tm = TM_PROJ
    grid = (B, T // tm)
    row_blk = lambda w: pl.BlockSpec((1, tm, w), lambda b, i: (b, i, 0))
    kv_blk = pl.BlockSpec((1, NSA_GROUPS, tm, LANES), lambda b, i: (b, 0, i, 0))
    kv_shape = jax.ShapeDtypeStruct((B, NSA_GROUPS, T, LANES), BF16)
    vt_blk = pl.BlockSpec((1, NSA_GROUPS, V_ROWS, tm), lambda b, i: (b, 0, 0, i))
    vt_shape = jax.ShapeDtypeStruct((B, NSA_GROUPS, V_ROWS, T), BF16)
    return pl.pallas_call(
        _inproj_kernel,
        grid=grid,
        in_specs=[
            row_blk(D),
            pl.BlockSpec((1, D), lambda b, i: (0, 0)),
            pl.BlockSpec(w_a.shape, lambda b, i: (0, 0)),
            pl.BlockSpec((tm, LANES), lambda b, i: (i, 0)),
            pl.BlockSpec((tm, LANES), lambda b, i: (i, 0)),
        ],
        out_specs=[
            pl.BlockSpec((1, NSA_HEADS, tm, LANES), lambda b, i: (b, 0, i, 0)),
            row_blk(LANES), row_blk(LANES),
            kv_blk, vt_blk, kv_blk, vt_blk,
            pl.BlockSpec((1, GATE_ROWS, tm), lambda b, i: (b, 0, i)),
            pl.BlockSpec((1, tm // S5_L, S5_L * S5_WIDTH), lambda b, i: (b, i, 0)),
        ],
        out_shape=[
            jax.ShapeDtypeStruct((B, NSA_HEADS, T, LANES), BF16),
            jax.ShapeDtypeStruct((B, T, LANES), F32), jax.ShapeDtypeStruct((B, T, LANES), F32),
            kv_shape, vt_shape, kv_shape, vt_shape,
            jax.ShapeDtypeStruct((B, GATE_ROWS, T), F32),
            jax.ShapeDtypeStruct((B, T // S5_L, S5_L * S5_WIDTH), F32),
        ],
        scratch_shapes=[pltpu.VMEM((S5_WIDTH // LANES, tm, LANES), F32)],
        compiler_params=pltpu.CompilerParams(
            dimension_semantics=("parallel", "arbitrary"), vmem_limit_bytes=VMEM_LIMIT),
        name="inproj",
    )(x, norm_g, w_a, cos2, sin2)


def _compress_kernel(kc_ref, vc_ref, w1k_ref, w1v_ref, w2k_ref, w2vt_ref, pbk_ref, pbv_ref, ko_ref, vo_ref):
    nch = ko_ref.shape[2]
    H = CMP_HIDDEN

    def hidden(c_ref, w1_ref, pb_ref):
        acc = [jnp.zeros((nch, 2 * H), F32) for _ in range(NSA_GROUPS)]
        for j in range(CMP_STRIDE):
            rows = c_ref[0, pl.ds(j, nch, stride=CMP_STRIDE), :].astype(BF16)
            wj = w1_ref[HEAD_DIM * j:HEAD_DIM * (j + 1), :]
            for g in range(NSA_GROUPS):
                acc[g] = acc[g] + jnp.dot(rows[:, HEAD_DIM * g:HEAD_DIM * (g + 1)], wj, preferred_element_type=F32)
        return [_gelu_tanh(a[:, 0:H] + pltpu.roll(a[:, H:], nch - 1, 0) + pb_ref[...]).astype(BF16) for a in acc]

    hk = hidden(kc_ref, w1k_ref, pbk_ref)
    hv = hidden(vc_ref, w1v_ref, pbv_ref)
    for g in range(NSA_GROUPS):
        ko_ref[0, g] = jnp.dot(hk[g], w2k_ref[...], preferred_element_type=F32).astype(BF16)
        vo_ref[0, g] = _nt_dot(w2vt_ref[...], hv[g]).astype(BF16)


def _pos_bias_kernel(p_ref, w1_ref, o_ref):
    o_ref[...] = jnp.dot(p_ref[...].astype(BF16), w1_ref[...].astype(BF16), preferred_element_type=F32)


def _pos_bias(pos, w1):
    p8 = jnp.broadcast_to(pos.reshape(1, -1), (SUBLANES, pos.size))
    return pl.pallas_call(_pos_bias_kernel, out_shape=jax.ShapeDtypeStruct((SUBLANES, CMP_HIDDEN), F32),
                          name="pos_bias")(p8, w1)[0:1]


def _compress(kc, vc, w1k, w1v, w2k, w2vt, pbk, pbv):
    B, T, _ = kc.shape
    G = NSA_GROUPS
    nch = T // CMP_STRIDE
    c_blk = pl.BlockSpec((1, T, LANES), lambda b: (b, 0, 0))
    full = lambda a: pl.BlockSpec(a.shape, lambda b: (0,) * a.ndim)
    return pl.pallas_call(
        _compress_kernel,
        grid=(B,),
        in_specs=[c_blk, c_blk, full(w1k), full(w1v), full(w2k), full(w2vt), full(pbk), full(pbv)],
        out_specs=[pl.BlockSpec((1, G, nch, LANES), lambda b: (b, 0, 0, 0)),
                   pl.BlockSpec((1, G, HEAD_DIM, nch), lambda b: (b, 0, 0, 0))],
        out_shape=[jax.ShapeDtypeStruct((B, G, nch, LANES), BF16),
                   jax.ShapeDtypeStruct((B, G, HEAD_DIM, nch), BF16)],
        compiler_params=pltpu.CompilerParams(dimension_semantics=("parallel",), vmem_limit_bytes=VMEM_LIMIT),
        name="compress",
    )(kc, vc, w1k, w1v, w2k, w2vt, pbk, pbv)


def _nsa_kernel(qq_ref, kc_ref, vct_ref, ksa_ref, vst_ref, kw_ref, vwt_ref, glt_ref, ovt_ref, o_ref,
                qsel_scr, acc_scr, m_scr):
    units = [(bb, g) for bb in range(qq_ref.shape[0]) for g in range(NSA_GROUPS)]
    uidx = {u: i for i, u in enumerate(units)}
    t0 = pl.program_id(1) * TQ
    R = HEADS_PER_GROUP
    cols = R * TQ
    NCP = kc_ref.shape[2]
    NS = ovt_ref.shape[0]
    WK = WINDOW + TQ
    t_lane = t0 + (lax.broadcasted_iota(jnp.int32, (1, cols), 1) & (TQ - 1))
    low = lax.broadcasted_iota(jnp.int32, (cols, LANES), 1) < HEAD_DIM
    sub8 = lax.broadcasted_iota(jnp.int32, (SUBLANES, TQ), 0)
    c_end = lax.broadcasted_iota(jnp.int32, (NCP, cols), 0) * CMP_STRIDE + (CMP_BLOCK - 1)
    cmp_valid = c_end <= t_lane
    n_wc = WK // TQ
    w_pos = [t0 - WINDOW + TQ * c for c in range(n_wc)]
    w_start = [pl.multiple_of(jnp.maximum(p, 0), TQ) for p in w_pos]
    row_tq = lax.broadcasted_iota(jnp.int32, (TQ, cols), 0)

    def group_q(u):
        bb, g = u
        return qq_ref[bb, R * g:R * (g + 1)].reshape(cols, LANES)

    def cmp_scores(u):
        return _nt_dot(kc_ref[u[0], u[1]], group_q(u))

    def cmp_probs(s):
        s = jnp.where(cmp_valid, s, NEG)
        e = jnp.exp2(s - jnp.max(s, axis=0, keepdims=True))
        inv = 1.0 / jnp.maximum(jnp.sum(e, axis=0, keepdims=True), 1.0e-30)
        return e * jnp.where(t_lane >= CMP_BLOCK - 1, inv, 0.0)

    def win_scores(u):
        kw = jnp.concatenate([kw_ref[u[0], u[1], pl.ds(w_start[c], TQ), :] for c in range(n_wc)], axis=0)
        return _nt_dot(kw, group_q(u))

    def win_probs(sw):
        parts = []
        for c in range(n_wc):
            sc = sw[TQ * c:TQ * (c + 1)]
            if c == 0:
                sc = jnp.where(w_pos[0] + row_tq > t_lane - WINDOW, sc, NEG)
            if c == n_wc - 1:
                sc = jnp.where(t0 + row_tq <= t_lane, sc, NEG)
            else:
                sc = jnp.where(w_pos[c] >= 0, sc, NEG)
            parts.append(sc.astype(BF16))
        sw = jnp.concatenate(parts, axis=0)
        return jnp.exp2(sw - jnp.max(sw, axis=0, keepdims=True))

    def win_out(u, ew):
        vw = jnp.concatenate([vwt_ref[u[0], u[1], :, pl.ds(w_start[c], TQ)] for c in range(n_wc)], axis=1)
        ow = jnp.dot(vw, ew, preferred_element_type=F32)
        return ow[0:HEAD_DIM] * (1.0 / ow[HEAD_DIM:HEAD_DIM + 1])

    def select_blocks(u, p):
        psum = p[:, 0:TQ] + p[:, TQ:2 * TQ] + p[:, 2 * TQ:3 * TQ] + p[:, 3 * TQ:4 * TQ]
        p_hi = psum.astype(BF16)
        p_lo = (psum - p_hi.astype(F32)).astype(BF16)
        imp = (jnp.dot(ovt_ref[...], p_hi, preferred_element_type=F32)
               + jnp.dot(ovt_ref[...], p_lo, preferred_element_type=F32))
        blk = lax.broadcasted_iota(jnp.int32, (NS, TQ), 0)
        t_l = t0 + lax.broadcasted_iota(jnp.int32, (NS, TQ), 1)
        cur = t_l // SEL_BLOCK
        imp = jnp.where(blk * SEL_BLOCK <= t_l, imp, -1.0)
        imp = jnp.where(blk == 0, FORCED_SCORE, imp)
        imp = jnp.where(blk == cur, FORCED_SCORE, imp)
        imp = jnp.where(blk == cur - 1, FORCED_SCORE, imp)
        nv = NS // SUBLANES
        imp8 = [imp[SUBLANES * j:SUBLANES * (j + 1)] for j in range(nv)]
        rank8 = [jnp.zeros((SUBLANES, TQ), F32) for _ in range(nv)]
        for mm in range(NS):
            row = imp[mm:mm + 1, :]
            jm = mm // SUBLANES
            for j in range(nv):
                if j < jm:
                    ahead = jnp.where(row > imp8[j], 1.0, 0.0)
                elif j > jm:
                    ahead = jnp.where(row >= imp8[j], 1.0, 0.0)
                else:
                    tie = jnp.where(sub8 > (mm % SUBLANES), 1.0, 0.0)
                    ahead = jnp.where(row > imp8[j], 1.0, 0.0) + jnp.where(row == imp8[j], tie, 0.0)
                rank8[j] = rank8[j] + ahead
        rank = jnp.concatenate(rank8, axis=0)
        pen = jnp.where(rank < float(SEL_TOPK), 0.0, NEG)
        pen_t = jnp.concatenate([jnp.zeros((LANES - NS, TQ), F32), pen], axis=0).T
        qsel_scr[uidx[u]] = jnp.where(low, group_q(u), jnp.concatenate([pen_t.astype(BF16)] * R, axis=0))

    nu = len(units)
    s_c = {0: cmp_scores(units[0]), 1: cmp_scores(units[1])}
    p_c, o_cmp, s_w = [], [], []
    for i, u in enumerate(units):
        p_c.append(cmp_probs(s_c.pop(i)))
        if i + 2 < nu:
            s_c[i + 2] = cmp_scores(units[i + 2])
        else:
            s_w.append(win_scores(units[i + 2 - nu]))
        o_cmp.append(jnp.dot(vct_ref[u[0], u[1]], p_c[i].astype(BF16), preferred_element_type=F32))
    o_win = []
    for i, u in enumerate(units):
        select_blocks(u, p_c[i])
        e_w = win_probs(s_w[i])
        if i + 2 < nu:
            s_w.append(win_scores(units[i + 2]))
        o_win.append(win_out(u, e_w))

    acc_scr[...] = jnp.zeros_like(acc_scr)
    m_scr[...] = jnp.full(m_scr.shape, NEG, F32)

    def sel_jobs(tiles):
        jobs = [(kt, u, masked) for (kt, masked) in tiles for u in units]

        def scores(job):
            kt, u, _ = job
            return _nt_dot(ksa_ref[u[0], u[1], pl.ds(pl.multiple_of(kt * TK, TK), TK), :], qsel_scr[uidx[u]])

        def stats(job, sc):
            kt, u, masked = job
            if masked:
                kpos = kt * TK + lax.broadcasted_iota(jnp.int32, (TK, cols), 0)
                sc = jnp.where(kpos <= t_lane, sc, NEG)
            sc = sc.astype(BF16)
            m_old = m_scr[uidx[u]]
            m_new = jnp.maximum(m_old, jnp.max(sc, axis=0, keepdims=True).astype(F32))
            m_scr[uidx[u]] = m_new
            return sc, m_new.astype(BF16), jnp.exp2(m_old - m_new)

        def accumulate(job, pp, alpha):
            kt, u, _ = job
            acc_scr[uidx[u]] = acc_scr[uidx[u]] * alpha + jnp.dot(
                vst_ref[u[0], u[1], :, pl.ds(pl.multiple_of(kt * TK, TK), TK)], pp, preferred_element_type=F32)

        n = len(jobs)
        pending = {i: scores(jobs[i]) for i in range(min(SEL_LOOKAHEAD, n))}
        st = {0: stats(jobs[0], pending.pop(0))}
        for i, job in enumerate(jobs):
            if i + 1 < n:
                st[i + 1] = stats(jobs[i + 1], pending.pop(i + 1))
            sc, m_new, alpha = st.pop(i)
            pp = jnp.exp2(sc - m_new)
            if i + SEL_LOOKAHEAD < n:
                pending[i + SEL_LOOKAHEAD] = scores(jobs[i + SEL_LOOKAHEAD])
            accumulate(job, pp, alpha)

    n_full = t0 // TK

    sel_jobs([(n_full, True)])

    def tile_pair(j, carry):
        sel_jobs([(2 * j, False), (2 * j + 1, False)])
        return carry

    lax.fori_loop(0, n_full // 2, tile_pair, 0)

    @pl.when(n_full % 2 == 1)
    def _():
        sel_jobs([(n_full - 1, False)])

    for bb in range(qq_ref.shape[0]):
        glt = glt_ref[bb]
        heads = []
        for g in range(NSA_GROUPS):
            ui = uidx[(bb, g)]
            acc = acc_scr[ui]
            o_sel = acc[0:HEAD_DIM] * (1.0 / acc[HEAD_DIM:HEAD_DIM + 1])
            for r in range(R):
                hh = R * g + r
                sl = slice(r * TQ, (r + 1) * TQ)
                heads.append(glt[3 * hh:3 * hh + 1] * o_cmp[ui][:, sl]
                             + glt[3 * hh + 1:3 * hh + 2] * o_sel[:, sl]
                             + glt[3 * hh + 2:3 * hh + 3] * o_win[ui][:, sl])
        o_ref[bb] = jnp.concatenate(heads, axis=0).T


def _nsa(qq, kcmp, vcmpt, ksa, vst, kw, vwt, glt, ovt):
    B, H, T, _ = qq.shape
    G = NSA_GROUPS
    NB = NSA_NB
    NCP = kcmp.shape[2]
    grid = (B // NB, T // TQ)
    k_blk = lambda n: pl.BlockSpec((NB, G, n, LANES), lambda b, i: (b, 0, 0, 0))
    vt_blk = lambda r, n: pl.BlockSpec((NB, G, r, n), lambda b, i: (b, 0, 0, 0))
    return pl.pallas_call(
        _nsa_kernel,
        grid=grid,
        in_specs=[
            pl.BlockSpec((NB, H, TQ, LANES), lambda b, i: (b, 0, i, 0)),
            k_blk(NCP), vt_blk(HEAD_DIM, NCP), k_blk(T), vt_blk(V_ROWS, T), k_blk(T), vt_blk(V_ROWS, T),
            pl.BlockSpec((NB, GATE_ROWS, TQ), lambda b, i: (b, 0, i)),
            pl.BlockSpec(ovt.shape, lambda b, i: (0, 0)),
        ],
        out_specs=pl.BlockSpec((NB, TQ, NSA_WIDTH), lambda b, i: (b, i, 0)),
        out_shape=jax.ShapeDtypeStruct((B, T, NSA_WIDTH), F32),
        scratch_shapes=[pltpu.VMEM((NB * G, HEADS_PER_GROUP * TQ, LANES), BF16),
                        pltpu.VMEM((NB * G, V_ROWS, HEADS_PER_GROUP * TQ), F32),
                        pltpu.VMEM((NB * G, 1, HEADS_PER_GROUP * TQ), F32)],
        compiler_params=pltpu.CompilerParams(
            dimension_semantics=("parallel", "arbitrary"), vmem_limit_bytes=VMEM_LIMIT),
        name="nsa",
    )(qq, kcmp, vcmpt, ksa, vst, kw, vwt, glt, ovt)


def _s5_prep_kernel(lre_ref, lim_ref, ldt_ref, bre_ref, bim_ref, are_ref, aim_ref, bbre_ref, bbim_ref):
    lre, lim = lre_ref[...], lim_ref[...]
    dt = jnp.exp(ldt_ref[...])
    mag = jnp.exp(lre * dt)
    a_re = mag * jnp.cos(lim * dt)
    a_im = mag * jnp.sin(lim * dt)
    den = lre * lre + lim * lim
    z_re = ((a_re - 1.0) * lre + a_im * lim) / den
    z_im = (a_im * lre - (a_re - 1.0) * lim) / den
    are_ref[...] = a_re
    aim_ref[...] = a_im
    bbre_ref[...] = z_re * bre_ref[...] - z_im * bim_ref[...]
    bbim_ref[...] = z_re * bim_ref[...] + z_im * bre_ref[...]


def _s5_prep(lre, lim, ldt, bre, bim):
    shp = jax.ShapeDtypeStruct(lre.shape, F32)
    return pl.pallas_call(_s5_prep_kernel, out_shape=[shp, shp, shp, shp], name="s5_prep")(lre, lim, ldt, bre, bim)


def _s5_weights_kernel(bbre_ref, bbim_ref, are_ref, aim_ref, crt_ref, cit_ref,
                       we_ref, wct_ref, tp_ref, alre_ref, alim_ref, bd_scr, arow_scr, tp_scr):
    gpb = S5_CPB // S5_GROUP
    bd_scr[...] = jnp.zeros_like(bd_scr)
    tp_scr[...] = jnp.zeros_like(tp_scr)
    for n, ref in enumerate((bbre_ref, bbim_ref, crt_ref, cit_ref)):
        for gl in range(gpb):
            bd_scr[n, S5_GROUP * gl:S5_GROUP * (gl + 1), S5_STATE * gl:S5_STATE * (gl + 1)] = (
                ref[S5_GROUP * gl:S5_GROUP * (gl + 1), :])
    for n, ref in enumerate((are_ref, aim_ref)):
        for gl in range(gpb):
            arow_scr[n, :, S5_STATE * gl:S5_STATE * (gl + 1)] = ref[S5_GROUP * gl:S5_GROUP * gl + 1, :]
    bbre, bbim, crt, cit = bd_scr[0], bd_scr[1], bd_scr[2], bd_scr[3]
    are, aim = arow_scr[0], arow_scr[1]
    pre, pim = jnp.ones_like(are), jnp.zeros_like(are)
    for k in range(S5_L):
        bpr = bbre * pre - bbim * pim
        bpi = bbre * pim + bbim * pre
        i = S5_L - 1 - k
        we_ref[0, S5_CPB * i:S5_CPB * (i + 1), :] = jnp.concatenate([bpr, bpi], axis=1).astype(BF16)
        tap = (_nt_dot(bpr.astype(BF16), crt.astype(BF16)) - _nt_dot(bpi.astype(BF16), cit.astype(BF16)))
        for i in range(S5_L - k):
            j = i + k
            tp_scr[S5_CPB * i:S5_CPB * (i + 1), S5_CPB * j:S5_CPB * (j + 1)] = tap
        pre, pim = pre * are - pim * aim, pre * aim + pim * are
        wct_ref[0, S5_CPB * k:S5_CPB * (k + 1), :] = jnp.concatenate(
            [crt * pre - cit * pim, -(crt * pim + cit * pre)], axis=1).astype(BF16)
    tp_ref[0] = tp_scr[...].astype(BF16)
    alre_ref[0] = pre
    alim_ref[0] = pim


def _s5_weights(bbre, bbim, are, aim, crt, cit):
    nblk = S5_WIDTH // S5_CPB
    sb = S5_CPB // S5_GROUP * S5_STATE
    lc = S5_L * S5_CPB
    rows = pl.BlockSpec((S5_CPB, S5_STATE), lambda q: (q, 0))
    blk = lambda r, c: pl.BlockSpec((1, r, c), lambda q: (q, 0, 0))
    return pl.pallas_call(
        _s5_weights_kernel,
        grid=(nblk,),
        in_specs=[rows] * 6,
        out_specs=[blk(lc, 2 * sb), blk(lc, 2 * sb), blk(lc, lc), blk(1, sb), blk(1, sb)],
        scratch_shapes=[pltpu.VMEM((4, S5_CPB, sb), F32), pltpu.VMEM((2, 1, sb), F32), pltpu.VMEM((lc, lc), F32)],
        out_shape=[jax.ShapeDtypeStruct((nblk, lc, 2 * sb), BF16),
                   jax.ShapeDtypeStruct((nblk, lc, 2 * sb), BF16),
                   jax.ShapeDtypeStruct((nblk, lc, lc), BF16),
                   jax.ShapeDtypeStruct((nblk, 1, sb), F32), jax.ShapeDtypeStruct((nblk, 1, sb), F32)],
        compiler_params=pltpu.CompilerParams(dimension_semantics=("parallel",), vmem_limit_bytes=VMEM_LIMIT),
        name="s5_weights",
    )(bbre, bbim, are, aim, crt, cit)


def _s5_kernel(u_ref, we_ref, wct_ref, tp_ref, alre_ref, alim_ref, d_ref, wg_ref, bg_ref, o_ref, e_scr, st_scr):
    nb, ct, _ = u_ref.shape
    L = S5_L
    nblk = S5_WIDTH // S5_CPB
    sb = S5_CPB // S5_GROUP * S5_STATE
    spb = 2 * sb // LANES
    W = S5_SCAN_SLABS * LANES

    @pl.when(pl.program_id(0) == 0)
    def _():
        st_scr[...] = jnp.zeros_like(st_scr)

    uf = u_ref[...].reshape(nb * ct, L * S5_WIDTH)
    ub = uf.astype(BF16)
    u_blk = [jnp.concatenate([ub[:, i * S5_WIDTH + S5_CPB * q:i * S5_WIDTH + S5_CPB * (q + 1)] for i in range(L)],
                             axis=1) for q in range(nblk)]

    for q in range(nblk):
        e = jnp.dot(u_blk[q], we_ref[q], preferred_element_type=F32)
        for s in range(spb):
            for b in range(nb):
                e_scr[spb * q + s, pl.ds(b, ct, stride=nb), :] = e[b * ct:(b + 1) * ct, LANES * s:LANES * (s + 1)]

    def slabs_of(cs):
        q, r = divmod(cs * LANES, sb)
        return spb * q + r // LANES, spb * q + (sb + r) // LANES

    for j in range(S5_GROUPS * S5_STATE // W):
        sl = [slabs_of(S5_SCAN_SLABS * j + n) for n in range(S5_SCAN_SLABS)]
        sl_re, sl_im = [s[0] for s in sl], [s[1] for s in sl]
        ar = alre_ref[:, W * j:W * (j + 1)]
        ai = alim_ref[:, W * j:W * (j + 1)]
        load = lambda r0, rows, slabs: jnp.concatenate([e_scr[s, pl.ds(r0, rows), :] for s in slabs], axis=1)
        state = lambda slabs: jnp.concatenate([st_scr[:, LANES * s:LANES * (s + 1)] for s in slabs], axis=1)
        sr, si = state(sl_re), state(sl_im)
        for k in range(ct // 2):
            r0 = k * 2 * nb
            er, ei = load(r0, 2 * nb, sl_re), load(r0, 2 * nb, sl_im)
            tr = ar * sr - ai * si + er[0:nb]
            ti = ar * si + ai * sr + ei[0:nb]
            xr = jnp.concatenate([sr, tr], axis=0)
            xi = jnp.concatenate([si, ti], axis=0)
            for n in range(S5_SCAN_SLABS):
                e_scr[sl_re[n], pl.ds(r0, 2 * nb), :] = xr[:, LANES * n:LANES * (n + 1)]
                e_scr[sl_im[n], pl.ds(r0, 2 * nb), :] = xi[:, LANES * n:LANES * (n + 1)]
            sr, si = ar * tr - ai * ti + er[nb:], ar * ti + ai * tr + ei[nb:]
        for n in range(S5_SCAN_SLABS):
            st_scr[:, LANES * sl_re[n]:LANES * (sl_re[n] + 1)] = sr[:, LANES * n:LANES * (n + 1)]
            st_scr[:, LANES * sl_im[n]:LANES * (sl_im[n] + 1)] = si[:, LANES * n:LANES * (n + 1)]

    ys = []
    for q in range(nblk):
        per_b = [jnp.concatenate([e_scr[spb * q + s, pl.ds(b, ct, stride=nb), :] for s in range(spb)], axis=1)
                 for b in range(nb)]
        x_in = jnp.concatenate(per_b, axis=0).astype(BF16)
        ys.append((_nt_dot(x_in, wct_ref[q]) + jnp.dot(u_blk[q], tp_ref[q], preferred_element_type=F32)).astype(BF16))
    outs = []
    for j in range(L):
        yj = jnp.concatenate([y[:, S5_CPB * j:S5_CPB * (j + 1)] for y in ys], axis=1).astype(F32)
        yj = yj + d_ref[...] * uf[:, j * S5_WIDTH:(j + 1) * S5_WIDTH]
        z = _gelu_tanh(yj)
        gate = jnp.dot(z.astype(BF16), wg_ref[...], preferred_element_type=F32) + bg_ref[...]
        outs.append(z * _sigmoid(gate))
    o_ref[...] = jnp.concatenate(outs, axis=1).reshape(nb, ct, L * S5_WIDTH)


def _s5(u4, we, wct, tp, alre, alim, d, w_glu, b_glu):
    nb, nchunks, w4 = u4.shape
    ct = S5_CT
    n_slabs = 2 * S5_GROUPS * S5_STATE // LANES
    resident = lambda a: pl.BlockSpec(a.shape, lambda i: (0,) * a.ndim, pipeline_mode=pl.Buffered(1))
    u_blk = pl.BlockSpec((nb, ct, w4), lambda i: (0, i, 0))
    return pl.pallas_call(
        _s5_kernel,
        grid=(nchunks // ct,),
        in_specs=[u_blk, resident(we), resident(wct), resident(tp), resident(alre), resident(alim),
                  resident(d), resident(w_glu), resident(b_glu)],
        out_specs=u_blk,
        out_shape=jax.ShapeDtypeStruct(u4.shape, F32),
        scratch_shapes=[pltpu.VMEM((n_slabs, nb * ct, LANES), F32), pltpu.VMEM((nb, n_slabs * LANES), F32)],
        compiler_params=pltpu.CompilerParams(dimension_semantics=("arbitrary",), vmem_limit_bytes=VMEM_LIMIT),
        name="s5",
    )(u4, we, wct, tp, alre, alim, d, w_glu, b_glu)


def _final_kernel(x_ref, g_ref, wb_ref, on_ref, os_ref, wpn_ref, wps_ref, wo_ref, fg_ref, o_ref, os_scr):
    xv = x_ref[0]
    h = (_rms_scale(xv) * g_ref[...]).astype(BF16)
    tm = xv.shape[0]
    o4 = os_ref[0]
    for i in range(S5_L):
        for s in range(S5_WIDTH // LANES):
            c0 = i * S5_WIDTH + LANES * s
            os_scr[s, pl.ds(i, tm // S5_L, stride=S5_L), :] = o4[:, c0:c0 + LANES]
    o_s5 = jnp.concatenate([os_scr[s] for s in range(S5_WIDTH // LANES)], axis=1)

    def proj(a, b):
        return jnp.dot(h, wb_ref[:, a:b], preferred_element_type=F32)

    def silu(v):
        return v * _sigmoid(v)

    a_in = (on_ref[0] * silu(proj(0, NSA_WIDTH))).astype(BF16)
    b_in = (o_s5 * silu(proj(NSA_WIDTH, NSA_WIDTH + S5_WIDTH))).astype(BF16)
    branch_a = jnp.dot(a_in, wpn_ref[...], preferred_element_type=F32)
    branch_b = jnp.dot(b_in, wps_ref[...], preferred_element_type=F32)
    o1 = NSA_WIDTH + S5_WIDTH
    merged = (_sigmoid(proj(o1, o1 + D_MODEL)) * branch_a
              + _sigmoid(proj(o1 + D_MODEL, o1 + 2 * D_MODEL)) * branch_b)
    y = xv + jnp.dot(merged.astype(BF16), wo_ref[...], preferred_element_type=F32)
    o_ref[0] = _rms_scale(y) * fg_ref[...]


def _final(x, norm_g, w_b, o_nsa, o_s5, wpn, wps, wo, final_g):
    B, T, D = x.shape
    tm = TM_PROJ
    row_blk = lambda w: pl.BlockSpec((1, tm, w), lambda b, i: (b, i, 0))
    full = lambda a: pl.BlockSpec(a.shape, lambda b, i: (0,) * a.ndim)
    return pl.pallas_call(
        _final_kernel,
        grid=(B, T // tm),
        in_specs=[row_blk(D), full(norm_g), full(w_b), row_blk(NSA_WIDTH),
                  pl.BlockSpec((1, tm // S5_L, S5_L * S5_WIDTH), lambda b, i: (b, i, 0)),
                  full(wpn), full(wps), full(wo), full(final_g)],
        out_specs=row_blk(D),
        out_shape=jax.ShapeDtypeStruct((B, T, D), F32),
        scratch_shapes=[pltpu.VMEM((S5_WIDTH // LANES, tm, LANES), F32)],
        compiler_params=pltpu.CompilerParams(
            dimension_semantics=("parallel", "arbitrary"), vmem_limit_bytes=VMEM_LIMIT),
        name="final",
    )(x, norm_g, w_b, o_nsa, o_s5, wpn, wps, wo, final_g)


def _rope_tables(T):
    half = HEAD_DIM // 2
    inv_freq = np.float32(ROPE_THETA) ** (-np.arange(half, dtype=np.float32) / np.float32(half))
    ang = np.arange(T, dtype=np.float32)[:, None] * inv_freq[None, :].astype(np.float32)
    cos, sin = np.cos(ang).astype(np.float32), np.sin(ang).astype(np.float32)
    cos2 = np.concatenate([cos, cos, cos, cos], axis=1)
    sin2 = np.concatenate([-sin, sin, -sin, sin], axis=1)
    return jnp.asarray(cos2), jnp.asarray(sin2)


def _compress_w1(w1):
    half_rows = CMP_STRIDE * HEAD_DIM
    return jnp.concatenate([w1[:half_rows], w1[half_rows:]], axis=1).astype(BF16)


def kernel(x, norm_g, w_in, cmp_pos_k, cmp_pos_v, cmp_w1_k, cmp_w2_k, cmp_w1_v, cmp_w2_v, s5_lam_re, s5_lam_im, s5_log_dt, s5_b_re, s5_b_im, s5_c_re, s5_c_im, s5_d, w_glu, b_glu, w_proj_nsa, w_proj_s5, w_out, final_g):
    B, T, D = x.shape
    assert w_in.shape[0] == 1, "single-layer block"
    NCH = T // CMP_STRIDE
    NS = T // SEL_BLOCK

    w = w_in[0]
    w_a = jnp.concatenate([w[:, :_OFF_GL], jnp.pad(w[:, _OFF_GL:_OFF_GN], ((0, 0), (0, LANES - 24))),
                           w[:, _OFF_U:_OFF_GS]], axis=1).astype(BF16)
    w_b = jnp.concatenate([w[:, _OFF_GN:_OFF_U], w[:, _OFF_GS:]], axis=1).astype(BF16)
    g2 = norm_g[0][None, :]
    cos2, sin2 = _rope_tables(T)

    qq, kc, vc, ksa, vst, kw, vwt, glt, u4 = _inproj(x, g2, w_a, cos2, sin2)

    w2k = jnp.concatenate([jnp.zeros_like(cmp_w2_k[0]), cmp_w2_k[0]], axis=1).astype(BF16)
    w2vt = cmp_w2_v[0].T.astype(BF16)
    kcmp, vcmpt = _compress(kc, vc, _compress_w1(cmp_w1_k[0]), _compress_w1(cmp_w1_v[0]), w2k, w2vt,
                            _pos_bias(cmp_pos_k[0], cmp_w1_k[0]), _pos_bias(cmp_pos_v[0], cmp_w1_v[0]))

    c_start = jnp.arange(NCH) * CMP_STRIDE
    s_start = jnp.arange(NS) * SEL_BLOCK
    ovt = ((c_start[None, :] < s_start[:, None] + SEL_BLOCK) & (c_start[None, :] + CMP_BLOCK > s_start[:, None])
           & (jnp.arange(NCH)[None, :] < NCH - 1)).astype(BF16)
    o_nsa = _nsa(qq, kcmp, vcmpt, ksa, vst, kw, vwt, glt, ovt)

    rep = lambda a: jnp.repeat(a, S5_GROUP, axis=0)
    tr = lambda b: b.transpose(0, 2, 1).reshape(S5_GROUPS * S5_GROUP, S5_STATE)
    a_re, a_im, bb_re, bb_im = _s5_prep(
        rep(s5_lam_re[0]), rep(s5_lam_im[0]),
        rep(jnp.broadcast_to(s5_log_dt[0][:, None], (S5_GROUPS, S5_STATE))),
        tr(s5_b_re[0]), tr(s5_b_im[0]))
    flat = lambda c: c.reshape(S5_GROUPS * S5_GROUP, S5_STATE)
    we, wct, tp, alre, alim = _s5_weights(bb_re, bb_im, a_re, a_im, flat(s5_c_re[0]), flat(s5_c_im[0]))
    o_s5 = _s5(u4, we, wct, tp, alre.reshape(1, -1), alim.reshape(1, -1), s5_d[0][None, :],
               w_glu[0].astype(BF16), b_glu[0][None, :])

    return _final(x, g2, w_b, o_nsa, o_s5, w_proj_nsa[0].astype(BF16), w_proj_s5[0].astype(BF16),
                  w_out[0].astype(BF16), final_g[None, :])
```

```python
import math

import jax
import jax.numpy as jnp
import numpy as np
from jax import lax
from jax.experimental import pallas as pl
from jax.experimental.pallas import tpu as pltpu

F32 = jnp.float32
BF16 = jnp.bfloat16

D_MODEL = 1024
NSA_HEADS = 8
NSA_GROUPS = 2
HEADS_PER_GROUP = 4
HEAD_DIM = 64
NSA_WIDTH = 512
CMP_BLOCK = 32
CMP_STRIDE = 16
CMP_HIDDEN = 256
SEL_BLOCK = 64
SEL_TOPK = 16
WINDOW = 512
ROPE_THETA = 10000.0
FORCED_SCORE = 1.0e4
NEG = -1.0e30
S5_WIDTH = 512
S5_GROUP = 16
S5_GROUPS = 32
S5_STATE = 64
RMS_EPS = 1.0e-6

LANES = 128
SUBLANES = 8
VMEM_LIMIT = 56 * 1024 * 1024

_OFF_GL = 1280
_OFF_GN = 1304
_OFF_U = 1816
_OFF_GS = 2328

TM_PROJ = 512
TQ = 128
TK = 512
NSA_NB = 2
SEL_LOOKAHEAD = 1
V_ROWS = 80
GATE_ROWS = 32
S5_L = 8
S5_CPB = 256 // S5_L
S5_CT = 64
S5_SCAN_SLABS = 4


def _gelu_tanh(x):
    c = math.sqrt(2.0 / math.pi)
    return 0.5 * x * (1.0 + jnp.tanh(c * (x + 0.044715 * (x * x * x))))


def _sigmoid(x):
    return 1.0 / (1.0 + jnp.exp(-x))


def _rms_scale(xv):
    ms = jnp.mean(xv * xv, axis=-1, keepdims=True)
    return xv * lax.rsqrt(ms + RMS_EPS)


def _nt_dot(a, b):
    return lax.dot_general(a, b, (((1,), (1,)), ((), ())), preferred_element_type=F32)


def _inproj_kernel(x_ref, g_ref, w_ref, cos_ref, sin_ref,
                   qq_ref, kc_ref, vc_ref, ks_ref, vs_ref, kw_ref, vw_ref, gl_ref, u_ref, us_scr):
    h = (_rms_scale(x_ref[0]) * g_ref[...]).astype(BF16)
    cos2 = cos_ref[...]
    sin2 = sin_ref[...]
    lane = lax.broadcasted_iota(jnp.int32, cos2.shape, 1)
    first_half = (lane & (HEAD_DIM - 1)) < (HEAD_DIM // 2)
    low = lane < HEAD_DIM

    wide = {}

    def proj(a, b):
        for (s0, s1) in ((0, 512), (512, 1280), (1280, 1920)):
            if s0 <= a and b <= s1:
                if s0 not in wide:
                    wide[s0] = jnp.dot(h, w_ref[:, s0:s1], preferred_element_type=F32)
                return wide[s0][:, a - s0:b - s0]
        raise ValueError((a, b))

    def rope(xs):
        partner = jnp.where(first_half, pltpu.roll(xs, 96, 1), pltpu.roll(xs, 32, 1))
        return xs * cos2 + partner * sin2

    scale = HEAD_DIM ** -0.5 * math.log2(math.e)
    for i in range(NSA_HEADS // 2):
        xs = proj(LANES * i, LANES * (i + 1)) * scale
        xr = rope(xs)
        qq_ref[0, 2 * i] = jnp.where(low, xr, pltpu.roll(xs, 64, 1)).astype(BF16)
        qq_ref[0, 2 * i + 1] = jnp.where(low, pltpu.roll(xr, 64, 1), xs).astype(BF16)

    kc_ref[0] = proj(512, 640)
    vc_ref[0] = proj(640, 768)
    tm = cos2.shape[0]
    t_row = pl.program_id(1) * tm + lax.broadcasted_iota(jnp.int32, cos2.shape, 0)
    blk_onehot = jnp.where(lane - HEAD_DIM == t_row // SEL_BLOCK, 1.0, 0.0)
    ones_rows = jnp.where(lax.broadcasted_iota(jnp.int32, (V_ROWS - HEAD_DIM, tm), 0) == 0, 1.0, 0.0)
    for (off, k_out, v_out, k_pad) in ((768, ks_ref, vs_ref, blk_onehot), (1024, kw_ref, vw_ref, 0.0)):
        kr = rope(proj(off, off + LANES))
        k_out[0, 0] = jnp.where(low, kr, k_pad).astype(BF16)
        k_out[0, 1] = jnp.where(low, pltpu.roll(kr, 64, 1), k_pad).astype(BF16)
        vt = proj(off + LANES, off + 2 * LANES).T
        for g in range(NSA_GROUPS):
            v_out[0, g] = jnp.concatenate([vt[HEAD_DIM * g:HEAD_DIM * (g + 1)], ones_rows], axis=0).astype(BF16)
    gl_ref[0] = _sigmoid(proj(1280, 1408)).T[0:GATE_ROWS]
    uv = proj(1408, 1920)
    for s in range(S5_WIDTH // LANES):
        us_scr[s] = uv[:, LANES * s:LANES * (s + 1)]
    for i in range(S5_L):
        for s in range(S5_WIDTH // LANES):
            c0 = i * S5_WIDTH + LANES * s
            u_ref[0, :, c0:c0 + LANES] = us_scr[s, pl.ds(i, tm // S5_L, stride=S5_L), :]


def _inproj(x, norm_g, w_a, cos2, sin2):
    B, T, D = x.shape
    tm = TM_PROJ
    grid = (B, T // tm)
    row_blk = lambda w: pl.BlockSpec((1, tm, w), lambda b, i: (b, i, 0))
    kv_blk = pl.BlockSpec((1, NSA_GROUPS, tm, LANES), lambda b, i: (b, 0, i, 0))
    kv_shape = jax.ShapeDtypeStruct((B, NSA_GROUPS, T, LANES), BF16)
    vt_blk = pl.BlockSpec((1, NSA_GROUPS, V_ROWS, tm), lambda b, i: (b, 0, 0, i))
    vt_shape = jax.ShapeDtypeStruct((B, NSA_GROUPS, V_ROWS, T), BF16)
    return pl.pallas_call(
        _inproj_kernel,
        grid=grid,
        in_specs=[
            row_blk(D),
            pl.BlockSpec((1, D), lambda b, i: (0, 0)),
            pl.BlockSpec(w_a.shape, lambda b, i: (0, 0)),
            pl.BlockSpec((tm, LANES), lambda b, i: (i, 0)),
            pl.BlockSpec((tm, LANES), lambda b, i: (i, 0)),
        ],
        out_specs=[
            pl.BlockSpec((1, NSA_HEADS, tm, LANES), lambda b, i: (b, 0, i, 0)),
            row_blk(LANES), row_blk(LANES),
            kv_blk, vt_blk, kv_blk, vt_blk,
            pl.BlockSpec((1, GATE_ROWS, tm), lambda b, i: (b, 0, i)),
            pl.BlockSpec((1, tm // S5_L, S5_L * S5_WIDTH), lambda b, i: (b, i, 0)),
        ],
        out_shape=[
            jax.ShapeDtypeStruct((B, NSA_HEADS, T, LANES), BF16),
            jax.ShapeDtypeStruct((B, T, LANES), F32), jax.ShapeDtypeStruct((B, T, LANES), F32),
            kv_shape, vt_shape, kv_shape, vt_shape,
            jax.ShapeDtypeStruct((B, GATE_ROWS, T), F32),
            jax.ShapeDtypeStruct((B, T // S5_L, S5_L * S5_WIDTH), F32),
        ],
        scratch_shapes=[pltpu.VMEM((S5_WIDTH // LANES, tm, LANES), F32)],
        compiler_params=pltpu.CompilerParams(
            dimension_semantics=("parallel", "arbitrary"), vmem_limit_bytes=VMEM_LIMIT),
        name="inproj",
    )(x, norm_g, w_a, cos2, sin2)


def _compress_kernel(kc_ref, vc_ref, w1k_ref, w1v_ref, w2k_ref, w2vt_ref, pbk_ref, pbv_ref, ko_ref, vo_ref):
    nch = ko_ref.shape[2]
    H = CMP_HIDDEN

    def hidden(c_ref, w1_ref, pb_ref):
        acc = [jnp.zeros((nch, 2 * H), F32) for _ in range(NSA_GROUPS)]
        for j in range(CMP_STRIDE):
            rows = c_ref[0, pl.ds(j, nch, stride=CMP_STRIDE), :].astype(BF16)
            wj = w1_ref[HEAD_DIM * j:HEAD_DIM * (j + 1), :]
            for g in range(NSA_GROUPS):
                acc[g] = acc[g] + jnp.dot(rows[:, HEAD_DIM * g:HEAD_DIM * (g + 1)], wj, preferred_element_type=F32)
        return [_gelu_tanh(a[:, 0:H] + pltpu.roll(a[:, H:], nch - 1, 0) + pb_ref[...]).astype(BF16) for a in acc]

    hk = hidden(kc_ref, w1k_ref, pbk_ref)
    hv = hidden(vc_ref, w1v_ref, pbv_ref)
    for g in range(NSA_GROUPS):
        ko_ref[0, g] = jnp.dot(hk[g], w2k_ref[...], preferred_element_type=F32).astype(BF16)
        vo_ref[0, g] = _nt_dot(w2vt_ref[...], hv[g]).astype(BF16)


def _pos_bias_kernel(p_ref, w1_ref, o_ref):
    o_ref[...] = jnp.dot(p_ref[...].astype(BF16), w1_ref[...].astype(BF16), preferred_element_type=F32)


def _pos_bias(pos, w1):
    p8 = jnp.broadcast_to(pos.reshape(1, -1), (SUBLANES, pos.size))
    return pl.pallas_call(_pos_bias_kernel, out_shape=jax.ShapeDtypeStruct((SUBLANES, CMP_HIDDEN), F32),
                          name="pos_bias")(p8, w1)[0:1]


def _compress(kc, vc, w1k, w1v, w2k, w2vt, pbk, pbv):
    B, T, _ = kc.shape
    G = NSA_GROUPS
    nch = T // CMP_STRIDE
    c_blk = pl.BlockSpec((1, T, LANES), lambda b: (b, 0, 0))
    full = lambda a: pl.BlockSpec(a.shape, lambda b: (0,) * a.ndim)
    return pl.pallas_call(
        _compress_kernel,
        grid=(B,),
        in_specs=[c_blk, c_blk, full(w1k), full(w1v), full(w2k), full(w2vt), full(pbk), full(pbv)],
        out_specs=[pl.BlockSpec((1, G, nch, LANES), lambda b: (b, 0, 0, 0)),
                   pl.BlockSpec((1, G, HEAD_DIM, nch), lambda b: (b, 0, 0, 0))],
        out_shape=[jax.ShapeDtypeStruct((B, G, nch, LANES), BF16),
                   jax.ShapeDtypeStruct((B, G, HEAD_DIM, nch), BF16)],
        compiler_params=pltpu.CompilerParams(dimension_semantics=("parallel",), vmem_limit_bytes=VMEM_LIMIT),
        name="compress",
    )(kc, vc, w1k, w1v, w2k, w2vt, pbk, pbv)


class _QTile:
    def __init__(self, x, t0, q_ref, g_ref, o_ref, cols, ncp, n_wc):
        self.x, self.t0, self.q_ref, self.g_ref, self.o_ref = x, t0, q_ref, g_ref, o_ref
        self.t_lane = t0 + (lax.broadcasted_iota(jnp.int32, (1, cols), 1) & (TQ - 1))
        c_end = lax.broadcasted_iota(jnp.int32, (ncp, cols), 0) * CMP_STRIDE + (CMP_BLOCK - 1)
        self.cmp_valid = c_end <= self.t_lane
        self.w_pos = [t0 - WINDOW + TQ * c for c in range(n_wc)]
        self.w_start = [pl.multiple_of(jnp.maximum(p, 0), TQ) for p in self.w_pos]


def _nsa_kernel(qa_ref, qb_ref, kc_ref, vct_ref, ksa_ref, vst_ref, kw_ref, vwt_ref, ga_ref, gb_ref, ovt_ref,
                oa_ref, ob_ref, qsel_scr, acc_scr, m_scr):
    units = [(bb, g) for bb in range(qa_ref.shape[0]) for g in range(NSA_GROUPS)]
    uidx = {u: i for i, u in enumerate(units)}
    n_qt = kw_ref.shape[2] // TQ
    R = HEADS_PER_GROUP
    cols = R * TQ
    NCP = kc_ref.shape[2]
    NS = ovt_ref.shape[0]
    n_wc = (WINDOW + TQ) // TQ
    step = pl.program_id(1)
    tiles = [_QTile(0, step * TQ, qa_ref, ga_ref, oa_ref, cols, NCP, n_wc),
             _QTile(1, (n_qt - 1 - step) * TQ, qb_ref, gb_ref, ob_ref, cols, NCP, n_wc)]
    work = [(c, u) for c in tiles for u in units]
    low = lax.broadcasted_iota(jnp.int32, (cols, LANES), 1) < HEAD_DIM
    sub8 = lax.broadcasted_iota(jnp.int32, (SUBLANES, TQ), 0)
    row_tq = lax.broadcasted_iota(jnp.int32, (TQ, cols), 0)

    def group_q(c, u):
        bb, g = u
        return c.q_ref[bb, R * g:R * (g + 1)].reshape(cols, LANES)

    def cmp_scores(c, u):
        return _nt_dot(kc_ref[u[0], u[1]], group_q(c, u))

    def cmp_probs(c, s):
        s = jnp.where(c.cmp_valid, s, NEG)
        e = jnp.exp2(s - jnp.max(s, axis=0, keepdims=True))
        inv = 1.0 / jnp.maximum(jnp.sum(e, axis=0, keepdims=True), 1.0e-30)
        return e * jnp.where(c.t_lane >= CMP_BLOCK - 1, inv, 0.0)

    def win_scores(c, u):
        kw = jnp.concatenate([kw_ref[u[0], u[1], pl.ds(c.w_start[n], TQ), :] for n in range(n_wc)], axis=0)
        return _nt_dot(kw, group_q(c, u))

    def win_probs(c, sw):
        parts = []
        for n in range(n_wc):
            sc = sw[TQ * n:TQ * (n + 1)]
            if n == 0:
                sc = jnp.where(c.w_pos[0] + row_tq > c.t_lane - WINDOW, sc, NEG)
            if n == n_wc - 1:
                sc = jnp.where(c.t0 + row_tq <= c.t_lane, sc, NEG)
            else:
                sc = jnp.where(c.w_pos[n] >= 0, sc, NEG)
            parts.append(sc.astype(BF16))
        sw = jnp.concatenate(parts, axis=0)
        return jnp.exp2(sw - jnp.max(sw, axis=0, keepdims=True))

    def win_out(c, u, ew):
        vw = jnp.concatenate([vwt_ref[u[0], u[1], :, pl.ds(c.w_start[n], TQ)] for n in range(n_wc)], axis=1)
        ow = jnp.dot(vw, ew, preferred_element_type=F32)
        return ow[0:HEAD_DIM] * (1.0 / ow[HEAD_DIM:HEAD_DIM + 1])

    def select_blocks(c, u, p):
        psum = p[:, 0:TQ] + p[:, TQ:2 * TQ] + p[:, 2 * TQ:3 * TQ] + p[:, 3 * TQ:4 * TQ]
        p_hi = psum.astype(BF16)
        p_lo = (psum - p_hi.astype(F32)).astype(BF16)
        imp = (jnp.dot(ovt_ref[...], p_hi, preferred_element_type=F32)
               + jnp.dot(ovt_ref[...], p_lo, preferred_element_type=F32))
        blk = lax.broadcasted_iota(jnp.int32, (NS, TQ), 0)
        t_l = c.t0 + lax.broadcasted_iota(jnp.int32, (NS, TQ), 1)
        cur = t_l // SEL_BLOCK
        imp = jnp.where(blk * SEL_BLOCK <= t_l, imp, -1.0)
        imp = jnp.where(blk == 0, FORCED_SCORE, imp)
        imp = jnp.where(blk == cur, FORCED_SCORE, imp)
        imp = jnp.where(blk == cur - 1, FORCED_SCORE, imp)
        nv = NS // SUBLANES
        imp8 = [imp[SUBLANES * j:SUBLANES * (j + 1)] for j in range(nv)]
        rank8 = [jnp.zeros((SUBLANES, TQ), F32) for _ in range(nv)]
        for mm in range(NS):
            row = imp[mm:mm + 1, :]
            jm = mm // SUBLANES
            for j in range(nv):
                if j < jm:
                    ahead = jnp.where(row > imp8[j], 1.0, 0.0)
                elif j > jm:
                    ahead = jnp.where(row >= imp8[j], 1.0, 0.0)
                else:
                    tie = jnp.where(sub8 > (mm % SUBLANES), 1.0, 0.0)
                    ahead = jnp.where(row > imp8[j], 1.0, 0.0) + jnp.where(row == imp8[j], tie, 0.0)
                rank8[j] = rank8[j] + ahead
        rank = jnp.concatenate(rank8, axis=0)
        pen = jnp.where(rank < float(SEL_TOPK), 0.0, NEG)
        q_t = group_q(c, u).astype(F32).T.astype(BF16)
        qsel_scr[c.x, uidx[u]] = jnp.concatenate(
            [q_t[0:HEAD_DIM], jnp.concatenate([pen.astype(BF16)] * R, axis=1)], axis=0)

    p_c, o_cmp, o_win = {}, {}, {}

    def cmp_job(c, u):
        def finish(p, _):
            p_c[c.x, u] = p
            o_cmp[c.x, u] = jnp.dot(vct_ref[u[0], u[1]], p.astype(BF16), preferred_element_type=F32)
        return (lambda: cmp_scores(c, u)), (lambda s: (cmp_probs(c, s), None)), finish

    def win_job(c, u):
        def finish(e_w, _):
            o_win[c.x, u] = win_out(c, u, e_w)
        return (lambda: win_scores(c, u)), (lambda s: (win_probs(c, s), None)), finish

    def sel_job(x, kt, u, t_mask):
        k0 = kt * TK if isinstance(kt, int) else pl.multiple_of(kt * TK, TK)

        def probs(sc):
            if t_mask is not None:
                sc = jnp.where(kt * TK + lax.broadcasted_iota(jnp.int32, (TK, cols), 0) <= t_mask, sc, NEG)
            sc = sc.astype(BF16)
            m_old = m_scr[x, uidx[u]]
            m_new = jnp.maximum(m_old, jnp.max(sc, axis=0, keepdims=True).astype(F32))
            m_scr[x, uidx[u]] = m_new
            return jnp.exp2(sc - m_new.astype(BF16)), jnp.exp2(m_old - m_new)

        def finish(pp, alpha):
            acc_scr[x, uidx[u]] = acc_scr[x, uidx[u]] * alpha + jnp.dot(
                vst_ref[u[0], u[1], :, pl.ds(k0, TK)], pp, preferred_element_type=F32)

        return (lambda: jnp.dot(ksa_ref[u[0], u[1], pl.ds(k0, TK), :], qsel_scr[x, uidx[u]],
                                preferred_element_type=F32)), probs, finish

    def fuse(js):
        return ((lambda: [j[0]() for j in js]),
                (lambda ss: ([j[1](s) for j, s in zip(js, ss)], None)),
                (lambda outs, _: [j[2](*o) for j, o in zip(js, outs)]))

    def sel_tile(x, kt, t_mask):
        return [fuse([sel_job(x, kt, (bb, g), t_mask) for g in range(NSA_GROUPS)]) for bb in range(qa_ref.shape[0])]

    early, late = tiles
    n_slots = (n_qt - 1) * TQ // TK
    n_static = n_slots - n_slots // 2
    n_late = late.t0 // TK
    plan = [cmp_job(c, u) for c in (late, early) for u in units]
    for u in units:
        plan += [lambda u=u: select_blocks(late, u, p_c[late.x, u]), win_job(late, u)]
    plan += sel_tile(late.x, n_late, late.t_lane)
    for s in range(n_static):
        plan += sel_tile(late.x, s, None)
        for u in units[s::n_static]:
            plan += [lambda u=u: select_blocks(early, u, p_c[early.x, u]), win_job(early, u)]
    plan += sel_tile(early.x, early.t0 // TK, early.t_lane)
    for s in range(n_static, n_slots):
        is_late = s < n_late
        plan += sel_tile(jnp.where(is_late, late.x, early.x), jnp.where(is_late, s, s - n_late), None)

    acc_scr[...] = jnp.zeros_like(acc_scr)
    m_scr[...] = jnp.full(m_scr.shape, NEG, F32)
    job_pos = [k for k, e in enumerate(plan) if isinstance(e, tuple)]
    following = dict(zip(job_pos, job_pos[1:]))
    issued = {job_pos[0]: plan[job_pos[0]][0]()}
    for k, entry in enumerate(plan):
        if not isinstance(entry, tuple):
            entry()
            continue
        if k in following:
            issued[following[k]] = plan[following[k]][0]()
        _, probs, finish = entry
        finish(*probs(issued.pop(k)))

    for c in tiles:
        for bb in range(qa_ref.shape[0]):
            glt = c.g_ref[bb]
            heads = []
            for g in range(NSA_GROUPS):
                acc = acc_scr[c.x, uidx[bb, g]]
                o_sel = acc[0:HEAD_DIM] * (1.0 / acc[HEAD_DIM:HEAD_DIM + 1])
                for r in range(R):
                    hh = R * g + r
                    sl = slice(r * TQ, (r + 1) * TQ)
                    heads.append(glt[3 * hh:3 * hh + 1] * o_cmp[c.x, (bb, g)][:, sl]
                                 + glt[3 * hh + 1:3 * hh + 2] * o_sel[:, sl]
                                 + glt[3 * hh + 2:3 * hh + 3] * o_win[c.x, (bb, g)][:, sl])
            c.o_ref[bb] = jnp.concatenate(heads, axis=0).T


def _nsa(qq, kcmp, vcmpt, ksa, vst, kw, vwt, glt, ovt):
    B, H, T, _ = qq.shape
    G = NSA_GROUPS
    NB = NSA_NB
    NCP = kcmp.shape[2]
    n_qt = T // TQ
    grid = (B // NB, n_qt // 2)
    k_blk = lambda n: pl.BlockSpec((NB, G, n, LANES), lambda b, i: (b, 0, 0, 0))
    vt_blk = lambda r, n: pl.BlockSpec((NB, G, r, n), lambda b, i: (b, 0, 0, 0))
    lo_tile = lambda b, i: i
    hi_tile = lambda b, i: n_qt - 1 - i
    q_blk = lambda tile: pl.BlockSpec((NB, H, TQ, LANES), lambda b, i: (b, 0, tile(b, i), 0))
    g_blk = lambda tile: pl.BlockSpec((NB, GATE_ROWS, TQ), lambda b, i: (b, 0, tile(b, i)))
    half = jax.ShapeDtypeStruct((B, T // 2, NSA_WIDTH), F32)
    return pl.pallas_call(
        _nsa_kernel,
        grid=grid,
        in_specs=[
            q_blk(lo_tile), q_blk(hi_tile),
            k_blk(NCP), vt_blk(HEAD_DIM, NCP), k_blk(T), vt_blk(V_ROWS, T), k_blk(T), vt_blk(V_ROWS, T),
            g_blk(lo_tile), g_blk(hi_tile),
            pl.BlockSpec(ovt.shape, lambda b, i: (0, 0)),
        ],
        out_specs=[pl.BlockSpec((NB, TQ, NSA_WIDTH), lambda b, i: (b, i, 0)),
                   pl.BlockSpec((NB, TQ, NSA_WIDTH), lambda b, i: (b, n_qt // 2 - 1 - i, 0))],
        out_shape=[half, half],
        scratch_shapes=[pltpu.VMEM((2, NB * G, LANES, HEADS_PER_GROUP * TQ), BF16),
                        pltpu.VMEM((2, NB * G, V_ROWS, HEADS_PER_GROUP * TQ), F32),
                        pltpu.VMEM((2, NB * G, 1, HEADS_PER_GROUP * TQ), F32)],
        compiler_params=pltpu.CompilerParams(
            dimension_semantics=("parallel", "arbitrary"), vmem_limit_bytes=VMEM_LIMIT),
        name="nsa",
    )(qq, qq, kcmp, vcmpt, ksa, vst, kw, vwt, glt, glt, ovt)


def _s5_prep_kernel(lre_ref, lim_ref, ldt_ref, bre_ref, bim_ref, are_ref, aim_ref, bbre_ref, bbim_ref):
    lre, lim = lre_ref[...], lim_ref[...]
    dt = jnp.exp(ldt_ref[...])
    mag = jnp.exp(lre * dt)
    a_re = mag * jnp.cos(lim * dt)
    a_im = mag * jnp.sin(lim * dt)
    den = lre * lre + lim * lim
    z_re = ((a_re - 1.0) * lre + a_im * lim) / den
    z_im = (a_im * lre - (a_re - 1.0) * lim) / den
    are_ref[...] = a_re
    aim_ref[...] = a_im
    bbre_ref[...] = z_re * bre_ref[...] - z_im * bim_ref[...]
    bbim_ref[...] = z_re * bim_ref[...] + z_im * bre_ref[...]


def _s5_prep(lre, lim, ldt, bre, bim):
    shp = jax.ShapeDtypeStruct(lre.shape, F32)
    return pl.pallas_call(_s5_prep_kernel, out_shape=[shp, shp, shp, shp], name="s5_prep")(lre, lim, ldt, bre, bim)


def _s5_weights_kernel(bbre_ref, bbim_ref, are_ref, aim_ref, crt_ref, cit_ref,
                       we_ref, wct_ref, tp_ref, alre_ref, alim_ref, bd_scr, arow_scr, tp_scr):
    gpb = S5_CPB // S5_GROUP
    bd_scr[...] = jnp.zeros_like(bd_scr)
    tp_scr[...] = jnp.zeros_like(tp_scr)
    for n, ref in enumerate((bbre_ref, bbim_ref, crt_ref, cit_ref)):
        for gl in range(gpb):
            bd_scr[n, S5_GROUP * gl:S5_GROUP * (gl + 1), S5_STATE * gl:S5_STATE * (gl + 1)] = (
                ref[S5_GROUP * gl:S5_GROUP * (gl + 1), :])
    for n, ref in enumerate((are_ref, aim_ref)):
        for gl in range(gpb):
            arow_scr[n, :, S5_STATE * gl:S5_STATE * (gl + 1)] = ref[S5_GROUP * gl:S5_GROUP * gl + 1, :]
    bbre, bbim, crt, cit = bd_scr[0], bd_scr[1], bd_scr[2], bd_scr[3]
    are, aim = arow_scr[0], arow_scr[1]
    pre, pim = jnp.ones_like(are), jnp.zeros_like(are)
    for k in range(S5_L):
        bpr = bbre * pre - bbim * pim
        bpi = bbre * pim + bbim * pre
        i = S5_L - 1 - k
        we_ref[0, S5_CPB * i:S5_CPB * (i + 1), :] = jnp.concatenate([bpr, bpi], axis=1).astype(BF16)
        tap = (_nt_dot(bpr.astype(BF16), crt.astype(BF16)) - _nt_dot(bpi.astype(BF16), cit.astype(BF16)))
        for i in range(S5_L - k):
            j = i + k
            tp_scr[S5_CPB * i:S5_CPB * (i + 1), S5_CPB * j:S5_CPB * (j + 1)] = tap
        pre, pim = pre * are - pim * aim, pre * aim + pim * are
        wct_ref[0, S5_CPB * k:S5_CPB * (k + 1), :] = jnp.concatenate(
            [crt * pre - cit * pim, -(crt * pim + cit * pre)], axis=1).astype(BF16)
    tp_ref[0] = tp_scr[...].astype(BF16)
    alre_ref[0] = pre
    alim_ref[0] = pim


def _s5_weights(bbre, bbim, are, aim, crt, cit):
    nblk = S5_WIDTH // S5_CPB
    sb = S5_CPB // S5_GROUP * S5_STATE
    lc = S5_L * S5_CPB
    rows = pl.BlockSpec((S5_CPB, S5_STATE), lambda q: (q, 0))
    blk = lambda r, c: pl.BlockSpec((1, r, c), lambda q: (q, 0, 0))
    return pl.pallas_call(
        _s5_weights_kernel,
        grid=(nblk,),
        in_specs=[rows] * 6,
        out_specs=[blk(lc, 2 * sb), blk(lc, 2 * sb), blk(lc, lc), blk(1, sb), blk(1, sb)],
        scratch_shapes=[pltpu.VMEM((4, S5_CPB, sb), F32), pltpu.VMEM((2, 1, sb), F32), pltpu.VMEM((lc, lc), F32)],
        out_shape=[jax.ShapeDtypeStruct((nblk, lc, 2 * sb), BF16),
                   jax.ShapeDtypeStruct((nblk, lc, 2 * sb), BF16),
                   jax.ShapeDtypeStruct((nblk, lc, lc), BF16),
                   jax.ShapeDtypeStruct((nblk, 1, sb), F32), jax.ShapeDtypeStruct((nblk, 1, sb), F32)],
        compiler_params=pltpu.CompilerParams(dimension_semantics=("parallel",), vmem_limit_bytes=VMEM_LIMIT),
        name="s5_weights",
    )(bbre, bbim, are, aim, crt, cit)


def _s5_kernel(u_ref, we_ref, wct_ref, tp_ref, alre_ref, alim_ref, d_ref, wg_ref, bg_ref, o_ref, e_scr, st_scr):
    nb, ct, _ = u_ref.shape
    L = S5_L
    nblk = S5_WIDTH // S5_CPB
    sb = S5_CPB // S5_GROUP * S5_STATE
    spb = 2 * sb // LANES
    W = S5_SCAN_SLABS * LANES

    @pl.when(pl.program_id(0) == 0)
    def _():
        st_scr[...] = jnp.zeros_like(st_scr)

    uf = u_ref[...].reshape(nb * ct, L * S5_WIDTH)
    ub = uf.astype(BF16)
    u_blk = [jnp.concatenate([ub[:, i * S5_WIDTH + S5_CPB * q:i * S5_WIDTH + S5_CPB * (q + 1)] for i in range(L)],
                             axis=1) for q in range(nblk)]

    for q in range(nblk):
        e = jnp.dot(u_blk[q], we_ref[q], preferred_element_type=F32)
        for s in range(spb):
            for b in range(nb):
                e_scr[spb * q + s, pl.ds(b, ct, stride=nb), :] = e[b * ct:(b + 1) * ct, LANES * s:LANES * (s + 1)]

    def slabs_of(cs):
        q, r = divmod(cs * LANES, sb)
        return spb * q + r // LANES, spb * q + (sb + r) // LANES

    for j in range(S5_GROUPS * S5_STATE // W):
        sl = [slabs_of(S5_SCAN_SLABS * j + n) for n in range(S5_SCAN_SLABS)]
        sl_re, sl_im = [s[0] for s in sl], [s[1] for s in sl]
        ar = alre_ref[:, W * j:W * (j + 1)]
        ai = alim_ref[:, W * j:W * (j + 1)]
        load = lambda r0, rows, slabs: jnp.concatenate([e_scr[s, pl.ds(r0, rows), :] for s in slabs], axis=1)
        state = lambda slabs: jnp.concatenate([st_scr[:, LANES * s:LANES * (s + 1)] for s in slabs], axis=1)
        sr, si = state(sl_re), state(sl_im)
        for k in range(ct // 2):
            r0 = k * 2 * nb
            er, ei = load(r0, 2 * nb, sl_re), load(r0, 2 * nb, sl_im)
            tr = ar * sr - ai * si + er[0:nb]
            ti = ar * si + ai * sr + ei[0:nb]
            xr = jnp.concatenate([sr, tr], axis=0)
            xi = jnp.concatenate([si, ti], axis=0)
            for n in range(S5_SCAN_SLABS):
                e_scr[sl_re[n], pl.ds(r0, 2 * nb), :] = xr[:, LANES * n:LANES * (n + 1)]
                e_scr[sl_im[n], pl.ds(r0, 2 * nb), :] = xi[:, LANES * n:LANES * (n + 1)]
            sr, si = ar * tr - ai * ti + er[nb:], ar * ti + ai * tr + ei[nb:]
        for n in range(S5_SCAN_SLABS):
            st_scr[:, LANES * sl_re[n]:LANES * (sl_re[n] + 1)] = sr[:, LANES * n:LANES * (n + 1)]
            st_scr[:, LANES * sl_im[n]:LANES * (sl_im[n] + 1)] = si[:, LANES * n:LANES * (n + 1)]

    ys = []
    for q in range(nblk):
        per_b = [jnp.concatenate([e_scr[spb * q + s, pl.ds(b, ct, stride=nb), :] for s in range(spb)], axis=1)
                 for b in range(nb)]
        x_in = jnp.concatenate(per_b, axis=0).astype(BF16)
        ys.append((_nt_dot(x_in, wct_ref[q]) + jnp.dot(u_blk[q], tp_ref[q], preferred_element_type=F32)).astype(BF16))
    outs = []
    for j in range(L):
        yj = jnp.concatenate([y[:, S5_CPB * j:S5_CPB * (j + 1)] for y in ys], axis=1).astype(F32)
        yj = yj + d_ref[...] * uf[:, j * S5_WIDTH:(j + 1) * S5_WIDTH]
        z = _gelu_tanh(yj)
        gate = jnp.dot(z.astype(BF16), wg_ref[...], preferred_element_type=F32) + bg_ref[...]
        outs.append(z * _sigmoid(gate))
    o_ref[...] = jnp.concatenate(outs, axis=1).reshape(nb, ct, L * S5_WIDTH)


def _s5(u4, we, wct, tp, alre, alim, d, w_glu, b_glu):
    nb, nchunks, w4 = u4.shape
    ct = S5_CT
    n_slabs = 2 * S5_GROUPS * S5_STATE // LANES
    resident = lambda a: pl.BlockSpec(a.shape, lambda i: (0,) * a.ndim, pipeline_mode=pl.Buffered(1))
    u_blk = pl.BlockSpec((nb, ct, w4), lambda i: (0, i, 0))
    return pl.pallas_call(
        _s5_kernel,
        grid=(nchunks // ct,),
        in_specs=[u_blk, resident(we), resident(wct), resident(tp), resident(alre), resident(alim),
                  resident(d), resident(w_glu), resident(b_glu)],
        out_specs=u_blk,
        out_shape=jax.ShapeDtypeStruct(u4.shape, F32),
        scratch_shapes=[pltpu.VMEM((n_slabs, nb * ct, LANES), F32), pltpu.VMEM((nb, n_slabs * LANES), F32)],
        compiler_params=pltpu.CompilerParams(dimension_semantics=("arbitrary",), vmem_limit_bytes=VMEM_LIMIT),
        name="s5",
    )(u4, we, wct, tp, alre, alim, d, w_glu, b_glu)


def _final_kernel(x_ref, g_ref, wb_ref, on_ref, os_ref, wpn_ref, wps_ref, wo_ref, fg_ref, o_ref, os_scr):
    xv = x_ref[0]
    h = (_rms_scale(xv) * g_ref[...]).astype(BF16)
    tm = xv.shape[0]
    o4 = os_ref[0]
    for i in range(S5_L):
        for s in range(S5_WIDTH // LANES):
            c0 = i * S5_WIDTH + LANES * s
            os_scr[s, pl.ds(i, tm // S5_L, stride=S5_L), :] = o4[:, c0:c0 + LANES]
    o_s5 = jnp.concatenate([os_scr[s] for s in range(S5_WIDTH // LANES)], axis=1)

    def proj(a, b):
        return jnp.dot(h, wb_ref[:, a:b], preferred_element_type=F32)

    def silu(v):
        return v * _sigmoid(v)

    a_in = (on_ref[0] * silu(proj(0, NSA_WIDTH))).astype(BF16)
    b_in = (o_s5 * silu(proj(NSA_WIDTH, NSA_WIDTH + S5_WIDTH))).astype(BF16)
    branch_a = jnp.dot(a_in, wpn_ref[...], preferred_element_type=F32)
    branch_b = jnp.dot(b_in, wps_ref[...], preferred_element_type=F32)
    o1 = NSA_WIDTH + S5_WIDTH
    merged = (_sigmoid(proj(o1, o1 + D_MODEL)) * branch_a
              + _sigmoid(proj(o1 + D_MODEL, o1 + 2 * D_MODEL)) * branch_b)
    y = xv + jnp.dot(merged.astype(BF16), wo_ref[...], preferred_element_type=F32)
    o_ref[0] = _rms_scale(y) * fg_ref[...]


def _final(x, norm_g, w_b, o_nsa, o_s5, wpn, wps, wo, final_g):
    B, T, D = x.shape
    tm = TM_PROJ
    row_blk = lambda w: pl.BlockSpec((1, tm, w), lambda b, i: (b, i, 0))
    full = lambda a: pl.BlockSpec(a.shape, lambda b, i: (0,) * a.ndim)
    return pl.pallas_call(
        _final_kernel,
        grid=(B, T // tm),
        in_specs=[row_blk(D), full(norm_g), full(w_b), row_blk(NSA_WIDTH),
                  pl.BlockSpec((1, tm // S5_L, S5_L * S5_WIDTH), lambda b, i: (b, i, 0)),
                  full(wpn), full(wps), full(wo), full(final_g)],
        out_specs=row_blk(D),
        out_shape=jax.ShapeDtypeStruct((B, T, D), F32),
        scratch_shapes=[pltpu.VMEM((S5_WIDTH // LANES, tm, LANES), F32)],
        compiler_params=pltpu.CompilerParams(
            dimension_semantics=("parallel", "arbitrary"), vmem_limit_bytes=VMEM_LIMIT),
        name="final",
    )(x, norm_g, w_b, o_nsa, o_s5, wpn, wps, wo, final_g)


def _rope_tables(T):
    half = HEAD_DIM // 2
    inv_freq = np.float32(ROPE_THETA) ** (-np.arange(half, dtype=np.float32) / np.float32(half))
    ang = np.arange(T, dtype=np.float32)[:, None] * inv_freq[None, :].astype(np.float32)
    cos, sin = np.cos(ang).astype(np.float32), np.sin(ang).astype(np.float32)
    cos2 = np.concatenate([cos, cos, cos, cos], axis=1)
    sin2 = np.concatenate([-sin, sin, -sin, sin], axis=1)
    return jnp.asarray(cos2), jnp.asarray(sin2)


def _compress_w1(w1):
    half_rows = CMP_STRIDE * HEAD_DIM
    return jnp.concatenate([w1[:half_rows], w1[half_rows:]], axis=1).astype(BF16)


def kernel(x, norm_g, w_in, cmp_pos_k, cmp_pos_v, cmp_w1_k, cmp_w2_k, cmp_w1_v, cmp_w2_v, s5_lam_re, s5_lam_im, s5_log_dt, s5_b_re, s5_b_im, s5_c_re, s5_c_im, s5_d, w_glu, b_glu, w_proj_nsa, w_proj_s5, w_out, final_g):
    B, T, D = x.shape
    assert w_in.shape[0] == 1, "single-layer block"
    NCH = T // CMP_STRIDE
    NS = T // SEL_BLOCK

    w = w_in[0]
    w_a = jnp.concatenate([w[:, :_OFF_GL], jnp.pad(w[:, _OFF_GL:_OFF_GN], ((0, 0), (0, LANES - 24))),
                           w[:, _OFF_U:_OFF_GS]], axis=1).astype(BF16)
    w_b = jnp.concatenate([w[:, _OFF_GN:_OFF_U], w[:, _OFF_GS:]], axis=1).astype(BF16)
    g2 = norm_g[0][None, :]
    cos2, sin2 = _rope_tables(T)

    qq, kc, vc, ksa, vst, kw, vwt, glt, u4 = _inproj(x, g2, w_a, cos2, sin2)

    w2k = jnp.concatenate([jnp.zeros_like(cmp_w2_k[0]), cmp_w2_k[0]], axis=1).astype(BF16)
    w2vt = cmp_w2_v[0].T.astype(BF16)
    kcmp, vcmpt = _compress(kc, vc, _compress_w1(cmp_w1_k[0]), _compress_w1(cmp_w1_v[0]), w2k, w2vt,
                            _pos_bias(cmp_pos_k[0], cmp_w1_k[0]), _pos_bias(cmp_pos_v[0], cmp_w1_v[0]))

    c_start = jnp.arange(NCH) * CMP_STRIDE
    s_start = jnp.arange(NS) * SEL_BLOCK
    ovt = ((c_start[None, :] < s_start[:, None] + SEL_BLOCK) & (c_start[None, :] + CMP_BLOCK > s_start[:, None])
           & (jnp.arange(NCH)[None, :] < NCH - 1)).astype(BF16)
    o_nsa = jnp.concatenate(_nsa(qq, kcmp, vcmpt, ksa, vst, kw, vwt, glt, ovt), axis=1)

    rep = lambda a: jnp.repeat(a, S5_GROUP, axis=0)
    tr = lambda b: b.transpose(0, 2, 1).reshape(S5_GROUPS * S5_GROUP, S5_STATE)
    a_re, a_im, bb_re, bb_im = _s5_prep(
        rep(s5_lam_re[0]), rep(s5_lam_im[0]),
        rep(jnp.broadcast_to(s5_log_dt[0][:, None], (S5_GROUPS, S5_STATE))),
        tr(s5_b_re[0]), tr(s5_b_im[0]))
    flat = lambda c: c.reshape(S5_GROUPS * S5_GROUP, S5_STATE)
    we, wct, tp, alre, alim = _s5_weights(bb_re, bb_im, a_re, a_im, flat(s5_c_re[0]), flat(s5_c_im[0]))
    o_s5 = _s5(u4, we, wct, tp, alre.reshape(1, -1), alim.reshape(1, -1), s5_d[0][None, :],
               w_glu[0].astype(BF16), b_glu[0][None, :])

    return _final(x, g2, w_b, o_nsa, o_s5, w_proj_nsa[0].astype(BF16), w_proj_s5[0].astype(BF16),
                  w_out[0].astype(BF16), final_g[None, :])
```

```python
import math

import jax
import jax.numpy as jnp
import numpy as np
from jax import lax
from jax.experimental import pallas as pl
from jax.experimental.pallas import tpu as pltpu

F32 = jnp.float32
BF16 = jnp.bfloat16

D_MODEL = 1024
NSA_HEADS = 8
NSA_GROUPS = 2
HEADS_PER_GROUP = 4
HEAD_DIM = 64
NSA_WIDTH = 512
CMP_BLOCK = 32
CMP_STRIDE = 16
CMP_HIDDEN = 256
SEL_BLOCK = 64
SEL_TOPK = 16
WINDOW = 512
ROPE_THETA = 10000.0
FORCED_SCORE = 1.0e4
NEG = -1.0e30
S5_WIDTH = 512
S5_GROUP = 16
S5_GROUPS = 32
S5_STATE = 64
RMS_EPS = 1.0e-6

LANES = 128
SUBLANES = 8
VMEM_LIMIT = 56 * 1024 * 1024

_OFF_GL = 1280
_OFF_GN = 1304
_OFF_U = 1816
_OFF_GS = 2328

TM_PROJ = 512
FINAL_SUB = 2
TQ = 128
TK = 512
NSA_NB = 2
SEL_LOOKAHEAD = 1
V_ROWS = 80
GATE_ROWS = 32
S5_L = 8
S5_CPB = 256 // S5_L
S5_CT = 64
S5_SCAN_SLABS = 4


def _gelu_tanh(x):
    c = math.sqrt(2.0 / math.pi)
    return 0.5 * x * (1.0 + jnp.tanh(c * (x + 0.044715 * (x * x * x))))


def _sigmoid(x):
    return 1.0 / (1.0 + jnp.exp(-x))


def _rms_scale(xv):
    ms = jnp.mean(xv * xv, axis=-1, keepdims=True)
    return xv * lax.rsqrt(ms + RMS_EPS)


def _nt_dot(a, b):
    return lax.dot_general(a, b, (((1,), (1,)), ((), ())), preferred_element_type=F32)


def _inproj_kernel(x_ref, g_ref, w_ref, cos_ref, sin_ref,
                   qq_ref, kc_ref, vc_ref, ks_ref, vs_ref, kw_ref, vw_ref, gl_ref, u_ref, us_scr):
    h = (_rms_scale(x_ref[0]) * g_ref[...]).astype(BF16)
    cos2 = cos_ref[...]
    sin2 = sin_ref[...]
    lane = lax.broadcasted_iota(jnp.int32, cos2.shape, 1)
    first_half = (lane & (HEAD_DIM - 1)) < (HEAD_DIM // 2)
    low = lane < HEAD_DIM

    wide = {}

    def proj(a, b):
        for (s0, s1) in ((0, 512), (512, 1280), (1280, 1920)):
            if s0 <= a and b <= s1:
                if s0 not in wide:
                    wide[s0] = jnp.dot(h, w_ref[:, s0:s1], preferred_element_type=F32)
                return wide[s0][:, a - s0:b - s0]
        raise ValueError((a, b))

    def rope(xs):
        partner = jnp.where(first_half, pltpu.roll(xs, 96, 1), pltpu.roll(xs, 32, 1))
        return xs * cos2 + partner * sin2

    scale = HEAD_DIM ** -0.5 * math.log2(math.e)
    for i in range(NSA_HEADS // 2):
        xs = proj(LANES * i, LANES * (i + 1)) * scale
        xr = rope(xs)
        qq_ref[0, 2 * i] = jnp.where(low, xr, pltpu.roll(xs, 64, 1)).astype(BF16)
        qq_ref[0, 2 * i + 1] = jnp.where(low, pltpu.roll(xr, 64, 1), xs).astype(BF16)

    kc_ref[0] = proj(512, 640)
    vc_ref[0] = proj(640, 768)
    tm = cos2.shape[0]
    t_row = pl.program_id(1) * tm + lax.broadcasted_iota(jnp.int32, cos2.shape, 0)
    blk_onehot = jnp.where(lane - HEAD_DIM == t_row // SEL_BLOCK, 1.0, 0.0)
    ones_rows = jnp.where(lax.broadcasted_iota(jnp.int32, (V_ROWS - HEAD_DIM, tm), 0) == 0, 1.0, 0.0)
    for (off, k_out, v_out, k_pad) in ((768, ks_ref, vs_ref, blk_onehot), (1024, kw_ref, vw_ref, 0.0)):
        kr = rope(proj(off, off + LANES))
        k_out[0, 0] = jnp.where(low, kr, k_pad).astype(BF16)
        k_out[0, 1] = jnp.where(low, pltpu.roll(kr, 64, 1), k_pad).astype(BF16)
        vt = proj(off + LANES, off + 2 * LANES).T
        for g in range(NSA_GROUPS):
            v_out[0, g] = jnp.concatenate([vt[HEAD_DIM * g:HEAD_DIM * (g + 1)], ones_rows], axis=0).astype(BF16)
    gl_ref[0] = _sigmoid(proj(1280, 1408)).T[0:GATE_ROWS]
    uv = proj(1408, 1920)
    for s in range(S5_WIDTH // LANES):
        us_scr[s] = uv[:, LANES * s:LANES * (s + 1)]
    for i in range(S5_L):
        for s in range(S5_WIDTH // LANES):
            c0 = i * S5_WIDTH + LANES * s
            u_ref[0, :, c0:c0 + LANES] = us_scr[s, pl.ds(i, tm // S5_L, stride=S5_L), :]


def _inproj(x, norm_g, w_a, cos2, sin2):
    B, T, D = x.shape
    tm = TM_PROJ
    grid = (B, T // tm)
    row_blk = lambda w: pl.BlockSpec((1, tm, w), lambda b, i: (b, i, 0))
    kv_blk = pl.BlockSpec((1, NSA_GROUPS, tm, LANES), lambda b, i: (b, 0, i, 0))
    kv_shape = jax.ShapeDtypeStruct((B, NSA_GROUPS, T, LANES), BF16)
    vt_blk = pl.BlockSpec((1, NSA_GROUPS, V_ROWS, tm), lambda b, i: (b, 0, 0, i))
    vt_shape = jax.ShapeDtypeStruct((B, NSA_GROUPS, V_ROWS, T), BF16)
    return pl.pallas_call(
        _inproj_kernel,
        grid=grid,
        in_specs=[
            row_blk(D),
            pl.BlockSpec((1, D), lambda b, i: (0, 0)),
            pl.BlockSpec(w_a.shape, lambda b, i: (0, 0)),
            pl.BlockSpec((tm, LANES), lambda b, i: (i, 0)),
            pl.BlockSpec((tm, LANES), lambda b, i: (i, 0)),
        ],
        out_specs=[
            pl.BlockSpec((1, NSA_HEADS, tm, LANES), lambda b, i: (b, 0, i, 0)),
            row_blk(LANES), row_blk(LANES),
            kv_blk, vt_blk, kv_blk, vt_blk,
            pl.BlockSpec((1, GATE_ROWS, tm), lambda b, i: (b, 0, i)),
            pl.BlockSpec((1, tm // S5_L, S5_L * S5_WIDTH), lambda b, i: (b, i, 0)),
        ],
        out_shape=[
            jax.ShapeDtypeStruct((B, NSA_HEADS, T, LANES), BF16),
            jax.ShapeDtypeStruct((B, T, LANES), F32), jax.ShapeDtypeStruct((B, T, LANES), F32),
            kv_shape, vt_shape, kv_shape, vt_shape,
            jax.ShapeDtypeStruct((B, GATE_ROWS, T), F32),
            jax.ShapeDtypeStruct((B, T // S5_L, S5_L * S5_WIDTH), F32),
        ],
        scratch_shapes=[pltpu.VMEM((S5_WIDTH // LANES, tm, LANES), F32)],
        compiler_params=pltpu.CompilerParams(
            dimension_semantics=("parallel", "arbitrary"), vmem_limit_bytes=VMEM_LIMIT),
        name="inproj",
    )(x, norm_g, w_a, cos2, sin2)


def _compress_kernel(kc_ref, vc_ref, w1k_ref, w1v_ref, w2k_ref, w2vt_ref, pbk_ref, pbv_ref, ko_ref, vo_ref):
    nch = ko_ref.shape[2]
    H = CMP_HIDDEN

    def hidden(c_ref, w1_ref, pb_ref):
        acc = [jnp.zeros((nch, 2 * H), F32) for _ in range(NSA_GROUPS)]
        for j in range(CMP_STRIDE):
            rows = c_ref[0, pl.ds(j, nch, stride=CMP_STRIDE), :].astype(BF16)
            wj = w1_ref[HEAD_DIM * j:HEAD_DIM * (j + 1), :]
            for g in range(NSA_GROUPS):
                acc[g] = acc[g] + jnp.dot(rows[:, HEAD_DIM * g:HEAD_DIM * (g + 1)], wj, preferred_element_type=F32)
        return [_gelu_tanh(a[:, 0:H] + pltpu.roll(a[:, H:], nch - 1, 0) + pb_ref[...]).astype(BF16) for a in acc]

    hk = hidden(kc_ref, w1k_ref, pbk_ref)
    hv = hidden(vc_ref, w1v_ref, pbv_ref)
    for g in range(NSA_GROUPS):
        ko_ref[0, g] = jnp.dot(hk[g], w2k_ref[...], preferred_element_type=F32).astype(BF16)
        vo_ref[0, g] = _nt_dot(w2vt_ref[...], hv[g]).astype(BF16)


def _pos_bias_kernel(p_ref, w1_ref, o_ref):
    o_ref[...] = jnp.dot(p_ref[...].astype(BF16), w1_ref[...].astype(BF16), preferred_element_type=F32)


def _pos_bias(pos, w1):
    p8 = jnp.broadcast_to(pos.reshape(1, -1), (SUBLANES, pos.size))
    return pl.pallas_call(_pos_bias_kernel, out_shape=jax.ShapeDtypeStruct((SUBLANES, CMP_HIDDEN), F32),
                          name="pos_bias")(p8, w1)[0:1]


def _compress(kc, vc, w1k, w1v, w2k, w2vt, pbk, pbv):
    B, T, _ = kc.shape
    G = NSA_GROUPS
    nch = T // CMP_STRIDE
    c_blk = pl.BlockSpec((1, T, LANES), lambda b: (b, 0, 0))
    full = lambda a: pl.BlockSpec(a.shape, lambda b: (0,) * a.ndim)
    return pl.pallas_call(
        _compress_kernel,
        grid=(B,),
        in_specs=[c_blk, c_blk, full(w1k), full(w1v), full(w2k), full(w2vt), full(pbk), full(pbv)],
        out_specs=[pl.BlockSpec((1, G, nch, LANES), lambda b: (b, 0, 0, 0)),
                   pl.BlockSpec((1, G, HEAD_DIM, nch), lambda b: (b, 0, 0, 0))],
        out_shape=[jax.ShapeDtypeStruct((B, G, nch, LANES), BF16),
                   jax.ShapeDtypeStruct((B, G, HEAD_DIM, nch), BF16)],
        compiler_params=pltpu.CompilerParams(dimension_semantics=("parallel",), vmem_limit_bytes=VMEM_LIMIT),
        name="compress",
    )(kc, vc, w1k, w1v, w2k, w2vt, pbk, pbv)


class _QTile:
    def __init__(self, x, t0, q_ref, g_ref, o_ref, cols, ncp, n_wc):
        self.x, self.t0, self.q_ref, self.g_ref, self.o_ref = x, t0, q_ref, g_ref, o_ref
        self.t_lane = t0 + (lax.broadcasted_iota(jnp.int32, (1, cols), 1) & (TQ - 1))
        c_end = lax.broadcasted_iota(jnp.int32, (ncp, cols), 0) * CMP_STRIDE + (CMP_BLOCK - 1)
        self.cmp_valid = c_end <= self.t_lane
        self.w_pos = [t0 - WINDOW + TQ * c for c in range(n_wc)]
        self.w_start = [pl.multiple_of(jnp.maximum(p, 0), TQ) for p in self.w_pos]


def _nsa_kernel(qa_ref, qb_ref, kc_ref, vct_ref, ksa_ref, vst_ref, kw_ref, vwt_ref, ga_ref, gb_ref, ovt_ref,
                oa_ref, ob_ref, qsel_scr, acc_scr, m_scr):
    units = [(bb, g) for bb in range(qa_ref.shape[0]) for g in range(NSA_GROUPS)]
    uidx = {u: i for i, u in enumerate(units)}
    n_qt = kw_ref.shape[2] // TQ
    R = HEADS_PER_GROUP
    cols = R * TQ
    NCP = kc_ref.shape[2]
    NS = ovt_ref.shape[0]
    n_wc = (WINDOW + TQ) // TQ
    step = pl.program_id(1)
    tiles = [_QTile(0, step * TQ, qa_ref, ga_ref, oa_ref, cols, NCP, n_wc),
             _QTile(1, (n_qt - 1 - step) * TQ, qb_ref, gb_ref, ob_ref, cols, NCP, n_wc)]
    work = [(c, u) for c in tiles for u in units]
    low = lax.broadcasted_iota(jnp.int32, (cols, LANES), 1) < HEAD_DIM
    sub8 = lax.broadcasted_iota(jnp.int32, (SUBLANES, TQ), 0)
    row_tq = lax.broadcasted_iota(jnp.int32, (TQ, cols), 0)

    def group_q(c, u):
        bb, g = u
        return c.q_ref[bb, R * g:R * (g + 1)].reshape(cols, LANES)

    def cmp_scores(c, u):
        return _nt_dot(kc_ref[u[0], u[1]], group_q(c, u))

    def cmp_probs(c, s):
        s = jnp.where(c.cmp_valid, s, NEG)
        e = jnp.exp2(s - jnp.max(s, axis=0, keepdims=True))
        inv = 1.0 / jnp.maximum(jnp.sum(e, axis=0, keepdims=True), 1.0e-30)
        return e * jnp.where(c.t_lane >= CMP_BLOCK - 1, inv, 0.0)

    def win_scores(c, u):
        kw = jnp.concatenate([kw_ref[u[0], u[1], pl.ds(c.w_start[n], TQ), :] for n in range(n_wc)], axis=0)
        return _nt_dot(kw, group_q(c, u))

    def win_probs(c, sw):
        parts = []
        for n in range(n_wc):
            sc = sw[TQ * n:TQ * (n + 1)]
            if n == 0:
                sc = jnp.where(c.w_pos[0] + row_tq > c.t_lane - WINDOW, sc, NEG)
            if n == n_wc - 1:
                sc = jnp.where(c.t0 + row_tq <= c.t_lane, sc, NEG)
            else:
                sc = jnp.where(c.w_pos[n] >= 0, sc, NEG)
            parts.append(sc.astype(BF16))
        sw = jnp.concatenate(parts, axis=0)
        return jnp.exp2(sw - jnp.max(sw, axis=0, keepdims=True))

    def win_out(c, u, ew):
        vw = jnp.concatenate([vwt_ref[u[0], u[1], :, pl.ds(c.w_start[n], TQ)] for n in range(n_wc)], axis=1)
        ow = jnp.dot(vw, ew, preferred_element_type=F32)
        return ow[0:HEAD_DIM] * (1.0 / ow[HEAD_DIM:HEAD_DIM + 1])

    def select_blocks(c, u, p):
        psum = p[:, 0:TQ] + p[:, TQ:2 * TQ] + p[:, 2 * TQ:3 * TQ] + p[:, 3 * TQ:4 * TQ]
        p_hi = psum.astype(BF16)
        p_lo = (psum - p_hi.astype(F32)).astype(BF16)
        imp = (jnp.dot(ovt_ref[...], p_hi, preferred_element_type=F32)
               + jnp.dot(ovt_ref[...], p_lo, preferred_element_type=F32))
        blk = lax.broadcasted_iota(jnp.int32, (NS, TQ), 0)
        t_l = c.t0 + lax.broadcasted_iota(jnp.int32, (NS, TQ), 1)
        cur = t_l // SEL_BLOCK
        imp = jnp.where(blk * SEL_BLOCK <= t_l, imp, -1.0)
        imp = jnp.where(blk == 0, FORCED_SCORE, imp)
        imp = jnp.where(blk == cur, FORCED_SCORE, imp)
        imp = jnp.where(blk == cur - 1, FORCED_SCORE, imp)
        nv = NS // SUBLANES
        imp8 = [imp[SUBLANES * j:SUBLANES * (j + 1)] for j in range(nv)]
        rank8 = [jnp.zeros((SUBLANES, TQ), F32) for _ in range(nv)]
        for mm in range(NS):
            row = imp[mm:mm + 1, :]
            jm = mm // SUBLANES
            for j in range(nv):
                if j < jm:
                    ahead = jnp.where(row > imp8[j], 1.0, 0.0)
                elif j > jm:
                    ahead = jnp.where(row >= imp8[j], 1.0, 0.0)
                else:
                    tie = jnp.where(sub8 > (mm % SUBLANES), 1.0, 0.0)
                    ahead = jnp.where(row > imp8[j], 1.0, 0.0) + jnp.where(row == imp8[j], tie, 0.0)
                rank8[j] = rank8[j] + ahead
        rank = jnp.concatenate(rank8, axis=0)
        pen = jnp.where(rank < float(SEL_TOPK), 0.0, NEG)
        q_t = group_q(c, u).astype(F32).T.astype(BF16)
        qsel_scr[c.x, uidx[u]] = jnp.concatenate(
            [q_t[0:HEAD_DIM], jnp.concatenate([pen.astype(BF16)] * R, axis=1)], axis=0)

    p_c, o_cmp, o_win = {}, {}, {}

    def cmp_job(c, u):
        def finish(p, _):
            p_c[c.x, u] = p
            o_cmp[c.x, u] = jnp.dot(vct_ref[u[0], u[1]], p.astype(BF16), preferred_element_type=F32)
        return (lambda: cmp_scores(c, u)), (lambda s: (cmp_probs(c, s), None)), finish

    def win_job(c, u):
        def finish(e_w, _):
            o_win[c.x, u] = win_out(c, u, e_w)
        return (lambda: win_scores(c, u)), (lambda s: (win_probs(c, s), None)), finish

    def sel_job(x, kt, u, t_mask):
        k0 = kt * TK if isinstance(kt, int) else pl.multiple_of(kt * TK, TK)

        def probs(sc):
            if t_mask is not None:
                sc = jnp.where(kt * TK + lax.broadcasted_iota(jnp.int32, (TK, cols), 0) <= t_mask, sc, NEG)
            sc = sc.astype(BF16)
            m_old = m_scr[x, uidx[u]]
            m_new = jnp.maximum(m_old, jnp.max(sc, axis=0, keepdims=True).astype(F32))
            m_scr[x, uidx[u]] = m_new
            return jnp.exp2(sc - m_new.astype(BF16)), jnp.exp2(m_old - m_new)

        def finish(pp, alpha):
            acc_scr[x, uidx[u]] = acc_scr[x, uidx[u]] * alpha + jnp.dot(
                vst_ref[u[0], u[1], :, pl.ds(k0, TK)], pp, preferred_element_type=F32)

        return (lambda: jnp.dot(ksa_ref[u[0], u[1], pl.ds(k0, TK), :], qsel_scr[x, uidx[u]],
                                preferred_element_type=F32)), probs, finish

    def fuse(js):
        return ((lambda: [j[0]() for j in js]),
                (lambda ss: ([j[1](s) for j, s in zip(js, ss)], None)),
                (lambda outs, _: [j[2](*o) for j, o in zip(js, outs)]))

    def sel_tile(x, kt, t_mask):
        return [fuse([sel_job(x, kt, (bb, g), t_mask) for g in range(NSA_GROUPS)]) for bb in range(qa_ref.shape[0])]

    early, late = tiles
    n_slots = (n_qt - 1) * TQ // TK
    n_static = n_slots - n_slots // 2
    n_late = late.t0 // TK
    plan = [cmp_job(c, u) for c in (late, early) for u in units]
    for u in units:
        plan += [lambda u=u: select_blocks(late, u, p_c[late.x, u]), win_job(late, u)]
    plan += sel_tile(late.x, n_late, late.t_lane)
    for s in range(n_static):
        plan += sel_tile(late.x, s, None)
        for u in units[s::n_static]:
            plan += [lambda u=u: select_blocks(early, u, p_c[early.x, u]), win_job(early, u)]
    plan += sel_tile(early.x, early.t0 // TK, early.t_lane)
    for s in range(n_static, n_slots):
        is_late = s < n_late
        plan += sel_tile(jnp.where(is_late, late.x, early.x), jnp.where(is_late, s, s - n_late), None)

    acc_scr[...] = jnp.zeros_like(acc_scr)
    m_scr[...] = jnp.full(m_scr.shape, NEG, F32)
    job_pos = [k for k, e in enumerate(plan) if isinstance(e, tuple)]
    following = dict(zip(job_pos, job_pos[1:]))
    issued = {job_pos[0]: plan[job_pos[0]][0]()}
    for k, entry in enumerate(plan):
        if not isinstance(entry, tuple):
            entry()
            continue
        if k in following:
            issued[following[k]] = plan[following[k]][0]()
        _, probs, finish = entry
        finish(*probs(issued.pop(k)))

    for c in tiles:
        for bb in range(qa_ref.shape[0]):
            glt = c.g_ref[bb]
            heads = []
            for g in range(NSA_GROUPS):
                acc = acc_scr[c.x, uidx[bb, g]]
                o_sel = acc[0:HEAD_DIM] * (1.0 / acc[HEAD_DIM:HEAD_DIM + 1])
                for r in range(R):
                    hh = R * g + r
                    sl = slice(r * TQ, (r + 1) * TQ)
                    heads.append(glt[3 * hh:3 * hh + 1] * o_cmp[c.x, (bb, g)][:, sl]
                                 + glt[3 * hh + 1:3 * hh + 2] * o_sel[:, sl]
                                 + glt[3 * hh + 2:3 * hh + 3] * o_win[c.x, (bb, g)][:, sl])
            c.o_ref[bb] = jnp.concatenate(heads, axis=0).T


def _nsa(qq, kcmp, vcmpt, ksa, vst, kw, vwt, glt, ovt):
    B, H, T, _ = qq.shape
    G = NSA_GROUPS
    NB = NSA_NB
    NCP = kcmp.shape[2]
    n_qt = T // TQ
    grid = (B // NB, n_qt // 2)
    k_blk = lambda n: pl.BlockSpec((NB, G, n, LANES), lambda b, i: (b, 0, 0, 0))
    vt_blk = lambda r, n: pl.BlockSpec((NB, G, r, n), lambda b, i: (b, 0, 0, 0))
    lo_tile = lambda b, i: i
    hi_tile = lambda b, i: n_qt - 1 - i
    q_blk = lambda tile: pl.BlockSpec((NB, H, TQ, LANES), lambda b, i: (b, 0, tile(b, i), 0))
    g_blk = lambda tile: pl.BlockSpec((NB, GATE_ROWS, TQ), lambda b, i: (b, 0, tile(b, i)))
    half = jax.ShapeDtypeStruct((B, T // 2, NSA_WIDTH), F32)
    return pl.pallas_call(
        _nsa_kernel,
        grid=grid,
        in_specs=[
            q_blk(lo_tile), q_blk(hi_tile),
            k_blk(NCP), vt_blk(HEAD_DIM, NCP), k_blk(T), vt_blk(V_ROWS, T), k_blk(T), vt_blk(V_ROWS, T),
            g_blk(lo_tile), g_blk(hi_tile),
            pl.BlockSpec(ovt.shape, lambda b, i: (0, 0)),
        ],
        out_specs=[pl.BlockSpec((NB, TQ, NSA_WIDTH), lambda b, i: (b, i, 0)),
                   pl.BlockSpec((NB, TQ, NSA_WIDTH), lambda b, i: (b, n_qt // 2 - 1 - i, 0))],
        out_shape=[half, half],
        scratch_shapes=[pltpu.VMEM((2, NB * G, LANES, HEADS_PER_GROUP * TQ), BF16),
                        pltpu.VMEM((2, NB * G, V_ROWS, HEADS_PER_GROUP * TQ), F32),
                        pltpu.VMEM((2, NB * G, 1, HEADS_PER_GROUP * TQ), F32)],
        compiler_params=pltpu.CompilerParams(
            dimension_semantics=("parallel", "arbitrary"), vmem_limit_bytes=VMEM_LIMIT),
        name="nsa",
    )(qq, qq, kcmp, vcmpt, ksa, vst, kw, vwt, glt, glt, ovt)


def _s5_prep_kernel(lre_ref, lim_ref, ldt_ref, bre_ref, bim_ref, are_ref, aim_ref, bbre_ref, bbim_ref):
    lre, lim = lre_ref[...], lim_ref[...]
    dt = jnp.exp(ldt_ref[...])
    mag = jnp.exp(lre * dt)
    a_re = mag * jnp.cos(lim * dt)
    a_im = mag * jnp.sin(lim * dt)
    den = lre * lre + lim * lim
    z_re = ((a_re - 1.0) * lre + a_im * lim) / den
    z_im = (a_im * lre - (a_re - 1.0) * lim) / den
    are_ref[...] = a_re
    aim_ref[...] = a_im
    bbre_ref[...] = z_re * bre_ref[...] - z_im * bim_ref[...]
    bbim_ref[...] = z_re * bim_ref[...] + z_im * bre_ref[...]


def _s5_prep(lre, lim, ldt, bre, bim):
    shp = jax.ShapeDtypeStruct(lre.shape, F32)
    return pl.pallas_call(_s5_prep_kernel, out_shape=[shp, shp, shp, shp], name="s5_prep")(lre, lim, ldt, bre, bim)


def _s5_weights_kernel(bbre_ref, bbim_ref, are_ref, aim_ref, crt_ref, cit_ref,
                       we_ref, wct_ref, tp_ref, alre_ref, alim_ref, bd_scr, arow_scr, tp_scr):
    gpb = S5_CPB // S5_GROUP
    bd_scr[...] = jnp.zeros_like(bd_scr)
    tp_scr[...] = jnp.zeros_like(tp_scr)
    for n, ref in enumerate((bbre_ref, bbim_ref, crt_ref, cit_ref)):
        for gl in range(gpb):
            bd_scr[n, S5_GROUP * gl:S5_GROUP * (gl + 1), S5_STATE * gl:S5_STATE * (gl + 1)] = (
                ref[S5_GROUP * gl:S5_GROUP * (gl + 1), :])
    for n, ref in enumerate((are_ref, aim_ref)):
        for gl in range(gpb):
            arow_scr[n, :, S5_STATE * gl:S5_STATE * (gl + 1)] = ref[S5_GROUP * gl:S5_GROUP * gl + 1, :]
    bbre, bbim, crt, cit = bd_scr[0], bd_scr[1], bd_scr[2], bd_scr[3]
    are, aim = arow_scr[0], arow_scr[1]
    pre, pim = jnp.ones_like(are), jnp.zeros_like(are)
    for k in range(S5_L):
        bpr = bbre * pre - bbim * pim
        bpi = bbre * pim + bbim * pre
        i = S5_L - 1 - k
        we_ref[0, S5_CPB * i:S5_CPB * (i + 1), :] = jnp.concatenate([bpr, bpi], axis=1).astype(BF16)
        tap = (_nt_dot(bpr.astype(BF16), crt.astype(BF16)) - _nt_dot(bpi.astype(BF16), cit.astype(BF16)))
        for i in range(S5_L - k):
            j = i + k
            tp_scr[S5_CPB * i:S5_CPB * (i + 1), S5_CPB * j:S5_CPB * (j + 1)] = tap
        pre, pim = pre * are - pim * aim, pre * aim + pim * are
        wct_ref[0, S5_CPB * k:S5_CPB * (k + 1), :] = jnp.concatenate(
            [crt * pre - cit * pim, -(crt * pim + cit * pre)], axis=1).astype(BF16)
    tp_ref[0] = tp_scr[...].astype(BF16)
    alre_ref[0] = pre
    alim_ref[0] = pim


def _s5_weights(bbre, bbim, are, aim, crt, cit):
    nblk = S5_WIDTH // S5_CPB
    sb = S5_CPB // S5_GROUP * S5_STATE
    lc = S5_L * S5_CPB
    rows = pl.BlockSpec((S5_CPB, S5_STATE), lambda q: (q, 0))
    blk = lambda r, c: pl.BlockSpec((1, r, c), lambda q: (q, 0, 0))
    return pl.pallas_call(
        _s5_weights_kernel,
        grid=(nblk,),
        in_specs=[rows] * 6,
        out_specs=[blk(lc, 2 * sb), blk(lc, 2 * sb), blk(lc, lc), blk(1, sb), blk(1, sb)],
        scratch_shapes=[pltpu.VMEM((4, S5_CPB, sb), F32), pltpu.VMEM((2, 1, sb), F32), pltpu.VMEM((lc, lc), F32)],
        out_shape=[jax.ShapeDtypeStruct((nblk, lc, 2 * sb), BF16),
                   jax.ShapeDtypeStruct((nblk, lc, 2 * sb), BF16),
                   jax.ShapeDtypeStruct((nblk, lc, lc), BF16),
                   jax.ShapeDtypeStruct((nblk, 1, sb), F32), jax.ShapeDtypeStruct((nblk, 1, sb), F32)],
        compiler_params=pltpu.CompilerParams(dimension_semantics=("parallel",), vmem_limit_bytes=VMEM_LIMIT),
        name="s5_weights",
    )(bbre, bbim, are, aim, crt, cit)


def _s5_kernel(u_ref, we_ref, wct_ref, tp_ref, alre_ref, alim_ref, d_ref, wg_ref, bg_ref, o_ref, e_scr, st_scr):
    nb, ct, _ = u_ref.shape
    L = S5_L
    nblk = S5_WIDTH // S5_CPB
    sb = S5_CPB // S5_GROUP * S5_STATE
    spb = 2 * sb // LANES
    W = S5_SCAN_SLABS * LANES

    @pl.when(pl.program_id(0) == 0)
    def _():
        st_scr[...] = jnp.zeros_like(st_scr)

    uf = u_ref[...].reshape(nb * ct, L * S5_WIDTH)
    ub = uf.astype(BF16)
    u_blk = [jnp.concatenate([ub[:, i * S5_WIDTH + S5_CPB * q:i * S5_WIDTH + S5_CPB * (q + 1)] for i in range(L)],
                             axis=1) for q in range(nblk)]

    for q in range(nblk):
        e = jnp.dot(u_blk[q], we_ref[q], preferred_element_type=F32)
        for s in range(spb):
            for b in range(nb):
                e_scr[spb * q + s, pl.ds(b, ct, stride=nb), :] = e[b * ct:(b + 1) * ct, LANES * s:LANES * (s + 1)]

    def slabs_of(cs):
        q, r = divmod(cs * LANES, sb)
        return spb * q + r // LANES, spb * q + (sb + r) // LANES

    for j in range(S5_GROUPS * S5_STATE // W):
        sl = [slabs_of(S5_SCAN_SLABS * j + n) for n in range(S5_SCAN_SLABS)]
        sl_re, sl_im = [s[0] for s in sl], [s[1] for s in sl]
        ar = alre_ref[:, W * j:W * (j + 1)]
        ai = alim_ref[:, W * j:W * (j + 1)]
        load = lambda r0, rows, slabs: jnp.concatenate([e_scr[s, pl.ds(r0, rows), :] for s in slabs], axis=1)
        state = lambda slabs: jnp.concatenate([st_scr[:, LANES * s:LANES * (s + 1)] for s in slabs], axis=1)
        sr, si = state(sl_re), state(sl_im)
        for k in range(ct // 2):
            r0 = k * 2 * nb
            er, ei = load(r0, 2 * nb, sl_re), load(r0, 2 * nb, sl_im)
            tr = ar * sr - ai * si + er[0:nb]
            ti = ar * si + ai * sr + ei[0:nb]
            xr = jnp.concatenate([sr, tr], axis=0)
            xi = jnp.concatenate([si, ti], axis=0)
            for n in range(S5_SCAN_SLABS):
                e_scr[sl_re[n], pl.ds(r0, 2 * nb), :] = xr[:, LANES * n:LANES * (n + 1)]
                e_scr[sl_im[n], pl.ds(r0, 2 * nb), :] = xi[:, LANES * n:LANES * (n + 1)]
            sr, si = ar * tr - ai * ti + er[nb:], ar * ti + ai * tr + ei[nb:]
        for n in range(S5_SCAN_SLABS):
            st_scr[:, LANES * sl_re[n]:LANES * (sl_re[n] + 1)] = sr[:, LANES * n:LANES * (n + 1)]
            st_scr[:, LANES * sl_im[n]:LANES * (sl_im[n] + 1)] = si[:, LANES * n:LANES * (n + 1)]

    ys = []
    for q in range(nblk):
        per_b = [jnp.concatenate([e_scr[spb * q + s, pl.ds(b, ct, stride=nb), :] for s in range(spb)], axis=1)
                 for b in range(nb)]
        x_in = jnp.concatenate(per_b, axis=0).astype(BF16)
        ys.append((_nt_dot(x_in, wct_ref[q]) + jnp.dot(u_blk[q], tp_ref[q], preferred_element_type=F32)).astype(BF16))
    outs = []
    for j in range(L):
        yj = jnp.concatenate([y[:, S5_CPB * j:S5_CPB * (j + 1)] for y in ys], axis=1).astype(F32)
        yj = yj + d_ref[...] * uf[:, j * S5_WIDTH:(j + 1) * S5_WIDTH]
        z = _gelu_tanh(yj)
        gate = jnp.dot(z.astype(BF16), wg_ref[...], preferred_element_type=F32) + bg_ref[...]
        outs.append(z * _sigmoid(gate))
    o_ref[...] = jnp.concatenate(outs, axis=1).reshape(nb, ct, L * S5_WIDTH)


def _s5(u4, we, wct, tp, alre, alim, d, w_glu, b_glu):
    nb, nchunks, w4 = u4.shape
    ct = S5_CT
    n_slabs = 2 * S5_GROUPS * S5_STATE // LANES
    resident = lambda a: pl.BlockSpec(a.shape, lambda i: (0,) * a.ndim, pipeline_mode=pl.Buffered(1))
    u_blk = pl.BlockSpec((nb, ct, w4), lambda i: (0, i, 0))
    return pl.pallas_call(
        _s5_kernel,
        grid=(nchunks // ct,),
        in_specs=[u_blk, resident(we), resident(wct), resident(tp), resident(alre), resident(alim),
                  resident(d), resident(w_glu), resident(b_glu)],
        out_specs=u_blk,
        out_shape=jax.ShapeDtypeStruct(u4.shape, F32),
        scratch_shapes=[pltpu.VMEM((n_slabs, nb * ct, LANES), F32), pltpu.VMEM((nb, n_slabs * LANES), F32)],
        compiler_params=pltpu.CompilerParams(dimension_semantics=("arbitrary",), vmem_limit_bytes=VMEM_LIMIT),
        name="s5",
    )(u4, we, wct, tp, alre, alim, d, w_glu, b_glu)


def _final_kernel(x_ref, g_ref, wb_ref, onl_ref, onh_ref, os_ref, wpn_ref, wps_ref, wo_ref, fg_ref, o_ref, os_scr):
    tm = x_ref.shape[1]
    o4 = os_ref[0]
    for i in range(S5_L):
        for s in range(S5_WIDTH // LANES):
            c0 = i * S5_WIDTH + LANES * s
            os_scr[s, pl.ds(i, tm // S5_L, stride=S5_L), :] = o4[:, c0:c0 + LANES]
    first_half = pl.program_id(1) < pl.num_programs(1) // 2

    def silu(v):
        return v * _sigmoid(v)

    rows = [slice(r, r + tm // FINAL_SUB) for r in range(0, tm, tm // FINAL_SUB)]
    hs = [(_rms_scale(x_ref[0, r]) * g_ref[...]).astype(BF16) for r in rows]
    for r, h in zip(rows, hs):
        proj = lambda a, b, h=h: jnp.dot(h, wb_ref[:, a:b], preferred_element_type=F32)
        o_nsa = jnp.where(first_half, onl_ref[0, r], onh_ref[0, r])
        o_s5 = jnp.concatenate([os_scr[s, r] for s in range(S5_WIDTH // LANES)], axis=1)
        a_in = (o_nsa * silu(proj(0, NSA_WIDTH))).astype(BF16)
        b_in = (o_s5 * silu(proj(NSA_WIDTH, NSA_WIDTH + S5_WIDTH))).astype(BF16)
        branch_a = jnp.dot(a_in, wpn_ref[...], preferred_element_type=F32)
        branch_b = jnp.dot(b_in, wps_ref[...], preferred_element_type=F32)
        o1 = NSA_WIDTH + S5_WIDTH
        merged = (_sigmoid(proj(o1, o1 + D_MODEL)) * branch_a
                  + _sigmoid(proj(o1 + D_MODEL, o1 + 2 * D_MODEL)) * branch_b)
        y = x_ref[0, r] + jnp.dot(merged.astype(BF16), wo_ref[...], preferred_element_type=F32)
        o_ref[0, r] = _rms_scale(y) * fg_ref[...]


def _final(x, norm_g, w_b, o_nsa_lo, o_nsa_hi, o_s5, wpn, wps, wo, final_g):
    B, T, D = x.shape
    tm = TM_PROJ
    nh = T // tm // 2
    row_blk = lambda w: pl.BlockSpec((1, tm, w), lambda b, i: (b, i, 0))
    full = lambda a: pl.BlockSpec(a.shape, lambda b, i: (0,) * a.ndim)
    return pl.pallas_call(
        _final_kernel,
        grid=(B, T // tm),
        in_specs=[row_blk(D), full(norm_g), full(w_b),
                  pl.BlockSpec((1, tm, NSA_WIDTH), lambda b, i: (b, jnp.minimum(i, nh - 1), 0)),
                  pl.BlockSpec((1, tm, NSA_WIDTH), lambda b, i: (b, jnp.maximum(i - nh, 0), 0)),
                  pl.BlockSpec((1, tm // S5_L, S5_L * S5_WIDTH), lambda b, i: (b, i, 0)),
                  full(wpn), full(wps), full(wo), full(final_g)],
        out_specs=row_blk(D),
        out_shape=jax.ShapeDtypeStruct((B, T, D), F32),
        scratch_shapes=[pltpu.VMEM((S5_WIDTH // LANES, tm, LANES), F32)],
        compiler_params=pltpu.CompilerParams(
            dimension_semantics=("parallel", "arbitrary"), vmem_limit_bytes=VMEM_LIMIT),
        name="final",
    )(x, norm_g, w_b, o_nsa_lo, o_nsa_hi, o_s5, wpn, wps, wo, final_g)


def _rope_tables(T):
    half = HEAD_DIM // 2
    inv_freq = np.float32(ROPE_THETA) ** (-np.arange(half, dtype=np.float32) / np.float32(half))
    ang = np.arange(T, dtype=np.float32)[:, None] * inv_freq[None, :].astype(np.float32)
    cos, sin = np.cos(ang).astype(np.float32), np.sin(ang).astype(np.float32)
    cos2 = np.concatenate([cos, cos, cos, cos], axis=1)
    sin2 = np.concatenate([-sin, sin, -sin, sin], axis=1)
    return jnp.asarray(cos2), jnp.asarray(sin2)


def _compress_w1(w1):
    half_rows = CMP_STRIDE * HEAD_DIM
    return jnp.concatenate([w1[:half_rows], w1[half_rows:]], axis=1).astype(BF16)


def kernel(x, norm_g, w_in, cmp_pos_k, cmp_pos_v, cmp_w1_k, cmp_w2_k, cmp_w1_v, cmp_w2_v, s5_lam_re, s5_lam_im, s5_log_dt, s5_b_re, s5_b_im, s5_c_re, s5_c_im, s5_d, w_glu, b_glu, w_proj_nsa, w_proj_s5, w_out, final_g):
    B, T, D = x.shape
    assert w_in.shape[0] == 1, "single-layer block"
    NCH = T // CMP_STRIDE
    NS = T // SEL_BLOCK

    w = w_in[0]
    w_a = jnp.concatenate([w[:, :_OFF_GL], jnp.pad(w[:, _OFF_GL:_OFF_GN], ((0, 0), (0, LANES - 24))),
                           w[:, _OFF_U:_OFF_GS]], axis=1).astype(BF16)
    w_b = jnp.concatenate([w[:, _OFF_GN:_OFF_U], w[:, _OFF_GS:]], axis=1).astype(BF16)
    g2 = norm_g[0][None, :]
    cos2, sin2 = _rope_tables(T)

    qq, kc, vc, ksa, vst, kw, vwt, glt, u4 = _inproj(x, g2, w_a, cos2, sin2)

    w2k = jnp.concatenate([jnp.zeros_like(cmp_w2_k[0]), cmp_w2_k[0]], axis=1).astype(BF16)
    w2vt = cmp_w2_v[0].T.astype(BF16)
    kcmp, vcmpt = _compress(kc, vc, _compress_w1(cmp_w1_k[0]), _compress_w1(cmp_w1_v[0]), w2k, w2vt,
                            _pos_bias(cmp_pos_k[0], cmp_w1_k[0]), _pos_bias(cmp_pos_v[0], cmp_w1_v[0]))

    c_start = jnp.arange(NCH) * CMP_STRIDE
    s_start = jnp.arange(NS) * SEL_BLOCK
    ovt = ((c_start[None, :] < s_start[:, None] + SEL_BLOCK) & (c_start[None, :] + CMP_BLOCK > s_start[:, None])
           & (jnp.arange(NCH)[None, :] < NCH - 1)).astype(BF16)
    o_nsa_lo, o_nsa_hi = _nsa(qq, kcmp, vcmpt, ksa, vst, kw, vwt, glt, ovt)

    rep = lambda a: jnp.repeat(a, S5_GROUP, axis=0)
    tr = lambda b: b.transpose(0, 2, 1).reshape(S5_GROUPS * S5_GROUP, S5_STATE)
    a_re, a_im, bb_re, bb_im = _s5_prep(
        rep(s5_lam_re[0]), rep(s5_lam_im[0]),
        rep(jnp.broadcast_to(s5_log_dt[0][:, None], (S5_GROUPS, S5_STATE))),
        tr(s5_b_re[0]), tr(s5_b_im[0]))
    flat = lambda c: c.reshape(S5_GROUPS * S5_GROUP, S5_STATE)
    we, wct, tp, alre, alim = _s5_weights(bb_re, bb_im, a_re, a_im, flat(s5_c_re[0]), flat(s5_c_im[0]))
    o_s5 = _s5(u4, we, wct, tp, alre.reshape(1, -1), alim.reshape(1, -1), s5_d[0][None, :],
               w_glu[0].astype(BF16), b_glu[0][None, :])

    return _final(x, g2, w_b, o_nsa_lo, o_nsa_hi, o_s5, w_proj_nsa[0].astype(BF16), w_proj_s5[0].astype(BF16),
                  w_out[0].astype(BF16), final_g[None, :])
```

```python
import math

import jax
import jax.numpy as jnp
import numpy as np
from jax import lax
from jax.experimental import pallas as pl
from jax.experimental.pallas import tpu as pltpu

F32 = jnp.float32
BF16 = jnp.bfloat16

D_MODEL = 1024
NSA_HEADS = 8
NSA_GROUPS = 2
HEADS_PER_GROUP = 4
HEAD_DIM = 64
NSA_WIDTH = 512
CMP_BLOCK = 32
CMP_STRIDE = 16
CMP_HIDDEN = 256
SEL_BLOCK = 64
SEL_TOPK = 16
WINDOW = 512
ROPE_THETA = 10000.0
FORCED_SCORE = 1.0e4
NEG = -1.0e30
S5_WIDTH = 512
S5_GROUP = 16
S5_GROUPS = 32
S5_STATE = 64
RMS_EPS = 1.0e-6

LANES = 128
SUBLANES = 8
VMEM_LIMIT = 56 * 1024 * 1024

_OFF_GL = 1280
_OFF_GN = 1304
_OFF_U = 1816
_OFF_GS = 2328

TM_PROJ = 512
FINAL_SUB = 2
TQ = 128
TK = 512
NSA_NB = 2
SEL_LOOKAHEAD = 1
V_ROWS = 80
GATE_ROWS = 32
S5_L = 8
S5_CPB = 256 // S5_L
S5_CT = 64
S5_SCAN_SLABS = 4


def _gelu_tanh(x):
    c = math.sqrt(2.0 / math.pi)
    return 0.5 * x * (1.0 + jnp.tanh(c * (x + 0.044715 * (x * x * x))))


def _sigmoid(x):
    return 1.0 / (1.0 + jnp.exp(-x))


def _rms_scale(xv):
    ms = jnp.mean(xv * xv, axis=-1, keepdims=True)
    return xv * lax.rsqrt(ms + RMS_EPS)


def _nt_dot(a, b):
    return lax.dot_general(a, b, (((1,), (1,)), ((), ())), preferred_element_type=F32)


def _inproj_kernel(x_ref, g_ref, w_ref, cos_ref, sin_ref,
                   qq_ref, kc_ref, vc_ref, ks_ref, vs_ref, kw_ref, vw_ref, gl_ref, u_ref, us_scr):
    h = (_rms_scale(x_ref[0]) * g_ref[...]).astype(BF16)
    cos2 = cos_ref[...]
    sin2 = sin_ref[...]
    lane = lax.broadcasted_iota(jnp.int32, cos2.shape, 1)
    first_half = (lane & (HEAD_DIM - 1)) < (HEAD_DIM // 2)
    low = lane < HEAD_DIM

    wide = {}

    def proj(a, b):
        for (s0, s1) in ((0, 512), (512, 1280), (1280, 1920)):
            if s0 <= a and b <= s1:
                if s0 not in wide:
                    wide[s0] = jnp.dot(h, w_ref[:, s0:s1], preferred_element_type=F32)
                return wide[s0][:, a - s0:b - s0]
        raise ValueError((a, b))

    def rope(xs):
        partner = jnp.where(first_half, pltpu.roll(xs, 96, 1), pltpu.roll(xs, 32, 1))
        return xs * cos2 + partner * sin2

    scale = HEAD_DIM ** -0.5 * math.log2(math.e)
    for i in range(NSA_HEADS // 2):
        xs = proj(LANES * i, LANES * (i + 1)) * scale
        xr = rope(xs)
        qq_ref[0, 2 * i] = jnp.where(low, xr, pltpu.roll(xs, 64, 1)).astype(BF16)
        qq_ref[0, 2 * i + 1] = jnp.where(low, pltpu.roll(xr, 64, 1), xs).astype(BF16)

    kc_ref[0] = proj(512, 640)
    vc_ref[0] = proj(640, 768)
    tm = cos2.shape[0]
    t_row = pl.program_id(1) * tm + lax.broadcasted_iota(jnp.int32, cos2.shape, 0)
    blk_onehot = jnp.where(lane - HEAD_DIM == t_row // SEL_BLOCK, 1.0, 0.0)
    ones_rows = jnp.where(lax.broadcasted_iota(jnp.int32, (V_ROWS - HEAD_DIM, tm), 0) == 0, 1.0, 0.0)
    for (off, k_out, v_out, k_pad) in ((768, ks_ref, vs_ref, blk_onehot), (1024, kw_ref, vw_ref, 0.0)):
        kr = rope(proj(off, off + LANES))
        k_out[0, 0] = jnp.where(low, kr, k_pad).astype(BF16)
        k_out[0, 1] = jnp.where(low, pltpu.roll(kr, 64, 1), k_pad).astype(BF16)
        vt = proj(off + LANES, off + 2 * LANES).T
        for g in range(NSA_GROUPS):
            v_out[0, g] = jnp.concatenate([vt[HEAD_DIM * g:HEAD_DIM * (g + 1)], ones_rows], axis=0).astype(BF16)
    gl_ref[0] = _sigmoid(proj(1280, 1408)).T[0:GATE_ROWS]
    uv = proj(1408, 1920)
    for s in range(S5_WIDTH // LANES):
        us_scr[s] = uv[:, LANES * s:LANES * (s + 1)]
    for i in range(S5_L):
        for s in range(S5_WIDTH // LANES):
            c0 = i * S5_WIDTH + LANES * s
            u_ref[0, :, c0:c0 + LANES] = us_scr[s, pl.ds(i, tm // S5_L, stride=S5_L), :]


def _inproj(x, norm_g, w_a, cos2, sin2):
    B, T, D = x.shape
    tm = TM_PROJ
    grid = (B, T // tm)
    row_blk = lambda w: pl.BlockSpec((1, tm, w), lambda b, i: (b, i, 0))
    kv_blk = pl.BlockSpec((1, NSA_GROUPS, tm, LANES), lambda b, i: (b, 0, i, 0))
    kv_shape = jax.ShapeDtypeStruct((B, NSA_GROUPS, T, LANES), BF16)
    vt_blk = pl.BlockSpec((1, NSA_GROUPS, V_ROWS, tm), lambda b, i: (b, 0, 0, i))
    vt_shape = jax.ShapeDtypeStruct((B, NSA_GROUPS, V_ROWS, T), BF16)
    return pl.pallas_call(
        _inproj_kernel,
        grid=grid,
        in_specs=[
            row_blk(D),
            pl.BlockSpec((1, D), lambda b, i: (0, 0)),
            pl.BlockSpec(w_a.shape, lambda b, i: (0, 0)),
            pl.BlockSpec((tm, LANES), lambda b, i: (i, 0)),
            pl.BlockSpec((tm, LANES), lambda b, i: (i, 0)),
        ],
        out_specs=[
            pl.BlockSpec((1, NSA_HEADS, tm, LANES), lambda b, i: (b, 0, i, 0)),
            row_blk(LANES), row_blk(LANES),
            kv_blk, vt_blk, kv_blk, vt_blk,
            pl.BlockSpec((1, GATE_ROWS, tm), lambda b, i: (b, 0, i)),
            pl.BlockSpec((1, tm // S5_L, S5_L * S5_WIDTH), lambda b, i: (b, i, 0)),
        ],
        out_shape=[
            jax.ShapeDtypeStruct((B, NSA_HEADS, T, LANES), BF16),
            jax.ShapeDtypeStruct((B, T, LANES), F32), jax.ShapeDtypeStruct((B, T, LANES), F32),
            kv_shape, vt_shape, kv_shape, vt_shape,
            jax.ShapeDtypeStruct((B, GATE_ROWS, T), F32),
            jax.ShapeDtypeStruct((B, T // S5_L, S5_L * S5_WIDTH), F32),
        ],
        scratch_shapes=[pltpu.VMEM((S5_WIDTH // LANES, tm, LANES), F32)],
        compiler_params=pltpu.CompilerParams(
            dimension_semantics=("parallel", "arbitrary"), vmem_limit_bytes=VMEM_LIMIT),
        name="inproj",
    )(x, norm_g, w_a, cos2, sin2)


def _compress_kernel(kc_ref, vc_ref, w1k_ref, w1v_ref, w2k_ref, w2vt_ref, pbk_ref, pbv_ref, ko_ref, vo_ref):
    nch = ko_ref.shape[2]
    H = CMP_HIDDEN

    def hidden(c_ref, w1_ref, pb_ref):
        acc = [jnp.zeros((nch, 2 * H), F32) for _ in range(NSA_GROUPS)]
        for j in range(CMP_STRIDE):
            rows = c_ref[0, pl.ds(j, nch, stride=CMP_STRIDE), :].astype(BF16)
            wj = w1_ref[HEAD_DIM * j:HEAD_DIM * (j + 1), :]
            for g in range(NSA_GROUPS):
                acc[g] = acc[g] + jnp.dot(rows[:, HEAD_DIM * g:HEAD_DIM * (g + 1)], wj, preferred_element_type=F32)
        return [_gelu_tanh(a[:, 0:H] + pltpu.roll(a[:, H:], nch - 1, 0) + pb_ref[...]).astype(BF16) for a in acc]

    hk = hidden(kc_ref, w1k_ref, pbk_ref)
    hv = hidden(vc_ref, w1v_ref, pbv_ref)
    for g in range(NSA_GROUPS):
        ko_ref[0, g] = jnp.dot(hk[g], w2k_ref[...], preferred_element_type=F32).astype(BF16)
        vo_ref[0, g] = _nt_dot(w2vt_ref[...], hv[g]).astype(BF16)


def _pos_bias_kernel(p_ref, w1_ref, o_ref):
    o_ref[...] = jnp.dot(p_ref[...].astype(BF16), w1_ref[...].astype(BF16), preferred_element_type=F32)


def _pos_bias(pos, w1):
    p8 = jnp.broadcast_to(pos.reshape(1, -1), (SUBLANES, pos.size))
    return pl.pallas_call(_pos_bias_kernel, out_shape=jax.ShapeDtypeStruct((SUBLANES, CMP_HIDDEN), F32),
                          name="pos_bias")(p8, w1)[0:1]


def _compress(kc, vc, w1k, w1v, w2k, w2vt, pbk, pbv):
    B, T, _ = kc.shape
    G = NSA_GROUPS
    nch = T // CMP_STRIDE
    c_blk = pl.BlockSpec((1, T, LANES), lambda b: (b, 0, 0))
    full = lambda a: pl.BlockSpec(a.shape, lambda b: (0,) * a.ndim)
    return pl.pallas_call(
        _compress_kernel,
        grid=(B,),
        in_specs=[c_blk, c_blk, full(w1k), full(w1v), full(w2k), full(w2vt), full(pbk), full(pbv)],
        out_specs=[pl.BlockSpec((1, G, nch, LANES), lambda b: (b, 0, 0, 0)),
                   pl.BlockSpec((1, G, HEAD_DIM, nch), lambda b: (b, 0, 0, 0))],
        out_shape=[jax.ShapeDtypeStruct((B, G, nch, LANES), BF16),
                   jax.ShapeDtypeStruct((B, G, HEAD_DIM, nch), BF16)],
        compiler_params=pltpu.CompilerParams(dimension_semantics=("parallel",), vmem_limit_bytes=VMEM_LIMIT),
        name="compress",
    )(kc, vc, w1k, w1v, w2k, w2vt, pbk, pbv)


class _QTile:
    def __init__(self, x, t0, t_end, q_ref, g_ref, o_ref, cols, n_wc):
        self.x, self.t0, self.q_ref, self.g_ref, self.o_ref = x, t0, q_ref, g_ref, o_ref
        self.ncp = t_end // CMP_STRIDE
        self.ns = t_end // SEL_BLOCK
        self.t_lane = t0 + (lax.broadcasted_iota(jnp.int32, (1, cols), 1) & (TQ - 1))
        c_end = lax.broadcasted_iota(jnp.int32, (self.ncp, cols), 0) * CMP_STRIDE + (CMP_BLOCK - 1)
        self.cmp_valid = c_end <= self.t_lane
        self.w_pos = [t0 - WINDOW + TQ * c for c in range(n_wc)]
        self.w_start = [pl.multiple_of(jnp.maximum(p, 0), TQ) for p in self.w_pos]


def _nsa_kernel(qa_ref, qb_ref, kc_ref, vct_ref, ksa_ref, vst_ref, kw_ref, vwt_ref, ga_ref, gb_ref, ovt_ref,
                oa_ref, ob_ref, qsel_scr, acc_scr, m_scr):
    units = [(bb, g) for bb in range(qa_ref.shape[0]) for g in range(NSA_GROUPS)]
    uidx = {u: i for i, u in enumerate(units)}
    n_qt = kw_ref.shape[2] // TQ
    R = HEADS_PER_GROUP
    cols = R * TQ
    NS = ovt_ref.shape[0]
    n_wc = (WINDOW + TQ) // TQ
    step = pl.program_id(1)
    tiles = [_QTile(0, step * TQ, n_qt // 2 * TQ, qa_ref, ga_ref, oa_ref, cols, n_wc),
             _QTile(1, (n_qt - 1 - step) * TQ, n_qt * TQ, qb_ref, gb_ref, ob_ref, cols, n_wc)]
    work = [(c, u) for c in tiles for u in units]
    low = lax.broadcasted_iota(jnp.int32, (cols, LANES), 1) < HEAD_DIM
    sub8 = lax.broadcasted_iota(jnp.int32, (SUBLANES, TQ), 0)
    row_tq = lax.broadcasted_iota(jnp.int32, (TQ, cols), 0)

    def group_q(c, u):
        bb, g = u
        return c.q_ref[bb, R * g:R * (g + 1)].reshape(cols, LANES)

    def cmp_scores(c, u):
        return _nt_dot(kc_ref[u[0], u[1], 0:c.ncp, :], group_q(c, u))

    def cmp_probs(c, s):
        s = jnp.where(c.cmp_valid, s, NEG)
        e = jnp.exp2(s - jnp.max(s, axis=0, keepdims=True))
        inv = 1.0 / jnp.maximum(jnp.sum(e, axis=0, keepdims=True), 1.0e-30)
        return e * jnp.where(c.t_lane >= CMP_BLOCK - 1, inv, 0.0)

    def win_scores(c, u):
        kw = jnp.concatenate([kw_ref[u[0], u[1], pl.ds(c.w_start[n], TQ), :] for n in range(n_wc)], axis=0)
        return _nt_dot(kw, group_q(c, u))

    def win_probs(c, sw):
        parts = []
        for n in range(n_wc):
            sc = sw[TQ * n:TQ * (n + 1)]
            if n == 0:
                sc = jnp.where(c.w_pos[0] + row_tq > c.t_lane - WINDOW, sc, NEG)
            if n == n_wc - 1:
                sc = jnp.where(c.t0 + row_tq <= c.t_lane, sc, NEG)
            else:
                sc = jnp.where(c.w_pos[n] >= 0, sc, NEG)
            parts.append(sc.astype(BF16))
        sw = jnp.concatenate(parts, axis=0)
        return jnp.exp2(sw - jnp.max(sw, axis=0, keepdims=True))

    def win_out(c, u, ew):
        vw = jnp.concatenate([vwt_ref[u[0], u[1], :, pl.ds(c.w_start[n], TQ)] for n in range(n_wc)], axis=1)
        ow = jnp.dot(vw, ew, preferred_element_type=F32)
        return ow[0:HEAD_DIM] * (1.0 / ow[HEAD_DIM:HEAD_DIM + 1])

    def select_blocks(c, u, p):
        ns = c.ns
        psum = p[:, 0:TQ] + p[:, TQ:2 * TQ] + p[:, 2 * TQ:3 * TQ] + p[:, 3 * TQ:4 * TQ]
        p_hi = psum.astype(BF16)
        p_lo = (psum - p_hi.astype(F32)).astype(BF16)
        ov = ovt_ref[0:ns, 0:c.ncp]
        imp = (jnp.dot(ov, p_hi, preferred_element_type=F32) + jnp.dot(ov, p_lo, preferred_element_type=F32))
        blk = lax.broadcasted_iota(jnp.int32, (ns, TQ), 0)
        t_l = c.t0 + lax.broadcasted_iota(jnp.int32, (ns, TQ), 1)
        cur = t_l // SEL_BLOCK
        imp = jnp.where(blk * SEL_BLOCK <= t_l, imp, -1.0)
        imp = jnp.where(blk == 0, FORCED_SCORE, imp)
        imp = jnp.where(blk == cur, FORCED_SCORE, imp)
        imp = jnp.where(blk == cur - 1, FORCED_SCORE, imp)
        nv = ns // SUBLANES
        imp8 = [imp[SUBLANES * j:SUBLANES * (j + 1)] for j in range(nv)]
        rank8 = [jnp.zeros((SUBLANES, TQ), F32) for _ in range(nv)]
        for mm in range(ns):
            row = imp[mm:mm + 1, :]
            jm = mm // SUBLANES
            for j in range(nv):
                if j < jm:
                    ahead = jnp.where(row > imp8[j], 1.0, 0.0)
                elif j > jm:
                    ahead = jnp.where(row >= imp8[j], 1.0, 0.0)
                else:
                    tie = jnp.where(sub8 > (mm % SUBLANES), 1.0, 0.0)
                    ahead = jnp.where(row > imp8[j], 1.0, 0.0) + jnp.where(row == imp8[j], tie, 0.0)
                rank8[j] = rank8[j] + ahead
        pen = jnp.where(jnp.concatenate(rank8, axis=0) < float(SEL_TOPK), 0.0, NEG)
        if ns < NS:
            pen = jnp.concatenate([pen, jnp.zeros((NS - ns, TQ), F32)], axis=0)
        q_t = group_q(c, u).astype(F32).T.astype(BF16)
        qsel_scr[c.x, uidx[u]] = jnp.concatenate(
            [q_t[0:HEAD_DIM], jnp.concatenate([pen.astype(BF16)] * R, axis=1)], axis=0)

    p_c, o_cmp, o_win = {}, {}, {}

    def cmp_job(c, u):
        def finish(p, _):
            p_c[c.x, u] = p
            o_cmp[c.x, u] = jnp.dot(vct_ref[u[0], u[1], :, 0:c.ncp], p.astype(BF16),
                                    preferred_element_type=F32)
        return (lambda: cmp_scores(c, u)), (lambda s: (cmp_probs(c, s), None)), finish

    def win_job(c, u):
        def finish(e_w, _):
            o_win[c.x, u] = win_out(c, u, e_w)
        return (lambda: win_scores(c, u)), (lambda s: (win_probs(c, s), None)), finish

    def sel_job(x, kt, u, t_mask):
        k0 = kt * TK if isinstance(kt, int) else pl.multiple_of(kt * TK, TK)

        def probs(sc):
            if t_mask is not None:
                sc = jnp.where(kt * TK + lax.broadcasted_iota(jnp.int32, (TK, cols), 0) <= t_mask, sc, NEG)
            sc = sc.astype(BF16)
            m_old = m_scr[x, uidx[u]]
            m_new = jnp.maximum(m_old, jnp.max(sc, axis=0, keepdims=True).astype(F32))
            m_scr[x, uidx[u]] = m_new
            return jnp.exp2(sc - m_new.astype(BF16)), jnp.exp2(m_old - m_new)

        def finish(pp, alpha):
            acc_scr[x, uidx[u]] = acc_scr[x, uidx[u]] * alpha + jnp.dot(
                vst_ref[u[0], u[1], :, pl.ds(k0, TK)], pp, preferred_element_type=F32)

        return (lambda: jnp.dot(ksa_ref[u[0], u[1], pl.ds(k0, TK), :], qsel_scr[x, uidx[u]],
                                preferred_element_type=F32)), probs, finish

    def fuse(js):
        return ((lambda: [j[0]() for j in js]),
                (lambda ss: ([j[1](s) for j, s in zip(js, ss)], None)),
                (lambda outs, _: [j[2](*o) for j, o in zip(js, outs)]))

    def sel_tile(x, kt, t_mask):
        return [fuse([sel_job(x, kt, (bb, g), t_mask) for g in range(NSA_GROUPS)]) for bb in range(qa_ref.shape[0])]

    early, late = tiles
    n_slots = (n_qt - 1) * TQ // TK
    n_static = n_slots - n_slots // 2
    n_late = late.t0 // TK
    plan = [cmp_job(c, u) for c in (late, early) for u in units]
    for u in units:
        plan += [lambda u=u: select_blocks(late, u, p_c[late.x, u]), win_job(late, u)]
    plan += sel_tile(late.x, n_late, late.t_lane)
    for s in range(n_static):
        plan += sel_tile(late.x, s, None)
        for u in units[s::n_static]:
            plan += [lambda u=u: select_blocks(early, u, p_c[early.x, u]), win_job(early, u)]
    plan += sel_tile(early.x, early.t0 // TK, early.t_lane)
    for s in range(n_static, n_slots):
        is_late = s < n_late
        plan += sel_tile(jnp.where(is_late, late.x, early.x), jnp.where(is_late, s, s - n_late), None)

    acc_scr[...] = jnp.zeros_like(acc_scr)
    m_scr[...] = jnp.full(m_scr.shape, NEG, F32)
    job_pos = [k for k, e in enumerate(plan) if isinstance(e, tuple)]
    following = dict(zip(job_pos, job_pos[1:]))
    issued = {job_pos[0]: plan[job_pos[0]][0]()}
    for k, entry in enumerate(plan):
        if not isinstance(entry, tuple):
            entry()
            continue
        if k in following:
            issued[following[k]] = plan[following[k]][0]()
        _, probs, finish = entry
        finish(*probs(issued.pop(k)))

    for c in tiles:
        for bb in range(qa_ref.shape[0]):
            glt = c.g_ref[bb]
            heads = []
            for g in range(NSA_GROUPS):
                acc = acc_scr[c.x, uidx[bb, g]]
                o_sel = acc[0:HEAD_DIM] * (1.0 / acc[HEAD_DIM:HEAD_DIM + 1])
                for r in range(R):
                    hh = R * g + r
                    sl = slice(r * TQ, (r + 1) * TQ)
                    heads.append(glt[3 * hh:3 * hh + 1] * o_cmp[c.x, (bb, g)][:, sl]
                                 + glt[3 * hh + 1:3 * hh + 2] * o_sel[:, sl]
                                 + glt[3 * hh + 2:3 * hh + 3] * o_win[c.x, (bb, g)][:, sl])
            c.o_ref[bb] = jnp.concatenate(heads, axis=0).T


def _nsa(qq, kcmp, vcmpt, ksa, vst, kw, vwt, glt, ovt):
    B, H, T, _ = qq.shape
    G = NSA_GROUPS
    NB = NSA_NB
    NCP = kcmp.shape[2]
    n_qt = T // TQ
    grid = (B // NB, n_qt // 2)
    k_blk = lambda n: pl.BlockSpec((NB, G, n, LANES), lambda b, i: (b, 0, 0, 0))
    vt_blk = lambda r, n: pl.BlockSpec((NB, G, r, n), lambda b, i: (b, 0, 0, 0))
    lo_tile = lambda b, i: i
    hi_tile = lambda b, i: n_qt - 1 - i
    q_blk = lambda tile: pl.BlockSpec((NB, H, TQ, LANES), lambda b, i: (b, 0, tile(b, i), 0))
    g_blk = lambda tile: pl.BlockSpec((NB, GATE_ROWS, TQ), lambda b, i: (b, 0, tile(b, i)))
    half = jax.ShapeDtypeStruct((B, T // 2, NSA_WIDTH), F32)
    return pl.pallas_call(
        _nsa_kernel,
        grid=grid,
        in_specs=[
            q_blk(lo_tile), q_blk(hi_tile),
            k_blk(NCP), vt_blk(HEAD_DIM, NCP), k_blk(T), vt_blk(V_ROWS, T), k_blk(T), vt_blk(V_ROWS, T),
            g_blk(lo_tile), g_blk(hi_tile),
            pl.BlockSpec(ovt.shape, lambda b, i: (0, 0)),
        ],
        out_specs=[pl.BlockSpec((NB, TQ, NSA_WIDTH), lambda b, i: (b, i, 0)),
                   pl.BlockSpec((NB, TQ, NSA_WIDTH), lambda b, i: (b, n_qt // 2 - 1 - i, 0))],
        out_shape=[half, half],
        scratch_shapes=[pltpu.VMEM((2, NB * G, LANES, HEADS_PER_GROUP * TQ), BF16),
                        pltpu.VMEM((2, NB * G, V_ROWS, HEADS_PER_GROUP * TQ), F32),
                        pltpu.VMEM((2, NB * G, 1, HEADS_PER_GROUP * TQ), F32)],
        compiler_params=pltpu.CompilerParams(
            dimension_semantics=("parallel", "arbitrary"), vmem_limit_bytes=VMEM_LIMIT),
        name="nsa",
    )(qq, qq, kcmp, vcmpt, ksa, vst, kw, vwt, glt, glt, ovt)


def _s5_prep_kernel(lre_ref, lim_ref, ldt_ref, bre_ref, bim_ref, are_ref, aim_ref, bbre_ref, bbim_ref):
    lre, lim = lre_ref[...], lim_ref[...]
    dt = jnp.exp(ldt_ref[...])
    mag = jnp.exp(lre * dt)
    a_re = mag * jnp.cos(lim * dt)
    a_im = mag * jnp.sin(lim * dt)
    den = lre * lre + lim * lim
    z_re = ((a_re - 1.0) * lre + a_im * lim) / den
    z_im = (a_im * lre - (a_re - 1.0) * lim) / den
    are_ref[...] = a_re
    aim_ref[...] = a_im
    bbre_ref[...] = z_re * bre_ref[...] - z_im * bim_ref[...]
    bbim_ref[...] = z_re * bim_ref[...] + z_im * bre_ref[...]


def _s5_prep(lre, lim, ldt, bre, bim):
    shp = jax.ShapeDtypeStruct(lre.shape, F32)
    return pl.pallas_call(_s5_prep_kernel, out_shape=[shp, shp, shp, shp], name="s5_prep")(lre, lim, ldt, bre, bim)


def _s5_weights_kernel(bbre_ref, bbim_ref, are_ref, aim_ref, crt_ref, cit_ref,
                       we_ref, wct_ref, tp_ref, alre_ref, alim_ref, bd_scr, arow_scr, tp_scr):
    gpb = S5_CPB // S5_GROUP
    bd_scr[...] = jnp.zeros_like(bd_scr)
    tp_scr[...] = jnp.zeros_like(tp_scr)
    for n, ref in enumerate((bbre_ref, bbim_ref, crt_ref, cit_ref)):
        for gl in range(gpb):
            bd_scr[n, S5_GROUP * gl:S5_GROUP * (gl + 1), S5_STATE * gl:S5_STATE * (gl + 1)] = (
                ref[S5_GROUP * gl:S5_GROUP * (gl + 1), :])
    for n, ref in enumerate((are_ref, aim_ref)):
        for gl in range(gpb):
            arow_scr[n, :, S5_STATE * gl:S5_STATE * (gl + 1)] = ref[S5_GROUP * gl:S5_GROUP * gl + 1, :]
    bbre, bbim, crt, cit = bd_scr[0], bd_scr[1], bd_scr[2], bd_scr[3]
    are, aim = arow_scr[0], arow_scr[1]
    pre, pim = jnp.ones_like(are), jnp.zeros_like(are)
    for k in range(S5_L):
        bpr = bbre * pre - bbim * pim
        bpi = bbre * pim + bbim * pre
        i = S5_L - 1 - k
        we_ref[0, S5_CPB * i:S5_CPB * (i + 1), :] = jnp.concatenate([bpr, bpi], axis=1).astype(BF16)
        tap = (_nt_dot(bpr.astype(BF16), crt.astype(BF16)) - _nt_dot(bpi.astype(BF16), cit.astype(BF16)))
        for i in range(S5_L - k):
            j = i + k
            tp_scr[S5_CPB * i:S5_CPB * (i + 1), S5_CPB * j:S5_CPB * (j + 1)] = tap
        pre, pim = pre * are - pim * aim, pre * aim + pim * are
        wct_ref[0, S5_CPB * k:S5_CPB * (k + 1), :] = jnp.concatenate(
            [crt * pre - cit * pim, -(crt * pim + cit * pre)], axis=1).astype(BF16)
    tp_ref[0] = tp_scr[...].astype(BF16)
    alre_ref[0] = pre
    alim_ref[0] = pim


def _s5_weights(bbre, bbim, are, aim, crt, cit):
    nblk = S5_WIDTH // S5_CPB
    sb = S5_CPB // S5_GROUP * S5_STATE
    lc = S5_L * S5_CPB
    rows = pl.BlockSpec((S5_CPB, S5_STATE), lambda q: (q, 0))
    blk = lambda r, c: pl.BlockSpec((1, r, c), lambda q: (q, 0, 0))
    return pl.pallas_call(
        _s5_weights_kernel,
        grid=(nblk,),
        in_specs=[rows] * 6,
        out_specs=[blk(lc, 2 * sb), blk(lc, 2 * sb), blk(lc, lc), blk(1, sb), blk(1, sb)],
        scratch_shapes=[pltpu.VMEM((4, S5_CPB, sb), F32), pltpu.VMEM((2, 1, sb), F32), pltpu.VMEM((lc, lc), F32)],
        out_shape=[jax.ShapeDtypeStruct((nblk, lc, 2 * sb), BF16),
                   jax.ShapeDtypeStruct((nblk, lc, 2 * sb), BF16),
                   jax.ShapeDtypeStruct((nblk, lc, lc), BF16),
                   jax.ShapeDtypeStruct((nblk, 1, sb), F32), jax.ShapeDtypeStruct((nblk, 1, sb), F32)],
        compiler_params=pltpu.CompilerParams(dimension_semantics=("parallel",), vmem_limit_bytes=VMEM_LIMIT),
        name="s5_weights",
    )(bbre, bbim, are, aim, crt, cit)


def _s5_kernel(u_ref, we_ref, wct_ref, tp_ref, alre_ref, alim_ref, d_ref, wg_ref, bg_ref, o_ref, e_scr, st_scr):
    nb, ct, _ = u_ref.shape
    L = S5_L
    nblk = S5_WIDTH // S5_CPB
    sb = S5_CPB // S5_GROUP * S5_STATE
    spb = 2 * sb // LANES
    W = S5_SCAN_SLABS * LANES

    @pl.when(pl.program_id(0) == 0)
    def _():
        st_scr[...] = jnp.zeros_like(st_scr)

    uf = u_ref[...].reshape(nb * ct, L * S5_WIDTH)
    ub = uf.astype(BF16)
    u_blk = [jnp.concatenate([ub[:, i * S5_WIDTH + S5_CPB * q:i * S5_WIDTH + S5_CPB * (q + 1)] for i in range(L)],
                             axis=1) for q in range(nblk)]

    for q in range(nblk):
        e = jnp.dot(u_blk[q], we_ref[q], preferred_element_type=F32)
        for s in range(spb):
            for b in range(nb):
                e_scr[spb * q + s, pl.ds(b, ct, stride=nb), :] = e[b * ct:(b + 1) * ct, LANES * s:LANES * (s + 1)]

    def slabs_of(cs):
        q, r = divmod(cs * LANES, sb)
        return spb * q + r // LANES, spb * q + (sb + r) // LANES

    for j in range(S5_GROUPS * S5_STATE // W):
        sl = [slabs_of(S5_SCAN_SLABS * j + n) for n in range(S5_SCAN_SLABS)]
        sl_re, sl_im = [s[0] for s in sl], [s[1] for s in sl]
        ar = alre_ref[:, W * j:W * (j + 1)]
        ai = alim_ref[:, W * j:W * (j + 1)]
        load = lambda r0, rows, slabs: jnp.concatenate([e_scr[s, pl.ds(r0, rows), :] for s in slabs], axis=1)
        state = lambda slabs: jnp.concatenate([st_scr[:, LANES * s:LANES * (s + 1)] for s in slabs], axis=1)
        sr, si = state(sl_re), state(sl_im)
        for k in range(ct // 2):
            r0 = k * 2 * nb
            er, ei = load(r0, 2 * nb, sl_re), load(r0, 2 * nb, sl_im)
            tr = ar * sr - ai * si + er[0:nb]
            ti = ar * si + ai * sr + ei[0:nb]
            xr = jnp.concatenate([sr, tr], axis=0)
            xi = jnp.concatenate([si, ti], axis=0)
            for n in range(S5_SCAN_SLABS):
                e_scr[sl_re[n], pl.ds(r0, 2 * nb), :] = xr[:, LANES * n:LANES * (n + 1)]
                e_scr[sl_im[n], pl.ds(r0, 2 * nb), :] = xi[:, LANES * n:LANES * (n + 1)]
            sr, si = ar * tr - ai * ti + er[nb:], ar * ti + ai * tr + ei[nb:]
        for n in range(S5_SCAN_SLABS):
            st_scr[:, LANES * sl_re[n]:LANES * (sl_re[n] + 1)] = sr[:, LANES * n:LANES * (n + 1)]
            st_scr[:, LANES * sl_im[n]:LANES * (sl_im[n] + 1)] = si[:, LANES * n:LANES * (n + 1)]

    ys = []
    for q in range(nblk):
        per_b = [jnp.concatenate([e_scr[spb * q + s, pl.ds(b, ct, stride=nb), :] for s in range(spb)], axis=1)
                 for b in range(nb)]
        x_in = jnp.concatenate(per_b, axis=0).astype(BF16)
        ys.append((_nt_dot(x_in, wct_ref[q]) + jnp.dot(u_blk[q], tp_ref[q], preferred_element_type=F32)).astype(BF16))
    outs = []
    for j in range(L):
        yj = jnp.concatenate([y[:, S5_CPB * j:S5_CPB * (j + 1)] for y in ys], axis=1).astype(F32)
        yj = yj + d_ref[...] * uf[:, j * S5_WIDTH:(j + 1) * S5_WIDTH]
        z = _gelu_tanh(yj)
        gate = jnp.dot(z.astype(BF16), wg_ref[...], preferred_element_type=F32) + bg_ref[...]
        outs.append(z * _sigmoid(gate))
    o_ref[...] = jnp.concatenate(outs, axis=1).reshape(nb, ct, L * S5_WIDTH)


def _s5(u4, we, wct, tp, alre, alim, d, w_glu, b_glu):
    nb, nchunks, w4 = u4.shape
    ct = S5_CT
    n_slabs = 2 * S5_GROUPS * S5_STATE // LANES
    resident = lambda a: pl.BlockSpec(a.shape, lambda i: (0,) * a.ndim, pipeline_mode=pl.Buffered(1))
    u_blk = pl.BlockSpec((nb, ct, w4), lambda i: (0, i, 0))
    return pl.pallas_call(
        _s5_kernel,
        grid=(nchunks // ct,),
        in_specs=[u_blk, resident(we), resident(wct), resident(tp), resident(alre), resident(alim),
                  resident(d), resident(w_glu), resident(b_glu)],
        out_specs=u_blk,
        out_shape=jax.ShapeDtypeStruct(u4.shape, F32),
        scratch_shapes=[pltpu.VMEM((n_slabs, nb * ct, LANES), F32), pltpu.VMEM((nb, n_slabs * LANES), F32)],
        compiler_params=pltpu.CompilerParams(dimension_semantics=("arbitrary",), vmem_limit_bytes=VMEM_LIMIT),
        name="s5",
    )(u4, we, wct, tp, alre, alim, d, w_glu, b_glu)


def _final_kernel(x_ref, g_ref, wb_ref, onl_ref, onh_ref, os_ref, wpn_ref, wps_ref, wo_ref, fg_ref, o_ref, os_scr):
    tm = x_ref.shape[1]
    o4 = os_ref[0]
    for i in range(S5_L):
        for s in range(S5_WIDTH // LANES):
            c0 = i * S5_WIDTH + LANES * s
            os_scr[s, pl.ds(i, tm // S5_L, stride=S5_L), :] = o4[:, c0:c0 + LANES]
    first_half = pl.program_id(1) < pl.num_programs(1) // 2

    def silu(v):
        return v * _sigmoid(v)

    rows = [slice(r, r + tm // FINAL_SUB) for r in range(0, tm, tm // FINAL_SUB)]
    hs = [(_rms_scale(x_ref[0, r]) * g_ref[...]).astype(BF16) for r in rows]
    for r, h in zip(rows, hs):
        proj = lambda a, b, h=h: jnp.dot(h, wb_ref[:, a:b], preferred_element_type=F32)
        o_nsa = jnp.where(first_half, onl_ref[0, r], onh_ref[0, r])
        o_s5 = jnp.concatenate([os_scr[s, r] for s in range(S5_WIDTH // LANES)], axis=1)
        a_in = (o_nsa * silu(proj(0, NSA_WIDTH))).astype(BF16)
        b_in = (o_s5 * silu(proj(NSA_WIDTH, NSA_WIDTH + S5_WIDTH))).astype(BF16)
        branch_a = jnp.dot(a_in, wpn_ref[...], preferred_element_type=F32)
        branch_b = jnp.dot(b_in, wps_ref[...], preferred_element_type=F32)
        o1 = NSA_WIDTH + S5_WIDTH
        merged = (_sigmoid(proj(o1, o1 + D_MODEL)) * branch_a
                  + _sigmoid(proj(o1 + D_MODEL, o1 + 2 * D_MODEL)) * branch_b)
        y = x_ref[0, r] + jnp.dot(merged.astype(BF16), wo_ref[...], preferred_element_type=F32)
        o_ref[0, r] = _rms_scale(y) * fg_ref[...]


def _final(x, norm_g, w_b, o_nsa_lo, o_nsa_hi, o_s5, wpn, wps, wo, final_g):
    B, T, D = x.shape
    tm = TM_PROJ
    nh = T // tm // 2
    row_blk = lambda w: pl.BlockSpec((1, tm, w), lambda b, i: (b, i, 0))
    full = lambda a: pl.BlockSpec(a.shape, lambda b, i: (0,) * a.ndim)
    return pl.pallas_call(
        _final_kernel,
        grid=(B, T // tm),
        in_specs=[row_blk(D), full(norm_g), full(w_b),
                  pl.BlockSpec((1, tm, NSA_WIDTH), lambda b, i: (b, jnp.minimum(i, nh - 1), 0)),
                  pl.BlockSpec((1, tm, NSA_WIDTH), lambda b, i: (b, jnp.maximum(i - nh, 0), 0)),
                  pl.BlockSpec((1, tm // S5_L, S5_L * S5_WIDTH), lambda b, i: (b, i, 0)),
                  full(wpn), full(wps), full(wo), full(final_g)],
        out_specs=row_blk(D),
        out_shape=jax.ShapeDtypeStruct((B, T, D), F32),
        scratch_shapes=[pltpu.VMEM((S5_WIDTH // LANES, tm, LANES), F32)],
        compiler_params=pltpu.CompilerParams(
            dimension_semantics=("parallel", "arbitrary"), vmem_limit_bytes=VMEM_LIMIT),
        name="final",
    )(x, norm_g, w_b, o_nsa_lo, o_nsa_hi, o_s5, wpn, wps, wo, final_g)


def _rope_tables(T):
    half = HEAD_DIM // 2
    inv_freq = np.float32(ROPE_THETA) ** (-np.arange(half, dtype=np.float32) / np.float32(half))
    ang = np.arange(T, dtype=np.float32)[:, None] * inv_freq[None, :].astype(np.float32)
    cos, sin = np.cos(ang).astype(np.float32), np.sin(ang).astype(np.float32)
    cos2 = np.concatenate([cos, cos, cos, cos], axis=1)
    sin2 = np.concatenate([-sin, sin, -sin, sin], axis=1)
    return jnp.asarray(cos2), jnp.asarray(sin2)


def _compress_w1(w1):
    half_rows = CMP_STRIDE * HEAD_DIM
    return jnp.concatenate([w1[:half_rows], w1[half_rows:]], axis=1).astype(BF16)


def kernel(x, norm_g, w_in, cmp_pos_k, cmp_pos_v, cmp_w1_k, cmp_w2_k, cmp_w1_v, cmp_w2_v, s5_lam_re, s5_lam_im, s5_log_dt, s5_b_re, s5_b_im, s5_c_re, s5_c_im, s5_d, w_glu, b_glu, w_proj_nsa, w_proj_s5, w_out, final_g):
    B, T, D = x.shape
    assert w_in.shape[0] == 1, "single-layer block"
    NCH = T // CMP_STRIDE
    NS = T // SEL_BLOCK

    w = w_in[0]
    w_a = jnp.concatenate([w[:, :_OFF_GL], jnp.pad(w[:, _OFF_GL:_OFF_GN], ((0, 0), (0, LANES - 24))),
                           w[:, _OFF_U:_OFF_GS]], axis=1).astype(BF16)
    w_b = jnp.concatenate([w[:, _OFF_GN:_OFF_U], w[:, _OFF_GS:]], axis=1).astype(BF16)
    g2 = norm_g[0][None, :]
    cos2, sin2 = _rope_tables(T)

    qq, kc, vc, ksa, vst, kw, vwt, glt, u4 = _inproj(x, g2, w_a, cos2, sin2)

    w2k = jnp.concatenate([jnp.zeros_like(cmp_w2_k[0]), cmp_w2_k[0]], axis=1).astype(BF16)
    w2vt = cmp_w2_v[0].T.astype(BF16)
    kcmp, vcmpt = _compress(kc, vc, _compress_w1(cmp_w1_k[0]), _compress_w1(cmp_w1_v[0]), w2k, w2vt,
                            _pos_bias(cmp_pos_k[0], cmp_w1_k[0]), _pos_bias(cmp_pos_v[0], cmp_w1_v[0]))

    c_start = jnp.arange(NCH) * CMP_STRIDE
    s_start = jnp.arange(NS) * SEL_BLOCK
    ovt = ((c_start[None, :] < s_start[:, None] + SEL_BLOCK) & (c_start[None, :] + CMP_BLOCK > s_start[:, None])
           & (jnp.arange(NCH)[None, :] < NCH - 1)).astype(BF16)
    o_nsa_lo, o_nsa_hi = _nsa(qq, kcmp, vcmpt, ksa, vst, kw, vwt, glt, ovt)

    rep = lambda a: jnp.repeat(a, S5_GROUP, axis=0)
    tr = lambda b: b.transpose(0, 2, 1).reshape(S5_GROUPS * S5_GROUP, S5_STATE)
    a_re, a_im, bb_re, bb_im = _s5_prep(
        rep(s5_lam_re[0]), rep(s5_lam_im[0]),
        rep(jnp.broadcast_to(s5_log_dt[0][:, None], (S5_GROUPS, S5_STATE))),
        tr(s5_b_re[0]), tr(s5_b_im[0]))
    flat = lambda c: c.reshape(S5_GROUPS * S5_GROUP, S5_STATE)
    we, wct, tp, alre, alim = _s5_weights(bb_re, bb_im, a_re, a_im, flat(s5_c_re[0]), flat(s5_c_im[0]))
    o_s5 = _s5(u4, we, wct, tp, alre.reshape(1, -1), alim.reshape(1, -1), s5_d[0][None, :],
               w_glu[0].astype(BF16), b_glu[0][None, :])

    return _final(x, g2, w_b, o_nsa_lo, o_nsa_hi, o_s5, w_proj_nsa[0].astype(BF16), w_proj_s5[0].astype(BF16),
                  w_out[0].astype(BF16), final_g[None, :])
```

```python
import math

import jax
import jax.numpy as jnp
import numpy as np
from jax import lax
from jax.experimental import pallas as pl
from jax.experimental.pallas import tpu as pltpu

F32 = jnp.float32
BF16 = jnp.bfloat16

D_MODEL = 1024
NSA_HEADS = 8
NSA_GROUPS = 2
HEADS_PER_GROUP = 4
HEAD_DIM = 64
NSA_WIDTH = 512
CMP_BLOCK = 32
CMP_STRIDE = 16
CMP_HIDDEN = 256
SEL_BLOCK = 64
SEL_TOPK = 16
WINDOW = 512
ROPE_THETA = 10000.0
FORCED_SCORE = 1.0e4
NEG = -1.0e30
S5_WIDTH = 512
S5_GROUP = 16
S5_GROUPS = 32
S5_STATE = 64
RMS_EPS = 1.0e-6

LANES = 128
SUBLANES = 8
VMEM_LIMIT = 56 * 1024 * 1024

_OFF_GL = 1280
_OFF_GN = 1304
_W_GL, _W_GN, _W_U, _W_GS, _W_MG, _W_END = 1280, 1408, 1920, 2432, 2944, 4992

TM_PROJ = 512
FINAL_SUB = 2
TQ = 128
TK = 512
NSA_NB = 2
SEL_LOOKAHEAD = 1
V_ROWS = 80
GATE_ROWS = 32
S5_L = 8
S5_CPB = 256 // S5_L
S5_CT = 64
S5W_BLOCKS_PER_STEP = 4
S5_SCAN_SLABS = 4


def _gelu_tanh(x):
    c = math.sqrt(2.0 / math.pi)
    return 0.5 * x * (1.0 + jnp.tanh(c * (x + 0.044715 * (x * x * x))))


def _sigmoid(x):
    return 1.0 / (1.0 + jnp.exp(-x))


def _rms_scale(xv):
    ms = jnp.mean(xv * xv, axis=-1, keepdims=True)
    return xv * lax.rsqrt(ms + RMS_EPS)


def _nt_dot(a, b):
    return lax.dot_general(a, b, (((1,), (1,)), ((), ())), preferred_element_type=F32)


def _inproj_kernel(x_ref, g_ref, w_ref, cos_ref, sin_ref,
                   qq_ref, kc_ref, vc_ref, ks_ref, vs_ref, kw_ref, vw_ref, gl_ref, u_ref, us_scr):
    h = (_rms_scale(x_ref[0]) * g_ref[...]).astype(BF16)
    cos2 = cos_ref[...]
    sin2 = sin_ref[...]
    lane = lax.broadcasted_iota(jnp.int32, cos2.shape, 1)
    first_half = (lane & (HEAD_DIM - 1)) < (HEAD_DIM // 2)
    low = lane < HEAD_DIM

    wide = {}

    def proj(a, b):
        for (s0, s1) in ((0, 512), (512, _W_GN), (_W_U, _W_GS)):
            if s0 <= a and b <= s1:
                if s0 not in wide:
                    wide[s0] = jnp.dot(h, w_ref[:, s0:s1], preferred_element_type=F32)
                return wide[s0][:, a - s0:b - s0]
        raise ValueError((a, b))

    def rope(xs):
        partner = jnp.where(first_half, pltpu.roll(xs, 96, 1), pltpu.roll(xs, 32, 1))
        return xs * cos2 + partner * sin2

    scale = HEAD_DIM ** -0.5 * math.log2(math.e)
    for i in range(NSA_HEADS // 2):
        xs = proj(LANES * i, LANES * (i + 1)) * scale
        xr = rope(xs)
        qq_ref[0, 2 * i] = jnp.where(low, xr, pltpu.roll(xs, 64, 1)).astype(BF16)
        qq_ref[0, 2 * i + 1] = jnp.where(low, pltpu.roll(xr, 64, 1), xs).astype(BF16)

    kc_ref[0] = proj(512, 640)
    vc_ref[0] = proj(640, 768)
    tm = cos2.shape[0]
    t_row = pl.program_id(1) * tm + lax.broadcasted_iota(jnp.int32, cos2.shape, 0)
    blk_onehot = jnp.where(lane - HEAD_DIM == t_row // SEL_BLOCK, 1.0, 0.0)
    ones_rows = jnp.where(lax.broadcasted_iota(jnp.int32, (V_ROWS - HEAD_DIM, tm), 0) == 0, 1.0, 0.0)
    for (off, k_out, v_out, k_pad) in ((768, ks_ref, vs_ref, blk_onehot), (1024, kw_ref, vw_ref, 0.0)):
        kr = rope(proj(off, off + LANES))
        k_out[0, 0] = jnp.where(low, kr, k_pad).astype(BF16)
        k_out[0, 1] = jnp.where(low, pltpu.roll(kr, 64, 1), k_pad).astype(BF16)
        vt = proj(off + LANES, off + 2 * LANES).T
        for g in range(NSA_GROUPS):
            v_out[0, g] = jnp.concatenate([vt[HEAD_DIM * g:HEAD_DIM * (g + 1)], ones_rows], axis=0).astype(BF16)
    gl_ref[0] = _sigmoid(proj(_W_GL, _W_GN)).T[0:GATE_ROWS]
    uv = proj(_W_U, _W_GS)
    for s in range(S5_WIDTH // LANES):
        us_scr[s] = uv[:, LANES * s:LANES * (s + 1)]
    for i in range(S5_L):
        for s in range(S5_WIDTH // LANES):
            c0 = i * S5_WIDTH + LANES * s
            u_ref[0, :, c0:c0 + LANES] = us_scr[s, pl.ds(i, tm // S5_L, stride=S5_L), :]


def _inproj(x, norm_g, w_a, cos2, sin2):
    B, T, D = x.shape
    tm = TM_PROJ
    grid = (B, T // tm)
    row_blk = lambda w: pl.BlockSpec((1, tm, w), lambda b, i: (b, i, 0))
    kv_blk = pl.BlockSpec((1, NSA_GROUPS, tm, LANES), lambda b, i: (b, 0, i, 0))
    kv_shape = jax.ShapeDtypeStruct((B, NSA_GROUPS, T, LANES), BF16)
    vt_blk = pl.BlockSpec((1, NSA_GROUPS, V_ROWS, tm), lambda b, i: (b, 0, 0, i))
    vt_shape = jax.ShapeDtypeStruct((B, NSA_GROUPS, V_ROWS, T), BF16)
    return pl.pallas_call(
        _inproj_kernel,
        grid=grid,
        in_specs=[
            row_blk(D),
            pl.BlockSpec((1, D), lambda b, i: (0, 0)),
            pl.BlockSpec(w_a.shape, lambda b, i: (0, 0), pipeline_mode=pl.Buffered(1)),
            pl.BlockSpec((tm, LANES), lambda b, i: (i, 0)),
            pl.BlockSpec((tm, LANES), lambda b, i: (i, 0)),
        ],
        out_specs=[
            pl.BlockSpec((1, NSA_HEADS, tm, LANES), lambda b, i: (b, 0, i, 0)),
            row_blk(LANES), row_blk(LANES),
            kv_blk, vt_blk, kv_blk, vt_blk,
            pl.BlockSpec((1, GATE_ROWS, tm), lambda b, i: (b, 0, i)),
            pl.BlockSpec((1, tm // S5_L, S5_L * S5_WIDTH), lambda b, i: (b, i, 0)),
        ],
        out_shape=[
            jax.ShapeDtypeStruct((B, NSA_HEADS, T, LANES), BF16),
            jax.ShapeDtypeStruct((B, T, LANES), F32), jax.ShapeDtypeStruct((B, T, LANES), F32),
            kv_shape, vt_shape, kv_shape, vt_shape,
            jax.ShapeDtypeStruct((B, GATE_ROWS, T), F32),
            jax.ShapeDtypeStruct((B, T // S5_L, S5_L * S5_WIDTH), F32),
        ],
        scratch_shapes=[pltpu.VMEM((S5_WIDTH // LANES, tm, LANES), F32)],
        compiler_params=pltpu.CompilerParams(
            dimension_semantics=("parallel", "arbitrary"), vmem_limit_bytes=VMEM_LIMIT),
        name="inproj",
    )(x, norm_g, w_a, cos2, sin2)


def _compress_kernel(kc_ref, vc_ref, w1k_ref, w1v_ref, w2k_ref, w2vt_ref, pbk_ref, pbv_ref, ko_ref, vo_ref):
    nch = ko_ref.shape[2]
    H = CMP_HIDDEN

    def hidden(c_ref, w1_ref, pb_ref):
        acc = [jnp.zeros((nch, 2 * H), F32) for _ in range(NSA_GROUPS)]
        for j in range(CMP_STRIDE):
            rows = c_ref[0, pl.ds(j, nch, stride=CMP_STRIDE), :].astype(BF16)
            wj = w1_ref[HEAD_DIM * j:HEAD_DIM * (j + 1), :]
            for g in range(NSA_GROUPS):
                acc[g] = acc[g] + jnp.dot(rows[:, HEAD_DIM * g:HEAD_DIM * (g + 1)], wj, preferred_element_type=F32)
        pw = jnp.dot(pb_ref[...].astype(BF16), w1_ref[...], preferred_element_type=F32)
        pos_bias = pw[0:1, 0:H] + pw[1:2, H:]
        return [_gelu_tanh(a[:, 0:H] + pltpu.roll(a[:, H:], nch - 1, 0) + pos_bias).astype(BF16) for a in acc]

    hk = hidden(kc_ref, w1k_ref, pbk_ref)
    hv = hidden(vc_ref, w1v_ref, pbv_ref)
    for g in range(NSA_GROUPS):
        ko_ref[0, g] = jnp.dot(hk[g], w2k_ref[...], preferred_element_type=F32).astype(BF16)
        vo_ref[0, g] = _nt_dot(w2vt_ref[...], hv[g]).astype(BF16)


def _compress(kc, vc, w1k, w1v, w2k, w2vt, pbk, pbv):
    B, T, _ = kc.shape
    G = NSA_GROUPS
    nch = T // CMP_STRIDE
    c_blk = pl.BlockSpec((1, T, LANES), lambda b: (b, 0, 0))
    full = lambda a: pl.BlockSpec(a.shape, lambda b: (0,) * a.ndim)
    return pl.pallas_call(
        _compress_kernel,
        grid=(B,),
        in_specs=[c_blk, c_blk, full(w1k), full(w1v), full(w2k), full(w2vt), full(pbk), full(pbv)],
        out_specs=[pl.BlockSpec((1, G, nch, LANES), lambda b: (b, 0, 0, 0)),
                   pl.BlockSpec((1, G, HEAD_DIM, nch), lambda b: (b, 0, 0, 0))],
        out_shape=[jax.ShapeDtypeStruct((B, G, nch, LANES), BF16),
                   jax.ShapeDtypeStruct((B, G, HEAD_DIM, nch), BF16)],
        compiler_params=pltpu.CompilerParams(dimension_semantics=("parallel",), vmem_limit_bytes=VMEM_LIMIT),
        name="compress",
    )(kc, vc, w1k, w1v, w2k, w2vt, pbk, pbv)


class _QTile:
    def __init__(self, x, t0, t_end, q_ref, g_ref, o_ref, cols, n_wc):
        self.x, self.t0, self.q_ref, self.g_ref, self.o_ref = x, t0, q_ref, g_ref, o_ref
        self.ncp = t_end // CMP_STRIDE
        self.ns = t_end // SEL_BLOCK
        self.t_lane = t0 + (lax.broadcasted_iota(jnp.int32, (1, cols), 1) & (TQ - 1))
        c_end = lax.broadcasted_iota(jnp.int32, (self.ncp, cols), 0) * CMP_STRIDE + (CMP_BLOCK - 1)
        self.cmp_valid = c_end <= self.t_lane
        self.w_pos = [t0 - WINDOW + TQ * c for c in range(n_wc)]
        self.w_start = [pl.multiple_of(jnp.maximum(p, 0), TQ) for p in self.w_pos]


def _nsa_kernel(qa_ref, qb_ref, kc_ref, vct_ref, ksa_ref, vst_ref, kw_ref, vwt_ref, ga_ref, gb_ref, ovt_ref,
                oa_ref, ob_ref, qsel_scr, acc_scr, m_scr):
    units = [(bb, g) for bb in range(qa_ref.shape[0]) for g in range(NSA_GROUPS)]
    uidx = {u: i for i, u in enumerate(units)}
    n_qt = kw_ref.shape[2] // TQ
    R = HEADS_PER_GROUP
    cols = R * TQ
    NS = ovt_ref.shape[0]
    n_wc = (WINDOW + TQ) // TQ
    step = pl.program_id(1)
    tiles = [_QTile(0, step * TQ, n_qt // 2 * TQ, qa_ref, ga_ref, oa_ref, cols, n_wc),
             _QTile(1, (n_qt - 1 - step) * TQ, n_qt * TQ, qb_ref, gb_ref, ob_ref, cols, n_wc)]
    work = [(c, u) for c in tiles for u in units]
    low = lax.broadcasted_iota(jnp.int32, (cols, LANES), 1) < HEAD_DIM
    sub8 = lax.broadcasted_iota(jnp.int32, (SUBLANES, TQ), 0)
    row_tq = lax.broadcasted_iota(jnp.int32, (TQ, cols), 0)

    def group_q(c, u):
        bb, g = u
        return c.q_ref[bb, R * g:R * (g + 1)].reshape(cols, LANES)

    def cmp_scores(c, u):
        return _nt_dot(kc_ref[u[0], u[1], 0:c.ncp, :], group_q(c, u))

    def cmp_probs(c, s):
        s = jnp.where(c.cmp_valid, s, NEG)
        e = jnp.exp2(s - jnp.max(s, axis=0, keepdims=True))
        inv = 1.0 / jnp.maximum(jnp.sum(e, axis=0, keepdims=True), 1.0e-30)
        return e * jnp.where(c.t_lane >= CMP_BLOCK - 1, inv, 0.0)

    def win_scores(c, u):
        kw = jnp.concatenate([kw_ref[u[0], u[1], pl.ds(c.w_start[n], TQ), :] for n in range(n_wc)], axis=0)
        return _nt_dot(kw, group_q(c, u))

    def win_probs(c, sw):
        parts = []
        for n in range(n_wc):
            sc = sw[TQ * n:TQ * (n + 1)]
            if n == 0:
                sc = jnp.where(c.w_pos[0] + row_tq > c.t_lane - WINDOW, sc, NEG)
            if n == n_wc - 1:
                sc = jnp.where(c.t0 + row_tq <= c.t_lane, sc, NEG)
            else:
                sc = jnp.where(c.w_pos[n] >= 0, sc, NEG)
            parts.append(sc.astype(BF16))
        sw = jnp.concatenate(parts, axis=0)
        return jnp.exp2(sw - jnp.max(sw, axis=0, keepdims=True))

    def win_out(c, u, ew):
        vw = jnp.concatenate([vwt_ref[u[0], u[1], :, pl.ds(c.w_start[n], TQ)] for n in range(n_wc)], axis=1)
        ow = jnp.dot(vw, ew, preferred_element_type=F32)
        return ow[0:HEAD_DIM] * (1.0 / ow[HEAD_DIM:HEAD_DIM + 1])

    def select_blocks(c, u, p):
        ns = c.ns
        psum = p[:, 0:TQ] + p[:, TQ:2 * TQ] + p[:, 2 * TQ:3 * TQ] + p[:, 3 * TQ:4 * TQ]
        p_hi = psum.astype(BF16)
        p_lo = (psum - p_hi.astype(F32)).astype(BF16)
        ov = ovt_ref[0:ns, 0:c.ncp]
        imp = (jnp.dot(ov, p_hi, preferred_element_type=F32) + jnp.dot(ov, p_lo, preferred_element_type=F32))
        blk = lax.broadcasted_iota(jnp.int32, (ns, TQ), 0)
        t_l = c.t0 + lax.broadcasted_iota(jnp.int32, (ns, TQ), 1)
        cur = t_l // SEL_BLOCK
        imp = jnp.where(blk * SEL_BLOCK <= t_l, imp, -1.0)
        imp = jnp.where(blk == 0, FORCED_SCORE, imp)
        imp = jnp.where(blk == cur, FORCED_SCORE, imp)
        imp = jnp.where(blk == cur - 1, FORCED_SCORE, imp)
        nv = ns // SUBLANES
        imp8 = [imp[SUBLANES * j:SUBLANES * (j + 1)] for j in range(nv)]
        rank8 = [jnp.zeros((SUBLANES, TQ), F32) for _ in range(nv)]
        for mm in range(ns):
            row = imp[mm:mm + 1, :]
            jm = mm // SUBLANES
            for j in range(nv):
                if j < jm:
                    ahead = jnp.where(row > imp8[j], 1.0, 0.0)
                elif j > jm:
                    ahead = jnp.where(row >= imp8[j], 1.0, 0.0)
                else:
                    tie = jnp.where(sub8 > (mm % SUBLANES), 1.0, 0.0)
                    ahead = jnp.where(row > imp8[j], 1.0, 0.0) + jnp.where(row == imp8[j], tie, 0.0)
                rank8[j] = rank8[j] + ahead
        pen = jnp.where(jnp.concatenate(rank8, axis=0) < float(SEL_TOPK), 0.0, NEG)
        if ns < NS:
            pen = jnp.concatenate([pen, jnp.zeros((NS - ns, TQ), F32)], axis=0)
        q_t = group_q(c, u).astype(F32).T.astype(BF16)
        qsel_scr[c.x, uidx[u]] = jnp.concatenate(
            [q_t[0:HEAD_DIM], jnp.concatenate([pen.astype(BF16)] * R, axis=1)], axis=0)

    p_c, o_cmp, o_win = {}, {}, {}

    def cmp_job(c, u):
        def finish(p, _):
            p_c[c.x, u] = p
            o_cmp[c.x, u] = jnp.dot(vct_ref[u[0], u[1], :, 0:c.ncp], p.astype(BF16),
                                    preferred_element_type=F32)
        return (lambda: cmp_scores(c, u)), (lambda s: (cmp_probs(c, s), None)), finish

    def win_job(c, u):
        def finish(e_w, _):
            o_win[c.x, u] = win_out(c, u, e_w)
        return (lambda: win_scores(c, u)), (lambda s: (win_probs(c, s), None)), finish

    def sel_job(x, kt, u, t_mask):
        k0 = kt * TK if isinstance(kt, int) else pl.multiple_of(kt * TK, TK)

        def probs(sc):
            if t_mask is not None:
                sc = jnp.where(kt * TK + lax.broadcasted_iota(jnp.int32, (TK, cols), 0) <= t_mask, sc, NEG)
            sc = sc.astype(BF16)
            m_old = m_scr[x, uidx[u]]
            m_new = jnp.maximum(m_old, jnp.max(sc, axis=0, keepdims=True).astype(F32))
            m_scr[x, uidx[u]] = m_new
            return jnp.exp2(sc - m_new.astype(BF16)), jnp.exp2(m_old - m_new)

        def finish(pp, alpha):
            acc_scr[x, uidx[u]] = acc_scr[x, uidx[u]] * alpha + jnp.dot(
                vst_ref[u[0], u[1], :, pl.ds(k0, TK)], pp, preferred_element_type=F32)

        return (lambda: jnp.dot(ksa_ref[u[0], u[1], pl.ds(k0, TK), :], qsel_scr[x, uidx[u]],
                                preferred_element_type=F32)), probs, finish

    def fuse(js):
        return ((lambda: [j[0]() for j in js]),
                (lambda ss: ([j[1](s) for j, s in zip(js, ss)], None)),
                (lambda outs, _: [j[2](*o) for j, o in zip(js, outs)]))

    def sel_tile(x, kt, t_mask):
        return [fuse([sel_job(x, kt, (bb, g), t_mask) for g in range(NSA_GROUPS)]) for bb in range(qa_ref.shape[0])]

    early, late = tiles
    n_slots = (n_qt - 1) * TQ // TK
    n_static = n_slots - n_slots // 2
    n_late = late.t0 // TK
    plan = [cmp_job(c, u) for c in (late, early) for u in units]
    for u in units:
        plan += [lambda u=u: select_blocks(late, u, p_c[late.x, u]), win_job(late, u)]
    plan += sel_tile(late.x, n_late, late.t_lane)
    for s in range(n_static):
        plan += sel_tile(late.x, s, None)
        for u in units[s::n_static]:
            plan += [lambda u=u: select_blocks(early, u, p_c[early.x, u]), win_job(early, u)]
    plan += sel_tile(early.x, early.t0 // TK, early.t_lane)
    for s in range(n_static, n_slots):
        is_late = s < n_late
        plan += sel_tile(jnp.where(is_late, late.x, early.x), jnp.where(is_late, s, s - n_late), None)

    acc_scr[...] = jnp.zeros_like(acc_scr)
    m_scr[...] = jnp.full(m_scr.shape, NEG, F32)
    job_pos = [k for k, e in enumerate(plan) if isinstance(e, tuple)]
    following = dict(zip(job_pos, job_pos[1:]))
    issued = {job_pos[0]: plan[job_pos[0]][0]()}
    for k, entry in enumerate(plan):
        if not isinstance(entry, tuple):
            entry()
            continue
        if k in following:
            issued[following[k]] = plan[following[k]][0]()
        _, probs, finish = entry
        finish(*probs(issued.pop(k)))

    for c in tiles:
        for bb in range(qa_ref.shape[0]):
            glt = c.g_ref[bb]
            heads = []
            for g in range(NSA_GROUPS):
                acc = acc_scr[c.x, uidx[bb, g]]
                o_sel = acc[0:HEAD_DIM] * (1.0 / acc[HEAD_DIM:HEAD_DIM + 1])
                for r in range(R):
                    hh = R * g + r
                    sl = slice(r * TQ, (r + 1) * TQ)
                    heads.append(glt[3 * hh:3 * hh + 1] * o_cmp[c.x, (bb, g)][:, sl]
                                 + glt[3 * hh + 1:3 * hh + 2] * o_sel[:, sl]
                                 + glt[3 * hh + 2:3 * hh + 3] * o_win[c.x, (bb, g)][:, sl])
            c.o_ref[bb] = jnp.concatenate(heads, axis=0).T


def _nsa(qq, kcmp, vcmpt, ksa, vst, kw, vwt, glt, ovt):
    B, H, T, _ = qq.shape
    G = NSA_GROUPS
    NB = NSA_NB
    NCP = kcmp.shape[2]
    n_qt = T // TQ
    grid = (B // NB, n_qt // 2)
    k_blk = lambda n: pl.BlockSpec((NB, G, n, LANES), lambda b, i: (b, 0, 0, 0))
    vt_blk = lambda r, n: pl.BlockSpec((NB, G, r, n), lambda b, i: (b, 0, 0, 0))
    lo_tile = lambda b, i: i
    hi_tile = lambda b, i: n_qt - 1 - i
    q_blk = lambda tile: pl.BlockSpec((NB, H, TQ, LANES), lambda b, i: (b, 0, tile(b, i), 0))
    g_blk = lambda tile: pl.BlockSpec((NB, GATE_ROWS, TQ), lambda b, i: (b, 0, tile(b, i)))
    half = jax.ShapeDtypeStruct((B, T // 2, NSA_WIDTH), F32)
    return pl.pallas_call(
        _nsa_kernel,
        grid=grid,
        in_specs=[
            q_blk(lo_tile), q_blk(hi_tile),
            k_blk(NCP), vt_blk(HEAD_DIM, NCP), k_blk(T), vt_blk(V_ROWS, T), k_blk(T), vt_blk(V_ROWS, T),
            g_blk(lo_tile), g_blk(hi_tile),
            pl.BlockSpec(ovt.shape, lambda b, i: (0, 0)),
        ],
        out_specs=[pl.BlockSpec((NB, TQ, NSA_WIDTH), lambda b, i: (b, i, 0)),
                   pl.BlockSpec((NB, TQ, NSA_WIDTH), lambda b, i: (b, n_qt // 2 - 1 - i, 0))],
        out_shape=[half, half],
        scratch_shapes=[pltpu.VMEM((2, NB * G, LANES, HEADS_PER_GROUP * TQ), BF16),
                        pltpu.VMEM((2, NB * G, V_ROWS, HEADS_PER_GROUP * TQ), F32),
                        pltpu.VMEM((2, NB * G, 1, HEADS_PER_GROUP * TQ), F32)],
        compiler_params=pltpu.CompilerParams(
            dimension_semantics=("parallel", "arbitrary"), vmem_limit_bytes=VMEM_LIMIT),
        name="nsa",
    )(qq, qq, kcmp, vcmpt, ksa, vst, kw, vwt, glt, glt, ovt)


def _s5_prep_kernel(lre_ref, lim_ref, ldt_ref, bre_ref, bim_ref, are_ref, aim_ref, bbre_ref, bbim_ref):
    lre, lim = lre_ref[...], lim_ref[...]
    dt = jnp.exp(ldt_ref[...])
    mag = jnp.exp(lre * dt)
    a_re = mag * jnp.cos(lim * dt)
    a_im = mag * jnp.sin(lim * dt)
    den = lre * lre + lim * lim
    z_re = ((a_re - 1.0) * lre + a_im * lim) / den
    z_im = (a_im * lre - (a_re - 1.0) * lim) / den
    are_ref[...] = a_re
    aim_ref[...] = a_im
    bbre_ref[...] = z_re * bre_ref[...] - z_im * bim_ref[...]
    bbim_ref[...] = z_re * bim_ref[...] + z_im * bre_ref[...]


def _s5_prep(lre, lim, ldt, bre, bim):
    shp = jax.ShapeDtypeStruct(lre.shape, F32)
    return pl.pallas_call(_s5_prep_kernel, out_shape=[shp, shp, shp, shp], name="s5_prep")(lre, lim, ldt, bre, bim)


def _s5_weights_kernel(bbre_ref, bbim_ref, are_ref, aim_ref, crt_ref, cit_ref,
                       we_ref, wct_ref, tp_ref, alre_ref, alim_ref, bd_scr, arow_scr, tp_scr):
    gpb = S5_CPB // S5_GROUP
    for q in range(we_ref.shape[0]):
        r0 = S5_CPB * q
        bd_scr[...] = jnp.zeros_like(bd_scr)
        tp_scr[...] = jnp.zeros_like(tp_scr)
        for n, ref in enumerate((bbre_ref, bbim_ref, crt_ref, cit_ref)):
            for gl in range(gpb):
                bd_scr[n, S5_GROUP * gl:S5_GROUP * (gl + 1), S5_STATE * gl:S5_STATE * (gl + 1)] = (
                    ref[r0 + S5_GROUP * gl:r0 + S5_GROUP * (gl + 1), :])
        for n, ref in enumerate((are_ref, aim_ref)):
            for gl in range(gpb):
                arow_scr[n, :, S5_STATE * gl:S5_STATE * (gl + 1)] = ref[r0 + S5_GROUP * gl:r0 + S5_GROUP * gl + 1, :]
        bbre, bbim, crt, cit = bd_scr[0], bd_scr[1], bd_scr[2], bd_scr[3]
        are, aim = arow_scr[0], arow_scr[1]
        pre, pim = jnp.ones_like(are), jnp.zeros_like(are)
        for k in range(S5_L):
            bpr = bbre * pre - bbim * pim
            bpi = bbre * pim + bbim * pre
            i = S5_L - 1 - k
            we_ref[q, S5_CPB * i:S5_CPB * (i + 1), :] = jnp.concatenate([bpr, bpi], axis=1).astype(BF16)
            tap = _nt_dot(bpr.astype(BF16), crt.astype(BF16)) - _nt_dot(bpi.astype(BF16), cit.astype(BF16))
            for i in range(S5_L - k):
                j = i + k
                tp_scr[S5_CPB * i:S5_CPB * (i + 1), S5_CPB * j:S5_CPB * (j + 1)] = tap
            pre, pim = pre * are - pim * aim, pre * aim + pim * are
            wct_ref[q, S5_CPB * k:S5_CPB * (k + 1), :] = jnp.concatenate(
                [crt * pre - cit * pim, -(crt * pim + cit * pre)], axis=1).astype(BF16)
        tp_ref[q] = tp_scr[...].astype(BF16)
        alre_ref[q] = pre
        alim_ref[q] = pim


def _s5_weights(bbre, bbim, are, aim, crt, cit):
    nblk = S5_WIDTH // S5_CPB
    sb = S5_CPB // S5_GROUP * S5_STATE
    lc = S5_L * S5_CPB
    bps = S5W_BLOCKS_PER_STEP
    rows = pl.BlockSpec((bps * S5_CPB, S5_STATE), lambda q: (q, 0))
    blk = lambda r, c: pl.BlockSpec((bps, r, c), lambda q: (q, 0, 0))
    return pl.pallas_call(
        _s5_weights_kernel,
        grid=(nblk // bps,),
        in_specs=[rows] * 6,
        out_specs=[blk(lc, 2 * sb), blk(lc, 2 * sb), blk(lc, lc), blk(1, sb), blk(1, sb)],
        scratch_shapes=[pltpu.VMEM((4, S5_CPB, sb), F32), pltpu.VMEM((2, 1, sb), F32), pltpu.VMEM((lc, lc), F32)],
        out_shape=[jax.ShapeDtypeStruct((nblk, lc, 2 * sb), BF16),
                   jax.ShapeDtypeStruct((nblk, lc, 2 * sb), BF16),
                   jax.ShapeDtypeStruct((nblk, lc, lc), BF16),
                   jax.ShapeDtypeStruct((nblk, 1, sb), F32), jax.ShapeDtypeStruct((nblk, 1, sb), F32)],
        compiler_params=pltpu.CompilerParams(dimension_semantics=("parallel",), vmem_limit_bytes=VMEM_LIMIT),
        name="s5_weights",
    )(bbre, bbim, are, aim, crt, cit)


def _s5_kernel(u_ref, we_ref, wct_ref, tp_ref, alre_ref, alim_ref, d_ref, wg_ref, bg_ref, o_ref, e_scr, st_scr):
    nb, ct, _ = u_ref.shape
    L = S5_L
    nblk = S5_WIDTH // S5_CPB
    sb = S5_CPB // S5_GROUP * S5_STATE
    spb = 2 * sb // LANES
    W = S5_SCAN_SLABS * LANES

    @pl.when(pl.program_id(0) == 0)
    def _():
        st_scr[...] = jnp.zeros_like(st_scr)

    uf = u_ref[...].reshape(nb * ct, L * S5_WIDTH)
    ub = uf.astype(BF16)
    u_blk = [jnp.concatenate([ub[:, i * S5_WIDTH + S5_CPB * q:i * S5_WIDTH + S5_CPB * (q + 1)] for i in range(L)],
                             axis=1) for q in range(nblk)]

    for q in range(nblk):
        e = jnp.dot(u_blk[q], we_ref[q], preferred_element_type=F32)
        for s in range(spb):
            for b in range(nb):
                e_scr[spb * q + s, pl.ds(b, ct, stride=nb), :] = e[b * ct:(b + 1) * ct, LANES * s:LANES * (s + 1)]

    def slabs_of(cs):
        q, r = divmod(cs * LANES, sb)
        return spb * q + r // LANES, spb * q + (sb + r) // LANES

    for j in range(S5_GROUPS * S5_STATE // W):
        sl = [slabs_of(S5_SCAN_SLABS * j + n) for n in range(S5_SCAN_SLABS)]
        sl_re, sl_im = [s[0] for s in sl], [s[1] for s in sl]
        ar = alre_ref[:, W * j:W * (j + 1)]
        ai = alim_ref[:, W * j:W * (j + 1)]
        load = lambda r0, rows, slabs: jnp.concatenate([e_scr[s, pl.ds(r0, rows), :] for s in slabs], axis=1)
        state = lambda slabs: jnp.concatenate([st_scr[:, LANES * s:LANES * (s + 1)] for s in slabs], axis=1)
        sr, si = state(sl_re), state(sl_im)
        for k in range(ct // 2):
            r0 = k * 2 * nb
            er, ei = load(r0, 2 * nb, sl_re), load(r0, 2 * nb, sl_im)
            tr = ar * sr - ai * si + er[0:nb]
            ti = ar * si + ai * sr + ei[0:nb]
            xr = jnp.concatenate([sr, tr], axis=0)
            xi = jnp.concatenate([si, ti], axis=0)
            for n in range(S5_SCAN_SLABS):
                e_scr[sl_re[n], pl.ds(r0, 2 * nb), :] = xr[:, LANES * n:LANES * (n + 1)]
                e_scr[sl_im[n], pl.ds(r0, 2 * nb), :] = xi[:, LANES * n:LANES * (n + 1)]
            sr, si = ar * tr - ai * ti + er[nb:], ar * ti + ai * tr + ei[nb:]
        for n in range(S5_SCAN_SLABS):
            st_scr[:, LANES * sl_re[n]:LANES * (sl_re[n] + 1)] = sr[:, LANES * n:LANES * (n + 1)]
            st_scr[:, LANES * sl_im[n]:LANES * (sl_im[n] + 1)] = si[:, LANES * n:LANES * (n + 1)]

    ys = []
    for q in range(nblk):
        per_b = [jnp.concatenate([e_scr[spb * q + s, pl.ds(b, ct, stride=nb), :] for s in range(spb)], axis=1)
                 for b in range(nb)]
        x_in = jnp.concatenate(per_b, axis=0).astype(BF16)
        ys.append((_nt_dot(x_in, wct_ref[q]) + jnp.dot(u_blk[q], tp_ref[q], preferred_element_type=F32)).astype(BF16))
    outs = []
    for j in range(L):
        yj = jnp.concatenate([y[:, S5_CPB * j:S5_CPB * (j + 1)] for y in ys], axis=1).astype(F32)
        yj = yj + d_ref[...] * uf[:, j * S5_WIDTH:(j + 1) * S5_WIDTH]
        z = _gelu_tanh(yj)
        gate = jnp.dot(z.astype(BF16), wg_ref[...], preferred_element_type=F32) + bg_ref[...]
        outs.append(z * _sigmoid(gate))
    o_ref[...] = jnp.concatenate(outs, axis=1).reshape(nb, ct, L * S5_WIDTH)


def _s5(u4, we, wct, tp, alre, alim, d, w_glu, b_glu):
    nb, nchunks, w4 = u4.shape
    ct = S5_CT
    n_slabs = 2 * S5_GROUPS * S5_STATE // LANES
    resident = lambda a: pl.BlockSpec(a.shape, lambda i: (0,) * a.ndim, pipeline_mode=pl.Buffered(1))
    u_blk = pl.BlockSpec((nb, ct, w4), lambda i: (0, i, 0))
    return pl.pallas_call(
        _s5_kernel,
        grid=(nchunks // ct,),
        in_specs=[u_blk, resident(we), resident(wct), resident(tp), resident(alre), resident(alim),
                  resident(d), resident(w_glu), resident(b_glu)],
        out_specs=u_blk,
        out_shape=jax.ShapeDtypeStruct(u4.shape, F32),
        scratch_shapes=[pltpu.VMEM((n_slabs, nb * ct, LANES), F32), pltpu.VMEM((nb, n_slabs * LANES), F32)],
        compiler_params=pltpu.CompilerParams(dimension_semantics=("arbitrary",), vmem_limit_bytes=VMEM_LIMIT),
        name="s5",
    )(u4, we, wct, tp, alre, alim, d, w_glu, b_glu)


def _final_kernel(x_ref, g_ref, wb_ref, onl_ref, onh_ref, os_ref, wpn_ref, wps_ref, wo_ref, fg_ref, o_ref, os_scr):
    tm = x_ref.shape[1]
    o4 = os_ref[0]
    for i in range(S5_L):
        for s in range(S5_WIDTH // LANES):
            c0 = i * S5_WIDTH + LANES * s
            os_scr[s, pl.ds(i, tm // S5_L, stride=S5_L), :] = o4[:, c0:c0 + LANES]
    first_half = pl.program_id(1) < pl.num_programs(1) // 2

    def silu(v):
        return v * _sigmoid(v)

    rows = [slice(r, r + tm // FINAL_SUB) for r in range(0, tm, tm // FINAL_SUB)]
    hs = [(_rms_scale(x_ref[0, r]) * g_ref[...]).astype(BF16) for r in rows]
    for r, h in zip(rows, hs):
        proj = lambda a, b, h=h: jnp.dot(h, wb_ref[:, a:b], preferred_element_type=F32)
        o_nsa = jnp.where(first_half, onl_ref[0, r], onh_ref[0, r])
        o_s5 = jnp.concatenate([os_scr[s, r] for s in range(S5_WIDTH // LANES)], axis=1)
        a_in = (o_nsa * silu(proj(_W_GN, _W_U))).astype(BF16)
        b_in = (o_s5 * silu(proj(_W_GS, _W_MG))).astype(BF16)
        branch_a = jnp.dot(a_in, wpn_ref[...], preferred_element_type=F32)
        branch_b = jnp.dot(b_in, wps_ref[...], preferred_element_type=F32)
        merged = (_sigmoid(proj(_W_MG, _W_MG + D_MODEL)) * branch_a
                  + _sigmoid(proj(_W_MG + D_MODEL, _W_END)) * branch_b)
        y = x_ref[0, r] + jnp.dot(merged.astype(BF16), wo_ref[...], preferred_element_type=F32)
        o_ref[0, r] = _rms_scale(y) * fg_ref[...]


def _final(x, norm_g, w_b, o_nsa_lo, o_nsa_hi, o_s5, wpn, wps, wo, final_g):
    B, T, D = x.shape
    tm = TM_PROJ
    nh = T // tm // 2
    row_blk = lambda w: pl.BlockSpec((1, tm, w), lambda b, i: (b, i, 0))
    full = lambda a: pl.BlockSpec(a.shape, lambda b, i: (0,) * a.ndim)
    return pl.pallas_call(
        _final_kernel,
        grid=(B, T // tm),
        in_specs=[row_blk(D), full(norm_g), pl.BlockSpec(w_b.shape, lambda b, i: (0, 0), pipeline_mode=pl.Buffered(1)),
                  pl.BlockSpec((1, tm, NSA_WIDTH), lambda b, i: (b, jnp.minimum(i, nh - 1), 0)),
                  pl.BlockSpec((1, tm, NSA_WIDTH), lambda b, i: (b, jnp.maximum(i - nh, 0), 0)),
                  pl.BlockSpec((1, tm // S5_L, S5_L * S5_WIDTH), lambda b, i: (b, i, 0)),
                  full(wpn), full(wps), full(wo), full(final_g)],
        out_specs=row_blk(D),
        out_shape=jax.ShapeDtypeStruct((B, T, D), F32),
        scratch_shapes=[pltpu.VMEM((S5_WIDTH // LANES, tm, LANES), F32)],
        compiler_params=pltpu.CompilerParams(
            dimension_semantics=("parallel", "arbitrary"), vmem_limit_bytes=VMEM_LIMIT),
        name="final",
    )(x, norm_g, w_b, o_nsa_lo, o_nsa_hi, o_s5, wpn, wps, wo, final_g)


def _rope_tables(T):
    half = HEAD_DIM // 2
    inv_freq = np.float32(ROPE_THETA) ** (-np.arange(half, dtype=np.float32) / np.float32(half))
    ang = np.arange(T, dtype=np.float32)[:, None] * inv_freq[None, :].astype(np.float32)
    cos, sin = np.cos(ang).astype(np.float32), np.sin(ang).astype(np.float32)
    cos2 = np.concatenate([cos, cos, cos, cos], axis=1)
    sin2 = np.concatenate([-sin, sin, -sin, sin], axis=1)
    return jnp.asarray(cos2), jnp.asarray(sin2)


def _compress_w1(w1):
    half_rows = CMP_STRIDE * HEAD_DIM
    return jnp.concatenate([w1[:half_rows], w1[half_rows:]], axis=1).astype(BF16)


def kernel(x, norm_g, w_in, cmp_pos_k, cmp_pos_v, cmp_w1_k, cmp_w2_k, cmp_w1_v, cmp_w2_v, s5_lam_re, s5_lam_im, s5_log_dt, s5_b_re, s5_b_im, s5_c_re, s5_c_im, s5_d, w_glu, b_glu, w_proj_nsa, w_proj_s5, w_out, final_g):
    B, T, D = x.shape
    assert w_in.shape[0] == 1, "single-layer block"
    NCH = T // CMP_STRIDE
    NS = T // SEL_BLOCK

    w = w_in[0]
    w_all = jnp.concatenate([w[:, :_OFF_GL], jnp.pad(w[:, _OFF_GL:_OFF_GN], ((0, 0), (0, LANES - 24))),
                             w[:, _OFF_GN:]], axis=1).astype(BF16)
    g2 = norm_g[0][None, :]
    cos2, sin2 = _rope_tables(T)

    qq, kc, vc, ksa, vst, kw, vwt, glt, u4 = _inproj(x, g2, w_all, cos2, sin2)

    w2k = jnp.concatenate([jnp.zeros_like(cmp_w2_k[0]), cmp_w2_k[0]], axis=1).astype(BF16)
    w2vt = cmp_w2_v[0].T.astype(BF16)
    pos_rows = lambda p: jnp.pad(p.reshape(2, CMP_STRIDE * HEAD_DIM), ((0, 2 * SUBLANES - 2), (0, 0)))
    kcmp, vcmpt = _compress(kc, vc, _compress_w1(cmp_w1_k[0]), _compress_w1(cmp_w1_v[0]), w2k, w2vt,
                            pos_rows(cmp_pos_k[0]), pos_rows(cmp_pos_v[0]))

    c_start = jnp.arange(NCH) * CMP_STRIDE
    s_start = jnp.arange(NS) * SEL_BLOCK
    ovt = ((c_start[None, :] < s_start[:, None] + SEL_BLOCK) & (c_start[None, :] + CMP_BLOCK > s_start[:, None])
           & (jnp.arange(NCH)[None, :] < NCH - 1)).astype(BF16)
    o_nsa_lo, o_nsa_hi = _nsa(qq, kcmp, vcmpt, ksa, vst, kw, vwt, glt, ovt)

    rep = lambda a: jnp.repeat(a, S5_GROUP, axis=0)
    tr = lambda b: b.transpose(0, 2, 1).reshape(S5_GROUPS * S5_GROUP, S5_STATE)
    a_re, a_im, bb_re, bb_im = _s5_prep(
        rep(s5_lam_re[0]), rep(s5_lam_im[0]),
        rep(jnp.broadcast_to(s5_log_dt[0][:, None], (S5_GROUPS, S5_STATE))),
        tr(s5_b_re[0]), tr(s5_b_im[0]))
    flat = lambda c: c.reshape(S5_GROUPS * S5_GROUP, S5_STATE)
    we, wct, tp, alre, alim = _s5_weights(bb_re, bb_im, a_re, a_im, flat(s5_c_re[0]), flat(s5_c_im[0]))
    o_s5 = _s5(u4, we, wct, tp, alre.reshape(1, -1), alim.reshape(1, -1), s5_d[0][None, :],
               w_glu[0].astype(BF16), b_glu[0][None, :])

    return _final(x, g2, w_all, o_nsa_lo, o_nsa_hi, o_s5, w_proj_nsa[0].astype(BF16), w_proj_s5[0].astype(BF16),
                  w_out[0].astype(BF16), final_g[None, :])
```

```python
import math

import jax
import jax.numpy as jnp
import numpy as np
from jax import lax
from jax.experimental import pallas as pl
from jax.experimental.pallas import tpu as pltpu

F32 = jnp.float32
BF16 = jnp.bfloat16

D_MODEL = 1024
NSA_HEADS = 8
NSA_GROUPS = 2
HEADS_PER_GROUP = 4
HEAD_DIM = 64
NSA_WIDTH = 512
CMP_BLOCK = 32
CMP_STRIDE = 16
CMP_HIDDEN = 256
SEL_BLOCK = 64
SEL_TOPK = 16
WINDOW = 512
ROPE_THETA = 10000.0
FORCED_SCORE = 1.0e4
NEG = -1.0e30
S5_WIDTH = 512
S5_GROUP = 16
S5_GROUPS = 32
S5_STATE = 64
RMS_EPS = 1.0e-6

LANES = 128
SUBLANES = 8
VMEM_LIMIT = 56 * 1024 * 1024

_OFF_GL = 1280
_OFF_GN = 1304
_W_GL, _W_GN, _W_U, _W_GS, _W_MG, _W_END = 1280, 1408, 1920, 2432, 2944, 4992

TM_PROJ = 512
FINAL_SUB = 2
TQ = 128
TK = 512
NSA_NB = 2
SEL_LOOKAHEAD = 1
V_ROWS = 80
GATE_ROWS = 32
S5_L = 8
S5_CPB = 256 // S5_L
S5_CT = 64
S5W_BLOCKS_PER_STEP = 4
S5_SCAN_SLABS = 4


def _gelu_tanh(x):
    c = math.sqrt(2.0 / math.pi)
    return 0.5 * x * (1.0 + jnp.tanh(c * (x + 0.044715 * (x * x * x))))


def _sigmoid(x):
    return 1.0 / (1.0 + jnp.exp(-x))


def _rms_scale(xv):
    ms = jnp.mean(xv * xv, axis=-1, keepdims=True)
    return xv * lax.rsqrt(ms + RMS_EPS)


def _nt_dot(a, b):
    return lax.dot_general(a, b, (((1,), (1,)), ((), ())), preferred_element_type=F32)


def _inproj_kernel(x_ref, g_ref, w_ref, cos_ref, sin_ref,
                   qq_ref, kc_ref, vc_ref, ks_ref, vs_ref, kw_ref, vw_ref, gl_ref, u_ref, us_scr):
    h = (_rms_scale(x_ref[0]) * g_ref[...]).astype(BF16)
    cos2 = cos_ref[...]
    sin2 = sin_ref[...]
    lane = lax.broadcasted_iota(jnp.int32, cos2.shape, 1)
    first_half = (lane & (HEAD_DIM - 1)) < (HEAD_DIM // 2)
    low = lane < HEAD_DIM

    wide = {}

    def proj(a, b):
        for (s0, s1) in ((0, 512), (512, _W_GN), (_W_U, _W_GS)):
            if s0 <= a and b <= s1:
                if s0 not in wide:
                    wide[s0] = jnp.dot(h, w_ref[:, s0:s1], preferred_element_type=F32)
                return wide[s0][:, a - s0:b - s0]
        raise ValueError((a, b))

    def rope(xs):
        partner = jnp.where(first_half, pltpu.roll(xs, 96, 1), pltpu.roll(xs, 32, 1))
        return xs * cos2 + partner * sin2

    scale = HEAD_DIM ** -0.5 * math.log2(math.e)
    for i in range(NSA_HEADS // 2):
        xs = proj(LANES * i, LANES * (i + 1)) * scale
        xr = rope(xs)
        qq_ref[0, 2 * i] = jnp.where(low, xr, pltpu.roll(xs, 64, 1)).astype(BF16)
        qq_ref[0, 2 * i + 1] = jnp.where(low, pltpu.roll(xr, 64, 1), xs).astype(BF16)

    kc_ref[0] = proj(512, 640)
    vc_ref[0] = proj(640, 768)
    tm = cos2.shape[0]
    t_row = pl.program_id(1) * tm + lax.broadcasted_iota(jnp.int32, cos2.shape, 0)
    blk_onehot = jnp.where(lane - HEAD_DIM == t_row // SEL_BLOCK, 1.0, 0.0)
    ones_rows = jnp.where(lax.broadcasted_iota(jnp.int32, (V_ROWS - HEAD_DIM, tm), 0) == 0, 1.0, 0.0)
    for (off, k_out, v_out, k_pad) in ((768, ks_ref, vs_ref, blk_onehot), (1024, kw_ref, vw_ref, 0.0)):
        kr = rope(proj(off, off + LANES))
        k_out[0, 0] = jnp.where(low, kr, k_pad).astype(BF16)
        k_out[0, 1] = jnp.where(low, pltpu.roll(kr, 64, 1), k_pad).astype(BF16)
        vt = proj(off + LANES, off + 2 * LANES).T
        for g in range(NSA_GROUPS):
            v_out[0, g] = jnp.concatenate([vt[HEAD_DIM * g:HEAD_DIM * (g + 1)], ones_rows], axis=0).astype(BF16)
    gl_ref[0] = _sigmoid(proj(_W_GL, _W_GN)).T[0:GATE_ROWS]
    uv = proj(_W_U, _W_GS)
    for s in range(S5_WIDTH // LANES):
        us_scr[s] = uv[:, LANES * s:LANES * (s + 1)]
    for i in range(S5_L):
        for s in range(S5_WIDTH // LANES):
            c0 = i * S5_WIDTH + LANES * s
            u_ref[0, :, c0:c0 + LANES] = us_scr[s, pl.ds(i, tm // S5_L, stride=S5_L), :]


def _inproj(x, norm_g, w_a, cos2, sin2):
    B, T, D = x.shape
    tm = TM_PROJ
    grid = (B, T // tm)
    row_blk = lambda w: pl.BlockSpec((1, tm, w), lambda b, i: (b, i, 0))
    kv_blk = pl.BlockSpec((1, NSA_GROUPS, tm, LANES), lambda b, i: (b, 0, i, 0))
    kv_shape = jax.ShapeDtypeStruct((B, NSA_GROUPS, T, LANES), BF16)
    vt_blk = pl.BlockSpec((1, NSA_GROUPS, V_ROWS, tm), lambda b, i: (b, 0, 0, i))
    vt_shape = jax.ShapeDtypeStruct((B, NSA_GROUPS, V_ROWS, T), BF16)
    return pl.pallas_call(
        _inproj_kernel,
        grid=grid,
        in_specs=[
            row_blk(D),
            pl.BlockSpec((1, D), lambda b, i: (0, 0)),
            pl.BlockSpec(w_a.shape, lambda b, i: (0, 0), pipeline_mode=pl.Buffered(1)),
            pl.BlockSpec((tm, LANES), lambda b, i: (i, 0)),
            pl.BlockSpec((tm, LANES), lambda b, i: (i, 0)),
        ],
        out_specs=[
            pl.BlockSpec((1, NSA_HEADS, tm, LANES), lambda b, i: (b, 0, i, 0)),
            row_blk(LANES), row_blk(LANES),
            kv_blk, vt_blk, kv_blk, vt_blk,
            pl.BlockSpec((1, GATE_ROWS, tm), lambda b, i: (b, 0, i)),
            pl.BlockSpec((1, tm // S5_L, S5_L * S5_WIDTH), lambda b, i: (b, i, 0)),
        ],
        out_shape=[
            jax.ShapeDtypeStruct((B, NSA_HEADS, T, LANES), BF16),
            jax.ShapeDtypeStruct((B, T, LANES), F32), jax.ShapeDtypeStruct((B, T, LANES), F32),
            kv_shape, vt_shape, kv_shape, vt_shape,
            jax.ShapeDtypeStruct((B, GATE_ROWS, T), F32),
            jax.ShapeDtypeStruct((B, T // S5_L, S5_L * S5_WIDTH), F32),
        ],
        scratch_shapes=[pltpu.VMEM((S5_WIDTH // LANES, tm, LANES), F32)],
        compiler_params=pltpu.CompilerParams(
            dimension_semantics=("parallel", "arbitrary"), vmem_limit_bytes=VMEM_LIMIT),
        name="inproj",
    )(x, norm_g, w_a, cos2, sin2)


def _compress_kernel(kc_ref, vc_ref, w1k_ref, w1v_ref, w2k_ref, w2vt_ref, pbk_ref, pbv_ref, ko_ref, vo_ref, pb_scr):
    nch = ko_ref.shape[2]
    H = CMP_HIDDEN

    @pl.when(pl.program_id(0) == 0)
    def _():
        for n, (p_ref, w1_ref) in enumerate(((pbk_ref, w1k_ref), (pbv_ref, w1v_ref))):
            pw = jnp.dot(p_ref[...].astype(BF16), w1_ref[...], preferred_element_type=F32)
            pb_scr[n] = pw[0:1, 0:H] + pw[1:2, H:]

    def hidden(c_ref, w1_ref, n):
        acc = [jnp.zeros((nch, 2 * H), F32) for _ in range(NSA_GROUPS)]
        for j in range(CMP_STRIDE):
            rows = c_ref[0, pl.ds(j, nch, stride=CMP_STRIDE), :].astype(BF16)
            wj = w1_ref[HEAD_DIM * j:HEAD_DIM * (j + 1), :]
            for g in range(NSA_GROUPS):
                acc[g] = acc[g] + jnp.dot(rows[:, HEAD_DIM * g:HEAD_DIM * (g + 1)], wj, preferred_element_type=F32)
        return [_gelu_tanh(a[:, 0:H] + pltpu.roll(a[:, H:], nch - 1, 0) + pb_scr[n]).astype(BF16) for a in acc]

    hk = hidden(kc_ref, w1k_ref, 0)
    hv = hidden(vc_ref, w1v_ref, 1)
    for g in range(NSA_GROUPS):
        ko_ref[0, g] = jnp.dot(hk[g], w2k_ref[...], preferred_element_type=F32).astype(BF16)
        vo_ref[0, g] = _nt_dot(w2vt_ref[...], hv[g]).astype(BF16)


def _compress(kc, vc, w1k, w1v, w2k, w2vt, pbk, pbv):
    B, T, _ = kc.shape
    G = NSA_GROUPS
    nch = T // CMP_STRIDE
    c_blk = pl.BlockSpec((1, T, LANES), lambda b: (b, 0, 0))
    full = lambda a: pl.BlockSpec(a.shape, lambda b: (0,) * a.ndim)
    return pl.pallas_call(
        _compress_kernel,
        grid=(B,),
        in_specs=[c_blk, c_blk, full(w1k), full(w1v), full(w2k), full(w2vt), full(pbk), full(pbv)],
        out_specs=[pl.BlockSpec((1, G, nch, LANES), lambda b: (b, 0, 0, 0)),
                   pl.BlockSpec((1, G, HEAD_DIM, nch), lambda b: (b, 0, 0, 0))],
        out_shape=[jax.ShapeDtypeStruct((B, G, nch, LANES), BF16),
                   jax.ShapeDtypeStruct((B, G, HEAD_DIM, nch), BF16)],
        scratch_shapes=[pltpu.VMEM((2, 1, CMP_HIDDEN), F32)],
        compiler_params=pltpu.CompilerParams(dimension_semantics=("arbitrary",), vmem_limit_bytes=VMEM_LIMIT),
        name="compress",
    )(kc, vc, w1k, w1v, w2k, w2vt, pbk, pbv)


class _QTile:
    def __init__(self, x, t0, t_end, q_ref, g_ref, o_ref, cols, n_wc):
        self.x, self.t0, self.q_ref, self.g_ref, self.o_ref = x, t0, q_ref, g_ref, o_ref
        self.ncp = t_end // CMP_STRIDE
        self.ns = t_end // SEL_BLOCK
        self.t_lane = t0 + (lax.broadcasted_iota(jnp.int32, (1, cols), 1) & (TQ - 1))
        c_end = lax.broadcasted_iota(jnp.int32, (self.ncp, cols), 0) * CMP_STRIDE + (CMP_BLOCK - 1)
        self.cmp_valid = c_end <= self.t_lane
        self.w_pos = [t0 - WINDOW + TQ * c for c in range(n_wc)]
        self.w_start = [pl.multiple_of(jnp.maximum(p, 0), TQ) for p in self.w_pos]


def _nsa_kernel(qa_ref, qb_ref, kc_ref, vct_ref, ksa_ref, vst_ref, kw_ref, vwt_ref, ga_ref, gb_ref, ovt_ref,
                oa_ref, ob_ref, qsel_scr, acc_scr, m_scr):
    units = [(bb, g) for bb in range(qa_ref.shape[0]) for g in range(NSA_GROUPS)]
    uidx = {u: i for i, u in enumerate(units)}
    n_qt = kw_ref.shape[2] // TQ
    R = HEADS_PER_GROUP
    cols = R * TQ
    NS = ovt_ref.shape[0]
    n_wc = (WINDOW + TQ) // TQ
    step = pl.program_id(1)
    tiles = [_QTile(0, step * TQ, n_qt // 2 * TQ, qa_ref, ga_ref, oa_ref, cols, n_wc),
             _QTile(1, (n_qt - 1 - step) * TQ, n_qt * TQ, qb_ref, gb_ref, ob_ref, cols, n_wc)]
    work = [(c, u) for c in tiles for u in units]
    low = lax.broadcasted_iota(jnp.int32, (cols, LANES), 1) < HEAD_DIM
    sub8 = lax.broadcasted_iota(jnp.int32, (SUBLANES, TQ), 0)
    row_tq = lax.broadcasted_iota(jnp.int32, (TQ, cols), 0)

    def group_q(c, u):
        bb, g = u
        return c.q_ref[bb, R * g:R * (g + 1)].reshape(cols, LANES)

    def cmp_scores(c, u):
        return _nt_dot(kc_ref[u[0], u[1], 0:c.ncp, :], group_q(c, u))

    def cmp_probs(c, s):
        s = jnp.where(c.cmp_valid, s, NEG)
        e = jnp.exp2(s - jnp.max(s, axis=0, keepdims=True))
        inv = 1.0 / jnp.maximum(jnp.sum(e, axis=0, keepdims=True), 1.0e-30)
        return e * jnp.where(c.t_lane >= CMP_BLOCK - 1, inv, 0.0)

    def win_scores(c, u):
        kw = jnp.concatenate([kw_ref[u[0], u[1], pl.ds(c.w_start[n], TQ), :] for n in range(n_wc)], axis=0)
        return _nt_dot(kw, group_q(c, u))

    def win_probs(c, sw):
        parts = []
        for n in range(n_wc):
            sc = sw[TQ * n:TQ * (n + 1)]
            if n == 0:
                sc = jnp.where(c.w_pos[0] + row_tq > c.t_lane - WINDOW, sc, NEG)
            if n == n_wc - 1:
                sc = jnp.where(c.t0 + row_tq <= c.t_lane, sc, NEG)
            else:
                sc = jnp.where(c.w_pos[n] >= 0, sc, NEG)
            parts.append(sc.astype(BF16))
        sw = jnp.concatenate(parts, axis=0)
        return jnp.exp2(sw - jnp.max(sw, axis=0, keepdims=True))

    def win_out(c, u, ew):
        vw = jnp.concatenate([vwt_ref[u[0], u[1], :, pl.ds(c.w_start[n], TQ)] for n in range(n_wc)], axis=1)
        ow = jnp.dot(vw, ew, preferred_element_type=F32)
        return ow[0:HEAD_DIM] * (1.0 / ow[HEAD_DIM:HEAD_DIM + 1])

    def select_blocks(c, u, p):
        ns = c.ns
        psum = p[:, 0:TQ] + p[:, TQ:2 * TQ] + p[:, 2 * TQ:3 * TQ] + p[:, 3 * TQ:4 * TQ]
        p_hi = psum.astype(BF16)
        p_lo = (psum - p_hi.astype(F32)).astype(BF16)
        ov = ovt_ref[0:ns, 0:c.ncp]
        imp = (jnp.dot(ov, p_hi, preferred_element_type=F32) + jnp.dot(ov, p_lo, preferred_element_type=F32))
        blk = lax.broadcasted_iota(jnp.int32, (ns, TQ), 0)
        t_l = c.t0 + lax.broadcasted_iota(jnp.int32, (ns, TQ), 1)
        cur = t_l // SEL_BLOCK
        imp = jnp.where(blk * SEL_BLOCK <= t_l, imp, -1.0)
        imp = jnp.where(blk == 0, FORCED_SCORE, imp)
        imp = jnp.where(blk == cur, FORCED_SCORE, imp)
        imp = jnp.where(blk == cur - 1, FORCED_SCORE, imp)
        nv = ns // SUBLANES
        imp8 = [imp[SUBLANES * j:SUBLANES * (j + 1)] for j in range(nv)]
        rank8 = [jnp.zeros((SUBLANES, TQ), F32) for _ in range(nv)]
        for mm in range(ns):
            row = imp[mm:mm + 1, :]
            jm = mm // SUBLANES
            for j in range(nv):
                if j < jm:
                    ahead = jnp.where(row > imp8[j], 1.0, 0.0)
                elif j > jm:
                    ahead = jnp.where(row >= imp8[j], 1.0, 0.0)
                else:
                    tie = jnp.where(sub8 > (mm % SUBLANES), 1.0, 0.0)
                    ahead = jnp.where(row > imp8[j], 1.0, 0.0) + jnp.where(row == imp8[j], tie, 0.0)
                rank8[j] = rank8[j] + ahead
        pen = jnp.where(jnp.concatenate(rank8, axis=0) < float(SEL_TOPK), 0.0, NEG)
        if ns < NS:
            pen = jnp.concatenate([pen, jnp.zeros((NS - ns, TQ), F32)], axis=0)
        q_t = group_q(c, u).astype(F32).T.astype(BF16)
        qsel_scr[c.x, uidx[u]] = jnp.concatenate(
            [q_t[0:HEAD_DIM], jnp.concatenate([pen.astype(BF16)] * R, axis=1)], axis=0)

    p_c, o_cmp, o_win = {}, {}, {}

    def cmp_job(c, u):
        def finish(p, _):
            p_c[c.x, u] = p
            o_cmp[c.x, u] = jnp.dot(vct_ref[u[0], u[1], :, 0:c.ncp], p.astype(BF16),
                                    preferred_element_type=F32)
        return (lambda: cmp_scores(c, u)), (lambda s: (cmp_probs(c, s), None)), finish

    def win_job(c, u):
        def finish(e_w, _):
            o_win[c.x, u] = win_out(c, u, e_w)
        return (lambda: win_scores(c, u)), (lambda s: (win_probs(c, s), None)), finish

    def sel_job(x, kt, u, t_mask):
        k0 = kt * TK if isinstance(kt, int) else pl.multiple_of(kt * TK, TK)

        def probs(sc):
            if t_mask is not None:
                sc = jnp.where(kt * TK + lax.broadcasted_iota(jnp.int32, (TK, cols), 0) <= t_mask, sc, NEG)
            sc = sc.astype(BF16)
            m_old = m_scr[x, uidx[u]]
            m_new = jnp.maximum(m_old, jnp.max(sc, axis=0, keepdims=True).astype(F32))
            m_scr[x, uidx[u]] = m_new
            return jnp.exp2(sc - m_new.astype(BF16)), jnp.exp2(m_old - m_new)

        def finish(pp, alpha):
            acc_scr[x, uidx[u]] = acc_scr[x, uidx[u]] * alpha + jnp.dot(
                vst_ref[u[0], u[1], :, pl.ds(k0, TK)], pp, preferred_element_type=F32)

        return (lambda: jnp.dot(ksa_ref[u[0], u[1], pl.ds(k0, TK), :], qsel_scr[x, uidx[u]],
                                preferred_element_type=F32)), probs, finish

    def fuse(js):
        return ((lambda: [j[0]() for j in js]),
                (lambda ss: ([j[1](s) for j, s in zip(js, ss)], None)),
                (lambda outs, _: [j[2](*o) for j, o in zip(js, outs)]))

    def sel_tile(x, kt, t_mask):
        return [fuse([sel_job(x, kt, (bb, g), t_mask) for g in range(NSA_GROUPS)]) for bb in range(qa_ref.shape[0])]

    early, late = tiles
    n_slots = (n_qt - 1) * TQ // TK
    n_static = n_slots - n_slots // 2
    n_late = late.t0 // TK
    plan = [cmp_job(c, u) for c in (late, early) for u in units]
    for u in units:
        plan += [lambda u=u: select_blocks(late, u, p_c[late.x, u]), win_job(late, u)]
    plan += sel_tile(late.x, n_late, late.t_lane)
    for s in range(n_static):
        plan += sel_tile(late.x, s, None)
        for u in units[s::n_static]:
            plan += [lambda u=u: select_blocks(early, u, p_c[early.x, u]), win_job(early, u)]
    plan += sel_tile(early.x, early.t0 // TK, early.t_lane)
    for s in range(n_static, n_slots):
        is_late = s < n_late
        plan += sel_tile(jnp.where(is_late, late.x, early.x), jnp.where(is_late, s, s - n_late), None)

    acc_scr[...] = jnp.zeros_like(acc_scr)
    m_scr[...] = jnp.full(m_scr.shape, NEG, F32)
    job_pos = [k for k, e in enumerate(plan) if isinstance(e, tuple)]
    following = dict(zip(job_pos, job_pos[1:]))
    issued = {job_pos[0]: plan[job_pos[0]][0]()}
    for k, entry in enumerate(plan):
        if not isinstance(entry, tuple):
            entry()
            continue
        if k in following:
            issued[following[k]] = plan[following[k]][0]()
        _, probs, finish = entry
        finish(*probs(issued.pop(k)))

    for c in tiles:
        for bb in range(qa_ref.shape[0]):
            glt = c.g_ref[bb]
            heads = []
            for g in range(NSA_GROUPS):
                acc = acc_scr[c.x, uidx[bb, g]]
                o_sel = acc[0:HEAD_DIM] * (1.0 / acc[HEAD_DIM:HEAD_DIM + 1])
                for r in range(R):
                    hh = R * g + r
                    sl = slice(r * TQ, (r + 1) * TQ)
                    heads.append(glt[3 * hh:3 * hh + 1] * o_cmp[c.x, (bb, g)][:, sl]
                                 + glt[3 * hh + 1:3 * hh + 2] * o_sel[:, sl]
                                 + glt[3 * hh + 2:3 * hh + 3] * o_win[c.x, (bb, g)][:, sl])
            c.o_ref[bb] = jnp.concatenate(heads, axis=0).T


def _nsa(qq, kcmp, vcmpt, ksa, vst, kw, vwt, glt, ovt):
    B, H, T, _ = qq.shape
    G = NSA_GROUPS
    NB = NSA_NB
    NCP = kcmp.shape[2]
    n_qt = T // TQ
    grid = (B // NB, n_qt // 2)
    k_blk = lambda n: pl.BlockSpec((NB, G, n, LANES), lambda b, i: (b, 0, 0, 0))
    vt_blk = lambda r, n: pl.BlockSpec((NB, G, r, n), lambda b, i: (b, 0, 0, 0))
    lo_tile = lambda b, i: i
    hi_tile = lambda b, i: n_qt - 1 - i
    q_blk = lambda tile: pl.BlockSpec((NB, H, TQ, LANES), lambda b, i: (b, 0, tile(b, i), 0))
    g_blk = lambda tile: pl.BlockSpec((NB, GATE_ROWS, TQ), lambda b, i: (b, 0, tile(b, i)))
    half = jax.ShapeDtypeStruct((B, T // 2, NSA_WIDTH), F32)
    return pl.pallas_call(
        _nsa_kernel,
        grid=grid,
        in_specs=[
            q_blk(lo_tile), q_blk(hi_tile),
            k_blk(NCP), vt_blk(HEAD_DIM, NCP), k_blk(T), vt_blk(V_ROWS, T), k_blk(T), vt_blk(V_ROWS, T),
            g_blk(lo_tile), g_blk(hi_tile),
            pl.BlockSpec(ovt.shape, lambda b, i: (0, 0)),
        ],
        out_specs=[pl.BlockSpec((NB, TQ, NSA_WIDTH), lambda b, i: (b, i, 0)),
                   pl.BlockSpec((NB, TQ, NSA_WIDTH), lambda b, i: (b, n_qt // 2 - 1 - i, 0))],
        out_shape=[half, half],
        scratch_shapes=[pltpu.VMEM((2, NB * G, LANES, HEADS_PER_GROUP * TQ), BF16),
                        pltpu.VMEM((2, NB * G, V_ROWS, HEADS_PER_GROUP * TQ), F32),
                        pltpu.VMEM((2, NB * G, 1, HEADS_PER_GROUP * TQ), F32)],
        compiler_params=pltpu.CompilerParams(
            dimension_semantics=("parallel", "arbitrary"), vmem_limit_bytes=VMEM_LIMIT),
        name="nsa",
    )(qq, qq, kcmp, vcmpt, ksa, vst, kw, vwt, glt, glt, ovt)


def _s5_prep_kernel(lre_ref, lim_ref, ldt_ref, bre_ref, bim_ref, are_ref, aim_ref, bbre_ref, bbim_ref):
    lre, lim = lre_ref[...], lim_ref[...]
    dt = jnp.exp(ldt_ref[...])
    mag = jnp.exp(lre * dt)
    a_re = mag * jnp.cos(lim * dt)
    a_im = mag * jnp.sin(lim * dt)
    den = lre * lre + lim * lim
    z_re = ((a_re - 1.0) * lre + a_im * lim) / den
    z_im = (a_im * lre - (a_re - 1.0) * lim) / den
    are_ref[...] = a_re
    aim_ref[...] = a_im
    bbre_ref[...] = z_re * bre_ref[...] - z_im * bim_ref[...]
    bbim_ref[...] = z_re * bim_ref[...] + z_im * bre_ref[...]


def _s5_prep(lre, lim, ldt, bre, bim):
    shp = jax.ShapeDtypeStruct(lre.shape, F32)
    return pl.pallas_call(_s5_prep_kernel, out_shape=[shp, shp, shp, shp], name="s5_prep")(lre, lim, ldt, bre, bim)


def _s5_weights_kernel(bbre_ref, bbim_ref, are_ref, aim_ref, crt_ref, cit_ref,
                       we_ref, wct_ref, tp_ref, alre_ref, alim_ref, bd_scr, arow_scr, tp_scr):
    gpb = S5_CPB // S5_GROUP
    for q in range(we_ref.shape[0]):
        r0 = S5_CPB * q
        bd_scr[...] = jnp.zeros_like(bd_scr)
        tp_scr[...] = jnp.zeros_like(tp_scr)
        for n, ref in enumerate((bbre_ref, bbim_ref, crt_ref, cit_ref)):
            for gl in range(gpb):
                bd_scr[n, S5_GROUP * gl:S5_GROUP * (gl + 1), S5_STATE * gl:S5_STATE * (gl + 1)] = (
                    ref[r0 + S5_GROUP * gl:r0 + S5_GROUP * (gl + 1), :])
        for n, ref in enumerate((are_ref, aim_ref)):
            for gl in range(gpb):
                arow_scr[n, :, S5_STATE * gl:S5_STATE * (gl + 1)] = ref[r0 + S5_GROUP * gl:r0 + S5_GROUP * gl + 1, :]
        bbre, bbim, crt, cit = bd_scr[0], bd_scr[1], bd_scr[2], bd_scr[3]
        are, aim = arow_scr[0], arow_scr[1]
        pre, pim = jnp.ones_like(are), jnp.zeros_like(are)
        for k in range(S5_L):
            bpr = bbre * pre - bbim * pim
            bpi = bbre * pim + bbim * pre
            i = S5_L - 1 - k
            we_ref[q, S5_CPB * i:S5_CPB * (i + 1), :] = jnp.concatenate([bpr, bpi], axis=1).astype(BF16)
            tap = _nt_dot(bpr.astype(BF16), crt.astype(BF16)) - _nt_dot(bpi.astype(BF16), cit.astype(BF16))
            for i in range(S5_L - k):
                j = i + k
                tp_scr[S5_CPB * i:S5_CPB * (i + 1), S5_CPB * j:S5_CPB * (j + 1)] = tap
            pre, pim = pre * are - pim * aim, pre * aim + pim * are
            wct_ref[q, S5_CPB * k:S5_CPB * (k + 1), :] = jnp.concatenate(
                [crt * pre - cit * pim, -(crt * pim + cit * pre)], axis=1).astype(BF16)
        tp_ref[q] = tp_scr[...].astype(BF16)
        alre_ref[q] = pre
        alim_ref[q] = pim


def _s5_weights(bbre, bbim, are, aim, crt, cit):
    nblk = S5_WIDTH // S5_CPB
    sb = S5_CPB // S5_GROUP * S5_STATE
    lc = S5_L * S5_CPB
    bps = S5W_BLOCKS_PER_STEP
    rows = pl.BlockSpec((bps * S5_CPB, S5_STATE), lambda q: (q, 0))
    blk = lambda r, c: pl.BlockSpec((bps, r, c), lambda q: (q, 0, 0))
    return pl.pallas_call(
        _s5_weights_kernel,
        grid=(nblk // bps,),
        in_specs=[rows] * 6,
        out_specs=[blk(lc, 2 * sb), blk(lc, 2 * sb), blk(lc, lc), blk(1, sb), blk(1, sb)],
        scratch_shapes=[pltpu.VMEM((4, S5_CPB, sb), F32), pltpu.VMEM((2, 1, sb), F32), pltpu.VMEM((lc, lc), F32)],
        out_shape=[jax.ShapeDtypeStruct((nblk, lc, 2 * sb), BF16),
                   jax.ShapeDtypeStruct((nblk, lc, 2 * sb), BF16),
                   jax.ShapeDtypeStruct((nblk, lc, lc), BF16),
                   jax.ShapeDtypeStruct((nblk, 1, sb), F32), jax.ShapeDtypeStruct((nblk, 1, sb), F32)],
        compiler_params=pltpu.CompilerParams(dimension_semantics=("parallel",), vmem_limit_bytes=VMEM_LIMIT),
        name="s5_weights",
    )(bbre, bbim, are, aim, crt, cit)


def _s5_kernel(u_ref, we_ref, wct_ref, tp_ref, alre_ref, alim_ref, d_ref, wg_ref, bg_ref, o_ref, e_scr, st_scr):
    nb, ct, _ = u_ref.shape
    L = S5_L
    nblk = S5_WIDTH // S5_CPB
    sb = S5_CPB // S5_GROUP * S5_STATE
    spb = 2 * sb // LANES
    W = S5_SCAN_SLABS * LANES

    @pl.when(pl.program_id(0) == 0)
    def _():
        st_scr[...] = jnp.zeros_like(st_scr)

    uf = u_ref[...].reshape(nb * ct, L * S5_WIDTH)
    ub = uf.astype(BF16)
    u_blk = [jnp.concatenate([ub[:, i * S5_WIDTH + S5_CPB * q:i * S5_WIDTH + S5_CPB * (q + 1)] for i in range(L)],
                             axis=1) for q in range(nblk)]

    for q in range(nblk):
        e = jnp.dot(u_blk[q], we_ref[q], preferred_element_type=F32)
        for s in range(spb):
            for b in range(nb):
                e_scr[spb * q + s, pl.ds(b, ct, stride=nb), :] = e[b * ct:(b + 1) * ct, LANES * s:LANES * (s + 1)]

    def slabs_of(cs):
        q, r = divmod(cs * LANES, sb)
        return spb * q + r // LANES, spb * q + (sb + r) // LANES

    for j in range(S5_GROUPS * S5_STATE // W):
        sl = [slabs_of(S5_SCAN_SLABS * j + n) for n in range(S5_SCAN_SLABS)]
        sl_re, sl_im = [s[0] for s in sl], [s[1] for s in sl]
        ar = alre_ref[:, W * j:W * (j + 1)]
        ai = alim_ref[:, W * j:W * (j + 1)]
        load = lambda r0, rows, slabs: jnp.concatenate([e_scr[s, pl.ds(r0, rows), :] for s in slabs], axis=1)
        state = lambda slabs: jnp.concatenate([st_scr[:, LANES * s:LANES * (s + 1)] for s in slabs], axis=1)
        sr, si = state(sl_re), state(sl_im)
        for k in range(ct // 2):
            r0 = k * 2 * nb
            er, ei = load(r0, 2 * nb, sl_re), load(r0, 2 * nb, sl_im)
            tr = ar * sr - ai * si + er[0:nb]
            ti = ar * si + ai * sr + ei[0:nb]
            xr = jnp.concatenate([sr, tr], axis=0)
            xi = jnp.concatenate([si, ti], axis=0)
            for n in range(S5_SCAN_SLABS):
                e_scr[sl_re[n], pl.ds(r0, 2 * nb), :] = xr[:, LANES * n:LANES * (n + 1)]
                e_scr[sl_im[n], pl.ds(r0, 2 * nb), :] = xi[:, LANES * n:LANES * (n + 1)]
            sr, si = ar * tr - ai * ti + er[nb:], ar * ti + ai * tr + ei[nb:]
        for n in range(S5_SCAN_SLABS):
            st_scr[:, LANES * sl_re[n]:LANES * (sl_re[n] + 1)] = sr[:, LANES * n:LANES * (n + 1)]
            st_scr[:, LANES * sl_im[n]:LANES * (sl_im[n] + 1)] = si[:, LANES * n:LANES * (n + 1)]

    ys = []
    for q in range(nblk):
        per_b = [jnp.concatenate([e_scr[spb * q + s, pl.ds(b, ct, stride=nb), :] for s in range(spb)], axis=1)
                 for b in range(nb)]
        x_in = jnp.concatenate(per_b, axis=0).astype(BF16)
        ys.append((_nt_dot(x_in, wct_ref[q]) + jnp.dot(u_blk[q], tp_ref[q], preferred_element_type=F32)).astype(BF16))
    outs = []
    for j in range(L):
        yj = jnp.concatenate([y[:, S5_CPB * j:S5_CPB * (j + 1)] for y in ys], axis=1).astype(F32)
        yj = yj + d_ref[...] * uf[:, j * S5_WIDTH:(j + 1) * S5_WIDTH]
        z = _gelu_tanh(yj)
        gate = jnp.dot(z.astype(BF16), wg_ref[...], preferred_element_type=F32) + bg_ref[...]
        outs.append(z * _sigmoid(gate))
    o_ref[...] = jnp.concatenate(outs, axis=1).reshape(nb, ct, L * S5_WIDTH)


def _s5(u4, we, wct, tp, alre, alim, d, w_glu, b_glu):
    nb, nchunks, w4 = u4.shape
    ct = S5_CT
    n_slabs = 2 * S5_GROUPS * S5_STATE // LANES
    resident = lambda a: pl.BlockSpec(a.shape, lambda i: (0,) * a.ndim, pipeline_mode=pl.Buffered(1))
    u_blk = pl.BlockSpec((nb, ct, w4), lambda i: (0, i, 0))
    return pl.pallas_call(
        _s5_kernel,
        grid=(nchunks // ct,),
        in_specs=[u_blk, resident(we), resident(wct), resident(tp), resident(alre), resident(alim),
                  resident(d), resident(w_glu), resident(b_glu)],
        out_specs=u_blk,
        out_shape=jax.ShapeDtypeStruct(u4.shape, F32),
        scratch_shapes=[pltpu.VMEM((n_slabs, nb * ct, LANES), F32), pltpu.VMEM((nb, n_slabs * LANES), F32)],
        compiler_params=pltpu.CompilerParams(dimension_semantics=("arbitrary",), vmem_limit_bytes=VMEM_LIMIT),
        name="s5",
    )(u4, we, wct, tp, alre, alim, d, w_glu, b_glu)


def _final_kernel(x_ref, g_ref, wb_ref, onl_ref, onh_ref, os_ref, wpn_ref, wps_ref, wo_ref, fg_ref, o_ref, os_scr):
    tm = x_ref.shape[1]
    o4 = os_ref[0]
    for i in range(S5_L):
        for s in range(S5_WIDTH // LANES):
            c0 = i * S5_WIDTH + LANES * s
            os_scr[s, pl.ds(i, tm // S5_L, stride=S5_L), :] = o4[:, c0:c0 + LANES]
    first_half = pl.program_id(1) < pl.num_programs(1) // 2

    def silu(v):
        return v * _sigmoid(v)

    rows = [slice(r, r + tm // FINAL_SUB) for r in range(0, tm, tm // FINAL_SUB)]
    hs = [(_rms_scale(x_ref[0, r]) * g_ref[...]).astype(BF16) for r in rows]
    for r, h in zip(rows, hs):
        proj = lambda a, b, h=h: jnp.dot(h, wb_ref[:, a:b], preferred_element_type=F32)
        o_nsa = jnp.where(first_half, onl_ref[0, r], onh_ref[0, r])
        o_s5 = jnp.concatenate([os_scr[s, r] for s in range(S5_WIDTH // LANES)], axis=1)
        a_in = (o_nsa * silu(proj(_W_GN, _W_U))).astype(BF16)
        b_in = (o_s5 * silu(proj(_W_GS, _W_MG))).astype(BF16)
        branch_a = jnp.dot(a_in, wpn_ref[...], preferred_element_type=F32)
        branch_b = jnp.dot(b_in, wps_ref[...], preferred_element_type=F32)
        merged = (_sigmoid(proj(_W_MG, _W_MG + D_MODEL)) * branch_a
                  + _sigmoid(proj(_W_MG + D_MODEL, _W_END)) * branch_b)
        y = x_ref[0, r] + jnp.dot(merged.astype(BF16), wo_ref[...], preferred_element_type=F32)
        o_ref[0, r] = _rms_scale(y) * fg_ref[...]


def _final(x, norm_g, w_b, o_nsa_lo, o_nsa_hi, o_s5, wpn, wps, wo, final_g):
    B, T, D = x.shape
    tm = TM_PROJ
    nh = T // tm // 2
    row_blk = lambda w: pl.BlockSpec((1, tm, w), lambda b, i: (b, i, 0))
    full = lambda a: pl.BlockSpec(a.shape, lambda b, i: (0,) * a.ndim)
    return pl.pallas_call(
        _final_kernel,
        grid=(B, T // tm),
        in_specs=[row_blk(D), full(norm_g), pl.BlockSpec(w_b.shape, lambda b, i: (0, 0), pipeline_mode=pl.Buffered(1)),
                  pl.BlockSpec((1, tm, NSA_WIDTH), lambda b, i: (b, jnp.minimum(i, nh - 1), 0)),
                  pl.BlockSpec((1, tm, NSA_WIDTH), lambda b, i: (b, jnp.maximum(i - nh, 0), 0)),
                  pl.BlockSpec((1, tm // S5_L, S5_L * S5_WIDTH), lambda b, i: (b, i, 0)),
                  full(wpn), full(wps), full(wo), full(final_g)],
        out_specs=row_blk(D),
        out_shape=jax.ShapeDtypeStruct((B, T, D), F32),
        scratch_shapes=[pltpu.VMEM((S5_WIDTH // LANES, tm, LANES), F32)],
        compiler_params=pltpu.CompilerParams(
            dimension_semantics=("parallel", "arbitrary"), vmem_limit_bytes=VMEM_LIMIT),
        name="final",
    )(x, norm_g, w_b, o_nsa_lo, o_nsa_hi, o_s5, wpn, wps, wo, final_g)


def _rope_tables(T):
    half = HEAD_DIM // 2
    inv_freq = np.float32(ROPE_THETA) ** (-np.arange(half, dtype=np.float32) / np.float32(half))
    ang = np.arange(T, dtype=np.float32)[:, None] * inv_freq[None, :].astype(np.float32)
    cos, sin = np.cos(ang).astype(np.float32), np.sin(ang).astype(np.float32)
    cos2 = np.concatenate([cos, cos, cos, cos], axis=1)
    sin2 = np.concatenate([-sin, sin, -sin, sin], axis=1)
    return jnp.asarray(cos2), jnp.asarray(sin2)


def _compress_w1(w1):
    half_rows = CMP_STRIDE * HEAD_DIM
    return jnp.concatenate([w1[:half_rows], w1[half_rows:]], axis=1).astype(BF16)


def kernel(x, norm_g, w_in, cmp_pos_k, cmp_pos_v, cmp_w1_k, cmp_w2_k, cmp_w1_v, cmp_w2_v, s5_lam_re, s5_lam_im, s5_log_dt, s5_b_re, s5_b_im, s5_c_re, s5_c_im, s5_d, w_glu, b_glu, w_proj_nsa, w_proj_s5, w_out, final_g):
    B, T, D = x.shape
    assert w_in.shape[0] == 1, "single-layer block"
    NCH = T // CMP_STRIDE
    NS = T // SEL_BLOCK

    w = w_in[0]
    w_all = jnp.concatenate([w[:, :_OFF_GL], jnp.pad(w[:, _OFF_GL:_OFF_GN], ((0, 0), (0, LANES - 24))),
                             w[:, _OFF_GN:]], axis=1).astype(BF16)
    g2 = norm_g[0][None, :]
    cos2, sin2 = _rope_tables(T)

    qq, kc, vc, ksa, vst, kw, vwt, glt, u4 = _inproj(x, g2, w_all, cos2, sin2)

    w2k = jnp.concatenate([jnp.zeros_like(cmp_w2_k[0]), cmp_w2_k[0]], axis=1).astype(BF16)
    w2vt = cmp_w2_v[0].T.astype(BF16)
    pos_rows = lambda p: jnp.pad(p.reshape(2, CMP_STRIDE * HEAD_DIM), ((0, 2 * SUBLANES - 2), (0, 0)))
    kcmp, vcmpt = _compress(kc, vc, _compress_w1(cmp_w1_k[0]), _compress_w1(cmp_w1_v[0]), w2k, w2vt,
                            pos_rows(cmp_pos_k[0]), pos_rows(cmp_pos_v[0]))

    c_start = jnp.arange(NCH) * CMP_STRIDE
    s_start = jnp.arange(NS) * SEL_BLOCK
    ovt = ((c_start[None, :] < s_start[:, None] + SEL_BLOCK) & (c_start[None, :] + CMP_BLOCK > s_start[:, None])
           & (jnp.arange(NCH)[None, :] < NCH - 1)).astype(BF16)
    o_nsa_lo, o_nsa_hi = _nsa(qq, kcmp, vcmpt, ksa, vst, kw, vwt, glt, ovt)

    rep = lambda a: jnp.repeat(a, S5_GROUP, axis=0)
    tr = lambda b: b.transpose(0, 2, 1).reshape(S5_GROUPS * S5_GROUP, S5_STATE)
    a_re, a_im, bb_re, bb_im = _s5_prep(
        rep(s5_lam_re[0]), rep(s5_lam_im[0]),
        rep(jnp.broadcast_to(s5_log_dt[0][:, None], (S5_GROUPS, S5_STATE))),
        tr(s5_b_re[0]), tr(s5_b_im[0]))
    flat = lambda c: c.reshape(S5_GROUPS * S5_GROUP, S5_STATE)
    we, wct, tp, alre, alim = _s5_weights(bb_re, bb_im, a_re, a_im, flat(s5_c_re[0]), flat(s5_c_im[0]))
    o_s5 = _s5(u4, we, wct, tp, alre.reshape(1, -1), alim.reshape(1, -1), s5_d[0][None, :],
               w_glu[0].astype(BF16), b_glu[0][None, :])

    return _final(x, g2, w_all, o_nsa_lo, o_nsa_hi, o_s5, w_proj_nsa[0].astype(BF16), w_proj_s5[0].astype(BF16),
                  w_out[0].astype(BF16), final_g[None, :])
```

```python
import math

import jax
import jax.numpy as jnp
import numpy as np
from jax import lax
from jax.experimental import pallas as pl
from jax.experimental.pallas import tpu as pltpu

F32 = jnp.float32
BF16 = jnp.bfloat16

D_MODEL = 1024
NSA_HEADS = 8
NSA_GROUPS = 2
HEADS_PER_GROUP = 4
HEAD_DIM = 64
NSA_WIDTH = 512
CMP_BLOCK = 32
CMP_STRIDE = 16
CMP_HIDDEN = 256
SEL_BLOCK = 64
SEL_TOPK = 16
WINDOW = 512
ROPE_THETA = 10000.0
FORCED_SCORE = 1.0e4
NEG = -1.0e30
S5_WIDTH = 512
S5_GROUP = 16
S5_GROUPS = 32
S5_STATE = 64
RMS_EPS = 1.0e-6

LANES = 128
SUBLANES = 8
VMEM_LIMIT = 56 * 1024 * 1024

_OFF_GL = 1280
_OFF_GN = 1304
_W_GL, _W_GN, _W_U, _W_GS, _W_MG, _W_END = 1280, 1408, 1920, 2432, 2944, 4992

TM_PROJ = 512
FINAL_SUB = 2
TQ = 128
TK = 512
NSA_NB = 2
V_ROWS = 80
GATE_ROWS = 32
S5_L = 8
S5_CPB = 256 // S5_L
S5_CT = 64
S5W_BLOCKS_PER_STEP = 4
S5_SCAN_SLABS = 4


def _gelu_tanh(x):
    c = math.sqrt(2.0 / math.pi)
    return 0.5 * x * (1.0 + jnp.tanh(c * (x + 0.044715 * (x * x * x))))


def _sigmoid(x):
    return 1.0 / (1.0 + jnp.exp(-x))


def _rms_scale(xv):
    ms = jnp.mean(xv * xv, axis=-1, keepdims=True)
    return xv * lax.rsqrt(ms + RMS_EPS)


def _nt_dot(a, b):
    return lax.dot_general(a, b, (((1,), (1,)), ((), ())), preferred_element_type=F32)


def _inproj_kernel(x_ref, g_ref, w_ref, cos_ref, sin_ref,
                   qq_ref, kc_ref, vc_ref, ks_ref, vs_ref, kw_ref, vw_ref, gl_ref, u_ref, us_scr):
    h = (_rms_scale(x_ref[0]) * g_ref[...]).astype(BF16)
    cos2 = cos_ref[...]
    sin2 = sin_ref[...]
    lane = lax.broadcasted_iota(jnp.int32, cos2.shape, 1)
    first_half = (lane & (HEAD_DIM - 1)) < (HEAD_DIM // 2)
    low = lane < HEAD_DIM

    wide = {}

    def proj(a, b):
        for (s0, s1) in ((0, 512), (512, _W_GN), (_W_U, _W_GS)):
            if s0 <= a and b <= s1:
                if s0 not in wide:
                    wide[s0] = jnp.dot(h, w_ref[:, s0:s1], preferred_element_type=F32)
                return wide[s0][:, a - s0:b - s0]
        raise ValueError((a, b))

    def rope(xs):
        partner = jnp.where(first_half, pltpu.roll(xs, 96, 1), pltpu.roll(xs, 32, 1))
        return xs * cos2 + partner * sin2

    scale = HEAD_DIM ** -0.5 * math.log2(math.e)
    for i in range(NSA_HEADS // 2):
        xs = proj(LANES * i, LANES * (i + 1)) * scale
        xr = rope(xs)
        qq_ref[0, 2 * i] = jnp.where(low, xr, pltpu.roll(xs, 64, 1)).astype(BF16)
        qq_ref[0, 2 * i + 1] = jnp.where(low, pltpu.roll(xr, 64, 1), xs).astype(BF16)

    kc_ref[0] = proj(512, 640)
    vc_ref[0] = proj(640, 768)
    tm = cos2.shape[0]
    t_row = pl.program_id(1) * tm + lax.broadcasted_iota(jnp.int32, cos2.shape, 0)
    blk_onehot = jnp.where(lane - HEAD_DIM == t_row // SEL_BLOCK, 1.0, 0.0)
    ones_rows = jnp.where(lax.broadcasted_iota(jnp.int32, (V_ROWS - HEAD_DIM, tm), 0) == 0, 1.0, 0.0)
    for (off, k_out, v_out, k_pad) in ((768, ks_ref, vs_ref, blk_onehot), (1024, kw_ref, vw_ref, 0.0)):
        kr = rope(proj(off, off + LANES))
        k_out[0, 0] = jnp.where(low, kr, k_pad).astype(BF16)
        k_out[0, 1] = jnp.where(low, pltpu.roll(kr, 64, 1), k_pad).astype(BF16)
        vt = proj(off + LANES, off + 2 * LANES).T
        for g in range(NSA_GROUPS):
            v_out[0, g] = jnp.concatenate([vt[HEAD_DIM * g:HEAD_DIM * (g + 1)], ones_rows], axis=0).astype(BF16)
    gl_ref[0] = _sigmoid(proj(_W_GL, _W_GN)).T[0:GATE_ROWS]
    uv = proj(_W_U, _W_GS)
    for s in range(S5_WIDTH // LANES):
        us_scr[s] = uv[:, LANES * s:LANES * (s + 1)]
    for i in range(S5_L):
        for s in range(S5_WIDTH // LANES):
            c0 = i * S5_WIDTH + LANES * s
            u_ref[0, :, c0:c0 + LANES] = us_scr[s, pl.ds(i, tm // S5_L, stride=S5_L), :]


def _inproj(x, norm_g, w_a, cos2, sin2):
    B, T, D = x.shape
    tm = TM_PROJ
    grid = (B, T // tm)
    row_blk = lambda w: pl.BlockSpec((1, tm, w), lambda b, i: (b, i, 0))
    kv_blk = pl.BlockSpec((1, NSA_GROUPS, tm, LANES), lambda b, i: (b, 0, i, 0))
    kv_shape = jax.ShapeDtypeStruct((B, NSA_GROUPS, T, LANES), BF16)
    vt_blk = pl.BlockSpec((1, NSA_GROUPS, V_ROWS, tm), lambda b, i: (b, 0, 0, i))
    vt_shape = jax.ShapeDtypeStruct((B, NSA_GROUPS, V_ROWS, T), BF16)
    return pl.pallas_call(
        _inproj_kernel,
        grid=grid,
        in_specs=[
            row_blk(D),
            pl.BlockSpec((1, D), lambda b, i: (0, 0)),
            pl.BlockSpec(w_a.shape, lambda b, i: (0, 0), pipeline_mode=pl.Buffered(1)),
            pl.BlockSpec((tm, LANES), lambda b, i: (i, 0)),
            pl.BlockSpec((tm, LANES), lambda b, i: (i, 0)),
        ],
        out_specs=[
            pl.BlockSpec((1, NSA_HEADS, tm, LANES), lambda b, i: (b, 0, i, 0)),
            row_blk(LANES), row_blk(LANES),
            kv_blk, vt_blk, kv_blk, vt_blk,
            pl.BlockSpec((1, GATE_ROWS, tm), lambda b, i: (b, 0, i)),
            pl.BlockSpec((1, tm // S5_L, S5_L * S5_WIDTH), lambda b, i: (b, i, 0)),
        ],
        out_shape=[
            jax.ShapeDtypeStruct((B, NSA_HEADS, T, LANES), BF16),
            jax.ShapeDtypeStruct((B, T, LANES), F32), jax.ShapeDtypeStruct((B, T, LANES), F32),
            kv_shape, vt_shape, kv_shape, vt_shape,
            jax.ShapeDtypeStruct((B, GATE_ROWS, T), F32),
            jax.ShapeDtypeStruct((B, T // S5_L, S5_L * S5_WIDTH), F32),
        ],
        scratch_shapes=[pltpu.VMEM((S5_WIDTH // LANES, tm, LANES), F32)],
        compiler_params=pltpu.CompilerParams(
            dimension_semantics=("parallel", "arbitrary"), vmem_limit_bytes=VMEM_LIMIT),
        name="inproj",
    )(x, norm_g, w_a, cos2, sin2)


def _compress_kernel(kc_ref, vc_ref, w1k_ref, w1v_ref, w2k_ref, w2vt_ref, pbk_ref, pbv_ref, ko_ref, vo_ref, pb_scr):
    nch = ko_ref.shape[2]
    H = CMP_HIDDEN

    @pl.when(pl.program_id(0) == 0)
    def _():
        for n, (p_ref, w1_ref) in enumerate(((pbk_ref, w1k_ref), (pbv_ref, w1v_ref))):
            pw = jnp.dot(p_ref[...].astype(BF16), w1_ref[...], preferred_element_type=F32)
            pb_scr[n] = pw[0:1, 0:H] + pw[1:2, H:]

    def hidden(c_ref, w1_ref, n):
        acc = [jnp.zeros((nch, 2 * H), F32) for _ in range(NSA_GROUPS)]
        for j in range(CMP_STRIDE):
            rows = c_ref[0, pl.ds(j, nch, stride=CMP_STRIDE), :].astype(BF16)
            wj = w1_ref[HEAD_DIM * j:HEAD_DIM * (j + 1), :]
            for g in range(NSA_GROUPS):
                acc[g] = acc[g] + jnp.dot(rows[:, HEAD_DIM * g:HEAD_DIM * (g + 1)], wj, preferred_element_type=F32)
        return [_gelu_tanh(a[:, 0:H] + pltpu.roll(a[:, H:], nch - 1, 0) + pb_scr[n]).astype(BF16) for a in acc]

    hk = hidden(kc_ref, w1k_ref, 0)
    hv = hidden(vc_ref, w1v_ref, 1)
    for g in range(NSA_GROUPS):
        ko_ref[0, g] = jnp.dot(hk[g], w2k_ref[...], preferred_element_type=F32).astype(BF16)
        vo_ref[0, g] = _nt_dot(w2vt_ref[...], hv[g]).astype(BF16)


def _compress(kc, vc, w1k, w1v, w2k, w2vt, pbk, pbv):
    B, T, _ = kc.shape
    G = NSA_GROUPS
    nch = T // CMP_STRIDE
    c_blk = pl.BlockSpec((1, T, LANES), lambda b: (b, 0, 0))
    full = lambda a: pl.BlockSpec(a.shape, lambda b: (0,) * a.ndim)
    return pl.pallas_call(
        _compress_kernel,
        grid=(B,),
        in_specs=[c_blk, c_blk, full(w1k), full(w1v), full(w2k), full(w2vt), full(pbk), full(pbv)],
        out_specs=[pl.BlockSpec((1, G, nch, LANES), lambda b: (b, 0, 0, 0)),
                   pl.BlockSpec((1, G, HEAD_DIM, nch), lambda b: (b, 0, 0, 0))],
        out_shape=[jax.ShapeDtypeStruct((B, G, nch, LANES), BF16),
                   jax.ShapeDtypeStruct((B, G, HEAD_DIM, nch), BF16)],
        scratch_shapes=[pltpu.VMEM((2, 1, CMP_HIDDEN), F32)],
        compiler_params=pltpu.CompilerParams(dimension_semantics=("arbitrary",), vmem_limit_bytes=VMEM_LIMIT),
        name="compress",
    )(kc, vc, w1k, w1v, w2k, w2vt, pbk, pbv)


class _QTile:
    def __init__(self, x, t0, t_begin, t_end, q_ref, g_ref, o_ref, cols, n_wc):
        self.x, self.t0, self.q_ref, self.g_ref, self.o_ref = x, t0, q_ref, g_ref, o_ref
        self.window_inside = t_begin >= WINDOW
        self.ncp = t_end // CMP_STRIDE
        self.ns = t_end // SEL_BLOCK
        self.t_lane = t0 + (lax.broadcasted_iota(jnp.int32, (1, cols), 1) & (TQ - 1))
        c_end = lax.broadcasted_iota(jnp.int32, (self.ncp, cols), 0) * CMP_STRIDE + (CMP_BLOCK - 1)
        self.cmp_valid = c_end <= self.t_lane
        self.w_pos = [t0 - WINDOW + TQ * c for c in range(n_wc)]
        self.w_start = [pl.multiple_of(jnp.maximum(p, 0), TQ) for p in self.w_pos]


def _nsa_kernel(qa_ref, qb_ref, kc_ref, vct_ref, ksa_ref, vst_ref, kw_ref, vwt_ref, ga_ref, gb_ref, ovt_ref,
                oa_ref, ob_ref, qsel_scr, acc_scr, m_scr):
    units = [(bb, g) for bb in range(qa_ref.shape[0]) for g in range(NSA_GROUPS)]
    uidx = {u: i for i, u in enumerate(units)}
    n_qt = kw_ref.shape[2] // TQ
    R = HEADS_PER_GROUP
    cols = R * TQ
    NS = ovt_ref.shape[0]
    n_wc = (WINDOW + TQ) // TQ
    step = pl.program_id(1)
    t_mid = n_qt // 2 * TQ
    tiles = [_QTile(0, step * TQ, 0, t_mid, qa_ref, ga_ref, oa_ref, cols, n_wc),
             _QTile(1, (n_qt - 1 - step) * TQ, t_mid, n_qt * TQ, qb_ref, gb_ref, ob_ref, cols, n_wc)]
    sub8 = lax.broadcasted_iota(jnp.int32, (SUBLANES, TQ), 0)
    row_tq = lax.broadcasted_iota(jnp.int32, (TQ, cols), 0)

    def group_q(c, u):
        bb, g = u
        return c.q_ref[bb, R * g:R * (g + 1)].reshape(cols, LANES)

    def cmp_scores(c, u):
        return _nt_dot(kc_ref[u[0], u[1], 0:c.ncp, :], group_q(c, u))

    def cmp_probs(c, s):
        s = jnp.where(c.cmp_valid, s, NEG)
        e = jnp.exp2(s - jnp.max(s, axis=0, keepdims=True))
        inv = 1.0 / jnp.maximum(jnp.sum(e, axis=0, keepdims=True), 1.0e-30)
        return e * jnp.where(c.t_lane >= CMP_BLOCK - 1, inv, 0.0)

    def win_scores(c, u):
        kw = jnp.concatenate([kw_ref[u[0], u[1], pl.ds(c.w_start[n], TQ), :] for n in range(n_wc)], axis=0)
        return _nt_dot(kw, group_q(c, u))

    def win_probs(c, sw):
        parts = []
        for n in range(n_wc):
            sc = sw[TQ * n:TQ * (n + 1)]
            if n == 0:
                sc = jnp.where(c.w_pos[0] + row_tq > c.t_lane - WINDOW, sc, NEG)
            if n == n_wc - 1:
                sc = jnp.where(c.t0 + row_tq <= c.t_lane, sc, NEG)
            elif not c.window_inside:
                sc = jnp.where(c.w_pos[n] >= 0, sc, NEG)
            parts.append(sc.astype(BF16))
        sw = jnp.concatenate(parts, axis=0)
        return jnp.exp2(sw - jnp.max(sw, axis=0, keepdims=True))

    def win_out(c, u, ew):
        vw = jnp.concatenate([vwt_ref[u[0], u[1], :, pl.ds(c.w_start[n], TQ)] for n in range(n_wc)], axis=1)
        ow = jnp.dot(vw, ew, preferred_element_type=F32)
        return ow[0:HEAD_DIM] * (1.0 / ow[HEAD_DIM:HEAD_DIM + 1])

    def select_blocks(c, u, p):
        ns = c.ns
        psum = p[:, 0:TQ] + p[:, TQ:2 * TQ] + p[:, 2 * TQ:3 * TQ] + p[:, 3 * TQ:4 * TQ]
        p_hi = psum.astype(BF16)
        p_lo = (psum - p_hi.astype(F32)).astype(BF16)
        ov = ovt_ref[0:ns, 0:c.ncp]
        imp = (jnp.dot(ov, p_hi, preferred_element_type=F32) + jnp.dot(ov, p_lo, preferred_element_type=F32))
        blk = lax.broadcasted_iota(jnp.int32, (ns, TQ), 0)
        t_l = c.t0 + lax.broadcasted_iota(jnp.int32, (ns, TQ), 1)
        cur = t_l // SEL_BLOCK
        imp = jnp.where(blk * SEL_BLOCK <= t_l, imp, -1.0)
        imp = jnp.where(blk == 0, FORCED_SCORE, imp)
        imp = jnp.where(blk == cur, FORCED_SCORE, imp)
        imp = jnp.where(blk == cur - 1, FORCED_SCORE, imp)
        nv = ns // SUBLANES
        imp8 = [imp[SUBLANES * j:SUBLANES * (j + 1)] for j in range(nv)]
        rank8 = [jnp.zeros((SUBLANES, TQ), F32) for _ in range(nv)]
        for mm in range(ns):
            row = imp[mm:mm + 1, :]
            jm = mm // SUBLANES
            for j in range(nv):
                if j < jm:
                    ahead = jnp.where(row > imp8[j], 1.0, 0.0)
                elif j > jm:
                    ahead = jnp.where(row >= imp8[j], 1.0, 0.0)
                else:
                    tie = jnp.where(sub8 > (mm % SUBLANES), 1.0, 0.0)
                    ahead = jnp.where(row > imp8[j], 1.0, 0.0) + jnp.where(row == imp8[j], tie, 0.0)
                rank8[j] = rank8[j] + ahead
        pen = jnp.where(jnp.concatenate(rank8, axis=0) < float(SEL_TOPK), 0.0, NEG)
        if ns < NS:
            pen = jnp.concatenate([pen, jnp.zeros((NS - ns, TQ), F32)], axis=0)
        q_t = group_q(c, u).astype(F32).T.astype(BF16)
        qsel_scr[c.x, uidx[u]] = jnp.concatenate(
            [q_t[0:HEAD_DIM], jnp.concatenate([pen.astype(BF16)] * R, axis=1)], axis=0)

    p_c, o_cmp, o_win = {}, {}, {}

    def cmp_job(c, u):
        def finish(p, _):
            p_c[c.x, u] = p
            o_cmp[c.x, u] = jnp.dot(vct_ref[u[0], u[1], :, 0:c.ncp], p.astype(BF16),
                                    preferred_element_type=F32)
        return (lambda: cmp_scores(c, u)), (lambda s: (cmp_probs(c, s), None)), finish

    def win_job(c, u):
        def finish(e_w, _):
            o_win[c.x, u] = win_out(c, u, e_w)
        return (lambda: win_scores(c, u)), (lambda s: (win_probs(c, s), None)), finish

    def sel_job(x, kt, u, t_mask):
        k0 = kt * TK if isinstance(kt, int) else pl.multiple_of(kt * TK, TK)

        def probs(sc):
            if t_mask is not None:
                sc = jnp.where(kt * TK + lax.broadcasted_iota(jnp.int32, (TK, cols), 0) <= t_mask, sc, NEG)
            sc = sc.astype(BF16)
            m_old = m_scr[x, uidx[u]]
            m_new = jnp.maximum(m_old, jnp.max(sc, axis=0, keepdims=True).astype(F32))
            m_scr[x, uidx[u]] = m_new
            return jnp.exp2(sc - m_new.astype(BF16)), jnp.exp2(m_old - m_new)

        def finish(pp, alpha):
            acc_scr[x, uidx[u]] = acc_scr[x, uidx[u]] * alpha + jnp.dot(
                vst_ref[u[0], u[1], :, pl.ds(k0, TK)], pp, preferred_element_type=F32)

        return (lambda: jnp.dot(ksa_ref[u[0], u[1], pl.ds(k0, TK), :], qsel_scr[x, uidx[u]],
                                preferred_element_type=F32)), probs, finish

    def fuse(js):
        return ((lambda: [j[0]() for j in js]),
                (lambda ss: ([j[1](s) for j, s in zip(js, ss)], None)),
                (lambda outs, _: [j[2](*o) for j, o in zip(js, outs)]))

    def sel_tile(x, kt, t_mask):
        return [fuse([sel_job(x, kt, (bb, g), t_mask) for g in range(NSA_GROUPS)]) for bb in range(qa_ref.shape[0])]

    early, late = tiles
    n_slots = (n_qt - 1) * TQ // TK
    n_static = n_slots - n_slots // 2
    n_late = late.t0 // TK
    plan = [cmp_job(c, u) for c in (late, early) for u in units]
    for u in units:
        plan += [lambda u=u: select_blocks(late, u, p_c[late.x, u]), win_job(late, u)]
    plan += sel_tile(late.x, n_late, late.t_lane)
    for s in range(n_static):
        plan += sel_tile(late.x, s, None)
        for u in units[s::n_static]:
            plan += [lambda u=u: select_blocks(early, u, p_c[early.x, u]), win_job(early, u)]
    plan += sel_tile(early.x, early.t0 // TK, early.t_lane)
    for s in range(n_static, n_slots):
        is_late = s < n_late
        plan += sel_tile(jnp.where(is_late, late.x, early.x), jnp.where(is_late, s, s - n_late), None)

    acc_scr[...] = jnp.zeros_like(acc_scr)
    m_scr[...] = jnp.full(m_scr.shape, NEG, F32)
    job_pos = [k for k, e in enumerate(plan) if isinstance(e, tuple)]
    following = dict(zip(job_pos, job_pos[1:]))
    issued = {job_pos[0]: plan[job_pos[0]][0]()}
    for k, entry in enumerate(plan):
        if not isinstance(entry, tuple):
            entry()
            continue
        if k in following:
            issued[following[k]] = plan[following[k]][0]()
        _, probs, finish = entry
        finish(*probs(issued.pop(k)))

    for c in tiles:
        for bb in range(qa_ref.shape[0]):
            glt = c.g_ref[bb]
            heads = []
            for g in range(NSA_GROUPS):
                acc = acc_scr[c.x, uidx[bb, g]]
                o_sel = acc[0:HEAD_DIM] * (1.0 / acc[HEAD_DIM:HEAD_DIM + 1])
                for r in range(R):
                    hh = R * g + r
                    sl = slice(r * TQ, (r + 1) * TQ)
                    heads.append(glt[3 * hh:3 * hh + 1] * o_cmp[c.x, (bb, g)][:, sl]
                                 + glt[3 * hh + 1:3 * hh + 2] * o_sel[:, sl]
                                 + glt[3 * hh + 2:3 * hh + 3] * o_win[c.x, (bb, g)][:, sl])
            c.o_ref[bb] = jnp.concatenate(heads, axis=0).T


def _nsa(qq, kcmp, vcmpt, ksa, vst, kw, vwt, glt, ovt):
    B, H, T, _ = qq.shape
    G = NSA_GROUPS
    NB = NSA_NB
    NCP = kcmp.shape[2]
    n_qt = T // TQ
    grid = (B // NB, n_qt // 2)
    k_blk = lambda n: pl.BlockSpec((NB, G, n, LANES), lambda b, i: (b, 0, 0, 0))
    vt_blk = lambda r, n: pl.BlockSpec((NB, G, r, n), lambda b, i: (b, 0, 0, 0))
    lo_tile = lambda b, i: i
    hi_tile = lambda b, i: n_qt - 1 - i
    q_blk = lambda tile: pl.BlockSpec((NB, H, TQ, LANES), lambda b, i: (b, 0, tile(b, i), 0))
    g_blk = lambda tile: pl.BlockSpec((NB, GATE_ROWS, TQ), lambda b, i: (b, 0, tile(b, i)))
    half = jax.ShapeDtypeStruct((B, T // 2, NSA_WIDTH), F32)
    return pl.pallas_call(
        _nsa_kernel,
        grid=grid,
        in_specs=[
            q_blk(lo_tile), q_blk(hi_tile),
            k_blk(NCP), vt_blk(HEAD_DIM, NCP), k_blk(T), vt_blk(V_ROWS, T), k_blk(T), vt_blk(V_ROWS, T),
            g_blk(lo_tile), g_blk(hi_tile),
            pl.BlockSpec(ovt.shape, lambda b, i: (0, 0)),
        ],
        out_specs=[pl.BlockSpec((NB, TQ, NSA_WIDTH), lambda b, i: (b, i, 0)),
                   pl.BlockSpec((NB, TQ, NSA_WIDTH), lambda b, i: (b, n_qt // 2 - 1 - i, 0))],
        out_shape=[half, half],
        scratch_shapes=[pltpu.VMEM((2, NB * G, LANES, HEADS_PER_GROUP * TQ), BF16),
                        pltpu.VMEM((2, NB * G, V_ROWS, HEADS_PER_GROUP * TQ), F32),
                        pltpu.VMEM((2, NB * G, 1, HEADS_PER_GROUP * TQ), F32)],
        compiler_params=pltpu.CompilerParams(
            dimension_semantics=("parallel", "arbitrary"), vmem_limit_bytes=VMEM_LIMIT),
        name="nsa",
    )(qq, qq, kcmp, vcmpt, ksa, vst, kw, vwt, glt, glt, ovt)


def _s5_prep_kernel(lre_ref, lim_ref, ldt_ref, bre_ref, bim_ref, are_ref, aim_ref, bbre_ref, bbim_ref):
    lre, lim = lre_ref[...], lim_ref[...]
    dt = jnp.exp(ldt_ref[...])
    mag = jnp.exp(lre * dt)
    a_re = mag * jnp.cos(lim * dt)
    a_im = mag * jnp.sin(lim * dt)
    den = lre * lre + lim * lim
    z_re = ((a_re - 1.0) * lre + a_im * lim) / den
    z_im = (a_im * lre - (a_re - 1.0) * lim) / den
    are_ref[...] = a_re
    aim_ref[...] = a_im
    bbre_ref[...] = z_re * bre_ref[...] - z_im * bim_ref[...]
    bbim_ref[...] = z_re * bim_ref[...] + z_im * bre_ref[...]


def _s5_prep(lre, lim, ldt, bre, bim):
    shp = jax.ShapeDtypeStruct(lre.shape, F32)
    return pl.pallas_call(_s5_prep_kernel, out_shape=[shp, shp, shp, shp], name="s5_prep")(lre, lim, ldt, bre, bim)


def _s5_weights_kernel(bbre_ref, bbim_ref, are_ref, aim_ref, crt_ref, cit_ref,
                       we_ref, wct_ref, tp_ref, alre_ref, alim_ref, bd_scr, arow_scr, tp_scr):
    gpb = S5_CPB // S5_GROUP
    for q in range(we_ref.shape[0]):
        r0 = S5_CPB * q
        bd_scr[...] = jnp.zeros_like(bd_scr)
        tp_scr[...] = jnp.zeros_like(tp_scr)
        for n, ref in enumerate((bbre_ref, bbim_ref, crt_ref, cit_ref)):
            for gl in range(gpb):
                bd_scr[n, S5_GROUP * gl:S5_GROUP * (gl + 1), S5_STATE * gl:S5_STATE * (gl + 1)] = (
                    ref[r0 + S5_GROUP * gl:r0 + S5_GROUP * (gl + 1), :])
        for n, ref in enumerate((are_ref, aim_ref)):
            for gl in range(gpb):
                arow_scr[n, :, S5_STATE * gl:S5_STATE * (gl + 1)] = ref[r0 + S5_GROUP * gl:r0 + S5_GROUP * gl + 1, :]
        bbre, bbim, crt, cit = bd_scr[0], bd_scr[1], bd_scr[2], bd_scr[3]
        are, aim = arow_scr[0], arow_scr[1]
        pre, pim = jnp.ones_like(are), jnp.zeros_like(are)
        for k in range(S5_L):
            bpr = bbre * pre - bbim * pim
            bpi = bbre * pim + bbim * pre
            i = S5_L - 1 - k
            we_ref[q, S5_CPB * i:S5_CPB * (i + 1), :] = jnp.concatenate([bpr, bpi], axis=1).astype(BF16)
            tap = _nt_dot(bpr.astype(BF16), crt.astype(BF16)) - _nt_dot(bpi.astype(BF16), cit.astype(BF16))
            for i in range(S5_L - k):
                j = i + k
                tp_scr[S5_CPB * i:S5_CPB * (i + 1), S5_CPB * j:S5_CPB * (j + 1)] = tap
            pre, pim = pre * are - pim * aim, pre * aim + pim * are
            wct_ref[q, S5_CPB * k:S5_CPB * (k + 1), :] = jnp.concatenate(
                [crt * pre - cit * pim, -(crt * pim + cit * pre)], axis=1).astype(BF16)
        tp_ref[q] = tp_scr[...].astype(BF16)
        alre_ref[q] = pre
        alim_ref[q] = pim


def _s5_weights(bbre, bbim, are, aim, crt, cit):
    nblk = S5_WIDTH // S5_CPB
    sb = S5_CPB // S5_GROUP * S5_STATE
    lc = S5_L * S5_CPB
    bps = S5W_BLOCKS_PER_STEP
    rows = pl.BlockSpec((bps * S5_CPB, S5_STATE), lambda q: (q, 0))
    blk = lambda r, c: pl.BlockSpec((bps, r, c), lambda q: (q, 0, 0))
    return pl.pallas_call(
        _s5_weights_kernel,
        grid=(nblk // bps,),
        in_specs=[rows] * 6,
        out_specs=[blk(lc, 2 * sb), blk(lc, 2 * sb), blk(lc, lc), blk(1, sb), blk(1, sb)],
        scratch_shapes=[pltpu.VMEM((4, S5_CPB, sb), F32), pltpu.VMEM((2, 1, sb), F32), pltpu.VMEM((lc, lc), F32)],
        out_shape=[jax.ShapeDtypeStruct((nblk, lc, 2 * sb), BF16),
                   jax.ShapeDtypeStruct((nblk, lc, 2 * sb), BF16),
                   jax.ShapeDtypeStruct((nblk, lc, lc), BF16),
                   jax.ShapeDtypeStruct((nblk, 1, sb), F32), jax.ShapeDtypeStruct((nblk, 1, sb), F32)],
        compiler_params=pltpu.CompilerParams(dimension_semantics=("parallel",), vmem_limit_bytes=VMEM_LIMIT),
        name="s5_weights",
    )(bbre, bbim, are, aim, crt, cit)


def _s5_kernel(u_ref, we_ref, wct_ref, tp_ref, alre_ref, alim_ref, d_ref, wg_ref, bg_ref, o_ref, e_scr, st_scr):
    nb, ct, _ = u_ref.shape
    L = S5_L
    nblk = S5_WIDTH // S5_CPB
    sb = S5_CPB // S5_GROUP * S5_STATE
    spb = 2 * sb // LANES
    W = S5_SCAN_SLABS * LANES

    @pl.when(pl.program_id(0) == 0)
    def _():
        st_scr[...] = jnp.zeros_like(st_scr)

    uf = u_ref[...].reshape(nb * ct, L * S5_WIDTH)
    ub = uf.astype(BF16)
    u_blk = [jnp.concatenate([ub[:, i * S5_WIDTH + S5_CPB * q:i * S5_WIDTH + S5_CPB * (q + 1)] for i in range(L)],
                             axis=1) for q in range(nblk)]

    for q in range(nblk):
        e = jnp.dot(u_blk[q], we_ref[q], preferred_element_type=F32)
        for s in range(spb):
            for b in range(nb):
                e_scr[spb * q + s, pl.ds(b, ct, stride=nb), :] = e[b * ct:(b + 1) * ct, LANES * s:LANES * (s + 1)]

    def slabs_of(cs):
        q, r = divmod(cs * LANES, sb)
        return spb * q + r // LANES, spb * q + (sb + r) // LANES

    for j in range(S5_GROUPS * S5_STATE // W):
        sl = [slabs_of(S5_SCAN_SLABS * j + n) for n in range(S5_SCAN_SLABS)]
        sl_re, sl_im = [s[0] for s in sl], [s[1] for s in sl]
        ar = alre_ref[:, W * j:W * (j + 1)]
        ai = alim_ref[:, W * j:W * (j + 1)]
        load = lambda r0, rows, slabs: jnp.concatenate([e_scr[s, pl.ds(r0, rows), :] for s in slabs], axis=1)
        state = lambda slabs: jnp.concatenate([st_scr[:, LANES * s:LANES * (s + 1)] for s in slabs], axis=1)
        sr, si = state(sl_re), state(sl_im)
        for k in range(ct // 2):
            r0 = k * 2 * nb
            er, ei = load(r0, 2 * nb, sl_re), load(r0, 2 * nb, sl_im)
            tr = ar * sr - ai * si + er[0:nb]
            ti = ar * si + ai * sr + ei[0:nb]
            xr = jnp.concatenate([sr, tr], axis=0)
            xi = jnp.concatenate([si, ti], axis=0)
            for n in range(S5_SCAN_SLABS):
                e_scr[sl_re[n], pl.ds(r0, 2 * nb), :] = xr[:, LANES * n:LANES * (n + 1)]
                e_scr[sl_im[n], pl.ds(r0, 2 * nb), :] = xi[:, LANES * n:LANES * (n + 1)]
            sr, si = ar * tr - ai * ti + er[nb:], ar * ti + ai * tr + ei[nb:]
        for n in range(S5_SCAN_SLABS):
            st_scr[:, LANES * sl_re[n]:LANES * (sl_re[n] + 1)] = sr[:, LANES * n:LANES * (n + 1)]
            st_scr[:, LANES * sl_im[n]:LANES * (sl_im[n] + 1)] = si[:, LANES * n:LANES * (n + 1)]

    ys = []
    for q in range(nblk):
        per_b = [jnp.concatenate([e_scr[spb * q + s, pl.ds(b, ct, stride=nb), :] for s in range(spb)], axis=1)
                 for b in range(nb)]
        x_in = jnp.concatenate(per_b, axis=0).astype(BF16)
        ys.append((_nt_dot(x_in, wct_ref[q]) + jnp.dot(u_blk[q], tp_ref[q], preferred_element_type=F32)).astype(BF16))
    outs = []
    for j in range(L):
        yj = jnp.concatenate([y[:, S5_CPB * j:S5_CPB * (j + 1)] for y in ys], axis=1).astype(F32)
        yj = yj + d_ref[...] * uf[:, j * S5_WIDTH:(j + 1) * S5_WIDTH]
        z = _gelu_tanh(yj)
        gate = jnp.dot(z.astype(BF16), wg_ref[...], preferred_element_type=F32) + bg_ref[...]
        outs.append(z * _sigmoid(gate))
    o_ref[...] = jnp.concatenate(outs, axis=1).reshape(nb, ct, L * S5_WIDTH)


def _s5(u4, we, wct, tp, alre, alim, d, w_glu, b_glu):
    nb, nchunks, w4 = u4.shape
    ct = S5_CT
    n_slabs = 2 * S5_GROUPS * S5_STATE // LANES
    resident = lambda a: pl.BlockSpec(a.shape, lambda i: (0,) * a.ndim, pipeline_mode=pl.Buffered(1))
    u_blk = pl.BlockSpec((nb, ct, w4), lambda i: (0, i, 0))
    return pl.pallas_call(
        _s5_kernel,
        grid=(nchunks // ct,),
        in_specs=[u_blk, resident(we), resident(wct), resident(tp), resident(alre), resident(alim),
                  resident(d), resident(w_glu), resident(b_glu)],
        out_specs=u_blk,
        out_shape=jax.ShapeDtypeStruct(u4.shape, F32),
        scratch_shapes=[pltpu.VMEM((n_slabs, nb * ct, LANES), F32), pltpu.VMEM((nb, n_slabs * LANES), F32)],
        compiler_params=pltpu.CompilerParams(dimension_semantics=("arbitrary",), vmem_limit_bytes=VMEM_LIMIT),
        name="s5",
    )(u4, we, wct, tp, alre, alim, d, w_glu, b_glu)


def _final_kernel(x_ref, g_ref, wb_ref, onl_ref, onh_ref, os_ref, wpn_ref, wps_ref, wo_ref, fg_ref, o_ref, os_scr):
    tm = x_ref.shape[1]
    o4 = os_ref[0]
    for i in range(S5_L):
        for s in range(S5_WIDTH // LANES):
            c0 = i * S5_WIDTH + LANES * s
            os_scr[s, pl.ds(i, tm // S5_L, stride=S5_L), :] = o4[:, c0:c0 + LANES]
    first_half = pl.program_id(1) < pl.num_programs(1) // 2

    def silu(v):
        return v * _sigmoid(v)

    rows = [slice(r, r + tm // FINAL_SUB) for r in range(0, tm, tm // FINAL_SUB)]
    hs = [(_rms_scale(x_ref[0, r]) * g_ref[...]).astype(BF16) for r in rows]
    for r, h in zip(rows, hs):
        proj = lambda a, b, h=h: jnp.dot(h, wb_ref[:, a:b], preferred_element_type=F32)
        o_nsa = jnp.where(first_half, onl_ref[0, r], onh_ref[0, r])
        o_s5 = jnp.concatenate([os_scr[s, r] for s in range(S5_WIDTH // LANES)], axis=1)
        a_in = (o_nsa * silu(proj(_W_GN, _W_U))).astype(BF16)
        b_in = (o_s5 * silu(proj(_W_GS, _W_MG))).astype(BF16)
        branch_a = jnp.dot(a_in, wpn_ref[...], preferred_element_type=F32)
        branch_b = jnp.dot(b_in, wps_ref[...], preferred_element_type=F32)
        merged = (_sigmoid(proj(_W_MG, _W_MG + D_MODEL)) * branch_a
                  + _sigmoid(proj(_W_MG + D_MODEL, _W_END)) * branch_b)
        y = x_ref[0, r] + jnp.dot(merged.astype(BF16), wo_ref[...], preferred_element_type=F32)
        o_ref[0, r] = _rms_scale(y) * fg_ref[...]


def _final(x, norm_g, w_b, o_nsa_lo, o_nsa_hi, o_s5, wpn, wps, wo, final_g):
    B, T, D = x.shape
    tm = TM_PROJ
    nh = T // tm // 2
    row_blk = lambda w: pl.BlockSpec((1, tm, w), lambda b, i: (b, i, 0))
    full = lambda a: pl.BlockSpec(a.shape, lambda b, i: (0,) * a.ndim)
    return pl.pallas_call(
        _final_kernel,
        grid=(B, T // tm),
        in_specs=[row_blk(D), full(norm_g), pl.BlockSpec(w_b.shape, lambda b, i: (0, 0), pipeline_mode=pl.Buffered(1)),
                  pl.BlockSpec((1, tm, NSA_WIDTH), lambda b, i: (b, jnp.minimum(i, nh - 1), 0)),
                  pl.BlockSpec((1, tm, NSA_WIDTH), lambda b, i: (b, jnp.maximum(i - nh, 0), 0)),
                  pl.BlockSpec((1, tm // S5_L, S5_L * S5_WIDTH), lambda b, i: (b, i, 0)),
                  full(wpn), full(wps), full(wo), full(final_g)],
        out_specs=row_blk(D),
        out_shape=jax.ShapeDtypeStruct((B, T, D), F32),
        scratch_shapes=[pltpu.VMEM((S5_WIDTH // LANES, tm, LANES), F32)],
        compiler_params=pltpu.CompilerParams(
            dimension_semantics=("parallel", "arbitrary"), vmem_limit_bytes=VMEM_LIMIT),
        name="final",
    )(x, norm_g, w_b, o_nsa_lo, o_nsa_hi, o_s5, wpn, wps, wo, final_g)


def _rope_tables(T):
    half = HEAD_DIM // 2
    inv_freq = np.float32(ROPE_THETA) ** (-np.arange(half, dtype=np.float32) / np.float32(half))
    ang = np.arange(T, dtype=np.float32)[:, None] * inv_freq[None, :].astype(np.float32)
    cos, sin = np.cos(ang).astype(np.float32), np.sin(ang).astype(np.float32)
    cos2 = np.concatenate([cos, cos, cos, cos], axis=1)
    sin2 = np.concatenate([-sin, sin, -sin, sin], axis=1)
    return jnp.asarray(cos2), jnp.asarray(sin2)


def _compress_w1(w1):
    half_rows = CMP_STRIDE * HEAD_DIM
    return jnp.concatenate([w1[:half_rows], w1[half_rows:]], axis=1).astype(BF16)


def kernel(x, norm_g, w_in, cmp_pos_k, cmp_pos_v, cmp_w1_k, cmp_w2_k, cmp_w1_v, cmp_w2_v, s5_lam_re, s5_lam_im, s5_log_dt, s5_b_re, s5_b_im, s5_c_re, s5_c_im, s5_d, w_glu, b_glu, w_proj_nsa, w_proj_s5, w_out, final_g):
    B, T, D = x.shape
    assert w_in.shape[0] == 1, "single-layer block"
    NCH = T // CMP_STRIDE
    NS = T // SEL_BLOCK

    w = w_in[0]
    w_all = jnp.concatenate([w[:, :_OFF_GL], jnp.pad(w[:, _OFF_GL:_OFF_GN], ((0, 0), (0, LANES - 24))),
                             w[:, _OFF_GN:]], axis=1).astype(BF16)
    g2 = norm_g[0][None, :]
    cos2, sin2 = _rope_tables(T)

    qq, kc, vc, ksa, vst, kw, vwt, glt, u4 = _inproj(x, g2, w_all, cos2, sin2)

    w2k = jnp.concatenate([jnp.zeros_like(cmp_w2_k[0]), cmp_w2_k[0]], axis=1).astype(BF16)
    w2vt = cmp_w2_v[0].T.astype(BF16)
    pos_rows = lambda p: jnp.pad(p.reshape(2, CMP_STRIDE * HEAD_DIM), ((0, 2 * SUBLANES - 2), (0, 0)))
    kcmp, vcmpt = _compress(kc, vc, _compress_w1(cmp_w1_k[0]), _compress_w1(cmp_w1_v[0]), w2k, w2vt,
                            pos_rows(cmp_pos_k[0]), pos_rows(cmp_pos_v[0]))

    c_start = jnp.arange(NCH) * CMP_STRIDE
    s_start = jnp.arange(NS) * SEL_BLOCK
    ovt = ((c_start[None, :] < s_start[:, None] + SEL_BLOCK) & (c_start[None, :] + CMP_BLOCK > s_start[:, None])
           & (jnp.arange(NCH)[None, :] < NCH - 1)).astype(BF16)
    o_nsa_lo, o_nsa_hi = _nsa(qq, kcmp, vcmpt, ksa, vst, kw, vwt, glt, ovt)

    rep = lambda a: jnp.repeat(a, S5_GROUP, axis=0)
    tr = lambda b: b.transpose(0, 2, 1).reshape(S5_GROUPS * S5_GROUP, S5_STATE)
    a_re, a_im, bb_re, bb_im = _s5_prep(
        rep(s5_lam_re[0]), rep(s5_lam_im[0]),
        rep(jnp.broadcast_to(s5_log_dt[0][:, None], (S5_GROUPS, S5_STATE))),
        tr(s5_b_re[0]), tr(s5_b_im[0]))
    flat = lambda c: c.reshape(S5_GROUPS * S5_GROUP, S5_STATE)
    we, wct, tp, alre, alim = _s5_weights(bb_re, bb_im, a_re, a_im, flat(s5_c_re[0]), flat(s5_c_im[0]))
    o_s5 = _s5(u4, we, wct, tp, alre.reshape(1, -1), alim.reshape(1, -1), s5_d[0][None, :],
               w_glu[0].astype(BF16), b_glu[0][None, :])

    return _final(x, g2, w_all, o_nsa_lo, o_nsa_hi, o_s5, w_proj_nsa[0].astype(BF16), w_proj_s5[0].astype(BF16),
                  w_out[0].astype(BF16), final_g[None, :])
```

```python
import math

import jax
import jax.numpy as jnp
import numpy as np
from jax import lax
from jax.experimental import pallas as pl
from jax.experimental.pallas import tpu as pltpu

F32 = jnp.float32
BF16 = jnp.bfloat16

D_MODEL = 1024
NSA_HEADS = 8
NSA_GROUPS = 2
HEADS_PER_GROUP = 4
HEAD_DIM = 64
NSA_WIDTH = 512
CMP_BLOCK = 32
CMP_STRIDE = 16
CMP_HIDDEN = 256
SEL_BLOCK = 64
SEL_TOPK = 16
WINDOW = 512
ROPE_THETA = 10000.0
FORCED_SCORE = 1.0e4
NEG = -1.0e30
S5_WIDTH = 512
S5_GROUP = 16
S5_GROUPS = 32
S5_STATE = 64
RMS_EPS = 1.0e-6

LANES = 128
SUBLANES = 8
VMEM_LIMIT = 56 * 1024 * 1024

_OFF_GL = 1280
_OFF_GN = 1304
_W_GL, _W_GN, _W_U, _W_GS, _W_MG, _W_END = 1280, 1408, 1920, 2432, 2944, 4992

TM_PROJ = 512
FINAL_SUB = 2
TQ = 128
TK = 512
NSA_NB = 2
V_ROWS = 80
GATE_ROWS = 32
S5_L = 8
S5_CPB = 256 // S5_L
S5_CT = 64
S5W_BLOCKS_PER_STEP = 4
S5_SCAN_SLABS = 4


def _gelu_tanh(x):
    c = math.sqrt(2.0 / math.pi)
    return 0.5 * x * (1.0 + jnp.tanh(c * (x + 0.044715 * (x * x * x))))


def _sigmoid(x):
    return 1.0 / (1.0 + jnp.exp(-x))


def _rms_rinv(xv):
    return lax.rsqrt(jnp.mean(xv * xv, axis=-1, keepdims=True) + RMS_EPS)


def _rms_scale(xv):
    return xv * _rms_rinv(xv)


def _nt_dot(a, b):
    return lax.dot_general(a, b, (((1,), (1,)), ((), ())), preferred_element_type=F32)


def _inproj_kernel(x_ref, g_ref, w_ref, cos_ref, sin_ref,
                   qq_ref, kc_ref, vc_ref, ks_ref, vs_ref, kw_ref, vw_ref, gl_ref, u_ref, us_scr):
    xv = x_ref[0]
    r_norm = _rms_rinv(xv)
    h = (xv * g_ref[...]).astype(BF16)
    cos2 = cos_ref[...]
    sin2 = sin_ref[...]
    lane = lax.broadcasted_iota(jnp.int32, cos2.shape, 1)
    first_half = (lane & (HEAD_DIM - 1)) < (HEAD_DIM // 2)
    low = lane < HEAD_DIM

    wide = {}

    def proj(a, b):
        for (s0, s1) in ((0, 512), (512, _W_GN), (_W_U, _W_GS)):
            if s0 <= a and b <= s1:
                if s0 not in wide:
                    wide[s0] = jnp.dot(h, w_ref[:, s0:s1], preferred_element_type=F32) * r_norm
                return wide[s0][:, a - s0:b - s0]
        raise ValueError((a, b))

    def rope(xs):
        partner = jnp.where(first_half, pltpu.roll(xs, 96, 1), pltpu.roll(xs, 32, 1))
        return xs * cos2 + partner * sin2

    scale = HEAD_DIM ** -0.5 * math.log2(math.e)
    for i in range(NSA_HEADS // 2):
        xs = proj(LANES * i, LANES * (i + 1)) * scale
        xr = rope(xs)
        qq_ref[0, 2 * i] = jnp.where(low, xr, pltpu.roll(xs, 64, 1)).astype(BF16)
        qq_ref[0, 2 * i + 1] = jnp.where(low, pltpu.roll(xr, 64, 1), xs).astype(BF16)

    kc_ref[0] = proj(512, 640)
    vc_ref[0] = proj(640, 768)
    tm = cos2.shape[0]
    t_row = pl.program_id(1) * tm + lax.broadcasted_iota(jnp.int32, cos2.shape, 0)
    blk_onehot = jnp.where(lane - HEAD_DIM == t_row // SEL_BLOCK, 1.0, 0.0)
    ones_rows = jnp.where(lax.broadcasted_iota(jnp.int32, (V_ROWS - HEAD_DIM, tm), 0) == 0, 1.0, 0.0)
    for (off, k_out, v_out, k_pad) in ((768, ks_ref, vs_ref, blk_onehot), (1024, kw_ref, vw_ref, 0.0)):
        kr = rope(proj(off, off + LANES))
        k_out[0, 0] = jnp.where(low, kr, k_pad).astype(BF16)
        k_out[0, 1] = jnp.where(low, pltpu.roll(kr, 64, 1), k_pad).astype(BF16)
        vt = proj(off + LANES, off + 2 * LANES).T
        for g in range(NSA_GROUPS):
            v_out[0, g] = jnp.concatenate([vt[HEAD_DIM * g:HEAD_DIM * (g + 1)], ones_rows], axis=0).astype(BF16)
    gl_ref[0] = _sigmoid(proj(_W_GL, _W_GN)).T[0:GATE_ROWS]
    uv = proj(_W_U, _W_GS)
    for s in range(S5_WIDTH // LANES):
        us_scr[s] = uv[:, LANES * s:LANES * (s + 1)]
    for i in range(S5_L):
        for s in range(S5_WIDTH // LANES):
            c0 = i * S5_WIDTH + LANES * s
            u_ref[0, :, c0:c0 + LANES] = us_scr[s, pl.ds(i, tm // S5_L, stride=S5_L), :]


def _inproj(x, norm_g, w_a, cos2, sin2):
    B, T, D = x.shape
    tm = TM_PROJ
    grid = (B, T // tm)
    row_blk = lambda w: pl.BlockSpec((1, tm, w), lambda b, i: (b, i, 0))
    kv_blk = pl.BlockSpec((1, NSA_GROUPS, tm, LANES), lambda b, i: (b, 0, i, 0))
    kv_shape = jax.ShapeDtypeStruct((B, NSA_GROUPS, T, LANES), BF16)
    vt_blk = pl.BlockSpec((1, NSA_GROUPS, V_ROWS, tm), lambda b, i: (b, 0, 0, i))
    vt_shape = jax.ShapeDtypeStruct((B, NSA_GROUPS, V_ROWS, T), BF16)
    return pl.pallas_call(
        _inproj_kernel,
        grid=grid,
        in_specs=[
            row_blk(D),
            pl.BlockSpec((1, D), lambda b, i: (0, 0)),
            pl.BlockSpec(w_a.shape, lambda b, i: (0, 0), pipeline_mode=pl.Buffered(1)),
            pl.BlockSpec((tm, LANES), lambda b, i: (i, 0)),
            pl.BlockSpec((tm, LANES), lambda b, i: (i, 0)),
        ],
        out_specs=[
            pl.BlockSpec((1, NSA_HEADS, tm, LANES), lambda b, i: (b, 0, i, 0)),
            row_blk(LANES), row_blk(LANES),
            kv_blk, vt_blk, kv_blk, vt_blk,
            pl.BlockSpec((1, GATE_ROWS, tm), lambda b, i: (b, 0, i)),
            pl.BlockSpec((1, tm // S5_L, S5_L * S5_WIDTH), lambda b, i: (b, i, 0)),
        ],
        out_shape=[
            jax.ShapeDtypeStruct((B, NSA_HEADS, T, LANES), BF16),
            jax.ShapeDtypeStruct((B, T, LANES), F32), jax.ShapeDtypeStruct((B, T, LANES), F32),
            kv_shape, vt_shape, kv_shape, vt_shape,
            jax.ShapeDtypeStruct((B, GATE_ROWS, T), F32),
            jax.ShapeDtypeStruct((B, T // S5_L, S5_L * S5_WIDTH), F32),
        ],
        scratch_shapes=[pltpu.VMEM((S5_WIDTH // LANES, tm, LANES), F32)],
        compiler_params=pltpu.CompilerParams(
            dimension_semantics=("parallel", "arbitrary"), vmem_limit_bytes=VMEM_LIMIT),
        name="inproj",
    )(x, norm_g, w_a, cos2, sin2)


def _compress_kernel(kc_ref, vc_ref, w1k_ref, w1v_ref, w2k_ref, w2vt_ref, pbk_ref, pbv_ref, ko_ref, vo_ref, pb_scr):
    nch = ko_ref.shape[2]
    H = CMP_HIDDEN

    @pl.when(pl.program_id(0) == 0)
    def _():
        for n, (p_ref, w1_ref) in enumerate(((pbk_ref, w1k_ref), (pbv_ref, w1v_ref))):
            pw = jnp.dot(p_ref[...].astype(BF16), w1_ref[...], preferred_element_type=F32)
            pb_scr[n] = pw[0:1, 0:H] + pw[1:2, H:]

    def hidden(c_ref, w1_ref, n):
        acc = [jnp.zeros((nch, 2 * H), F32) for _ in range(NSA_GROUPS)]
        for j in range(CMP_STRIDE):
            rows = c_ref[0, pl.ds(j, nch, stride=CMP_STRIDE), :].astype(BF16)
            wj = w1_ref[HEAD_DIM * j:HEAD_DIM * (j + 1), :]
            for g in range(NSA_GROUPS):
                acc[g] = acc[g] + jnp.dot(rows[:, HEAD_DIM * g:HEAD_DIM * (g + 1)], wj, preferred_element_type=F32)
        return [_gelu_tanh(a[:, 0:H] + pltpu.roll(a[:, H:], nch - 1, 0) + pb_scr[n]).astype(BF16) for a in acc]

    hk = hidden(kc_ref, w1k_ref, 0)
    hv = hidden(vc_ref, w1v_ref, 1)
    for g in range(NSA_GROUPS):
        ko_ref[0, g] = jnp.dot(hk[g], w2k_ref[...], preferred_element_type=F32).astype(BF16)
        vo_ref[0, g] = _nt_dot(w2vt_ref[...], hv[g]).astype(BF16)


def _compress(kc, vc, w1k, w1v, w2k, w2vt, pbk, pbv):
    B, T, _ = kc.shape
    G = NSA_GROUPS
    nch = T // CMP_STRIDE
    c_blk = pl.BlockSpec((1, T, LANES), lambda b: (b, 0, 0))
    full = lambda a: pl.BlockSpec(a.shape, lambda b: (0,) * a.ndim)
    return pl.pallas_call(
        _compress_kernel,
        grid=(B,),
        in_specs=[c_blk, c_blk, full(w1k), full(w1v), full(w2k), full(w2vt), full(pbk), full(pbv)],
        out_specs=[pl.BlockSpec((1, G, nch, LANES), lambda b: (b, 0, 0, 0)),
                   pl.BlockSpec((1, G, HEAD_DIM, nch), lambda b: (b, 0, 0, 0))],
        out_shape=[jax.ShapeDtypeStruct((B, G, nch, LANES), BF16),
                   jax.ShapeDtypeStruct((B, G, HEAD_DIM, nch), BF16)],
        scratch_shapes=[pltpu.VMEM((2, 1, CMP_HIDDEN), F32)],
        compiler_params=pltpu.CompilerParams(dimension_semantics=("arbitrary",), vmem_limit_bytes=VMEM_LIMIT),
        name="compress",
    )(kc, vc, w1k, w1v, w2k, w2vt, pbk, pbv)


class _QTile:
    def __init__(self, x, t0, t_begin, t_end, q_ref, g_ref, o_ref, cols, n_wc):
        self.x, self.t0, self.q_ref, self.g_ref, self.o_ref = x, t0, q_ref, g_ref, o_ref
        self.window_inside = t_begin >= WINDOW
        self.ncp = t_end // CMP_STRIDE
        self.ns = t_end // SEL_BLOCK
        self.t_lane = t0 + (lax.broadcasted_iota(jnp.int32, (1, cols), 1) & (TQ - 1))
        c_end = lax.broadcasted_iota(jnp.int32, (self.ncp, cols), 0) * CMP_STRIDE + (CMP_BLOCK - 1)
        self.cmp_valid = c_end <= self.t_lane
        self.w_pos = [t0 - WINDOW + TQ * c for c in range(n_wc)]
        self.w_start = [pl.multiple_of(jnp.maximum(p, 0), TQ) for p in self.w_pos]


def _nsa_kernel(qa_ref, qb_ref, kc_ref, vct_ref, ksa_ref, vst_ref, kw_ref, vwt_ref, ga_ref, gb_ref, ovt_ref,
                oa_ref, ob_ref, qsel_scr, acc_scr, m_scr):
    units = [(bb, g) for bb in range(qa_ref.shape[0]) for g in range(NSA_GROUPS)]
    uidx = {u: i for i, u in enumerate(units)}
    n_qt = kw_ref.shape[2] // TQ
    R = HEADS_PER_GROUP
    cols = R * TQ
    NS = ovt_ref.shape[0]
    n_wc = (WINDOW + TQ) // TQ
    step = pl.program_id(1)
    t_mid = n_qt // 2 * TQ
    tiles = [_QTile(0, step * TQ, 0, t_mid, qa_ref, ga_ref, oa_ref, cols, n_wc),
             _QTile(1, (n_qt - 1 - step) * TQ, t_mid, n_qt * TQ, qb_ref, gb_ref, ob_ref, cols, n_wc)]
    sub8 = lax.broadcasted_iota(jnp.int32, (SUBLANES, TQ), 0)
    row_tq = lax.broadcasted_iota(jnp.int32, (TQ, cols), 0)

    def group_q(c, u):
        bb, g = u
        return c.q_ref[bb, R * g:R * (g + 1)].reshape(cols, LANES)

    def cmp_scores(c, u):
        return _nt_dot(kc_ref[u[0], u[1], 0:c.ncp, :], group_q(c, u))

    def cmp_probs(c, s):
        s = jnp.where(c.cmp_valid, s, NEG)
        e = jnp.exp2(s - jnp.max(s, axis=0, keepdims=True))
        inv = 1.0 / jnp.maximum(jnp.sum(e, axis=0, keepdims=True), 1.0e-30)
        return e * jnp.where(c.t_lane >= CMP_BLOCK - 1, inv, 0.0)

    def win_scores(c, u):
        kw = jnp.concatenate([kw_ref[u[0], u[1], pl.ds(c.w_start[n], TQ), :] for n in range(n_wc)], axis=0)
        return _nt_dot(kw, group_q(c, u))

    def win_probs(c, sw):
        parts = []
        for n in range(n_wc):
            sc = sw[TQ * n:TQ * (n + 1)]
            if n == 0:
                sc = jnp.where(c.w_pos[0] + row_tq > c.t_lane - WINDOW, sc, NEG)
            if n == n_wc - 1:
                sc = jnp.where(c.t0 + row_tq <= c.t_lane, sc, NEG)
            elif not c.window_inside:
                sc = jnp.where(c.w_pos[n] >= 0, sc, NEG)
            parts.append(sc.astype(BF16))
        sw = jnp.concatenate(parts, axis=0)
        return jnp.exp2(sw - jnp.max(sw, axis=0, keepdims=True))

    def win_out(c, u, ew):
        vw = jnp.concatenate([vwt_ref[u[0], u[1], :, pl.ds(c.w_start[n], TQ)] for n in range(n_wc)], axis=1)
        ow = jnp.dot(vw, ew, preferred_element_type=F32)
        return ow[0:HEAD_DIM] * (1.0 / ow[HEAD_DIM:HEAD_DIM + 1])

    def select_blocks(c, u, p):
        ns = c.ns
        psum = p[:, 0:TQ] + p[:, TQ:2 * TQ] + p[:, 2 * TQ:3 * TQ] + p[:, 3 * TQ:4 * TQ]
        p_hi = psum.astype(BF16)
        p_lo = (psum - p_hi.astype(F32)).astype(BF16)
        ov = ovt_ref[0:ns, 0:c.ncp]
        imp = (jnp.dot(ov, p_hi, preferred_element_type=F32) + jnp.dot(ov, p_lo, preferred_element_type=F32))
        blk = lax.broadcasted_iota(jnp.int32, (ns, TQ), 0)
        t_l = c.t0 + lax.broadcasted_iota(jnp.int32, (ns, TQ), 1)
        cur = t_l // SEL_BLOCK
        imp = jnp.where(blk * SEL_BLOCK <= t_l, imp, -1.0)
        imp = jnp.where(blk == 0, FORCED_SCORE, imp)
        imp = jnp.where(blk == cur, FORCED_SCORE, imp)
        imp = jnp.where(blk == cur - 1, FORCED_SCORE, imp)
        nv = ns // SUBLANES
        imp8 = [imp[SUBLANES * j:SUBLANES * (j + 1)] for j in range(nv)]
        rank8 = [jnp.zeros((SUBLANES, TQ), F32) for _ in range(nv)]
        for mm in range(ns):
            row = imp[mm:mm + 1, :]
            jm = mm // SUBLANES
            for j in range(nv):
                if j < jm:
                    ahead = jnp.where(row > imp8[j], 1.0, 0.0)
                elif j > jm:
                    ahead = jnp.where(row >= imp8[j], 1.0, 0.0)
                else:
                    tie = jnp.where(sub8 > (mm % SUBLANES), 1.0, 0.0)
                    ahead = jnp.where(row > imp8[j], 1.0, 0.0) + jnp.where(row == imp8[j], tie, 0.0)
                rank8[j] = rank8[j] + ahead
        pen = jnp.where(jnp.concatenate(rank8, axis=0) < float(SEL_TOPK), 0.0, NEG)
        if ns < NS:
            pen = jnp.concatenate([pen, jnp.zeros((NS - ns, TQ), F32)], axis=0)
        q_t = group_q(c, u).astype(F32).T.astype(BF16)
        qsel_scr[c.x, uidx[u]] = jnp.concatenate(
            [q_t[0:HEAD_DIM], jnp.concatenate([pen.astype(BF16)] * R, axis=1)], axis=0)

    p_c, o_cmp, o_win = {}, {}, {}

    def cmp_job(c, u):
        def finish(p, _):
            p_c[c.x, u] = p
            o_cmp[c.x, u] = jnp.dot(vct_ref[u[0], u[1], :, 0:c.ncp], p.astype(BF16),
                                    preferred_element_type=F32)
        return (lambda: cmp_scores(c, u)), (lambda s: (cmp_probs(c, s), None)), finish

    def win_job(c, u):
        def finish(e_w, _):
            o_win[c.x, u] = win_out(c, u, e_w)
        return (lambda: win_scores(c, u)), (lambda s: (win_probs(c, s), None)), finish

    def sel_job(x, kt, u, t_mask):
        k0 = kt * TK if isinstance(kt, int) else pl.multiple_of(kt * TK, TK)

        def probs(sc):
            if t_mask is not None:
                sc = jnp.where(kt * TK + lax.broadcasted_iota(jnp.int32, (TK, cols), 0) <= t_mask, sc, NEG)
            sc = sc.astype(BF16)
            m_old = m_scr[x, uidx[u]]
            m_new = jnp.maximum(m_old, jnp.max(sc, axis=0, keepdims=True).astype(F32))
            m_scr[x, uidx[u]] = m_new
            return jnp.exp2(sc - m_new.astype(BF16)), jnp.exp2(m_old - m_new)

        def finish(pp, alpha):
            acc_scr[x, uidx[u]] = acc_scr[x, uidx[u]] * alpha + jnp.dot(
                vst_ref[u[0], u[1], :, pl.ds(k0, TK)], pp, preferred_element_type=F32)

        return (lambda: jnp.dot(ksa_ref[u[0], u[1], pl.ds(k0, TK), :], qsel_scr[x, uidx[u]],
                                preferred_element_type=F32)), probs, finish

    def fuse(js):
        return ((lambda: [j[0]() for j in js]),
                (lambda ss: ([j[1](s) for j, s in zip(js, ss)], None)),
                (lambda outs, _: [j[2](*o) for j, o in zip(js, outs)]))

    def sel_tile(x, kt, t_mask):
        return [fuse([sel_job(x, kt, (bb, g), t_mask) for g in range(NSA_GROUPS)]) for bb in range(qa_ref.shape[0])]

    early, late = tiles
    n_slots = (n_qt - 1) * TQ // TK
    n_static = n_slots - n_slots // 2
    n_late = late.t0 // TK
    plan = [cmp_job(c, u) for c in (late, early) for u in units]
    for u in units:
        plan += [lambda u=u: select_blocks(late, u, p_c[late.x, u]), win_job(late, u)]
    plan += sel_tile(late.x, n_late, late.t_lane)
    for s in range(n_static):
        plan += sel_tile(late.x, s, None)
        for u in units[s::n_static]:
            plan += [lambda u=u: select_blocks(early, u, p_c[early.x, u]), win_job(early, u)]
    plan += sel_tile(early.x, early.t0 // TK, early.t_lane)
    for s in range(n_static, n_slots):
        is_late = s < n_late
        plan += sel_tile(jnp.where(is_late, late.x, early.x), jnp.where(is_late, s, s - n_late), None)

    acc_scr[...] = jnp.zeros_like(acc_scr)
    m_scr[...] = jnp.full(m_scr.shape, NEG, F32)
    job_pos = [k for k, e in enumerate(plan) if isinstance(e, tuple)]
    following = dict(zip(job_pos, job_pos[1:]))
    issued = {job_pos[0]: plan[job_pos[0]][0]()}
    for k, entry in enumerate(plan):
        if not isinstance(entry, tuple):
            entry()
            continue
        if k in following:
            issued[following[k]] = plan[following[k]][0]()
        _, probs, finish = entry
        finish(*probs(issued.pop(k)))

    for c in tiles:
        for bb in range(qa_ref.shape[0]):
            glt = c.g_ref[bb]
            heads = []
            for g in range(NSA_GROUPS):
                acc = acc_scr[c.x, uidx[bb, g]]
                o_sel = acc[0:HEAD_DIM] * (1.0 / acc[HEAD_DIM:HEAD_DIM + 1])
                for r in range(R):
                    hh = R * g + r
                    sl = slice(r * TQ, (r + 1) * TQ)
                    heads.append(glt[3 * hh:3 * hh + 1] * o_cmp[c.x, (bb, g)][:, sl]
                                 + glt[3 * hh + 1:3 * hh + 2] * o_sel[:, sl]
                                 + glt[3 * hh + 2:3 * hh + 3] * o_win[c.x, (bb, g)][:, sl])
            c.o_ref[bb] = jnp.concatenate(heads, axis=0).T


def _nsa(qq, kcmp, vcmpt, ksa, vst, kw, vwt, glt, ovt):
    B, H, T, _ = qq.shape
    G = NSA_GROUPS
    NB = NSA_NB
    NCP = kcmp.shape[2]
    n_qt = T // TQ
    grid = (B // NB, n_qt // 2)
    k_blk = lambda n: pl.BlockSpec((NB, G, n, LANES), lambda b, i: (b, 0, 0, 0))
    vt_blk = lambda r, n: pl.BlockSpec((NB, G, r, n), lambda b, i: (b, 0, 0, 0))
    lo_tile = lambda b, i: i
    hi_tile = lambda b, i: n_qt - 1 - i
    q_blk = lambda tile: pl.BlockSpec((NB, H, TQ, LANES), lambda b, i: (b, 0, tile(b, i), 0))
    g_blk = lambda tile: pl.BlockSpec((NB, GATE_ROWS, TQ), lambda b, i: (b, 0, tile(b, i)))
    half = jax.ShapeDtypeStruct((B, T // 2, NSA_WIDTH), F32)
    return pl.pallas_call(
        _nsa_kernel,
        grid=grid,
        in_specs=[
            q_blk(lo_tile), q_blk(hi_tile),
            k_blk(NCP), vt_blk(HEAD_DIM, NCP), k_blk(T), vt_blk(V_ROWS, T), k_blk(T), vt_blk(V_ROWS, T),
            g_blk(lo_tile), g_blk(hi_tile),
            pl.BlockSpec(ovt.shape, lambda b, i: (0, 0)),
        ],
        out_specs=[pl.BlockSpec((NB, TQ, NSA_WIDTH), lambda b, i: (b, i, 0)),
                   pl.BlockSpec((NB, TQ, NSA_WIDTH), lambda b, i: (b, n_qt // 2 - 1 - i, 0))],
        out_shape=[half, half],
        scratch_shapes=[pltpu.VMEM((2, NB * G, LANES, HEADS_PER_GROUP * TQ), BF16),
                        pltpu.VMEM((2, NB * G, V_ROWS, HEADS_PER_GROUP * TQ), F32),
                        pltpu.VMEM((2, NB * G, 1, HEADS_PER_GROUP * TQ), F32)],
        compiler_params=pltpu.CompilerParams(
            dimension_semantics=("parallel", "arbitrary"), vmem_limit_bytes=VMEM_LIMIT),
        name="nsa",
    )(qq, qq, kcmp, vcmpt, ksa, vst, kw, vwt, glt, glt, ovt)


def _s5_prep_kernel(lre_ref, lim_ref, ldt_ref, bre_ref, bim_ref, are_ref, aim_ref, bbre_ref, bbim_ref):
    lre, lim = lre_ref[...], lim_ref[...]
    dt = jnp.exp(ldt_ref[...])
    mag = jnp.exp(lre * dt)
    a_re = mag * jnp.cos(lim * dt)
    a_im = mag * jnp.sin(lim * dt)
    den = lre * lre + lim * lim
    z_re = ((a_re - 1.0) * lre + a_im * lim) / den
    z_im = (a_im * lre - (a_re - 1.0) * lim) / den
    are_ref[...] = a_re
    aim_ref[...] = a_im
    bbre_ref[...] = z_re * bre_ref[...] - z_im * bim_ref[...]
    bbim_ref[...] = z_re * bim_ref[...] + z_im * bre_ref[...]


def _s5_prep(lre, lim, ldt, bre, bim):
    shp = jax.ShapeDtypeStruct(lre.shape, F32)
    return pl.pallas_call(_s5_prep_kernel, out_shape=[shp, shp, shp, shp], name="s5_prep")(lre, lim, ldt, bre, bim)


def _s5_weights_kernel(bbre_ref, bbim_ref, are_ref, aim_ref, crt_ref, cit_ref,
                       we_ref, wct_ref, tp_ref, alre_ref, alim_ref, bd_scr, arow_scr, tp_scr):
    gpb = S5_CPB // S5_GROUP
    for q in range(we_ref.shape[0]):
        r0 = S5_CPB * q
        bd_scr[...] = jnp.zeros_like(bd_scr)
        tp_scr[...] = jnp.zeros_like(tp_scr)
        for n, ref in enumerate((bbre_ref, bbim_ref, crt_ref, cit_ref)):
            for gl in range(gpb):
                bd_scr[n, S5_GROUP * gl:S5_GROUP * (gl + 1), S5_STATE * gl:S5_STATE * (gl + 1)] = (
                    ref[r0 + S5_GROUP * gl:r0 + S5_GROUP * (gl + 1), :])
        for n, ref in enumerate((are_ref, aim_ref)):
            for gl in range(gpb):
                arow_scr[n, :, S5_STATE * gl:S5_STATE * (gl + 1)] = ref[r0 + S5_GROUP * gl:r0 + S5_GROUP * gl + 1, :]
        bbre, bbim, crt, cit = bd_scr[0], bd_scr[1], bd_scr[2], bd_scr[3]
        are, aim = arow_scr[0], arow_scr[1]
        pre, pim = jnp.ones_like(are), jnp.zeros_like(are)
        for k in range(S5_L):
            bpr = bbre * pre - bbim * pim
            bpi = bbre * pim + bbim * pre
            i = S5_L - 1 - k
            we_ref[q, S5_CPB * i:S5_CPB * (i + 1), :] = jnp.concatenate([bpr, bpi], axis=1).astype(BF16)
            tap = _nt_dot(bpr.astype(BF16), crt.astype(BF16)) - _nt_dot(bpi.astype(BF16), cit.astype(BF16))
            for i in range(S5_L - k):
                j = i + k
                tp_scr[S5_CPB * i:S5_CPB * (i + 1), S5_CPB * j:S5_CPB * (j + 1)] = tap
            pre, pim = pre * are - pim * aim, pre * aim + pim * are
            wct_ref[q, S5_CPB * k:S5_CPB * (k + 1), :] = jnp.concatenate(
                [crt * pre - cit * pim, -(crt * pim + cit * pre)], axis=1).astype(BF16)
        tp_ref[q] = tp_scr[...].astype(BF16)
        alre_ref[q] = pre
        alim_ref[q] = pim


def _s5_weights(bbre, bbim, are, aim, crt, cit):
    nblk = S5_WIDTH // S5_CPB
    sb = S5_CPB // S5_GROUP * S5_STATE
    lc = S5_L * S5_CPB
    bps = S5W_BLOCKS_PER_STEP
    rows = pl.BlockSpec((bps * S5_CPB, S5_STATE), lambda q: (q, 0))
    blk = lambda r, c: pl.BlockSpec((bps, r, c), lambda q: (q, 0, 0))
    return pl.pallas_call(
        _s5_weights_kernel,
        grid=(nblk // bps,),
        in_specs=[rows] * 6,
        out_specs=[blk(lc, 2 * sb), blk(lc, 2 * sb), blk(lc, lc), blk(1, sb), blk(1, sb)],
        scratch_shapes=[pltpu.VMEM((4, S5_CPB, sb), F32), pltpu.VMEM((2, 1, sb), F32), pltpu.VMEM((lc, lc), F32)],
        out_shape=[jax.ShapeDtypeStruct((nblk, lc, 2 * sb), BF16),
                   jax.ShapeDtypeStruct((nblk, lc, 2 * sb), BF16),
                   jax.ShapeDtypeStruct((nblk, lc, lc), BF16),
                   jax.ShapeDtypeStruct((nblk, 1, sb), F32), jax.ShapeDtypeStruct((nblk, 1, sb), F32)],
        compiler_params=pltpu.CompilerParams(dimension_semantics=("parallel",), vmem_limit_bytes=VMEM_LIMIT),
        name="s5_weights",
    )(bbre, bbim, are, aim, crt, cit)


def _s5_kernel(u_ref, we_ref, wct_ref, tp_ref, alre_ref, alim_ref, d_ref, wg_ref, bg_ref, o_ref, e_scr, st_scr):
    nb, ct, _ = u_ref.shape
    L = S5_L
    nblk = S5_WIDTH // S5_CPB
    sb = S5_CPB // S5_GROUP * S5_STATE
    spb = 2 * sb // LANES
    W = S5_SCAN_SLABS * LANES

    @pl.when(pl.program_id(0) == 0)
    def _():
        st_scr[...] = jnp.zeros_like(st_scr)

    uf = u_ref[...].reshape(nb * ct, L * S5_WIDTH)
    ub = uf.astype(BF16)
    u_blk = [jnp.concatenate([ub[:, i * S5_WIDTH + S5_CPB * q:i * S5_WIDTH + S5_CPB * (q + 1)] for i in range(L)],
                             axis=1) for q in range(nblk)]

    for q in range(nblk):
        e = jnp.dot(u_blk[q], we_ref[q], preferred_element_type=F32)
        for s in range(spb):
            for b in range(nb):
                e_scr[spb * q + s, pl.ds(b, ct, stride=nb), :] = e[b * ct:(b + 1) * ct, LANES * s:LANES * (s + 1)]

    def slabs_of(cs):
        q, r = divmod(cs * LANES, sb)
        return spb * q + r // LANES, spb * q + (sb + r) // LANES

    for j in range(S5_GROUPS * S5_STATE // W):
        sl = [slabs_of(S5_SCAN_SLABS * j + n) for n in range(S5_SCAN_SLABS)]
        sl_re, sl_im = [s[0] for s in sl], [s[1] for s in sl]
        ar = alre_ref[:, W * j:W * (j + 1)]
        ai = alim_ref[:, W * j:W * (j + 1)]
        load = lambda r0, rows, slabs: jnp.concatenate([e_scr[s, pl.ds(r0, rows), :] for s in slabs], axis=1)
        state = lambda slabs: jnp.concatenate([st_scr[:, LANES * s:LANES * (s + 1)] for s in slabs], axis=1)
        sr, si = state(sl_re), state(sl_im)
        for k in range(ct // 2):
            r0 = k * 2 * nb
            er, ei = load(r0, 2 * nb, sl_re), load(r0, 2 * nb, sl_im)
            tr = ar * sr - ai * si + er[0:nb]
            ti = ar * si + ai * sr + ei[0:nb]
            xr = jnp.concatenate([sr, tr], axis=0)
            xi = jnp.concatenate([si, ti], axis=0)
            for n in range(S5_SCAN_SLABS):
                e_scr[sl_re[n], pl.ds(r0, 2 * nb), :] = xr[:, LANES * n:LANES * (n + 1)]
                e_scr[sl_im[n], pl.ds(r0, 2 * nb), :] = xi[:, LANES * n:LANES * (n + 1)]
            sr, si = ar * tr - ai * ti + er[nb:], ar * ti + ai * tr + ei[nb:]
        for n in range(S5_SCAN_SLABS):
            st_scr[:, LANES * sl_re[n]:LANES * (sl_re[n] + 1)] = sr[:, LANES * n:LANES * (n + 1)]
            st_scr[:, LANES * sl_im[n]:LANES * (sl_im[n] + 1)] = si[:, LANES * n:LANES * (n + 1)]

    ys = []
    for q in range(nblk):
        per_b = [jnp.concatenate([e_scr[spb * q + s, pl.ds(b, ct, stride=nb), :] for s in range(spb)], axis=1)
                 for b in range(nb)]
        x_in = jnp.concatenate(per_b, axis=0).astype(BF16)
        ys.append((_nt_dot(x_in, wct_ref[q]) + jnp.dot(u_blk[q], tp_ref[q], preferred_element_type=F32)).astype(BF16))
    outs = []
    for j in range(L):
        yj = jnp.concatenate([y[:, S5_CPB * j:S5_CPB * (j + 1)] for y in ys], axis=1).astype(F32)
        yj = yj + d_ref[...] * uf[:, j * S5_WIDTH:(j + 1) * S5_WIDTH]
        z = _gelu_tanh(yj)
        gate = jnp.dot(z.astype(BF16), wg_ref[...], preferred_element_type=F32) + bg_ref[...]
        outs.append(z * _sigmoid(gate))
    o_ref[...] = jnp.concatenate(outs, axis=1).reshape(nb, ct, L * S5_WIDTH)


def _s5(u4, we, wct, tp, alre, alim, d, w_glu, b_glu):
    nb, nchunks, w4 = u4.shape
    ct = S5_CT
    n_slabs = 2 * S5_GROUPS * S5_STATE // LANES
    resident = lambda a: pl.BlockSpec(a.shape, lambda i: (0,) * a.ndim, pipeline_mode=pl.Buffered(1))
    u_blk = pl.BlockSpec((nb, ct, w4), lambda i: (0, i, 0))
    return pl.pallas_call(
        _s5_kernel,
        grid=(nchunks // ct,),
        in_specs=[u_blk, resident(we), resident(wct), resident(tp), resident(alre), resident(alim),
                  resident(d), resident(w_glu), resident(b_glu)],
        out_specs=u_blk,
        out_shape=jax.ShapeDtypeStruct(u4.shape, F32),
        scratch_shapes=[pltpu.VMEM((n_slabs, nb * ct, LANES), F32), pltpu.VMEM((nb, n_slabs * LANES), F32)],
        compiler_params=pltpu.CompilerParams(dimension_semantics=("arbitrary",), vmem_limit_bytes=VMEM_LIMIT),
        name="s5",
    )(u4, we, wct, tp, alre, alim, d, w_glu, b_glu)


def _final_kernel(x_ref, g_ref, wb_ref, onl_ref, onh_ref, os_ref, wpn_ref, wps_ref, wo_ref, fg_ref, o_ref, os_scr):
    tm = x_ref.shape[1]
    o4 = os_ref[0]
    for i in range(S5_L):
        for s in range(S5_WIDTH // LANES):
            c0 = i * S5_WIDTH + LANES * s
            os_scr[s, pl.ds(i, tm // S5_L, stride=S5_L), :] = o4[:, c0:c0 + LANES]
    first_half = pl.program_id(1) < pl.num_programs(1) // 2

    def silu(v):
        return v * _sigmoid(v)

    rows = [slice(r, r + tm // FINAL_SUB) for r in range(0, tm, tm // FINAL_SUB)]
    hs = [(_rms_scale(x_ref[0, r]) * g_ref[...]).astype(BF16) for r in rows]
    for r, h in zip(rows, hs):
        proj = lambda a, b, h=h: jnp.dot(h, wb_ref[:, a:b], preferred_element_type=F32)
        o_nsa = jnp.where(first_half, onl_ref[0, r], onh_ref[0, r])
        o_s5 = jnp.concatenate([os_scr[s, r] for s in range(S5_WIDTH // LANES)], axis=1)
        a_in = (o_nsa * silu(proj(_W_GN, _W_U))).astype(BF16)
        b_in = (o_s5 * silu(proj(_W_GS, _W_MG))).astype(BF16)
        branch_a = jnp.dot(a_in, wpn_ref[...], preferred_element_type=F32)
        branch_b = jnp.dot(b_in, wps_ref[...], preferred_element_type=F32)
        merged = (_sigmoid(proj(_W_MG, _W_MG + D_MODEL)) * branch_a
                  + _sigmoid(proj(_W_MG + D_MODEL, _W_END)) * branch_b)
        y = x_ref[0, r] + jnp.dot(merged.astype(BF16), wo_ref[...], preferred_element_type=F32)
        o_ref[0, r] = _rms_scale(y) * fg_ref[...]


def _final(x, norm_g, w_b, o_nsa_lo, o_nsa_hi, o_s5, wpn, wps, wo, final_g):
    B, T, D = x.shape
    tm = TM_PROJ
    nh = T // tm // 2
    row_blk = lambda w: pl.BlockSpec((1, tm, w), lambda b, i: (b, i, 0))
    full = lambda a: pl.BlockSpec(a.shape, lambda b, i: (0,) * a.ndim)
    return pl.pallas_call(
        _final_kernel,
        grid=(B, T // tm),
        in_specs=[row_blk(D), full(norm_g), pl.BlockSpec(w_b.shape, lambda b, i: (0, 0), pipeline_mode=pl.Buffered(1)),
                  pl.BlockSpec((1, tm, NSA_WIDTH), lambda b, i: (b, jnp.minimum(i, nh - 1), 0)),
                  pl.BlockSpec((1, tm, NSA_WIDTH), lambda b, i: (b, jnp.maximum(i - nh, 0), 0)),
                  pl.BlockSpec((1, tm // S5_L, S5_L * S5_WIDTH), lambda b, i: (b, i, 0)),
                  full(wpn), full(wps), full(wo), full(final_g)],
        out_specs=row_blk(D),
        out_shape=jax.ShapeDtypeStruct((B, T, D), F32),
        scratch_shapes=[pltpu.VMEM((S5_WIDTH // LANES, tm, LANES), F32)],
        compiler_params=pltpu.CompilerParams(
            dimension_semantics=("parallel", "arbitrary"), vmem_limit_bytes=VMEM_LIMIT),
        name="final",
    )(x, norm_g, w_b, o_nsa_lo, o_nsa_hi, o_s5, wpn, wps, wo, final_g)


def _rope_tables(T):
    half = HEAD_DIM // 2
    inv_freq = np.float32(ROPE_THETA) ** (-np.arange(half, dtype=np.float32) / np.float32(half))
    ang = np.arange(T, dtype=np.float32)[:, None] * inv_freq[None, :].astype(np.float32)
    cos, sin = np.cos(ang).astype(np.float32), np.sin(ang).astype(np.float32)
    cos2 = np.concatenate([cos, cos, cos, cos], axis=1)
    sin2 = np.concatenate([-sin, sin, -sin, sin], axis=1)
    return jnp.asarray(cos2), jnp.asarray(sin2)


def _compress_w1(w1):
    half_rows = CMP_STRIDE * HEAD_DIM
    return jnp.concatenate([w1[:half_rows], w1[half_rows:]], axis=1).astype(BF16)


def kernel(x, norm_g, w_in, cmp_pos_k, cmp_pos_v, cmp_w1_k, cmp_w2_k, cmp_w1_v, cmp_w2_v, s5_lam_re, s5_lam_im, s5_log_dt, s5_b_re, s5_b_im, s5_c_re, s5_c_im, s5_d, w_glu, b_glu, w_proj_nsa, w_proj_s5, w_out, final_g):
    B, T, D = x.shape
    assert w_in.shape[0] == 1, "single-layer block"
    NCH = T // CMP_STRIDE
    NS = T // SEL_BLOCK

    w = w_in[0]
    w_all = jnp.concatenate([w[:, :_OFF_GL], jnp.pad(w[:, _OFF_GL:_OFF_GN], ((0, 0), (0, LANES - 24))),
                             w[:, _OFF_GN:]], axis=1).astype(BF16)
    g2 = norm_g[0][None, :]
    cos2, sin2 = _rope_tables(T)

    qq, kc, vc, ksa, vst, kw, vwt, glt, u4 = _inproj(x, g2, w_all, cos2, sin2)

    w2k = jnp.concatenate([jnp.zeros_like(cmp_w2_k[0]), cmp_w2_k[0]], axis=1).astype(BF16)
    w2vt = cmp_w2_v[0].T.astype(BF16)
    pos_rows = lambda p: jnp.pad(p.reshape(2, CMP_STRIDE * HEAD_DIM), ((0, 2 * SUBLANES - 2), (0, 0)))
    kcmp, vcmpt = _compress(kc, vc, _compress_w1(cmp_w1_k[0]), _compress_w1(cmp_w1_v[0]), w2k, w2vt,
                            pos_rows(cmp_pos_k[0]), pos_rows(cmp_pos_v[0]))

    c_start = jnp.arange(NCH) * CMP_STRIDE
    s_start = jnp.arange(NS) * SEL_BLOCK
    ovt = ((c_start[None, :] < s_start[:, None] + SEL_BLOCK) & (c_start[None, :] + CMP_BLOCK > s_start[:, None])
           & (jnp.arange(NCH)[None, :] < NCH - 1)).astype(BF16)
    o_nsa_lo, o_nsa_hi = _nsa(qq, kcmp, vcmpt, ksa, vst, kw, vwt, glt, ovt)

    rep = lambda a: jnp.repeat(a, S5_GROUP, axis=0)
    tr = lambda b: b.transpose(0, 2, 1).reshape(S5_GROUPS * S5_GROUP, S5_STATE)
    a_re, a_im, bb_re, bb_im = _s5_prep(
        rep(s5_lam_re[0]), rep(s5_lam_im[0]),
        rep(jnp.broadcast_to(s5_log_dt[0][:, None], (S5_GROUPS, S5_STATE))),
        tr(s5_b_re[0]), tr(s5_b_im[0]))
    flat = lambda c: c.reshape(S5_GROUPS * S5_GROUP, S5_STATE)
    we, wct, tp, alre, alim = _s5_weights(bb_re, bb_im, a_re, a_im, flat(s5_c_re[0]), flat(s5_c_im[0]))
    o_s5 = _s5(u4, we, wct, tp, alre.reshape(1, -1), alim.reshape(1, -1), s5_d[0][None, :],
               w_glu[0].astype(BF16), b_glu[0][None, :])

    return _final(x, g2, w_all, o_nsa_lo, o_nsa_hi, o_s5, w_proj_nsa[0].astype(BF16), w_proj_s5[0].astype(BF16),
                  w_out[0].astype(BF16), final_g[None, :])
```

```python
import math

import jax
import jax.numpy as jnp
import numpy as np
from jax import lax
from jax.experimental import pallas as pl
from jax.experimental.pallas import tpu as pltpu

F32 = jnp.float32
BF16 = jnp.bfloat16

D_MODEL = 1024
NSA_HEADS = 8
NSA_GROUPS = 2
HEADS_PER_GROUP = 4
HEAD_DIM = 64
NSA_WIDTH = 512
CMP_BLOCK = 32
CMP_STRIDE = 16
CMP_HIDDEN = 256
SEL_BLOCK = 64
SEL_TOPK = 16
WINDOW = 512
ROPE_THETA = 10000.0
FORCED_SCORE = 1.0e4
NEG = -1.0e30
S5_WIDTH = 512
S5_GROUP = 16
S5_GROUPS = 32
S5_STATE = 64
RMS_EPS = 1.0e-6

LANES = 128
SUBLANES = 8
VMEM_LIMIT = 56 * 1024 * 1024

_OFF_GL = 1280
_OFF_GN = 1304
_W_GL, _W_GN, _W_U, _W_GS, _W_MG, _W_END = 1280, 1408, 1920, 2432, 2944, 4992

TM_PROJ = 512
FINAL_SUB = 2
TQ = 128
TK = 512
NSA_NB = 2
V_ROWS = 80
GATE_ROWS = 32
S5_L = 8
S5_CPB = 256 // S5_L
S5_CT = 64
S5W_BLOCKS_PER_STEP = 4
S5_SCAN_SLABS = 4


def _gelu_tanh(x):
    c = math.sqrt(2.0 / math.pi)
    return 0.5 * x * (1.0 + jnp.tanh(c * (x + 0.044715 * (x * x * x))))


def _sigmoid(x):
    return 1.0 / (1.0 + jnp.exp(-x))


def _rms_scale(xv):
    ms = jnp.mean(xv * xv, axis=-1, keepdims=True)
    return xv * lax.rsqrt(ms + RMS_EPS)


def _nt_dot(a, b):
    return lax.dot_general(a, b, (((1,), (1,)), ((), ())), preferred_element_type=F32)


def _inproj_kernel(x_ref, g_ref, w_ref, cos_ref, sin_ref,
                   qq_ref, kc_ref, vc_ref, ks_ref, vs_ref, kw_ref, vw_ref, gl_ref, u_ref, us_scr):
    h = (_rms_scale(x_ref[0]) * g_ref[...]).astype(BF16)
    cos2 = cos_ref[...]
    sin2 = sin_ref[...]
    lane = lax.broadcasted_iota(jnp.int32, cos2.shape, 1)
    first_half = (lane & (HEAD_DIM - 1)) < (HEAD_DIM // 2)
    low = lane < HEAD_DIM

    wide = {}

    def proj(a, b):
        for (s0, s1) in ((0, 512), (512, _W_GN), (_W_U, _W_GS)):
            if s0 <= a and b <= s1:
                if s0 not in wide:
                    wide[s0] = jnp.dot(h, w_ref[:, s0:s1], preferred_element_type=F32)
                return wide[s0][:, a - s0:b - s0]
        raise ValueError((a, b))

    def rope(xs):
        partner = jnp.where(first_half, pltpu.roll(xs, 96, 1), pltpu.roll(xs, 32, 1))
        return xs * cos2 + partner * sin2

    scale = HEAD_DIM ** -0.5 * math.log2(math.e)
    for i in range(NSA_HEADS // 2):
        xs = proj(LANES * i, LANES * (i + 1)) * scale
        xr = rope(xs)
        qq_ref[0, 2 * i] = jnp.where(low, xr, pltpu.roll(xs, 64, 1)).astype(BF16)
        qq_ref[0, 2 * i + 1] = jnp.where(low, pltpu.roll(xr, 64, 1), xs).astype(BF16)

    kc_ref[0] = proj(512, 640)
    vc_ref[0] = proj(640, 768)
    tm = cos2.shape[0]
    t_row = pl.program_id(1) * tm + lax.broadcasted_iota(jnp.int32, cos2.shape, 0)
    blk_onehot = jnp.where(lane - HEAD_DIM == t_row // SEL_BLOCK, 1.0, 0.0)
    ones_rows = jnp.where(lax.broadcasted_iota(jnp.int32, (V_ROWS - HEAD_DIM, tm), 0) == 0, 1.0, 0.0)
    for (off, k_out, v_out, k_pad) in ((768, ks_ref, vs_ref, blk_onehot), (1024, kw_ref, vw_ref, 0.0)):
        kr = rope(proj(off, off + LANES))
        k_out[0, 0] = jnp.where(low, kr, k_pad).astype(BF16)
        k_out[0, 1] = jnp.where(low, pltpu.roll(kr, 64, 1), k_pad).astype(BF16)
        vt = proj(off + LANES, off + 2 * LANES).T
        for g in range(NSA_GROUPS):
            v_out[0, g] = jnp.concatenate([vt[HEAD_DIM * g:HEAD_DIM * (g + 1)], ones_rows], axis=0).astype(BF16)
    gl_ref[0] = _sigmoid(proj(_W_GL, _W_GN)).T[0:GATE_ROWS]
    uv = proj(_W_U, _W_GS)
    for s in range(S5_WIDTH // LANES):
        us_scr[s] = uv[:, LANES * s:LANES * (s + 1)]
    for i in range(S5_L):
        for s in range(S5_WIDTH // LANES):
            c0 = i * S5_WIDTH + LANES * s
            u_ref[0, :, c0:c0 + LANES] = us_scr[s, pl.ds(i, tm // S5_L, stride=S5_L), :]


def _inproj(x, norm_g, w_a, cos2, sin2):
    B, T, D = x.shape
    tm = TM_PROJ
    grid = (B, T // tm)
    row_blk = lambda w: pl.BlockSpec((1, tm, w), lambda b, i: (b, i, 0))
    kv_blk = pl.BlockSpec((1, NSA_GROUPS, tm, LANES), lambda b, i: (b, 0, i, 0))
    kv_shape = jax.ShapeDtypeStruct((B, NSA_GROUPS, T, LANES), BF16)
    vt_blk = pl.BlockSpec((1, NSA_GROUPS, V_ROWS, tm), lambda b, i: (b, 0, 0, i))
    vt_shape = jax.ShapeDtypeStruct((B, NSA_GROUPS, V_ROWS, T), BF16)
    return pl.pallas_call(
        _inproj_kernel,
        grid=grid,
        in_specs=[
            row_blk(D),
            pl.BlockSpec((1, D), lambda b, i: (0, 0)),
            pl.BlockSpec(w_a.shape, lambda b, i: (0, 0), pipeline_mode=pl.Buffered(1)),
            pl.BlockSpec((tm, LANES), lambda b, i: (i, 0)),
            pl.BlockSpec((tm, LANES), lambda b, i: (i, 0)),
        ],
        out_specs=[
            pl.BlockSpec((1, NSA_HEADS, tm, LANES), lambda b, i: (b, 0, i, 0)),
            row_blk(LANES), row_blk(LANES),
            kv_blk, vt_blk, kv_blk, vt_blk,
            pl.BlockSpec((1, GATE_ROWS, tm), lambda b, i: (b, 0, i)),
            pl.BlockSpec((1, tm // S5_L, S5_L * S5_WIDTH), lambda b, i: (b, i, 0)),
        ],
        out_shape=[
            jax.ShapeDtypeStruct((B, NSA_HEADS, T, LANES), BF16),
            jax.ShapeDtypeStruct((B, T, LANES), F32), jax.ShapeDtypeStruct((B, T, LANES), F32),
            kv_shape, vt_shape, kv_shape, vt_shape,
            jax.ShapeDtypeStruct((B, GATE_ROWS, T), F32),
            jax.ShapeDtypeStruct((B, T // S5_L, S5_L * S5_WIDTH), F32),
        ],
        scratch_shapes=[pltpu.VMEM((S5_WIDTH // LANES, tm, LANES), F32)],
        compiler_params=pltpu.CompilerParams(
            dimension_semantics=("parallel", "arbitrary"), vmem_limit_bytes=VMEM_LIMIT),
        name="inproj",
    )(x, norm_g, w_a, cos2, sin2)


def _compress_kernel(kc_ref, vc_ref, w1k_ref, w1v_ref, w2k_ref, w2vt_ref, pbk_ref, pbv_ref, ko_ref, vo_ref, pb_scr):
    nch = ko_ref.shape[2]
    H = CMP_HIDDEN

    @pl.when(pl.program_id(0) == 0)
    def _():
        for n, (p_ref, w1_ref) in enumerate(((pbk_ref, w1k_ref), (pbv_ref, w1v_ref))):
            pw = jnp.dot(p_ref[...].astype(BF16), w1_ref[...], preferred_element_type=F32)
            pb_scr[n] = pw[0:1, 0:H] + pw[1:2, H:]

    def hidden(c_ref, w1_ref, n):
        acc = [jnp.zeros((nch, 2 * H), F32) for _ in range(NSA_GROUPS)]
        for j in range(CMP_STRIDE):
            rows = c_ref[0, pl.ds(j, nch, stride=CMP_STRIDE), :].astype(BF16)
            wj = w1_ref[HEAD_DIM * j:HEAD_DIM * (j + 1), :]
            for g in range(NSA_GROUPS):
                acc[g] = acc[g] + jnp.dot(rows[:, HEAD_DIM * g:HEAD_DIM * (g + 1)], wj, preferred_element_type=F32)
        return [_gelu_tanh(a[:, 0:H] + pltpu.roll(a[:, H:], nch - 1, 0) + pb_scr[n]).astype(BF16) for a in acc]

    hk = hidden(kc_ref, w1k_ref, 0)
    hv = hidden(vc_ref, w1v_ref, 1)
    for g in range(NSA_GROUPS):
        ko_ref[0, g] = jnp.dot(hk[g], w2k_ref[...], preferred_element_type=F32).astype(BF16)
        vo_ref[0, g] = _nt_dot(w2vt_ref[...], hv[g]).astype(BF16)


def _compress(kc, vc, w1k, w1v, w2k, w2vt, pbk, pbv):
    B, T, _ = kc.shape
    G = NSA_GROUPS
    nch = T // CMP_STRIDE
    c_blk = pl.BlockSpec((1, T, LANES), lambda b: (b, 0, 0))
    full = lambda a: pl.BlockSpec(a.shape, lambda b: (0,) * a.ndim)
    return pl.pallas_call(
        _compress_kernel,
        grid=(B,),
        in_specs=[c_blk, c_blk, full(w1k), full(w1v), full(w2k), full(w2vt), full(pbk), full(pbv)],
        out_specs=[pl.BlockSpec((1, G, nch, LANES), lambda b: (b, 0, 0, 0)),
                   pl.BlockSpec((1, G, HEAD_DIM, nch), lambda b: (b, 0, 0, 0))],
        out_shape=[jax.ShapeDtypeStruct((B, G, nch, LANES), BF16),
                   jax.ShapeDtypeStruct((B, G, HEAD_DIM, nch), BF16)],
        scratch_shapes=[pltpu.VMEM((2, 1, CMP_HIDDEN), F32)],
        compiler_params=pltpu.CompilerParams(dimension_semantics=("arbitrary",), vmem_limit_bytes=VMEM_LIMIT),
        name="compress",
    )(kc, vc, w1k, w1v, w2k, w2vt, pbk, pbv)


class _QTile:
    def __init__(self, x, t0, t_begin, t_end, q_ref, g_ref, o_ref, cols, n_wc):
        self.x, self.t0, self.q_ref, self.g_ref, self.o_ref = x, t0, q_ref, g_ref, o_ref
        self.window_inside = t_begin >= WINDOW
        self.ncp = t_end // CMP_STRIDE
        self.ns = t_end // SEL_BLOCK
        self.t_lane = t0 + (lax.broadcasted_iota(jnp.int32, (1, cols), 1) & (TQ - 1))
        c_end = lax.broadcasted_iota(jnp.int32, (self.ncp, cols), 0) * CMP_STRIDE + (CMP_BLOCK - 1)
        self.cmp_valid = c_end <= self.t_lane
        self.w_pos = [t0 - WINDOW + TQ * c for c in range(n_wc)]
        self.w_start = [pl.multiple_of(jnp.maximum(p, 0), TQ) for p in self.w_pos]


def _nsa_kernel(qa_ref, qb_ref, kc_ref, vct_ref, ksa_ref, vst_ref, kw_ref, vwt_ref, ga_ref, gb_ref, ovt_ref,
                oa_ref, ob_ref, qsel_scr, acc_scr, m_scr):
    units = [(bb, g) for bb in range(qa_ref.shape[0]) for g in range(NSA_GROUPS)]
    uidx = {u: i for i, u in enumerate(units)}
    n_qt = kw_ref.shape[2] // TQ
    R = HEADS_PER_GROUP
    cols = R * TQ
    NS = ovt_ref.shape[0]
    n_wc = (WINDOW + TQ) // TQ
    step = pl.program_id(1)
    t_mid = n_qt // 2 * TQ
    tiles = [_QTile(0, step * TQ, 0, t_mid, qa_ref, ga_ref, oa_ref, cols, n_wc),
             _QTile(1, (n_qt - 1 - step) * TQ, t_mid, n_qt * TQ, qb_ref, gb_ref, ob_ref, cols, n_wc)]
    sub8 = lax.broadcasted_iota(jnp.int32, (SUBLANES, TQ), 0)
    row_tq = lax.broadcasted_iota(jnp.int32, (TQ, cols), 0)

    def group_q(c, u):
        bb, g = u
        return c.q_ref[bb, R * g:R * (g + 1)].reshape(cols, LANES)

    def cmp_scores(c, u):
        return _nt_dot(kc_ref[u[0], u[1], 0:c.ncp, :], group_q(c, u))

    def cmp_probs(c, s):
        s = jnp.where(c.cmp_valid, s, NEG)
        e = jnp.exp2(s - jnp.max(s, axis=0, keepdims=True))
        inv = 1.0 / jnp.maximum(jnp.sum(e, axis=0, keepdims=True), 1.0e-30)
        return e * jnp.where(c.t_lane >= CMP_BLOCK - 1, inv, 0.0)

    def win_scores(c, u):
        kw = jnp.concatenate([kw_ref[u[0], u[1], pl.ds(c.w_start[n], TQ), :] for n in range(n_wc)], axis=0)
        return _nt_dot(kw, group_q(c, u))

    def win_probs(c, sw):
        parts = []
        for n in range(n_wc):
            sc = sw[TQ * n:TQ * (n + 1)]
            if n == 0:
                sc = jnp.where(c.w_pos[0] + row_tq > c.t_lane - WINDOW, sc, NEG)
            if n == n_wc - 1:
                sc = jnp.where(c.t0 + row_tq <= c.t_lane, sc, NEG)
            elif not c.window_inside:
                sc = jnp.where(c.w_pos[n] >= 0, sc, NEG)
            parts.append(sc.astype(BF16))
        sw = jnp.concatenate(parts, axis=0)
        return jnp.exp2(sw - jnp.max(sw, axis=0, keepdims=True))

    def win_out(c, u, ew):
        vw = jnp.concatenate([vwt_ref[u[0], u[1], :, pl.ds(c.w_start[n], TQ)] for n in range(n_wc)], axis=1)
        ow = jnp.dot(vw, ew, preferred_element_type=F32)
        return ow[0:HEAD_DIM] * (1.0 / ow[HEAD_DIM:HEAD_DIM + 1])

    def select_blocks(c, u, p):
        ns = c.ns
        psum = p[:, 0:TQ] + p[:, TQ:2 * TQ] + p[:, 2 * TQ:3 * TQ] + p[:, 3 * TQ:4 * TQ]
        p_hi = psum.astype(BF16)
        p_lo = (psum - p_hi.astype(F32)).astype(BF16)
        ov = ovt_ref[0:ns, 0:c.ncp]
        imp = (jnp.dot(ov, p_hi, preferred_element_type=F32) + jnp.dot(ov, p_lo, preferred_element_type=F32))
        blk = lax.broadcasted_iota(jnp.int32, (ns, TQ), 0)
        t_l = c.t0 + lax.broadcasted_iota(jnp.int32, (ns, TQ), 1)
        cur = t_l // SEL_BLOCK
        imp = jnp.where(blk * SEL_BLOCK <= t_l, imp, -1.0)
        imp = jnp.where(blk == 0, FORCED_SCORE, imp)
        imp = jnp.where(blk == cur, FORCED_SCORE, imp)
        imp = jnp.where(blk == cur - 1, FORCED_SCORE, imp)
        nv = ns // SUBLANES
        imp8 = [imp[SUBLANES * j:SUBLANES * (j + 1)] for j in range(nv)]
        rank8 = [jnp.zeros((SUBLANES, TQ), F32) for _ in range(nv)]
        for mm in range(ns):
            row = imp[mm:mm + 1, :]
            jm = mm // SUBLANES
            for j in range(nv):
                if j < jm:
                    ahead = jnp.where(row > imp8[j], 1.0, 0.0)
                elif j > jm:
                    ahead = jnp.where(row >= imp8[j], 1.0, 0.0)
                else:
                    tie = jnp.where(sub8 > (mm % SUBLANES), 1.0, 0.0)
                    ahead = jnp.where(row > imp8[j], 1.0, 0.0) + jnp.where(row == imp8[j], tie, 0.0)
                rank8[j] = rank8[j] + ahead
        pen = jnp.where(jnp.concatenate(rank8, axis=0) < float(SEL_TOPK), 0.0, NEG)
        if ns < NS:
            pen = jnp.concatenate([pen, jnp.zeros((NS - ns, TQ), F32)], axis=0)
        q_t = group_q(c, u).astype(F32).T.astype(BF16)
        qsel_scr[c.x, uidx[u]] = jnp.concatenate(
            [q_t[0:HEAD_DIM], jnp.concatenate([pen.astype(BF16)] * R, axis=1)], axis=0)

    p_c, o_cmp, o_win = {}, {}, {}

    def cmp_job(c, u):
        def finish(p, _):
            p_c[c.x, u] = p
            o_cmp[c.x, u] = jnp.dot(vct_ref[u[0], u[1], :, 0:c.ncp], p.astype(BF16),
                                    preferred_element_type=F32)
        return (lambda: cmp_scores(c, u)), (lambda s: (cmp_probs(c, s), None)), finish

    def win_job(c, u):
        def finish(e_w, _):
            o_win[c.x, u] = win_out(c, u, e_w)
        return (lambda: win_scores(c, u)), (lambda s: (win_probs(c, s), None)), finish

    own_keys_visible = row_tq <= (lax.broadcasted_iota(jnp.int32, (1, cols), 1) & (TQ - 1))

    def sel_job(x, k0, nk, u, own):
        k0 = k0 if isinstance(k0, int) else pl.multiple_of(k0, TQ)

        def probs(sc):
            if own:
                sc = jnp.where(own_keys_visible, sc, NEG)
            sc = sc.astype(BF16)
            m_old = m_scr[x, uidx[u]]
            m_new = jnp.maximum(m_old, jnp.max(sc, axis=0, keepdims=True).astype(F32))
            m_scr[x, uidx[u]] = m_new
            return jnp.exp2(sc - m_new.astype(BF16)), jnp.exp2(m_old - m_new)

        def finish(pp, alpha):
            acc_scr[x, uidx[u]] = acc_scr[x, uidx[u]] * alpha + jnp.dot(
                vst_ref[u[0], u[1], :, pl.ds(k0, nk)], pp, preferred_element_type=F32)

        return (lambda: jnp.dot(ksa_ref[u[0], u[1], pl.ds(k0, nk), :], qsel_scr[x, uidx[u]],
                                preferred_element_type=F32)), probs, finish

    def fuse(js):
        return ((lambda: [j[0]() for j in js]),
                (lambda ss: ([j[1](s) for j, s in zip(js, ss)], None)),
                (lambda outs, _: [j[2](*o) for j, o in zip(js, outs)]))

    def sel_keys(x, k0, nk, own=False):
        return [fuse([sel_job(x, k0, nk, (bb, g), own) for g in range(NSA_GROUPS)]) for bb in range(qa_ref.shape[0])]

    early, late = tiles
    n_slots = (n_qt - 1) * TQ // TK
    n_static = n_slots - n_slots // 2
    n_late = late.t0 // TK
    c_late = late.t0 % TK // TQ
    plan = [cmp_job(c, u) for c in (late, early) for u in units]
    for u in units:
        plan += [lambda u=u: select_blocks(late, u, p_c[late.x, u]), win_job(late, u)]
    plan += sel_keys(late.x, late.t0, TQ, own=True)
    for s in range(n_static):
        plan += sel_keys(late.x, s * TK, TK)
        for u in units[s::n_static]:
            plan += [lambda u=u: select_blocks(early, u, p_c[early.x, u]), win_job(early, u)]
    plan += sel_keys(early.x, early.t0, TQ, own=True)
    for s in range(n_static, n_slots):
        is_late = s < n_late
        plan += sel_keys(jnp.where(is_late, late.x, early.x), jnp.where(is_late, s, s - n_late) * TK, TK)
    for s in range(TK // TQ - 1):
        is_late = s < c_late
        base = jnp.where(is_late, n_late, early.t0 // TK) * TK
        plan += sel_keys(jnp.where(is_late, late.x, early.x), base + jnp.where(is_late, s, s - c_late) * TQ, TQ)

    acc_scr[...] = jnp.zeros_like(acc_scr)
    m_scr[...] = jnp.full(m_scr.shape, NEG, F32)
    job_pos = [k for k, e in enumerate(plan) if isinstance(e, tuple)]
    following = dict(zip(job_pos, job_pos[1:]))
    issued = {job_pos[0]: plan[job_pos[0]][0]()}
    for k, entry in enumerate(plan):
        if not isinstance(entry, tuple):
            entry()
            continue
        if k in following:
            issued[following[k]] = plan[following[k]][0]()
        _, probs, finish = entry
        finish(*probs(issued.pop(k)))

    for c in tiles:
        for bb in range(qa_ref.shape[0]):
            glt = c.g_ref[bb]
            heads = []
            for g in range(NSA_GROUPS):
                acc = acc_scr[c.x, uidx[bb, g]]
                o_sel = acc[0:HEAD_DIM] * (1.0 / acc[HEAD_DIM:HEAD_DIM + 1])
                for r in range(R):
                    hh = R * g + r
                    sl = slice(r * TQ, (r + 1) * TQ)
                    heads.append(glt[3 * hh:3 * hh + 1] * o_cmp[c.x, (bb, g)][:, sl]
                                 + glt[3 * hh + 1:3 * hh + 2] * o_sel[:, sl]
                                 + glt[3 * hh + 2:3 * hh + 3] * o_win[c.x, (bb, g)][:, sl])
            c.o_ref[bb] = jnp.concatenate(heads, axis=0).T


def _nsa(qq, kcmp, vcmpt, ksa, vst, kw, vwt, glt, ovt):
    B, H, T, _ = qq.shape
    G = NSA_GROUPS
    NB = NSA_NB
    NCP = kcmp.shape[2]
    n_qt = T // TQ
    grid = (B // NB, n_qt // 2)
    k_blk = lambda n: pl.BlockSpec((NB, G, n, LANES), lambda b, i: (b, 0, 0, 0))
    vt_blk = lambda r, n: pl.BlockSpec((NB, G, r, n), lambda b, i: (b, 0, 0, 0))
    lo_tile = lambda b, i: i
    hi_tile = lambda b, i: n_qt - 1 - i
    q_blk = lambda tile: pl.BlockSpec((NB, H, TQ, LANES), lambda b, i: (b, 0, tile(b, i), 0))
    g_blk = lambda tile: pl.BlockSpec((NB, GATE_ROWS, TQ), lambda b, i: (b, 0, tile(b, i)))
    half = jax.ShapeDtypeStruct((B, T // 2, NSA_WIDTH), F32)
    return pl.pallas_call(
        _nsa_kernel,
        grid=grid,
        in_specs=[
            q_blk(lo_tile), q_blk(hi_tile),
            k_blk(NCP), vt_blk(HEAD_DIM, NCP), k_blk(T), vt_blk(V_ROWS, T), k_blk(T), vt_blk(V_ROWS, T),
            g_blk(lo_tile), g_blk(hi_tile),
            pl.BlockSpec(ovt.shape, lambda b, i: (0, 0)),
        ],
        out_specs=[pl.BlockSpec((NB, TQ, NSA_WIDTH), lambda b, i: (b, i, 0)),
                   pl.BlockSpec((NB, TQ, NSA_WIDTH), lambda b, i: (b, n_qt // 2 - 1 - i, 0))],
        out_shape=[half, half],
        scratch_shapes=[pltpu.VMEM((2, NB * G, LANES, HEADS_PER_GROUP * TQ), BF16),
                        pltpu.VMEM((2, NB * G, V_ROWS, HEADS_PER_GROUP * TQ), F32),
                        pltpu.VMEM((2, NB * G, 1, HEADS_PER_GROUP * TQ), F32)],
        compiler_params=pltpu.CompilerParams(
            dimension_semantics=("parallel", "arbitrary"), vmem_limit_bytes=VMEM_LIMIT),
        name="nsa",
    )(qq, qq, kcmp, vcmpt, ksa, vst, kw, vwt, glt, glt, ovt)


def _s5_prep_kernel(lre_ref, lim_ref, ldt_ref, bre_ref, bim_ref, are_ref, aim_ref, bbre_ref, bbim_ref):
    lre, lim = lre_ref[...], lim_ref[...]
    dt = jnp.exp(ldt_ref[...])
    mag = jnp.exp(lre * dt)
    a_re = mag * jnp.cos(lim * dt)
    a_im = mag * jnp.sin(lim * dt)
    den = lre * lre + lim * lim
    z_re = ((a_re - 1.0) * lre + a_im * lim) / den
    z_im = (a_im * lre - (a_re - 1.0) * lim) / den
    are_ref[...] = a_re
    aim_ref[...] = a_im
    bbre_ref[...] = z_re * bre_ref[...] - z_im * bim_ref[...]
    bbim_ref[...] = z_re * bim_ref[...] + z_im * bre_ref[...]


def _s5_prep(lre, lim, ldt, bre, bim):
    shp = jax.ShapeDtypeStruct(lre.shape, F32)
    return pl.pallas_call(_s5_prep_kernel, out_shape=[shp, shp, shp, shp], name="s5_prep")(lre, lim, ldt, bre, bim)


def _s5_weights_kernel(bbre_ref, bbim_ref, are_ref, aim_ref, crt_ref, cit_ref,
                       we_ref, wct_ref, tp_ref, alre_ref, alim_ref, bd_scr, arow_scr, tp_scr):
    gpb = S5_CPB // S5_GROUP
    for q in range(we_ref.shape[0]):
        r0 = S5_CPB * q
        bd_scr[...] = jnp.zeros_like(bd_scr)
        tp_scr[...] = jnp.zeros_like(tp_scr)
        for n, ref in enumerate((bbre_ref, bbim_ref, crt_ref, cit_ref)):
            for gl in range(gpb):
                bd_scr[n, S5_GROUP * gl:S5_GROUP * (gl + 1), S5_STATE * gl:S5_STATE * (gl + 1)] = (
                    ref[r0 + S5_GROUP * gl:r0 + S5_GROUP * (gl + 1), :])
        for n, ref in enumerate((are_ref, aim_ref)):
            for gl in range(gpb):
                arow_scr[n, :, S5_STATE * gl:S5_STATE * (gl + 1)] = ref[r0 + S5_GROUP * gl:r0 + S5_GROUP * gl + 1, :]
        bbre, bbim, crt, cit = bd_scr[0], bd_scr[1], bd_scr[2], bd_scr[3]
        are, aim = arow_scr[0], arow_scr[1]
        pre, pim = jnp.ones_like(are), jnp.zeros_like(are)
        for k in range(S5_L):
            bpr = bbre * pre - bbim * pim
            bpi = bbre * pim + bbim * pre
            i = S5_L - 1 - k
            we_ref[q, S5_CPB * i:S5_CPB * (i + 1), :] = jnp.concatenate([bpr, bpi], axis=1).astype(BF16)
            tap = _nt_dot(bpr.astype(BF16), crt.astype(BF16)) - _nt_dot(bpi.astype(BF16), cit.astype(BF16))
            for i in range(S5_L - k):
                j = i + k
                tp_scr[S5_CPB * i:S5_CPB * (i + 1), S5_CPB * j:S5_CPB * (j + 1)] = tap
            pre, pim = pre * are - pim * aim, pre * aim + pim * are
            wct_ref[q, S5_CPB * k:S5_CPB * (k + 1), :] = jnp.concatenate(
                [crt * pre - cit * pim, -(crt * pim + cit * pre)], axis=1).astype(BF16)
        tp_ref[q] = tp_scr[...].astype(BF16)
        alre_ref[q] = pre
        alim_ref[q] = pim


def _s5_weights(bbre, bbim, are, aim, crt, cit):
    nblk = S5_WIDTH // S5_CPB
    sb = S5_CPB // S5_GROUP * S5_STATE
    lc = S5_L * S5_CPB
    bps = S5W_BLOCKS_PER_STEP
    rows = pl.BlockSpec((bps * S5_CPB, S5_STATE), lambda q: (q, 0))
    blk = lambda r, c: pl.BlockSpec((bps, r, c), lambda q: (q, 0, 0))
    return pl.pallas_call(
        _s5_weights_kernel,
        grid=(nblk // bps,),
        in_specs=[rows] * 6,
        out_specs=[blk(lc, 2 * sb), blk(lc, 2 * sb), blk(lc, lc), blk(1, sb), blk(1, sb)],
        scratch_shapes=[pltpu.VMEM((4, S5_CPB, sb), F32), pltpu.VMEM((2, 1, sb), F32), pltpu.VMEM((lc, lc), F32)],
        out_shape=[jax.ShapeDtypeStruct((nblk, lc, 2 * sb), BF16),
                   jax.ShapeDtypeStruct((nblk, lc, 2 * sb), BF16),
                   jax.ShapeDtypeStruct((nblk, lc, lc), BF16),
                   jax.ShapeDtypeStruct((nblk, 1, sb), F32), jax.ShapeDtypeStruct((nblk, 1, sb), F32)],
        compiler_params=pltpu.CompilerParams(dimension_semantics=("parallel",), vmem_limit_bytes=VMEM_LIMIT),
        name="s5_weights",
    )(bbre, bbim, are, aim, crt, cit)


def _s5_kernel(u_ref, we_ref, wct_ref, tp_ref, alre_ref, alim_ref, d_ref, wg_ref, bg_ref, o_ref, e_scr, st_scr):
    nb, ct, _ = u_ref.shape
    L = S5_L
    nblk = S5_WIDTH // S5_CPB
    sb = S5_CPB // S5_GROUP * S5_STATE
    spb = 2 * sb // LANES
    W = S5_SCAN_SLABS * LANES

    @pl.when(pl.program_id(0) == 0)
    def _():
        st_scr[...] = jnp.zeros_like(st_scr)

    uf = u_ref[...].reshape(nb * ct, L * S5_WIDTH)
    ub = uf.astype(BF16)
    u_blk = [jnp.concatenate([ub[:, i * S5_WIDTH + S5_CPB * q:i * S5_WIDTH + S5_CPB * (q + 1)] for i in range(L)],
                             axis=1) for q in range(nblk)]

    for q in range(nblk):
        e = jnp.dot(u_blk[q], we_ref[q], preferred_element_type=F32)
        for s in range(spb):
            for b in range(nb):
                e_scr[spb * q + s, pl.ds(b, ct, stride=nb), :] = e[b * ct:(b + 1) * ct, LANES * s:LANES * (s + 1)]

    def slabs_of(cs):
        q, r = divmod(cs * LANES, sb)
        return spb * q + r // LANES, spb * q + (sb + r) // LANES

    for j in range(S5_GROUPS * S5_STATE // W):
        sl = [slabs_of(S5_SCAN_SLABS * j + n) for n in range(S5_SCAN_SLABS)]
        sl_re, sl_im = [s[0] for s in sl], [s[1] for s in sl]
        ar = alre_ref[:, W * j:W * (j + 1)]
        ai = alim_ref[:, W * j:W * (j + 1)]
        load = lambda r0, rows, slabs: jnp.concatenate([e_scr[s, pl.ds(r0, rows), :] for s in slabs], axis=1)
        state = lambda slabs: jnp.concatenate([st_scr[:, LANES * s:LANES * (s + 1)] for s in slabs], axis=1)
        sr, si = state(sl_re), state(sl_im)
        for k in range(ct // 2):
            r0 = k * 2 * nb
            er, ei = load(r0, 2 * nb, sl_re), load(r0, 2 * nb, sl_im)
            tr = ar * sr - ai * si + er[0:nb]
            ti = ar * si + ai * sr + ei[0:nb]
            xr = jnp.concatenate([sr, tr], axis=0)
            xi = jnp.concatenate([si, ti], axis=0)
            for n in range(S5_SCAN_SLABS):
                e_scr[sl_re[n], pl.ds(r0, 2 * nb), :] = xr[:, LANES * n:LANES * (n + 1)]
                e_scr[sl_im[n], pl.ds(r0, 2 * nb), :] = xi[:, LANES * n:LANES * (n + 1)]
            sr, si = ar * tr - ai * ti + er[nb:], ar * ti + ai * tr + ei[nb:]
        for n in range(S5_SCAN_SLABS):
            st_scr[:, LANES * sl_re[n]:LANES * (sl_re[n] + 1)] = sr[:, LANES * n:LANES * (n + 1)]
            st_scr[:, LANES * sl_im[n]:LANES * (sl_im[n] + 1)] = si[:, LANES * n:LANES * (n + 1)]

    ys = []
    for q in range(nblk):
        per_b = [jnp.concatenate([e_scr[spb * q + s, pl.ds(b, ct, stride=nb), :] for s in range(spb)], axis=1)
                 for b in range(nb)]
        x_in = jnp.concatenate(per_b, axis=0).astype(BF16)
        ys.append((_nt_dot(x_in, wct_ref[q]) + jnp.dot(u_blk[q], tp_ref[q], preferred_element_type=F32)).astype(BF16))
    outs = []
    for j in range(L):
        yj = jnp.concatenate([y[:, S5_CPB * j:S5_CPB * (j + 1)] for y in ys], axis=1).astype(F32)
        yj = yj + d_ref[...] * uf[:, j * S5_WIDTH:(j + 1) * S5_WIDTH]
        z = _gelu_tanh(yj)
        gate = jnp.dot(z.astype(BF16), wg_ref[...], preferred_element_type=F32) + bg_ref[...]
        outs.append(z * _sigmoid(gate))
    o_ref[...] = jnp.concatenate(outs, axis=1).reshape(nb, ct, L * S5_WIDTH)


def _s5(u4, we, wct, tp, alre, alim, d, w_glu, b_glu):
    nb, nchunks, w4 = u4.shape
    ct = S5_CT
    n_slabs = 2 * S5_GROUPS * S5_STATE // LANES
    resident = lambda a: pl.BlockSpec(a.shape, lambda i: (0,) * a.ndim, pipeline_mode=pl.Buffered(1))
    u_blk = pl.BlockSpec((nb, ct, w4), lambda i: (0, i, 0))
    return pl.pallas_call(
        _s5_kernel,
        grid=(nchunks // ct,),
        in_specs=[u_blk, resident(we), resident(wct), resident(tp), resident(alre), resident(alim),
                  resident(d), resident(w_glu), resident(b_glu)],
        out_specs=u_blk,
        out_shape=jax.ShapeDtypeStruct(u4.shape, F32),
        scratch_shapes=[pltpu.VMEM((n_slabs, nb * ct, LANES), F32), pltpu.VMEM((nb, n_slabs * LANES), F32)],
        compiler_params=pltpu.CompilerParams(dimension_semantics=("arbitrary",), vmem_limit_bytes=VMEM_LIMIT),
        name="s5",
    )(u4, we, wct, tp, alre, alim, d, w_glu, b_glu)


def _final_kernel(x_ref, g_ref, wb_ref, onl_ref, onh_ref, os_ref, wpn_ref, wps_ref, wo_ref, fg_ref, o_ref, os_scr):
    tm = x_ref.shape[1]
    o4 = os_ref[0]
    for i in range(S5_L):
        for s in range(S5_WIDTH // LANES):
            c0 = i * S5_WIDTH + LANES * s
            os_scr[s, pl.ds(i, tm // S5_L, stride=S5_L), :] = o4[:, c0:c0 + LANES]
    first_half = pl.program_id(1) < pl.num_programs(1) // 2

    def silu(v):
        return v * _sigmoid(v)

    rows = [slice(r, r + tm // FINAL_SUB) for r in range(0, tm, tm // FINAL_SUB)]
    hs = [(_rms_scale(x_ref[0, r]) * g_ref[...]).astype(BF16) for r in rows]
    for r, h in zip(rows, hs):
        proj = lambda a, b, h=h: jnp.dot(h, wb_ref[:, a:b], preferred_element_type=F32)
        o_nsa = jnp.where(first_half, onl_ref[0, r], onh_ref[0, r])
        o_s5 = jnp.concatenate([os_scr[s, r] for s in range(S5_WIDTH // LANES)], axis=1)
        a_in = (o_nsa * silu(proj(_W_GN, _W_U))).astype(BF16)
        b_in = (o_s5 * silu(proj(_W_GS, _W_MG))).astype(BF16)
        branch_a = jnp.dot(a_in, wpn_ref[...], preferred_element_type=F32)
        branch_b = jnp.dot(b_in, wps_ref[...], preferred_element_type=F32)
        merged = (_sigmoid(proj(_W_MG, _W_MG + D_MODEL)) * branch_a
                  + _sigmoid(proj(_W_MG + D_MODEL, _W_END)) * branch_b)
        y = x_ref[0, r] + jnp.dot(merged.astype(BF16), wo_ref[...], preferred_element_type=F32)
        o_ref[0, r] = _rms_scale(y) * fg_ref[...]


def _final(x, norm_g, w_b, o_nsa_lo, o_nsa_hi, o_s5, wpn, wps, wo, final_g):
    B, T, D = x.shape
    tm = TM_PROJ
    nh = T // tm // 2
    row_blk = lambda w: pl.BlockSpec((1, tm, w), lambda b, i: (b, i, 0))
    full = lambda a: pl.BlockSpec(a.shape, lambda b, i: (0,) * a.ndim)
    return pl.pallas_call(
        _final_kernel,
        grid=(B, T // tm),
        in_specs=[row_blk(D), full(norm_g), pl.BlockSpec(w_b.shape, lambda b, i: (0, 0), pipeline_mode=pl.Buffered(1)),
                  pl.BlockSpec((1, tm, NSA_WIDTH), lambda b, i: (b, jnp.minimum(i, nh - 1), 0)),
                  pl.BlockSpec((1, tm, NSA_WIDTH), lambda b, i: (b, jnp.maximum(i - nh, 0), 0)),
                  pl.BlockSpec((1, tm // S5_L, S5_L * S5_WIDTH), lambda b, i: (b, i, 0)),
                  full(wpn), full(wps), full(wo), full(final_g)],
        out_specs=row_blk(D),
        out_shape=jax.ShapeDtypeStruct((B, T, D), F32),
        scratch_shapes=[pltpu.VMEM((S5_WIDTH // LANES, tm, LANES), F32)],
        compiler_params=pltpu.CompilerParams(
            dimension_semantics=("parallel", "arbitrary"), vmem_limit_bytes=VMEM_LIMIT),
        name="final",
    )(x, norm_g, w_b, o_nsa_lo, o_nsa_hi, o_s5, wpn, wps, wo, final_g)


def _rope_tables(T):
    half = HEAD_DIM // 2
    inv_freq = np.float32(ROPE_THETA) ** (-np.arange(half, dtype=np.float32) / np.float32(half))
    ang = np.arange(T, dtype=np.float32)[:, None] * inv_freq[None, :].astype(np.float32)
    cos, sin = np.cos(ang).astype(np.float32), np.sin(ang).astype(np.float32)
    cos2 = np.concatenate([cos, cos, cos, cos], axis=1)
    sin2 = np.concatenate([-sin, sin, -sin, sin], axis=1)
    return jnp.asarray(cos2), jnp.asarray(sin2)


def _compress_w1(w1):
    half_rows = CMP_STRIDE * HEAD_DIM
    return jnp.concatenate([w1[:half_rows], w1[half_rows:]], axis=1).astype(BF16)


def kernel(x, norm_g, w_in, cmp_pos_k, cmp_pos_v, cmp_w1_k, cmp_w2_k, cmp_w1_v, cmp_w2_v, s5_lam_re, s5_lam_im, s5_log_dt, s5_b_re, s5_b_im, s5_c_re, s5_c_im, s5_d, w_glu, b_glu, w_proj_nsa, w_proj_s5, w_out, final_g):
    B, T, D = x.shape
    assert w_in.shape[0] == 1, "single-layer block"
    NCH = T // CMP_STRIDE
    NS = T // SEL_BLOCK

    w = w_in[0]
    w_all = jnp.concatenate([w[:, :_OFF_GL], jnp.pad(w[:, _OFF_GL:_OFF_GN], ((0, 0), (0, LANES - 24))),
                             w[:, _OFF_GN:]], axis=1).astype(BF16)
    g2 = norm_g[0][None, :]
    cos2, sin2 = _rope_tables(T)

    qq, kc, vc, ksa, vst, kw, vwt, glt, u4 = _inproj(x, g2, w_all, cos2, sin2)

    w2k = jnp.concatenate([jnp.zeros_like(cmp_w2_k[0]), cmp_w2_k[0]], axis=1).astype(BF16)
    w2vt = cmp_w2_v[0].T.astype(BF16)
    pos_rows = lambda p: jnp.pad(p.reshape(2, CMP_STRIDE * HEAD_DIM), ((0, 2 * SUBLANES - 2), (0, 0)))
    kcmp, vcmpt = _compress(kc, vc, _compress_w1(cmp_w1_k[0]), _compress_w1(cmp_w1_v[0]), w2k, w2vt,
                            pos_rows(cmp_pos_k[0]), pos_rows(cmp_pos_v[0]))

    c_start = jnp.arange(NCH) * CMP_STRIDE
    s_start = jnp.arange(NS) * SEL_BLOCK
    ovt = ((c_start[None, :] < s_start[:, None] + SEL_BLOCK) & (c_start[None, :] + CMP_BLOCK > s_start[:, None])
           & (jnp.arange(NCH)[None, :] < NCH - 1)).astype(BF16)
    o_nsa_lo, o_nsa_hi = _nsa(qq, kcmp, vcmpt, ksa, vst, kw, vwt, glt, ovt)

    rep = lambda a: jnp.repeat(a, S5_GROUP, axis=0)
    tr = lambda b: b.transpose(0, 2, 1).reshape(S5_GROUPS * S5_GROUP, S5_STATE)
    a_re, a_im, bb_re, bb_im = _s5_prep(
        rep(s5_lam_re[0]), rep(s5_lam_im[0]),
        rep(jnp.broadcast_to(s5_log_dt[0][:, None], (S5_GROUPS, S5_STATE))),
        tr(s5_b_re[0]), tr(s5_b_im[0]))
    flat = lambda c: c.reshape(S5_GROUPS * S5_GROUP, S5_STATE)
    we, wct, tp, alre, alim = _s5_weights(bb_re, bb_im, a_re, a_im, flat(s5_c_re[0]), flat(s5_c_im[0]))
    o_s5 = _s5(u4, we, wct, tp, alre.reshape(1, -1), alim.reshape(1, -1), s5_d[0][None, :],
               w_glu[0].astype(BF16), b_glu[0][None, :])

    return _final(x, g2, w_all, o_nsa_lo, o_nsa_hi, o_s5, w_proj_nsa[0].astype(BF16), w_proj_s5[0].astype(BF16),
                  w_out[0].astype(BF16), final_g[None, :])
```

```python
import math

import jax
import jax.numpy as jnp
import numpy as np
from jax import lax
from jax.experimental import pallas as pl
from jax.experimental.pallas import tpu as pltpu

F32 = jnp.float32
BF16 = jnp.bfloat16

D_MODEL = 1024
NSA_HEADS = 8
NSA_GROUPS = 2
HEADS_PER_GROUP = 4
HEAD_DIM = 64
NSA_WIDTH = 512
CMP_BLOCK = 32
CMP_STRIDE = 16
CMP_HIDDEN = 256
SEL_BLOCK = 64
SEL_TOPK = 16
WINDOW = 512
ROPE_THETA = 10000.0
FORCED_SCORE = 1.0e4
NEG = -1.0e30
S5_WIDTH = 512
S5_GROUP = 16
S5_GROUPS = 32
S5_STATE = 64
RMS_EPS = 1.0e-6

LANES = 128
SUBLANES = 8
VMEM_LIMIT = 56 * 1024 * 1024

_OFF_GL = 1280
_OFF_GN = 1304
_W_GL, _W_GN, _W_U, _W_GS, _W_MG, _W_END = 1280, 1408, 1920, 2432, 2944, 4992

TM_PROJ = 512
FINAL_SUB = 2
TQ = 128
TK = 512
NSA_NB = 2
V_ROWS = 80
GATE_ROWS = 32
S5_L = 8
S5_CPB = 256 // S5_L
S5_CT = 64
S5W_BLOCKS_PER_STEP = 4
S5_SCAN_SLABS = 4


def _gelu_tanh(x):
    c = math.sqrt(2.0 / math.pi)
    return 0.5 * x * (1.0 + jnp.tanh(c * (x + 0.044715 * (x * x * x))))


def _sigmoid(x):
    return 1.0 / (1.0 + jnp.exp(-x))


def _rms_scale(xv):
    ms = jnp.mean(xv * xv, axis=-1, keepdims=True)
    return xv * lax.rsqrt(ms + RMS_EPS)


def _nt_dot(a, b):
    return lax.dot_general(a, b, (((1,), (1,)), ((), ())), preferred_element_type=F32)


def _inproj_kernel(x_ref, g_ref, w_ref, cos_ref, sin_ref,
                   qq_ref, kc_ref, vc_ref, ks_ref, vs_ref, kw_ref, vw_ref, gl_ref, u_ref, us_scr):
    h = (_rms_scale(x_ref[0]) * g_ref[...]).astype(BF16)
    cos2 = cos_ref[...]
    sin2 = sin_ref[...]
    lane = lax.broadcasted_iota(jnp.int32, cos2.shape, 1)
    first_half = (lane & (HEAD_DIM - 1)) < (HEAD_DIM // 2)
    low = lane < HEAD_DIM

    wide = {}

    def proj(a, b):
        for (s0, s1) in ((0, 512), (512, _W_GN), (_W_U, _W_GS)):
            if s0 <= a and b <= s1:
                if s0 not in wide:
                    wide[s0] = jnp.dot(h, w_ref[:, s0:s1], preferred_element_type=F32)
                return wide[s0][:, a - s0:b - s0]
        raise ValueError((a, b))

    def rope(xs):
        partner = jnp.where(first_half, pltpu.roll(xs, 96, 1), pltpu.roll(xs, 32, 1))
        return xs * cos2 + partner * sin2

    scale = HEAD_DIM ** -0.5 * math.log2(math.e)
    for i in range(NSA_HEADS // 2):
        xs = proj(LANES * i, LANES * (i + 1)) * scale
        xr = rope(xs)
        qq_ref[0, 2 * i] = jnp.where(low, xr, pltpu.roll(xs, 64, 1)).astype(BF16)
        qq_ref[0, 2 * i + 1] = jnp.where(low, pltpu.roll(xr, 64, 1), xs).astype(BF16)

    kc_ref[0] = proj(512, 640)
    vc_ref[0] = proj(640, 768)
    tm = cos2.shape[0]
    t_row = pl.program_id(1) * tm + lax.broadcasted_iota(jnp.int32, cos2.shape, 0)
    blk_onehot = jnp.where(lane - HEAD_DIM == t_row // SEL_BLOCK, 1.0, 0.0)
    ones_rows = jnp.where(lax.broadcasted_iota(jnp.int32, (V_ROWS - HEAD_DIM, tm), 0) == 0, 1.0, 0.0)
    for (off, k_out, v_out, k_pad) in ((768, ks_ref, vs_ref, blk_onehot), (1024, kw_ref, vw_ref, 0.0)):
        kr = rope(proj(off, off + LANES))
        k_out[0, 0] = jnp.where(low, kr, k_pad).astype(BF16)
        k_out[0, 1] = jnp.where(low, pltpu.roll(kr, 64, 1), k_pad).astype(BF16)
        vt = proj(off + LANES, off + 2 * LANES).T
        for g in range(NSA_GROUPS):
            v_out[0, g] = jnp.concatenate([vt[HEAD_DIM * g:HEAD_DIM * (g + 1)], ones_rows], axis=0).astype(BF16)
    gl_ref[0] = _sigmoid(proj(_W_GL, _W_GN)).T[0:GATE_ROWS]
    uv = proj(_W_U, _W_GS)
    for s in range(S5_WIDTH // LANES):
        us_scr[s] = uv[:, LANES * s:LANES * (s + 1)]
    for i in range(S5_L):
        for s in range(S5_WIDTH // LANES):
            c0 = i * S5_WIDTH + LANES * s
            u_ref[0, :, c0:c0 + LANES] = us_scr[s, pl.ds(i, tm // S5_L, stride=S5_L), :]


def _inproj(x, norm_g, w_a, cos2, sin2):
    B, T, D = x.shape
    tm = TM_PROJ
    grid = (B, T // tm)
    row_blk = lambda w: pl.BlockSpec((1, tm, w), lambda b, i: (b, i, 0))
    kv_blk = pl.BlockSpec((1, NSA_GROUPS, tm, LANES), lambda b, i: (b, 0, i, 0))
    kv_shape = jax.ShapeDtypeStruct((B, NSA_GROUPS, T, LANES), BF16)
    vt_blk = pl.BlockSpec((1, NSA_GROUPS, V_ROWS, tm), lambda b, i: (b, 0, 0, i))
    vt_shape = jax.ShapeDtypeStruct((B, NSA_GROUPS, V_ROWS, T), BF16)
    return pl.pallas_call(
        _inproj_kernel,
        grid=grid,
        in_specs=[
            row_blk(D),
            pl.BlockSpec((1, D), lambda b, i: (0, 0)),
            pl.BlockSpec(w_a.shape, lambda b, i: (0, 0), pipeline_mode=pl.Buffered(1)),
            pl.BlockSpec((tm, LANES), lambda b, i: (i, 0)),
            pl.BlockSpec((tm, LANES), lambda b, i: (i, 0)),
        ],
        out_specs=[
            pl.BlockSpec((1, NSA_HEADS, tm, LANES), lambda b, i: (b, 0, i, 0)),
            row_blk(LANES), row_blk(LANES),
            kv_blk, vt_blk, kv_blk, vt_blk,
            pl.BlockSpec((1, GATE_ROWS, tm), lambda b, i: (b, 0, i)),
            pl.BlockSpec((1, tm // S5_L, S5_L * S5_WIDTH), lambda b, i: (b, i, 0)),
        ],
        out_shape=[
            jax.ShapeDtypeStruct((B, NSA_HEADS, T, LANES), BF16),
            jax.ShapeDtypeStruct((B, T, LANES), F32), jax.ShapeDtypeStruct((B, T, LANES), F32),
            kv_shape, vt_shape, kv_shape, vt_shape,
            jax.ShapeDtypeStruct((B, GATE_ROWS, T), F32),
            jax.ShapeDtypeStruct((B, T // S5_L, S5_L * S5_WIDTH), F32),
        ],
        scratch_shapes=[pltpu.VMEM((S5_WIDTH // LANES, tm, LANES), F32)],
        compiler_params=pltpu.CompilerParams(
            dimension_semantics=("parallel", "arbitrary"), vmem_limit_bytes=VMEM_LIMIT),
        name="inproj",
    )(x, norm_g, w_a, cos2, sin2)


def _compress_kernel(kc_ref, vc_ref, w1k_ref, w1v_ref, w2k_ref, w2vt_ref, pbk_ref, pbv_ref, ko_ref, vo_ref, pb_scr):
    nch = ko_ref.shape[2]
    H = CMP_HIDDEN

    @pl.when(pl.program_id(0) == 0)
    def _():
        for n, (p_ref, w1_ref) in enumerate(((pbk_ref, w1k_ref), (pbv_ref, w1v_ref))):
            pw = jnp.dot(p_ref[...].astype(BF16), w1_ref[...], preferred_element_type=F32)
            pb_scr[n] = pw[0:1, 0:H] + pw[1:2, H:]

    def hidden(c_ref, w1_ref, n):
        acc = [jnp.zeros((nch, 2 * H), F32) for _ in range(NSA_GROUPS)]
        for j in range(CMP_STRIDE):
            rows = c_ref[0, pl.ds(j, nch, stride=CMP_STRIDE), :].astype(BF16)
            wj = w1_ref[HEAD_DIM * j:HEAD_DIM * (j + 1), :]
            for g in range(NSA_GROUPS):
                acc[g] = acc[g] + jnp.dot(rows[:, HEAD_DIM * g:HEAD_DIM * (g + 1)], wj, preferred_element_type=F32)
        return [_gelu_tanh(a[:, 0:H] + pltpu.roll(a[:, H:], nch - 1, 0) + pb_scr[n]).astype(BF16) for a in acc]

    hk = hidden(kc_ref, w1k_ref, 0)
    hv = hidden(vc_ref, w1v_ref, 1)
    for g in range(NSA_GROUPS):
        ko_ref[0, g] = jnp.dot(hk[g], w2k_ref[...], preferred_element_type=F32).astype(BF16)
        vo_ref[0, g] = _nt_dot(w2vt_ref[...], hv[g]).astype(BF16)


def _compress(kc, vc, w1k, w1v, w2k, w2vt, pbk, pbv):
    B, T, _ = kc.shape
    G = NSA_GROUPS
    nch = T // CMP_STRIDE
    c_blk = pl.BlockSpec((1, T, LANES), lambda b: (b, 0, 0))
    full = lambda a: pl.BlockSpec(a.shape, lambda b: (0,) * a.ndim)
    return pl.pallas_call(
        _compress_kernel,
        grid=(B,),
        in_specs=[c_blk, c_blk, full(w1k), full(w1v), full(w2k), full(w2vt), full(pbk), full(pbv)],
        out_specs=[pl.BlockSpec((1, G, nch, LANES), lambda b: (b, 0, 0, 0)),
                   pl.BlockSpec((1, G, HEAD_DIM, nch), lambda b: (b, 0, 0, 0))],
        out_shape=[jax.ShapeDtypeStruct((B, G, nch, LANES), BF16),
                   jax.ShapeDtypeStruct((B, G, HEAD_DIM, nch), BF16)],
        scratch_shapes=[pltpu.VMEM((2, 1, CMP_HIDDEN), F32)],
        compiler_params=pltpu.CompilerParams(dimension_semantics=("arbitrary",), vmem_limit_bytes=VMEM_LIMIT),
        name="compress",
    )(kc, vc, w1k, w1v, w2k, w2vt, pbk, pbv)


class _QTile:
    def __init__(self, x, t0, t_begin, t_end, q_ref, g_ref, o_ref, cols, n_wc):
        self.x, self.t0, self.q_ref, self.g_ref, self.o_ref = x, t0, q_ref, g_ref, o_ref
        self.window_inside = t_begin >= WINDOW
        self.ncp = t_end // CMP_STRIDE
        self.ns = t_end // SEL_BLOCK
        self.t_lane = t0 + (lax.broadcasted_iota(jnp.int32, (1, cols), 1) & (TQ - 1))
        c_end = lax.broadcasted_iota(jnp.int32, (self.ncp, cols), 0) * CMP_STRIDE + (CMP_BLOCK - 1)
        self.cmp_valid = c_end <= self.t_lane
        self.w_pos = [t0 - WINDOW + TQ * c for c in range(n_wc)]
        self.w_start = [pl.multiple_of(jnp.maximum(p, 0), TQ) for p in self.w_pos]


def _nsa_kernel(qa_ref, qb_ref, kc_ref, vct_ref, ksa_ref, vst_ref, kw_ref, vwt_ref, ga_ref, gb_ref, ovt_ref,
                oa_ref, ob_ref, qsel_scr, acc_scr, m_scr):
    units = [(bb, g) for bb in range(qa_ref.shape[0]) for g in range(NSA_GROUPS)]
    uidx = {u: i for i, u in enumerate(units)}
    n_qt = kw_ref.shape[2] // TQ
    R = HEADS_PER_GROUP
    cols = R * TQ
    NS = ovt_ref.shape[0]
    n_wc = (WINDOW + TQ) // TQ
    step = pl.program_id(1)
    t_mid = n_qt // 2 * TQ
    tiles = [_QTile(0, step * TQ, 0, t_mid, qa_ref, ga_ref, oa_ref, cols, n_wc),
             _QTile(1, (n_qt - 1 - step) * TQ, t_mid, n_qt * TQ, qb_ref, gb_ref, ob_ref, cols, n_wc)]
    sub8 = lax.broadcasted_iota(jnp.int32, (SUBLANES, TQ), 0)
    own_keys_visible = (lax.broadcasted_iota(jnp.int32, (TQ, cols), 0)
                        <= (lax.broadcasted_iota(jnp.int32, (1, cols), 1) & (TQ - 1)))

    def group_q(c, u):
        bb, g = u
        return c.q_ref[bb, R * g:R * (g + 1)].reshape(cols, LANES)

    def cmp_scores(c, u):
        return _nt_dot(kc_ref[u[0], u[1], 0:c.ncp, :], group_q(c, u))

    def cmp_probs(c, s):
        s = jnp.where(c.cmp_valid, s, NEG)
        e = jnp.exp2(s - jnp.max(s, axis=0, keepdims=True))
        inv = 1.0 / jnp.maximum(jnp.sum(e, axis=0, keepdims=True), 1.0e-30)
        return e * jnp.where(c.t_lane >= CMP_BLOCK - 1, inv, 0.0)

    def win_scores(c, u):
        kw = jnp.concatenate([kw_ref[u[0], u[1], pl.ds(c.w_start[n], TQ), :] for n in range(n_wc)], axis=0)
        return _nt_dot(kw, group_q(c, u))

    def win_probs(c, sw):
        parts = []
        for n in range(n_wc):
            sc = sw[TQ * n:TQ * (n + 1)]
            if n == 0:
                sc = jnp.where(own_keys_visible, NEG, sc)
            if n == n_wc - 1:
                sc = jnp.where(own_keys_visible, sc, NEG)
            elif not c.window_inside:
                sc = jnp.where(c.w_pos[n] >= 0, sc, NEG)
            parts.append(sc.astype(BF16))
        sw = jnp.concatenate(parts, axis=0)
        return jnp.exp2(sw - jnp.max(sw, axis=0, keepdims=True))

    def win_out(c, u, ew):
        vw = jnp.concatenate([vwt_ref[u[0], u[1], :, pl.ds(c.w_start[n], TQ)] for n in range(n_wc)], axis=1)
        ow = jnp.dot(vw, ew, preferred_element_type=F32)
        return ow[0:HEAD_DIM] * (1.0 / ow[HEAD_DIM:HEAD_DIM + 1])

    def select_blocks(c, u, p):
        ns = c.ns
        psum = p[:, 0:TQ] + p[:, TQ:2 * TQ] + p[:, 2 * TQ:3 * TQ] + p[:, 3 * TQ:4 * TQ]
        p_hi = psum.astype(BF16)
        p_lo = (psum - p_hi.astype(F32)).astype(BF16)
        ov = ovt_ref[0:ns, 0:c.ncp]
        imp = (jnp.dot(ov, p_hi, preferred_element_type=F32) + jnp.dot(ov, p_lo, preferred_element_type=F32))
        blk = lax.broadcasted_iota(jnp.int32, (ns, TQ), 0)
        t_l = c.t0 + lax.broadcasted_iota(jnp.int32, (ns, TQ), 1)
        cur = t_l // SEL_BLOCK
        imp = jnp.where(blk * SEL_BLOCK <= t_l, imp, -1.0)
        imp = jnp.where(blk == 0, FORCED_SCORE, imp)
        imp = jnp.where(blk == cur, FORCED_SCORE, imp)
        imp = jnp.where(blk == cur - 1, FORCED_SCORE, imp)
        nv = ns // SUBLANES
        imp8 = [imp[SUBLANES * j:SUBLANES * (j + 1)] for j in range(nv)]
        rank8 = [jnp.zeros((SUBLANES, TQ), F32) for _ in range(nv)]
        for mm in range(ns):
            row = imp[mm:mm + 1, :]
            jm = mm // SUBLANES
            for j in range(nv):
                if j < jm:
                    ahead = jnp.where(row > imp8[j], 1.0, 0.0)
                elif j > jm:
                    ahead = jnp.where(row >= imp8[j], 1.0, 0.0)
                else:
                    tie = jnp.where(sub8 > (mm % SUBLANES), 1.0, 0.0)
                    ahead = jnp.where(row > imp8[j], 1.0, 0.0) + jnp.where(row == imp8[j], tie, 0.0)
                rank8[j] = rank8[j] + ahead
        pen = jnp.where(jnp.concatenate(rank8, axis=0) < float(SEL_TOPK), 0.0, NEG)
        if ns < NS:
            pen = jnp.concatenate([pen, jnp.zeros((NS - ns, TQ), F32)], axis=0)
        q_t = group_q(c, u).astype(F32).T.astype(BF16)
        qsel_scr[c.x, uidx[u]] = jnp.concatenate(
            [q_t[0:HEAD_DIM], jnp.concatenate([pen.astype(BF16)] * R, axis=1)], axis=0)

    p_c, o_cmp, o_win = {}, {}, {}

    def cmp_job(c, u):
        def finish(p, _):
            p_c[c.x, u] = p
            o_cmp[c.x, u] = jnp.dot(vct_ref[u[0], u[1], :, 0:c.ncp], p.astype(BF16),
                                    preferred_element_type=F32)
        return (lambda: cmp_scores(c, u)), (lambda s: (cmp_probs(c, s), None)), finish

    def win_job(c, u):
        def finish(e_w, _):
            o_win[c.x, u] = win_out(c, u, e_w)
        return (lambda: win_scores(c, u)), (lambda s: (win_probs(c, s), None)), finish

    def sel_job(x, k0, nk, u, own):
        k0 = k0 if isinstance(k0, int) else pl.multiple_of(k0, TQ)

        def probs(sc):
            if own:
                sc = jnp.where(own_keys_visible, sc, NEG)
            sc = sc.astype(BF16)
            m_old = m_scr[x, uidx[u]]
            m_new = jnp.maximum(m_old, jnp.max(sc, axis=0, keepdims=True).astype(F32))
            m_scr[x, uidx[u]] = m_new
            return jnp.exp2(sc - m_new.astype(BF16)), jnp.exp2(m_old - m_new)

        def finish(pp, alpha):
            acc_scr[x, uidx[u]] = acc_scr[x, uidx[u]] * alpha + jnp.dot(
                vst_ref[u[0], u[1], :, pl.ds(k0, nk)], pp, preferred_element_type=F32)

        return (lambda: jnp.dot(ksa_ref[u[0], u[1], pl.ds(k0, nk), :], qsel_scr[x, uidx[u]],
                                preferred_element_type=F32)), probs, finish

    def fuse(js):
        return ((lambda: [j[0]() for j in js]),
                (lambda ss: ([j[1](s) for j, s in zip(js, ss)], None)),
                (lambda outs, _: [j[2](*o) for j, o in zip(js, outs)]))

    def sel_keys(x, k0, nk, own=False):
        return [fuse([sel_job(x, k0, nk, (bb, g), own) for g in range(NSA_GROUPS)]) for bb in range(qa_ref.shape[0])]

    early, late = tiles
    n_slots = (n_qt - 1) * TQ // TK
    n_static = n_slots - n_slots // 2
    n_late = late.t0 // TK
    c_late = late.t0 % TK // TQ
    plan = [cmp_job(c, u) for c in (late, early) for u in units]
    for u in units:
        plan += [lambda u=u: select_blocks(late, u, p_c[late.x, u]), win_job(late, u)]
    plan += sel_keys(late.x, late.t0, TQ, own=True)
    for s in range(n_static):
        plan += sel_keys(late.x, s * TK, TK)
        for u in units[s::n_static]:
            plan += [lambda u=u: select_blocks(early, u, p_c[early.x, u]), win_job(early, u)]
    plan += sel_keys(early.x, early.t0, TQ, own=True)
    for s in range(n_static, n_slots):
        is_late = s < n_late
        plan += sel_keys(jnp.where(is_late, late.x, early.x), jnp.where(is_late, s, s - n_late) * TK, TK)
    for s in range(TK // TQ - 1):
        is_late = s < c_late
        base = jnp.where(is_late, n_late, early.t0 // TK) * TK
        plan += sel_keys(jnp.where(is_late, late.x, early.x), base + jnp.where(is_late, s, s - c_late) * TQ, TQ)

    acc_scr[...] = jnp.zeros_like(acc_scr)
    m_scr[...] = jnp.full(m_scr.shape, NEG, F32)
    job_pos = [k for k, e in enumerate(plan) if isinstance(e, tuple)]
    following = dict(zip(job_pos, job_pos[1:]))
    issued = {job_pos[0]: plan[job_pos[0]][0]()}
    for k, entry in enumerate(plan):
        if not isinstance(entry, tuple):
            entry()
            continue
        if k in following:
            issued[following[k]] = plan[following[k]][0]()
        _, probs, finish = entry
        finish(*probs(issued.pop(k)))

    for c in tiles:
        for bb in range(qa_ref.shape[0]):
            glt = c.g_ref[bb]
            heads = []
            for g in range(NSA_GROUPS):
                acc = acc_scr[c.x, uidx[bb, g]]
                o_sel = acc[0:HEAD_DIM] * (1.0 / acc[HEAD_DIM:HEAD_DIM + 1])
                for r in range(R):
                    hh = R * g + r
                    sl = slice(r * TQ, (r + 1) * TQ)
                    heads.append(glt[3 * hh:3 * hh + 1] * o_cmp[c.x, (bb, g)][:, sl]
                                 + glt[3 * hh + 1:3 * hh + 2] * o_sel[:, sl]
                                 + glt[3 * hh + 2:3 * hh + 3] * o_win[c.x, (bb, g)][:, sl])
            c.o_ref[bb] = jnp.concatenate(heads, axis=0).T


def _nsa(qq, kcmp, vcmpt, ksa, vst, kw, vwt, glt, ovt):
    B, H, T, _ = qq.shape
    G = NSA_GROUPS
    NB = NSA_NB
    NCP = kcmp.shape[2]
    n_qt = T // TQ
    grid = (B // NB, n_qt // 2)
    k_blk = lambda n: pl.BlockSpec((NB, G, n, LANES), lambda b, i: (b, 0, 0, 0))
    vt_blk = lambda r, n: pl.BlockSpec((NB, G, r, n), lambda b, i: (b, 0, 0, 0))
    lo_tile = lambda b, i: i
    hi_tile = lambda b, i: n_qt - 1 - i
    q_blk = lambda tile: pl.BlockSpec((NB, H, TQ, LANES), lambda b, i: (b, 0, tile(b, i), 0))
    g_blk = lambda tile: pl.BlockSpec((NB, GATE_ROWS, TQ), lambda b, i: (b, 0, tile(b, i)))
    half = jax.ShapeDtypeStruct((B, T // 2, NSA_WIDTH), F32)
    return pl.pallas_call(
        _nsa_kernel,
        grid=grid,
        in_specs=[
            q_blk(lo_tile), q_blk(hi_tile),
            k_blk(NCP), vt_blk(HEAD_DIM, NCP), k_blk(T), vt_blk(V_ROWS, T), k_blk(T), vt_blk(V_ROWS, T),
            g_blk(lo_tile), g_blk(hi_tile),
            pl.BlockSpec(ovt.shape, lambda b, i: (0, 0)),
        ],
        out_specs=[pl.BlockSpec((NB, TQ, NSA_WIDTH), lambda b, i: (b, i, 0)),
                   pl.BlockSpec((NB, TQ, NSA_WIDTH), lambda b, i: (b, n_qt // 2 - 1 - i, 0))],
        out_shape=[half, half],
        scratch_shapes=[pltpu.VMEM((2, NB * G, LANES, HEADS_PER_GROUP * TQ), BF16),
                        pltpu.VMEM((2, NB * G, V_ROWS, HEADS_PER_GROUP * TQ), F32),
                        pltpu.VMEM((2, NB * G, 1, HEADS_PER_GROUP * TQ), F32)],
        compiler_params=pltpu.CompilerParams(
            dimension_semantics=("parallel", "arbitrary"), vmem_limit_bytes=VMEM_LIMIT),
        name="nsa",
    )(qq, qq, kcmp, vcmpt, ksa, vst, kw, vwt, glt, glt, ovt)


def _s5_prep_kernel(lre_ref, lim_ref, ldt_ref, bre_ref, bim_ref, are_ref, aim_ref, bbre_ref, bbim_ref):
    lre, lim = lre_ref[...], lim_ref[...]
    dt = jnp.exp(ldt_ref[...])
    mag = jnp.exp(lre * dt)
    a_re = mag * jnp.cos(lim * dt)
    a_im = mag * jnp.sin(lim * dt)
    den = lre * lre + lim * lim
    z_re = ((a_re - 1.0) * lre + a_im * lim) / den
    z_im = (a_im * lre - (a_re - 1.0) * lim) / den
    are_ref[...] = a_re
    aim_ref[...] = a_im
    bbre_ref[...] = z_re * bre_ref[...] - z_im * bim_ref[...]
    bbim_ref[...] = z_re * bim_ref[...] + z_im * bre_ref[...]


def _s5_prep(lre, lim, ldt, bre, bim):
    shp = jax.ShapeDtypeStruct(lre.shape, F32)
    return pl.pallas_call(_s5_prep_kernel, out_shape=[shp, shp, shp, shp], name="s5_prep")(lre, lim, ldt, bre, bim)


def _s5_weights_kernel(bbre_ref, bbim_ref, are_ref, aim_ref, crt_ref, cit_ref,
                       we_ref, wct_ref, tp_ref, alre_ref, alim_ref, bd_scr, arow_scr, tp_scr):
    gpb = S5_CPB // S5_GROUP
    for q in range(we_ref.shape[0]):
        r0 = S5_CPB * q
        bd_scr[...] = jnp.zeros_like(bd_scr)
        tp_scr[...] = jnp.zeros_like(tp_scr)
        for n, ref in enumerate((bbre_ref, bbim_ref, crt_ref, cit_ref)):
            for gl in range(gpb):
                bd_scr[n, S5_GROUP * gl:S5_GROUP * (gl + 1), S5_STATE * gl:S5_STATE * (gl + 1)] = (
                    ref[r0 + S5_GROUP * gl:r0 + S5_GROUP * (gl + 1), :])
        for n, ref in enumerate((are_ref, aim_ref)):
            for gl in range(gpb):
                arow_scr[n, :, S5_STATE * gl:S5_STATE * (gl + 1)] = ref[r0 + S5_GROUP * gl:r0 + S5_GROUP * gl + 1, :]
        bbre, bbim, crt, cit = bd_scr[0], bd_scr[1], bd_scr[2], bd_scr[3]
        are, aim = arow_scr[0], arow_scr[1]
        pre, pim = jnp.ones_like(are), jnp.zeros_like(are)
        for k in range(S5_L):
            bpr = bbre * pre - bbim * pim
            bpi = bbre * pim + bbim * pre
            i = S5_L - 1 - k
            we_ref[q, S5_CPB * i:S5_CPB * (i + 1), :] = jnp.concatenate([bpr, bpi], axis=1).astype(BF16)
            tap = _nt_dot(bpr.astype(BF16), crt.astype(BF16)) - _nt_dot(bpi.astype(BF16), cit.astype(BF16))
            for i in range(S5_L - k):
                j = i + k
                tp_scr[S5_CPB * i:S5_CPB * (i + 1), S5_CPB * j:S5_CPB * (j + 1)] = tap
            pre, pim = pre * are - pim * aim, pre * aim + pim * are
            wct_ref[q, S5_CPB * k:S5_CPB * (k + 1), :] = jnp.concatenate(
                [crt * pre - cit * pim, -(crt * pim + cit * pre)], axis=1).astype(BF16)
        tp_ref[q] = tp_scr[...].astype(BF16)
        alre_ref[q] = pre
        alim_ref[q] = pim


def _s5_weights(bbre, bbim, are, aim, crt, cit):
    nblk = S5_WIDTH // S5_CPB
    sb = S5_CPB // S5_GROUP * S5_STATE
    lc = S5_L * S5_CPB
    bps = S5W_BLOCKS_PER_STEP
    rows = pl.BlockSpec((bps * S5_CPB, S5_STATE), lambda q: (q, 0))
    blk = lambda r, c: pl.BlockSpec((bps, r, c), lambda q: (q, 0, 0))
    return pl.pallas_call(
        _s5_weights_kernel,
        grid=(nblk // bps,),
        in_specs=[rows] * 6,
        out_specs=[blk(lc, 2 * sb), blk(lc, 2 * sb), blk(lc, lc), blk(1, sb), blk(1, sb)],
        scratch_shapes=[pltpu.VMEM((4, S5_CPB, sb), F32), pltpu.VMEM((2, 1, sb), F32), pltpu.VMEM((lc, lc), F32)],
        out_shape=[jax.ShapeDtypeStruct((nblk, lc, 2 * sb), BF16),
                   jax.ShapeDtypeStruct((nblk, lc, 2 * sb), BF16),
                   jax.ShapeDtypeStruct((nblk, lc, lc), BF16),
                   jax.ShapeDtypeStruct((nblk, 1, sb), F32), jax.ShapeDtypeStruct((nblk, 1, sb), F32)],
        compiler_params=pltpu.CompilerParams(dimension_semantics=("parallel",), vmem_limit_bytes=VMEM_LIMIT),
        name="s5_weights",
    )(bbre, bbim, are, aim, crt, cit)


def _s5_kernel(u_ref, we_ref, wct_ref, tp_ref, alre_ref, alim_ref, d_ref, wg_ref, bg_ref, o_ref, e_scr, st_scr):
    nb, ct, _ = u_ref.shape
    L = S5_L
    nblk = S5_WIDTH // S5_CPB
    sb = S5_CPB // S5_GROUP * S5_STATE
    spb = 2 * sb // LANES
    W = S5_SCAN_SLABS * LANES

    @pl.when(pl.program_id(0) == 0)
    def _():
        st_scr[...] = jnp.zeros_like(st_scr)

    uf = u_ref[...].reshape(nb * ct, L * S5_WIDTH)
    ub = uf.astype(BF16)
    u_blk = [jnp.concatenate([ub[:, i * S5_WIDTH + S5_CPB * q:i * S5_WIDTH + S5_CPB * (q + 1)] for i in range(L)],
                             axis=1) for q in range(nblk)]

    for q in range(nblk):
        e = jnp.dot(u_blk[q], we_ref[q], preferred_element_type=F32)
        for s in range(spb):
            for b in range(nb):
                e_scr[spb * q + s, pl.ds(b, ct, stride=nb), :] = e[b * ct:(b + 1) * ct, LANES * s:LANES * (s + 1)]

    def slabs_of(cs):
        q, r = divmod(cs * LANES, sb)
        return spb * q + r // LANES, spb * q + (sb + r) // LANES

    for j in range(S5_GROUPS * S5_STATE // W):
        sl = [slabs_of(S5_SCAN_SLABS * j + n) for n in range(S5_SCAN_SLABS)]
        sl_re, sl_im = [s[0] for s in sl], [s[1] for s in sl]
        ar = alre_ref[:, W * j:W * (j + 1)]
        ai = alim_ref[:, W * j:W * (j + 1)]
        load = lambda r0, rows, slabs: jnp.concatenate([e_scr[s, pl.ds(r0, rows), :] for s in slabs], axis=1)
        state = lambda slabs: jnp.concatenate([st_scr[:, LANES * s:LANES * (s + 1)] for s in slabs], axis=1)
        sr, si = state(sl_re), state(sl_im)
        for k in range(ct // 2):
            r0 = k * 2 * nb
            er, ei = load(r0, 2 * nb, sl_re), load(r0, 2 * nb, sl_im)
            tr = ar * sr - ai * si + er[0:nb]
            ti = ar * si + ai * sr + ei[0:nb]
            xr = jnp.concatenate([sr, tr], axis=0)
            xi = jnp.concatenate([si, ti], axis=0)
            for n in range(S5_SCAN_SLABS):
                e_scr[sl_re[n], pl.ds(r0, 2 * nb), :] = xr[:, LANES * n:LANES * (n + 1)]
                e_scr[sl_im[n], pl.ds(r0, 2 * nb), :] = xi[:, LANES * n:LANES * (n + 1)]
            sr, si = ar * tr - ai * ti + er[nb:], ar * ti + ai * tr + ei[nb:]
        for n in range(S5_SCAN_SLABS):
            st_scr[:, LANES * sl_re[n]:LANES * (sl_re[n] + 1)] = sr[:, LANES * n:LANES * (n + 1)]
            st_scr[:, LANES * sl_im[n]:LANES * (sl_im[n] + 1)] = si[:, LANES * n:LANES * (n + 1)]

    ys = []
    for q in range(nblk):
        per_b = [jnp.concatenate([e_scr[spb * q + s, pl.ds(b, ct, stride=nb), :] for s in range(spb)], axis=1)
                 for b in range(nb)]
        x_in = jnp.concatenate(per_b, axis=0).astype(BF16)
        ys.append((_nt_dot(x_in, wct_ref[q]) + jnp.dot(u_blk[q], tp_ref[q], preferred_element_type=F32)).astype(BF16))
    outs = []
    for j in range(L):
        yj = jnp.concatenate([y[:, S5_CPB * j:S5_CPB * (j + 1)] for y in ys], axis=1).astype(F32)
        yj = yj + d_ref[...] * uf[:, j * S5_WIDTH:(j + 1) * S5_WIDTH]
        z = _gelu_tanh(yj)
        gate = jnp.dot(z.astype(BF16), wg_ref[...], preferred_element_type=F32) + bg_ref[...]
        outs.append(z * _sigmoid(gate))
    o_ref[...] = jnp.concatenate(outs, axis=1).reshape(nb, ct, L * S5_WIDTH)


def _s5(u4, we, wct, tp, alre, alim, d, w_glu, b_glu):
    nb, nchunks, w4 = u4.shape
    ct = S5_CT
    n_slabs = 2 * S5_GROUPS * S5_STATE // LANES
    resident = lambda a: pl.BlockSpec(a.shape, lambda i: (0,) * a.ndim, pipeline_mode=pl.Buffered(1))
    u_blk = pl.BlockSpec((nb, ct, w4), lambda i: (0, i, 0))
    return pl.pallas_call(
        _s5_kernel,
        grid=(nchunks // ct,),
        in_specs=[u_blk, resident(we), resident(wct), resident(tp), resident(alre), resident(alim),
                  resident(d), resident(w_glu), resident(b_glu)],
        out_specs=u_blk,
        out_shape=jax.ShapeDtypeStruct(u4.shape, F32),
        scratch_shapes=[pltpu.VMEM((n_slabs, nb * ct, LANES), F32), pltpu.VMEM((nb, n_slabs * LANES), F32)],
        compiler_params=pltpu.CompilerParams(dimension_semantics=("arbitrary",), vmem_limit_bytes=VMEM_LIMIT),
        name="s5",
    )(u4, we, wct, tp, alre, alim, d, w_glu, b_glu)


def _final_kernel(x_ref, g_ref, wb_ref, onl_ref, onh_ref, os_ref, wpn_ref, wps_ref, wo_ref, fg_ref, o_ref, os_scr):
    tm = x_ref.shape[1]
    o4 = os_ref[0]
    for i in range(S5_L):
        for s in range(S5_WIDTH // LANES):
            c0 = i * S5_WIDTH + LANES * s
            os_scr[s, pl.ds(i, tm // S5_L, stride=S5_L), :] = o4[:, c0:c0 + LANES]
    first_half = pl.program_id(1) < pl.num_programs(1) // 2

    def silu(v):
        return v * _sigmoid(v)

    rows = [slice(r, r + tm // FINAL_SUB) for r in range(0, tm, tm // FINAL_SUB)]
    hs = [(_rms_scale(x_ref[0, r]) * g_ref[...]).astype(BF16) for r in rows]
    for r, h in zip(rows, hs):
        proj = lambda a, b, h=h: jnp.dot(h, wb_ref[:, a:b], preferred_element_type=F32)
        o_nsa = jnp.where(first_half, onl_ref[0, r], onh_ref[0, r])
        o_s5 = jnp.concatenate([os_scr[s, r] for s in range(S5_WIDTH // LANES)], axis=1)
        a_in = (o_nsa * silu(proj(_W_GN, _W_U))).astype(BF16)
        b_in = (o_s5 * silu(proj(_W_GS, _W_MG))).astype(BF16)
        branch_a = jnp.dot(a_in, wpn_ref[...], preferred_element_type=F32)
        branch_b = jnp.dot(b_in, wps_ref[...], preferred_element_type=F32)
        merged = (_sigmoid(proj(_W_MG, _W_MG + D_MODEL)) * branch_a
                  + _sigmoid(proj(_W_MG + D_MODEL, _W_END)) * branch_b)
        y = x_ref[0, r] + jnp.dot(merged.astype(BF16), wo_ref[...], preferred_element_type=F32)
        o_ref[0, r] = _rms_scale(y) * fg_ref[...]


def _final(x, norm_g, w_b, o_nsa_lo, o_nsa_hi, o_s5, wpn, wps, wo, final_g):
    B, T, D = x.shape
    tm = TM_PROJ
    nh = T // tm // 2
    row_blk = lambda w: pl.BlockSpec((1, tm, w), lambda b, i: (b, i, 0))
    full = lambda a: pl.BlockSpec(a.shape, lambda b, i: (0,) * a.ndim)
    return pl.pallas_call(
        _final_kernel,
        grid=(B, T // tm),
        in_specs=[row_blk(D), full(norm_g), pl.BlockSpec(w_b.shape, lambda b, i: (0, 0), pipeline_mode=pl.Buffered(1)),
                  pl.BlockSpec((1, tm, NSA_WIDTH), lambda b, i: (b, jnp.minimum(i, nh - 1), 0)),
                  pl.BlockSpec((1, tm, NSA_WIDTH), lambda b, i: (b, jnp.maximum(i - nh, 0), 0)),
                  pl.BlockSpec((1, tm // S5_L, S5_L * S5_WIDTH), lambda b, i: (b, i, 0)),
                  full(wpn), full(wps), full(wo), full(final_g)],
        out_specs=row_blk(D),
        out_shape=jax.ShapeDtypeStruct((B, T, D), F32),
        scratch_shapes=[pltpu.VMEM((S5_WIDTH // LANES, tm, LANES), F32)],
        compiler_params=pltpu.CompilerParams(
            dimension_semantics=("parallel", "arbitrary"), vmem_limit_bytes=VMEM_LIMIT),
        name="final",
    )(x, norm_g, w_b, o_nsa_lo, o_nsa_hi, o_s5, wpn, wps, wo, final_g)


def _rope_tables(T):
    half = HEAD_DIM // 2
    inv_freq = np.float32(ROPE_THETA) ** (-np.arange(half, dtype=np.float32) / np.float32(half))
    ang = np.arange(T, dtype=np.float32)[:, None] * inv_freq[None, :].astype(np.float32)
    cos, sin = np.cos(ang).astype(np.float32), np.sin(ang).astype(np.float32)
    cos2 = np.concatenate([cos, cos, cos, cos], axis=1)
    sin2 = np.concatenate([-sin, sin, -sin, sin], axis=1)
    return jnp.asarray(cos2), jnp.asarray(sin2)


def _compress_w1(w1):
    half_rows = CMP_STRIDE * HEAD_DIM
    return jnp.concatenate([w1[:half_rows], w1[half_rows:]], axis=1).astype(BF16)


def kernel(x, norm_g, w_in, cmp_pos_k, cmp_pos_v, cmp_w1_k, cmp_w2_k, cmp_w1_v, cmp_w2_v, s5_lam_re, s5_lam_im, s5_log_dt, s5_b_re, s5_b_im, s5_c_re, s5_c_im, s5_d, w_glu, b_glu, w_proj_nsa, w_proj_s5, w_out, final_g):
    B, T, D = x.shape
    assert w_in.shape[0] == 1, "single-layer block"
    NCH = T // CMP_STRIDE
    NS = T // SEL_BLOCK

    w = w_in[0]
    w_all = jnp.concatenate([w[:, :_OFF_GL], jnp.pad(w[:, _OFF_GL:_OFF_GN], ((0, 0), (0, LANES - 24))),
                             w[:, _OFF_GN:]], axis=1).astype(BF16)
    g2 = norm_g[0][None, :]
    cos2, sin2 = _rope_tables(T)

    qq, kc, vc, ksa, vst, kw, vwt, glt, u4 = _inproj(x, g2, w_all, cos2, sin2)

    w2k = jnp.concatenate([jnp.zeros_like(cmp_w2_k[0]), cmp_w2_k[0]], axis=1).astype(BF16)
    w2vt = cmp_w2_v[0].T.astype(BF16)
    pos_rows = lambda p: jnp.pad(p.reshape(2, CMP_STRIDE * HEAD_DIM), ((0, 2 * SUBLANES - 2), (0, 0)))
    kcmp, vcmpt = _compress(kc, vc, _compress_w1(cmp_w1_k[0]), _compress_w1(cmp_w1_v[0]), w2k, w2vt,
                            pos_rows(cmp_pos_k[0]), pos_rows(cmp_pos_v[0]))

    c_start = jnp.arange(NCH) * CMP_STRIDE
    s_start = jnp.arange(NS) * SEL_BLOCK
    ovt = ((c_start[None, :] < s_start[:, None] + SEL_BLOCK) & (c_start[None, :] + CMP_BLOCK > s_start[:, None])
           & (jnp.arange(NCH)[None, :] < NCH - 1)).astype(BF16)
    o_nsa_lo, o_nsa_hi = _nsa(qq, kcmp, vcmpt, ksa, vst, kw, vwt, glt, ovt)

    rep = lambda a: jnp.repeat(a, S5_GROUP, axis=0)
    tr = lambda b: b.transpose(0, 2, 1).reshape(S5_GROUPS * S5_GROUP, S5_STATE)
    a_re, a_im, bb_re, bb_im = _s5_prep(
        rep(s5_lam_re[0]), rep(s5_lam_im[0]),
        rep(jnp.broadcast_to(s5_log_dt[0][:, None], (S5_GROUPS, S5_STATE))),
        tr(s5_b_re[0]), tr(s5_b_im[0]))
    flat = lambda c: c.reshape(S5_GROUPS * S5_GROUP, S5_STATE)
    we, wct, tp, alre, alim = _s5_weights(bb_re, bb_im, a_re, a_im, flat(s5_c_re[0]), flat(s5_c_im[0]))
    o_s5 = _s5(u4, we, wct, tp, alre.reshape(1, -1), alim.reshape(1, -1), s5_d[0][None, :],
               w_glu[0].astype(BF16), b_glu[0][None, :])

    return _final(x, g2, w_all, o_nsa_lo, o_nsa_hi, o_s5, w_proj_nsa[0].astype(BF16), w_proj_s5[0].astype(BF16),
                  w_out[0].astype(BF16), final_g[None, :])
```

```python
import math

import jax
import jax.numpy as jnp
import numpy as np
from jax import lax
from jax.experimental import pallas as pl
from jax.experimental.pallas import tpu as pltpu

F32 = jnp.float32
BF16 = jnp.bfloat16

D_MODEL = 1024
NSA_HEADS = 8
NSA_GROUPS = 2
HEADS_PER_GROUP = 4
HEAD_DIM = 64
NSA_WIDTH = 512
CMP_BLOCK = 32
CMP_STRIDE = 16
CMP_HIDDEN = 256
SEL_BLOCK = 64
SEL_TOPK = 16
WINDOW = 512
ROPE_THETA = 10000.0
FORCED_SCORE = 1.0e4
NEG = -1.0e30
S5_WIDTH = 512
S5_GROUP = 16
S5_GROUPS = 32
S5_STATE = 64
RMS_EPS = 1.0e-6

LANES = 128
SUBLANES = 8
VMEM_LIMIT = 56 * 1024 * 1024

_OFF_GL = 1280
_OFF_GN = 1304
_W_GL, _W_GN, _W_U, _W_GS, _W_MG, _W_END = 1280, 1408, 1920, 2432, 2944, 4992

TM_PROJ = 512
FINAL_SUB = 2
TQ = 128
TK = 512
NSA_NB = 2
V_ROWS = 80
GATE_ROWS = 32
S5_L = 8
S5_CPB = 256 // S5_L
S5_CT = 64
S5W_BLOCKS_PER_STEP = 4
S5_SCAN_SLABS = 4


def _gelu_tanh(x):
    c = math.sqrt(2.0 / math.pi)
    return 0.5 * x * (1.0 + jnp.tanh(c * (x + 0.044715 * (x * x * x))))


def _sigmoid(x):
    return 1.0 / (1.0 + jnp.exp(-x))


def _rms_scale(xv):
    ms = jnp.mean(xv * xv, axis=-1, keepdims=True)
    return xv * lax.rsqrt(ms + RMS_EPS)


def _nt_dot(a, b):
    return lax.dot_general(a, b, (((1,), (1,)), ((), ())), preferred_element_type=F32)


def _inproj_kernel(x_ref, g_ref, w_ref, cos_ref, sin_ref,
                   qq_ref, kc_ref, vc_ref, ks_ref, vs_ref, kw_ref, vw_ref, gl_ref, u_ref, us_scr):
    h = (_rms_scale(x_ref[0]) * g_ref[...]).astype(BF16)
    cos2 = cos_ref[...]
    sin2 = sin_ref[...]
    lane = lax.broadcasted_iota(jnp.int32, cos2.shape, 1)
    first_half = (lane & (HEAD_DIM - 1)) < (HEAD_DIM // 2)
    low = lane < HEAD_DIM

    wide = {}

    def proj(a, b):
        for (s0, s1) in ((0, 512), (512, _W_GN), (_W_U, _W_GS)):
            if s0 <= a and b <= s1:
                if s0 not in wide:
                    wide[s0] = jnp.dot(h, w_ref[:, s0:s1], preferred_element_type=F32)
                return wide[s0][:, a - s0:b - s0]
        raise ValueError((a, b))

    def rope(xs):
        partner = jnp.where(first_half, pltpu.roll(xs, 96, 1), pltpu.roll(xs, 32, 1))
        return xs * cos2 + partner * sin2

    scale = HEAD_DIM ** -0.5 * math.log2(math.e)
    for i in range(NSA_HEADS // 2):
        xs = proj(LANES * i, LANES * (i + 1)) * scale
        xr = rope(xs)
        qq_ref[0, 2 * i] = jnp.where(low, xr, pltpu.roll(xs, 64, 1)).astype(BF16)
        qq_ref[0, 2 * i + 1] = jnp.where(low, pltpu.roll(xr, 64, 1), xs).astype(BF16)

    kc_ref[0] = proj(512, 640)
    vc_ref[0] = proj(640, 768)
    tm = cos2.shape[0]
    t_row = pl.program_id(1) * tm + lax.broadcasted_iota(jnp.int32, cos2.shape, 0)
    blk_onehot = jnp.where(lane - HEAD_DIM == t_row // SEL_BLOCK, 1.0, 0.0)
    ones_rows = jnp.where(lax.broadcasted_iota(jnp.int32, (V_ROWS - HEAD_DIM, tm), 0) == 0, 1.0, 0.0)
    for (off, k_out, v_out, k_pad) in ((768, ks_ref, vs_ref, blk_onehot), (1024, kw_ref, vw_ref, 0.0)):
        kr = rope(proj(off, off + LANES))
        k_out[0, 0] = jnp.where(low, kr, k_pad).astype(BF16)
        k_out[0, 1] = jnp.where(low, pltpu.roll(kr, 64, 1), k_pad).astype(BF16)
        vt = proj(off + LANES, off + 2 * LANES).T
        for g in range(NSA_GROUPS):
            v_out[0, g] = jnp.concatenate([vt[HEAD_DIM * g:HEAD_DIM * (g + 1)], ones_rows], axis=0).astype(BF16)
    gl_ref[0] = _sigmoid(proj(_W_GL, _W_GN)).T[0:GATE_ROWS]
    uv = proj(_W_U, _W_GS)
    for s in range(S5_WIDTH // LANES):
        us_scr[s] = uv[:, LANES * s:LANES * (s + 1)]
    for i in range(S5_L):
        for s in range(S5_WIDTH // LANES):
            c0 = i * S5_WIDTH + LANES * s
            u_ref[0, :, c0:c0 + LANES] = us_scr[s, pl.ds(i, tm // S5_L, stride=S5_L), :]


def _inproj(x, norm_g, w_a, cos2, sin2):
    B, T, D = x.shape
    tm = TM_PROJ
    grid = (B, T // tm)
    row_blk = lambda w: pl.BlockSpec((1, tm, w), lambda b, i: (b, i, 0))
    kv_blk = pl.BlockSpec((1, NSA_GROUPS, tm, LANES), lambda b, i: (b, 0, i, 0))
    kv_shape = jax.ShapeDtypeStruct((B, NSA_GROUPS, T, LANES), BF16)
    vt_blk = pl.BlockSpec((1, NSA_GROUPS, V_ROWS, tm), lambda b, i: (b, 0, 0, i))
    vt_shape = jax.ShapeDtypeStruct((B, NSA_GROUPS, V_ROWS, T), BF16)
    return pl.pallas_call(
        _inproj_kernel,
        grid=grid,
        in_specs=[
            row_blk(D),
            pl.BlockSpec((1, D), lambda b, i: (0, 0)),
            pl.BlockSpec(w_a.shape, lambda b, i: (0, 0), pipeline_mode=pl.Buffered(1)),
            pl.BlockSpec((tm, LANES), lambda b, i: (i, 0)),
            pl.BlockSpec((tm, LANES), lambda b, i: (i, 0)),
        ],
        out_specs=[
            pl.BlockSpec((1, NSA_HEADS, tm, LANES), lambda b, i: (b, 0, i, 0)),
            row_blk(LANES), row_blk(LANES),
            kv_blk, vt_blk, kv_blk, vt_blk,
            pl.BlockSpec((1, GATE_ROWS, tm), lambda b, i: (b, 0, i)),
            pl.BlockSpec((1, tm // S5_L, S5_L * S5_WIDTH), lambda b, i: (b, i, 0)),
        ],
        out_shape=[
            jax.ShapeDtypeStruct((B, NSA_HEADS, T, LANES), BF16),
            jax.ShapeDtypeStruct((B, T, LANES), F32), jax.ShapeDtypeStruct((B, T, LANES), F32),
            kv_shape, vt_shape, kv_shape, vt_shape,
            jax.ShapeDtypeStruct((B, GATE_ROWS, T), F32),
            jax.ShapeDtypeStruct((B, T // S5_L, S5_L * S5_WIDTH), F32),
        ],
        scratch_shapes=[pltpu.VMEM((S5_WIDTH // LANES, tm, LANES), F32)],
        compiler_params=pltpu.CompilerParams(
            dimension_semantics=("parallel", "arbitrary"), vmem_limit_bytes=VMEM_LIMIT),
        name="inproj",
    )(x, norm_g, w_a, cos2, sin2)


def _compress_kernel(kc_ref, vc_ref, w1k_ref, w1v_ref, w2k_ref, w2vt_ref, pbk_ref, pbv_ref, ko_ref, vo_ref, pb_scr):
    nch = ko_ref.shape[2]
    H = CMP_HIDDEN

    @pl.when(pl.program_id(0) == 0)
    def _():
        for n, (p_ref, w1_ref) in enumerate(((pbk_ref, w1k_ref), (pbv_ref, w1v_ref))):
            pw = jnp.dot(p_ref[...].astype(BF16), w1_ref[...], preferred_element_type=F32)
            pb_scr[n] = pw[0:1, 0:H] + pw[1:2, H:]

    def hidden(c_ref, w1_ref, n):
        acc = [jnp.zeros((nch, 2 * H), F32) for _ in range(NSA_GROUPS)]
        for j in range(CMP_STRIDE):
            rows = c_ref[0, pl.ds(j, nch, stride=CMP_STRIDE), :].astype(BF16)
            wj = w1_ref[HEAD_DIM * j:HEAD_DIM * (j + 1), :]
            for g in range(NSA_GROUPS):
                acc[g] = acc[g] + jnp.dot(rows[:, HEAD_DIM * g:HEAD_DIM * (g + 1)], wj, preferred_element_type=F32)
        return [_gelu_tanh(a[:, 0:H] + pltpu.roll(a[:, H:], nch - 1, 0) + pb_scr[n]).astype(BF16) for a in acc]

    hk = hidden(kc_ref, w1k_ref, 0)
    hv = hidden(vc_ref, w1v_ref, 1)
    for g in range(NSA_GROUPS):
        ko_ref[0, g] = jnp.dot(hk[g], w2k_ref[...], preferred_element_type=F32).astype(BF16)
        vo_ref[0, g] = _nt_dot(w2vt_ref[...], hv[g]).astype(BF16)


def _compress(kc, vc, w1k, w1v, w2k, w2vt, pbk, pbv):
    B, T, _ = kc.shape
    G = NSA_GROUPS
    nch = T // CMP_STRIDE
    c_blk = pl.BlockSpec((1, T, LANES), lambda b: (b, 0, 0))
    full = lambda a: pl.BlockSpec(a.shape, lambda b: (0,) * a.ndim)
    return pl.pallas_call(
        _compress_kernel,
        grid=(B,),
        in_specs=[c_blk, c_blk, full(w1k), full(w1v), full(w2k), full(w2vt), full(pbk), full(pbv)],
        out_specs=[pl.BlockSpec((1, G, nch, LANES), lambda b: (b, 0, 0, 0)),
                   pl.BlockSpec((1, G, HEAD_DIM, nch), lambda b: (b, 0, 0, 0))],
        out_shape=[jax.ShapeDtypeStruct((B, G, nch, LANES), BF16),
                   jax.ShapeDtypeStruct((B, G, HEAD_DIM, nch), BF16)],
        scratch_shapes=[pltpu.VMEM((2, 1, CMP_HIDDEN), F32)],
        compiler_params=pltpu.CompilerParams(dimension_semantics=("arbitrary",), vmem_limit_bytes=VMEM_LIMIT),
        name="compress",
    )(kc, vc, w1k, w1v, w2k, w2vt, pbk, pbv)


class _QTile:
    def __init__(self, x, t0, t_begin, t_end, q_ref, g_ref, o_ref, cols, n_wc):
        self.x, self.t0, self.q_ref, self.g_ref, self.o_ref = x, t0, q_ref, g_ref, o_ref
        self.window_inside = t_begin >= WINDOW
        self.ncp = t_end // CMP_STRIDE
        self.ns = t_end // SEL_BLOCK
        self.t_lane = t0 + (lax.broadcasted_iota(jnp.int32, (1, cols), 1) & (TQ - 1))
        c_end = lax.broadcasted_iota(jnp.int32, (self.ncp, cols), 0) * CMP_STRIDE + (CMP_BLOCK - 1)
        self.cmp_valid = c_end <= self.t_lane
        self.w_pos = [t0 - WINDOW + TQ * c for c in range(n_wc)]
        self.w_start = [pl.multiple_of(jnp.maximum(p, 0), TQ) for p in self.w_pos]


def _nsa_kernel(qa_ref, qb_ref, kc_ref, vct_ref, ksa_ref, vst_ref, kw_ref, vwt_ref, ga_ref, gb_ref, ovt_ref,
                oa_ref, ob_ref, qsel_scr, acc_scr, m_scr):
    units = [(bb, g) for bb in range(qa_ref.shape[0]) for g in range(NSA_GROUPS)]
    uidx = {u: i for i, u in enumerate(units)}
    n_qt = kw_ref.shape[2] // TQ
    R = HEADS_PER_GROUP
    cols = R * TQ
    NS = ovt_ref.shape[0]
    n_wc = (WINDOW + TQ) // TQ
    step = pl.program_id(1)
    t_mid = n_qt // 2 * TQ
    tiles = [_QTile(0, step * TQ, 0, t_mid, qa_ref, ga_ref, oa_ref, cols, n_wc),
             _QTile(1, (n_qt - 1 - step) * TQ, t_mid, n_qt * TQ, qb_ref, gb_ref, ob_ref, cols, n_wc)]
    sub8 = lax.broadcasted_iota(jnp.int32, (SUBLANES, TQ), 0)
    own_keys_visible = (lax.broadcasted_iota(jnp.int32, (TQ, cols), 0)
                        <= (lax.broadcasted_iota(jnp.int32, (1, cols), 1) & (TQ - 1)))

    q_t_cache = {}

    def group_q_t(c, u):
        if (c.x, u) not in q_t_cache:
            bb, g = u
            q = c.q_ref[bb, R * g:R * (g + 1)].reshape(cols, LANES)
            q_t_cache[c.x, u] = q.astype(F32).T.astype(BF16)
        return q_t_cache[c.x, u]

    def cmp_scores(c, u):
        return jnp.dot(kc_ref[u[0], u[1], 0:c.ncp, :], group_q_t(c, u), preferred_element_type=F32)

    def cmp_probs(c, s):
        s = jnp.where(c.cmp_valid, s, NEG)
        e = jnp.exp2(s - jnp.max(s, axis=0, keepdims=True))
        inv = 1.0 / jnp.maximum(jnp.sum(e, axis=0, keepdims=True), 1.0e-30)
        return e * jnp.where(c.t_lane >= CMP_BLOCK - 1, inv, 0.0)

    def win_scores(c, u):
        kw = jnp.concatenate([kw_ref[u[0], u[1], pl.ds(c.w_start[n], TQ), :] for n in range(n_wc)], axis=0)
        return jnp.dot(kw, group_q_t(c, u), preferred_element_type=F32)

    def win_probs(c, sw):
        parts = []
        for n in range(n_wc):
            sc = sw[TQ * n:TQ * (n + 1)]
            if n == 0:
                sc = jnp.where(own_keys_visible, NEG, sc)
            if n == n_wc - 1:
                sc = jnp.where(own_keys_visible, sc, NEG)
            elif not c.window_inside:
                sc = jnp.where(c.w_pos[n] >= 0, sc, NEG)
            parts.append(sc.astype(BF16))
        sw = jnp.concatenate(parts, axis=0)
        return jnp.exp2(sw - jnp.max(sw, axis=0, keepdims=True))

    def win_out(c, u, ew):
        vw = jnp.concatenate([vwt_ref[u[0], u[1], :, pl.ds(c.w_start[n], TQ)] for n in range(n_wc)], axis=1)
        ow = jnp.dot(vw, ew, preferred_element_type=F32)
        return ow[0:HEAD_DIM] * (1.0 / ow[HEAD_DIM:HEAD_DIM + 1])

    def select_blocks(c, u, p):
        ns = c.ns
        psum = p[:, 0:TQ] + p[:, TQ:2 * TQ] + p[:, 2 * TQ:3 * TQ] + p[:, 3 * TQ:4 * TQ]
        p_hi = psum.astype(BF16)
        p_lo = (psum - p_hi.astype(F32)).astype(BF16)
        ov = ovt_ref[0:ns, 0:c.ncp]
        imp = (jnp.dot(ov, p_hi, preferred_element_type=F32) + jnp.dot(ov, p_lo, preferred_element_type=F32))
        blk = lax.broadcasted_iota(jnp.int32, (ns, TQ), 0)
        t_l = c.t0 + lax.broadcasted_iota(jnp.int32, (ns, TQ), 1)
        cur = t_l // SEL_BLOCK
        imp = jnp.where(blk * SEL_BLOCK <= t_l, imp, -1.0)
        imp = jnp.where(blk == 0, FORCED_SCORE, imp)
        imp = jnp.where(blk == cur, FORCED_SCORE, imp)
        imp = jnp.where(blk == cur - 1, FORCED_SCORE, imp)
        nv = ns // SUBLANES
        imp8 = [imp[SUBLANES * j:SUBLANES * (j + 1)] for j in range(nv)]
        rank8 = [jnp.zeros((SUBLANES, TQ), F32) for _ in range(nv)]
        for mm in range(ns):
            row = imp[mm:mm + 1, :]
            jm = mm // SUBLANES
            for j in range(nv):
                if j < jm:
                    ahead = jnp.where(row > imp8[j], 1.0, 0.0)
                elif j > jm:
                    ahead = jnp.where(row >= imp8[j], 1.0, 0.0)
                else:
                    tie = jnp.where(sub8 > (mm % SUBLANES), 1.0, 0.0)
                    ahead = jnp.where(row > imp8[j], 1.0, 0.0) + jnp.where(row == imp8[j], tie, 0.0)
                rank8[j] = rank8[j] + ahead
        pen = jnp.where(jnp.concatenate(rank8, axis=0) < float(SEL_TOPK), 0.0, NEG)
        if ns < NS:
            pen = jnp.concatenate([pen, jnp.zeros((NS - ns, TQ), F32)], axis=0)
        qsel_scr[c.x, uidx[u]] = jnp.concatenate(
            [group_q_t(c, u)[0:HEAD_DIM], jnp.concatenate([pen.astype(BF16)] * R, axis=1)], axis=0)

    p_c, o_cmp, o_win = {}, {}, {}

    def cmp_job(c, u):
        def finish(p, _):
            p_c[c.x, u] = p
            o_cmp[c.x, u] = jnp.dot(vct_ref[u[0], u[1], :, 0:c.ncp], p.astype(BF16),
                                    preferred_element_type=F32)
        return (lambda: cmp_scores(c, u)), (lambda s: (cmp_probs(c, s), None)), finish

    def win_job(c, u):
        def finish(e_w, _):
            o_win[c.x, u] = win_out(c, u, e_w)
        return (lambda: win_scores(c, u)), (lambda s: (win_probs(c, s), None)), finish

    def sel_job(x, k0, nk, u, own):
        k0 = k0 if isinstance(k0, int) else pl.multiple_of(k0, TQ)

        def probs(sc):
            if own:
                sc = jnp.where(own_keys_visible, sc, NEG)
            sc = sc.astype(BF16)
            m_old = m_scr[x, uidx[u]]
            m_new = jnp.maximum(m_old, jnp.max(sc, axis=0, keepdims=True).astype(F32))
            m_scr[x, uidx[u]] = m_new
            return jnp.exp2(sc - m_new.astype(BF16)), jnp.exp2(m_old - m_new)

        def finish(pp, alpha):
            acc_scr[x, uidx[u]] = acc_scr[x, uidx[u]] * alpha + jnp.dot(
                vst_ref[u[0], u[1], :, pl.ds(k0, nk)], pp, preferred_element_type=F32)

        return (lambda: jnp.dot(ksa_ref[u[0], u[1], pl.ds(k0, nk), :], qsel_scr[x, uidx[u]],
                                preferred_element_type=F32)), probs, finish

    def fuse(js):
        return ((lambda: [j[0]() for j in js]),
                (lambda ss: ([j[1](s) for j, s in zip(js, ss)], None)),
                (lambda outs, _: [j[2](*o) for j, o in zip(js, outs)]))

    def sel_keys(x, k0, nk, own=False):
        return [fuse([sel_job(x, k0, nk, (bb, g), own) for g in range(NSA_GROUPS)]) for bb in range(qa_ref.shape[0])]

    early, late = tiles
    n_slots = (n_qt - 1) * TQ // TK
    n_static = n_slots - n_slots // 2
    n_late = late.t0 // TK
    c_late = late.t0 % TK // TQ
    plan = [cmp_job(c, u) for c in (late, early) for u in units]
    for u in units:
        plan += [lambda u=u: select_blocks(late, u, p_c[late.x, u]), win_job(late, u)]
    plan += sel_keys(late.x, late.t0, TQ, own=True)
    for s in range(n_static):
        plan += sel_keys(late.x, s * TK, TK)
        for u in units[s::n_static]:
            plan += [lambda u=u: select_blocks(early, u, p_c[early.x, u]), win_job(early, u)]
    plan += sel_keys(early.x, early.t0, TQ, own=True)
    for s in range(n_static, n_slots):
        is_late = s < n_late
        plan += sel_keys(jnp.where(is_late, late.x, early.x), jnp.where(is_late, s, s - n_late) * TK, TK)
    for s in range(TK // TQ - 1):
        is_late = s < c_late
        base = jnp.where(is_late, n_late, early.t0 // TK) * TK
        plan += sel_keys(jnp.where(is_late, late.x, early.x), base + jnp.where(is_late, s, s - c_late) * TQ, TQ)

    acc_scr[...] = jnp.zeros_like(acc_scr)
    m_scr[...] = jnp.full(m_scr.shape, NEG, F32)
    job_pos = [k for k, e in enumerate(plan) if isinstance(e, tuple)]
    following = dict(zip(job_pos, job_pos[1:]))
    issued = {job_pos[0]: plan[job_pos[0]][0]()}
    for k, entry in enumerate(plan):
        if not isinstance(entry, tuple):
            entry()
            continue
        if k in following:
            issued[following[k]] = plan[following[k]][0]()
        _, probs, finish = entry
        finish(*probs(issued.pop(k)))

    for c in tiles:
        for bb in range(qa_ref.shape[0]):
            glt = c.g_ref[bb]
            heads = []
            for g in range(NSA_GROUPS):
                acc = acc_scr[c.x, uidx[bb, g]]
                o_sel = acc[0:HEAD_DIM] * (1.0 / acc[HEAD_DIM:HEAD_DIM + 1])
                for r in range(R):
                    hh = R * g + r
                    sl = slice(r * TQ, (r + 1) * TQ)
                    heads.append(glt[3 * hh:3 * hh + 1] * o_cmp[c.x, (bb, g)][:, sl]
                                 + glt[3 * hh + 1:3 * hh + 2] * o_sel[:, sl]
                                 + glt[3 * hh + 2:3 * hh + 3] * o_win[c.x, (bb, g)][:, sl])
            c.o_ref[bb] = jnp.concatenate(heads, axis=0).T


def _nsa(qq, kcmp, vcmpt, ksa, vst, kw, vwt, glt, ovt):
    B, H, T, _ = qq.shape
    G = NSA_GROUPS
    NB = NSA_NB
    NCP = kcmp.shape[2]
    n_qt = T // TQ
    grid = (B // NB, n_qt // 2)
    k_blk = lambda n: pl.BlockSpec((NB, G, n, LANES), lambda b, i: (b, 0, 0, 0))
    vt_blk = lambda r, n: pl.BlockSpec((NB, G, r, n), lambda b, i: (b, 0, 0, 0))
    lo_tile = lambda b, i: i
    hi_tile = lambda b, i: n_qt - 1 - i
    q_blk = lambda tile: pl.BlockSpec((NB, H, TQ, LANES), lambda b, i: (b, 0, tile(b, i), 0))
    g_blk = lambda tile: pl.BlockSpec((NB, GATE_ROWS, TQ), lambda b, i: (b, 0, tile(b, i)))
    half = jax.ShapeDtypeStruct((B, T // 2, NSA_WIDTH), F32)
    return pl.pallas_call(
        _nsa_kernel,
        grid=grid,
        in_specs=[
            q_blk(lo_tile), q_blk(hi_tile),
            k_blk(NCP), vt_blk(HEAD_DIM, NCP), k_blk(T), vt_blk(V_ROWS, T), k_blk(T), vt_blk(V_ROWS, T),
            g_blk(lo_tile), g_blk(hi_tile),
            pl.BlockSpec(ovt.shape, lambda b, i: (0, 0)),
        ],
        out_specs=[pl.BlockSpec((NB, TQ, NSA_WIDTH), lambda b, i: (b, i, 0)),
                   pl.BlockSpec((NB, TQ, NSA_WIDTH), lambda b, i: (b, n_qt // 2 - 1 - i, 0))],
        out_shape=[half, half],
        scratch_shapes=[pltpu.VMEM((2, NB * G, LANES, HEADS_PER_GROUP * TQ), BF16),
                        pltpu.VMEM((2, NB * G, V_ROWS, HEADS_PER_GROUP * TQ), F32),
                        pltpu.VMEM((2, NB * G, 1, HEADS_PER_GROUP * TQ), F32)],
        compiler_params=pltpu.CompilerParams(
            dimension_semantics=("parallel", "arbitrary"), vmem_limit_bytes=VMEM_LIMIT),
        name="nsa",
    )(qq, qq, kcmp, vcmpt, ksa, vst, kw, vwt, glt, glt, ovt)


def _s5_prep_kernel(lre_ref, lim_ref, ldt_ref, bre_ref, bim_ref, are_ref, aim_ref, bbre_ref, bbim_ref):
    lre, lim = lre_ref[...], lim_ref[...]
    dt = jnp.exp(ldt_ref[...])
    mag = jnp.exp(lre * dt)
    a_re = mag * jnp.cos(lim * dt)
    a_im = mag * jnp.sin(lim * dt)
    den = lre * lre + lim * lim
    z_re = ((a_re - 1.0) * lre + a_im * lim) / den
    z_im = (a_im * lre - (a_re - 1.0) * lim) / den
    are_ref[...] = a_re
    aim_ref[...] = a_im
    bbre_ref[...] = z_re * bre_ref[...] - z_im * bim_ref[...]
    bbim_ref[...] = z_re * bim_ref[...] + z_im * bre_ref[...]


def _s5_prep(lre, lim, ldt, bre, bim):
    shp = jax.ShapeDtypeStruct(lre.shape, F32)
    return pl.pallas_call(_s5_prep_kernel, out_shape=[shp, shp, shp, shp], name="s5_prep")(lre, lim, ldt, bre, bim)


def _s5_weights_kernel(bbre_ref, bbim_ref, are_ref, aim_ref, crt_ref, cit_ref,
                       we_ref, wct_ref, tp_ref, alre_ref, alim_ref, bd_scr, arow_scr, tp_scr):
    gpb = S5_CPB // S5_GROUP
    for q in range(we_ref.shape[0]):
        r0 = S5_CPB * q
        bd_scr[...] = jnp.zeros_like(bd_scr)
        tp_scr[...] = jnp.zeros_like(tp_scr)
        for n, ref in enumerate((bbre_ref, bbim_ref, crt_ref, cit_ref)):
            for gl in range(gpb):
                bd_scr[n, S5_GROUP * gl:S5_GROUP * (gl + 1), S5_STATE * gl:S5_STATE * (gl + 1)] = (
                    ref[r0 + S5_GROUP * gl:r0 + S5_GROUP * (gl + 1), :])
        for n, ref in enumerate((are_ref, aim_ref)):
            for gl in range(gpb):
                arow_scr[n, :, S5_STATE * gl:S5_STATE * (gl + 1)] = ref[r0 + S5_GROUP * gl:r0 + S5_GROUP * gl + 1, :]
        bbre, bbim, crt, cit = bd_scr[0], bd_scr[1], bd_scr[2], bd_scr[3]
        are, aim = arow_scr[0], arow_scr[1]
        pre, pim = jnp.ones_like(are), jnp.zeros_like(are)
        for k in range(S5_L):
            bpr = bbre * pre - bbim * pim
            bpi = bbre * pim + bbim * pre
            i = S5_L - 1 - k
            we_ref[q, S5_CPB * i:S5_CPB * (i + 1), :] = jnp.concatenate([bpr, bpi], axis=1).astype(BF16)
            tap = _nt_dot(bpr.astype(BF16), crt.astype(BF16)) - _nt_dot(bpi.astype(BF16), cit.astype(BF16))
            for i in range(S5_L - k):
                j = i + k
                tp_scr[S5_CPB * i:S5_CPB * (i + 1), S5_CPB * j:S5_CPB * (j + 1)] = tap
            pre, pim = pre * are - pim * aim, pre * aim + pim * are
            wct_ref[q, S5_CPB * k:S5_CPB * (k + 1), :] = jnp.concatenate(
                [crt * pre - cit * pim, -(crt * pim + cit * pre)], axis=1).astype(BF16)
        tp_ref[q] = tp_scr[...].astype(BF16)
        alre_ref[q] = pre
        alim_ref[q] = pim


def _s5_weights(bbre, bbim, are, aim, crt, cit):
    nblk = S5_WIDTH // S5_CPB
    sb = S5_CPB // S5_GROUP * S5_STATE
    lc = S5_L * S5_CPB
    bps = S5W_BLOCKS_PER_STEP
    rows = pl.BlockSpec((bps * S5_CPB, S5_STATE), lambda q: (q, 0))
    blk = lambda r, c: pl.BlockSpec((bps, r, c), lambda q: (q, 0, 0))
    return pl.pallas_call(
        _s5_weights_kernel,
        grid=(nblk // bps,),
        in_specs=[rows] * 6,
        out_specs=[blk(lc, 2 * sb), blk(lc, 2 * sb), blk(lc, lc), blk(1, sb), blk(1, sb)],
        scratch_shapes=[pltpu.VMEM((4, S5_CPB, sb), F32), pltpu.VMEM((2, 1, sb), F32), pltpu.VMEM((lc, lc), F32)],
        out_shape=[jax.ShapeDtypeStruct((nblk, lc, 2 * sb), BF16),
                   jax.ShapeDtypeStruct((nblk, lc, 2 * sb), BF16),
                   jax.ShapeDtypeStruct((nblk, lc, lc), BF16),
                   jax.ShapeDtypeStruct((nblk, 1, sb), F32), jax.ShapeDtypeStruct((nblk, 1, sb), F32)],
        compiler_params=pltpu.CompilerParams(dimension_semantics=("parallel",), vmem_limit_bytes=VMEM_LIMIT),
        name="s5_weights",
    )(bbre, bbim, are, aim, crt, cit)


def _s5_kernel(u_ref, we_ref, wct_ref, tp_ref, alre_ref, alim_ref, d_ref, wg_ref, bg_ref, o_ref, e_scr, st_scr):
    nb, ct, _ = u_ref.shape
    L = S5_L
    nblk = S5_WIDTH // S5_CPB
    sb = S5_CPB // S5_GROUP * S5_STATE
    spb = 2 * sb // LANES
    W = S5_SCAN_SLABS * LANES

    @pl.when(pl.program_id(0) == 0)
    def _():
        st_scr[...] = jnp.zeros_like(st_scr)

    uf = u_ref[...].reshape(nb * ct, L * S5_WIDTH)
    ub = uf.astype(BF16)
    u_blk = [jnp.concatenate([ub[:, i * S5_WIDTH + S5_CPB * q:i * S5_WIDTH + S5_CPB * (q + 1)] for i in range(L)],
                             axis=1) for q in range(nblk)]

    for q in range(nblk):
        e = jnp.dot(u_blk[q], we_ref[q], preferred_element_type=F32)
        for s in range(spb):
            for b in range(nb):
                e_scr[spb * q + s, pl.ds(b, ct, stride=nb), :] = e[b * ct:(b + 1) * ct, LANES * s:LANES * (s + 1)]

    def slabs_of(cs):
        q, r = divmod(cs * LANES, sb)
        return spb * q + r // LANES, spb * q + (sb + r) // LANES

    for j in range(S5_GROUPS * S5_STATE // W):
        sl = [slabs_of(S5_SCAN_SLABS * j + n) for n in range(S5_SCAN_SLABS)]
        sl_re, sl_im = [s[0] for s in sl], [s[1] for s in sl]
        ar = alre_ref[:, W * j:W * (j + 1)]
        ai = alim_ref[:, W * j:W * (j + 1)]
        load = lambda r0, rows, slabs: jnp.concatenate([e_scr[s, pl.ds(r0, rows), :] for s in slabs], axis=1)
        state = lambda slabs: jnp.concatenate([st_scr[:, LANES * s:LANES * (s + 1)] for s in slabs], axis=1)
        sr, si = state(sl_re), state(sl_im)
        for k in range(ct // 2):
            r0 = k * 2 * nb
            er, ei = load(r0, 2 * nb, sl_re), load(r0, 2 * nb, sl_im)
            tr = ar * sr - ai * si + er[0:nb]
            ti = ar * si + ai * sr + ei[0:nb]
            xr = jnp.concatenate([sr, tr], axis=0)
            xi = jnp.concatenate([si, ti], axis=0)
            for n in range(S5_SCAN_SLABS):
                e_scr[sl_re[n], pl.ds(r0, 2 * nb), :] = xr[:, LANES * n:LANES * (n + 1)]
                e_scr[sl_im[n], pl.ds(r0, 2 * nb), :] = xi[:, LANES * n:LANES * (n + 1)]
            sr, si = ar * tr - ai * ti + er[nb:], ar * ti + ai * tr + ei[nb:]
        for n in range(S5_SCAN_SLABS):
            st_scr[:, LANES * sl_re[n]:LANES * (sl_re[n] + 1)] = sr[:, LANES * n:LANES * (n + 1)]
            st_scr[:, LANES * sl_im[n]:LANES * (sl_im[n] + 1)] = si[:, LANES * n:LANES * (n + 1)]

    ys = []
    for q in range(nblk):
        per_b = [jnp.concatenate([e_scr[spb * q + s, pl.ds(b, ct, stride=nb), :] for s in range(spb)], axis=1)
                 for b in range(nb)]
        x_in = jnp.concatenate(per_b, axis=0).astype(BF16)
        ys.append((_nt_dot(x_in, wct_ref[q]) + jnp.dot(u_blk[q], tp_ref[q], preferred_element_type=F32)).astype(BF16))
    outs = []
    for j in range(L):
        yj = jnp.concatenate([y[:, S5_CPB * j:S5_CPB * (j + 1)] for y in ys], axis=1).astype(F32)
        yj = yj + d_ref[...] * uf[:, j * S5_WIDTH:(j + 1) * S5_WIDTH]
        z = _gelu_tanh(yj)
        gate = jnp.dot(z.astype(BF16), wg_ref[...], preferred_element_type=F32) + bg_ref[...]
        outs.append(z * _sigmoid(gate))
    o_ref[...] = jnp.concatenate(outs, axis=1).reshape(nb, ct, L * S5_WIDTH)


def _s5(u4, we, wct, tp, alre, alim, d, w_glu, b_glu):
    nb, nchunks, w4 = u4.shape
    ct = S5_CT
    n_slabs = 2 * S5_GROUPS * S5_STATE // LANES
    resident = lambda a: pl.BlockSpec(a.shape, lambda i: (0,) * a.ndim, pipeline_mode=pl.Buffered(1))
    u_blk = pl.BlockSpec((nb, ct, w4), lambda i: (0, i, 0))
    return pl.pallas_call(
        _s5_kernel,
        grid=(nchunks // ct,),
        in_specs=[u_blk, resident(we), resident(wct), resident(tp), resident(alre), resident(alim),
                  resident(d), resident(w_glu), resident(b_glu)],
        out_specs=u_blk,
        out_shape=jax.ShapeDtypeStruct(u4.shape, F32),
        scratch_shapes=[pltpu.VMEM((n_slabs, nb * ct, LANES), F32), pltpu.VMEM((nb, n_slabs * LANES), F32)],
        compiler_params=pltpu.CompilerParams(dimension_semantics=("arbitrary",), vmem_limit_bytes=VMEM_LIMIT),
        name="s5",
    )(u4, we, wct, tp, alre, alim, d, w_glu, b_glu)


def _final_kernel(x_ref, g_ref, wb_ref, onl_ref, onh_ref, os_ref, wpn_ref, wps_ref, wo_ref, fg_ref, o_ref, os_scr):
    tm = x_ref.shape[1]
    o4 = os_ref[0]
    for i in range(S5_L):
        for s in range(S5_WIDTH // LANES):
            c0 = i * S5_WIDTH + LANES * s
            os_scr[s, pl.ds(i, tm // S5_L, stride=S5_L), :] = o4[:, c0:c0 + LANES]
    first_half = pl.program_id(1) < pl.num_programs(1) // 2

    def silu(v):
        return v * _sigmoid(v)

    rows = [slice(r, r + tm // FINAL_SUB) for r in range(0, tm, tm // FINAL_SUB)]
    hs = [(_rms_scale(x_ref[0, r]) * g_ref[...]).astype(BF16) for r in rows]
    for r, h in zip(rows, hs):
        proj = lambda a, b, h=h: jnp.dot(h, wb_ref[:, a:b], preferred_element_type=F32)
        o_nsa = jnp.where(first_half, onl_ref[0, r], onh_ref[0, r])
        o_s5 = jnp.concatenate([os_scr[s, r] for s in range(S5_WIDTH // LANES)], axis=1)
        a_in = (o_nsa * silu(proj(_W_GN, _W_U))).astype(BF16)
        b_in = (o_s5 * silu(proj(_W_GS, _W_MG))).astype(BF16)
        branch_a = jnp.dot(a_in, wpn_ref[...], preferred_element_type=F32)
        branch_b = jnp.dot(b_in, wps_ref[...], preferred_element_type=F32)
        merged = (_sigmoid(proj(_W_MG, _W_MG + D_MODEL)) * branch_a
                  + _sigmoid(proj(_W_MG + D_MODEL, _W_END)) * branch_b)
        y = x_ref[0, r] + jnp.dot(merged.astype(BF16), wo_ref[...], preferred_element_type=F32)
        o_ref[0, r] = _rms_scale(y) * fg_ref[...]


def _final(x, norm_g, w_b, o_nsa_lo, o_nsa_hi, o_s5, wpn, wps, wo, final_g):
    B, T, D = x.shape
    tm = TM_PROJ
    nh = T // tm // 2
    row_blk = lambda w: pl.BlockSpec((1, tm, w), lambda b, i: (b, i, 0))
    full = lambda a: pl.BlockSpec(a.shape, lambda b, i: (0,) * a.ndim)
    return pl.pallas_call(
        _final_kernel,
        grid=(B, T // tm),
        in_specs=[row_blk(D), full(norm_g), pl.BlockSpec(w_b.shape, lambda b, i: (0, 0), pipeline_mode=pl.Buffered(1)),
                  pl.BlockSpec((1, tm, NSA_WIDTH), lambda b, i: (b, jnp.minimum(i, nh - 1), 0)),
                  pl.BlockSpec((1, tm, NSA_WIDTH), lambda b, i: (b, jnp.maximum(i - nh, 0), 0)),
                  pl.BlockSpec((1, tm // S5_L, S5_L * S5_WIDTH), lambda b, i: (b, i, 0)),
                  full(wpn), full(wps), full(wo), full(final_g)],
        out_specs=row_blk(D),
        out_shape=jax.ShapeDtypeStruct((B, T, D), F32),
        scratch_shapes=[pltpu.VMEM((S5_WIDTH // LANES, tm, LANES), F32)],
        compiler_params=pltpu.CompilerParams(
            dimension_semantics=("parallel", "arbitrary"), vmem_limit_bytes=VMEM_LIMIT),
        name="final",
    )(x, norm_g, w_b, o_nsa_lo, o_nsa_hi, o_s5, wpn, wps, wo, final_g)


def _rope_tables(T):
    half = HEAD_DIM // 2
    inv_freq = np.float32(ROPE_THETA) ** (-np.arange(half, dtype=np.float32) / np.float32(half))
    ang = np.arange(T, dtype=np.float32)[:, None] * inv_freq[None, :].astype(np.float32)
    cos, sin = np.cos(ang).astype(np.float32), np.sin(ang).astype(np.float32)
    cos2 = np.concatenate([cos, cos, cos, cos], axis=1)
    sin2 = np.concatenate([-sin, sin, -sin, sin], axis=1)
    return jnp.asarray(cos2), jnp.asarray(sin2)


def _compress_w1(w1):
    half_rows = CMP_STRIDE * HEAD_DIM
    return jnp.concatenate([w1[:half_rows], w1[half_rows:]], axis=1).astype(BF16)


def kernel(x, norm_g, w_in, cmp_pos_k, cmp_pos_v, cmp_w1_k, cmp_w2_k, cmp_w1_v, cmp_w2_v, s5_lam_re, s5_lam_im, s5_log_dt, s5_b_re, s5_b_im, s5_c_re, s5_c_im, s5_d, w_glu, b_glu, w_proj_nsa, w_proj_s5, w_out, final_g):
    B, T, D = x.shape
    assert w_in.shape[0] == 1, "single-layer block"
    NCH = T // CMP_STRIDE
    NS = T // SEL_BLOCK

    w = w_in[0]
    w_all = jnp.concatenate([w[:, :_OFF_GL], jnp.pad(w[:, _OFF_GL:_OFF_GN], ((0, 0), (0, LANES - 24))),
                             w[:, _OFF_GN:]], axis=1).astype(BF16)
    g2 = norm_g[0][None, :]
    cos2, sin2 = _rope_tables(T)

    qq, kc, vc, ksa, vst, kw, vwt, glt, u4 = _inproj(x, g2, w_all, cos2, sin2)

    w2k = jnp.concatenate([jnp.zeros_like(cmp_w2_k[0]), cmp_w2_k[0]], axis=1).astype(BF16)
    w2vt = cmp_w2_v[0].T.astype(BF16)
    pos_rows = lambda p: jnp.pad(p.reshape(2, CMP_STRIDE * HEAD_DIM), ((0, 2 * SUBLANES - 2), (0, 0)))
    kcmp, vcmpt = _compress(kc, vc, _compress_w1(cmp_w1_k[0]), _compress_w1(cmp_w1_v[0]), w2k, w2vt,
                            pos_rows(cmp_pos_k[0]), pos_rows(cmp_pos_v[0]))

    c_start = jnp.arange(NCH) * CMP_STRIDE
    s_start = jnp.arange(NS) * SEL_BLOCK
    ovt = ((c_start[None, :] < s_start[:, None] + SEL_BLOCK) & (c_start[None, :] + CMP_BLOCK > s_start[:, None])
           & (jnp.arange(NCH)[None, :] < NCH - 1)).astype(BF16)
    o_nsa_lo, o_nsa_hi = _nsa(qq, kcmp, vcmpt, ksa, vst, kw, vwt, glt, ovt)

    rep = lambda a: jnp.repeat(a, S5_GROUP, axis=0)
    tr = lambda b: b.transpose(0, 2, 1).reshape(S5_GROUPS * S5_GROUP, S5_STATE)
    a_re, a_im, bb_re, bb_im = _s5_prep(
        rep(s5_lam_re[0]), rep(s5_lam_im[0]),
        rep(jnp.broadcast_to(s5_log_dt[0][:, None], (S5_GROUPS, S5_STATE))),
        tr(s5_b_re[0]), tr(s5_b_im[0]))
    flat = lambda c: c.reshape(S5_GROUPS * S5_GROUP, S5_STATE)
    we, wct, tp, alre, alim = _s5_weights(bb_re, bb_im, a_re, a_im, flat(s5_c_re[0]), flat(s5_c_im[0]))
    o_s5 = _s5(u4, we, wct, tp, alre.reshape(1, -1), alim.reshape(1, -1), s5_d[0][None, :],
               w_glu[0].astype(BF16), b_glu[0][None, :])

    return _final(x, g2, w_all, o_nsa_lo, o_nsa_hi, o_s5, w_proj_nsa[0].astype(BF16), w_proj_s5[0].astype(BF16),
                  w_out[0].astype(BF16), final_g[None, :])
```

```python
import math

import jax
import jax.numpy as jnp
import numpy as np
from jax import lax
from jax.experimental import pallas as pl
from jax.experimental.pallas import tpu as pltpu

F32 = jnp.float32
BF16 = jnp.bfloat16

D_MODEL = 1024
NSA_HEADS = 8
NSA_GROUPS = 2
HEADS_PER_GROUP = 4
HEAD_DIM = 64
NSA_WIDTH = 512
CMP_BLOCK = 32
CMP_STRIDE = 16
CMP_HIDDEN = 256
SEL_BLOCK = 64
SEL_TOPK = 16
WINDOW = 512
ROPE_THETA = 10000.0
FORCED_SCORE = 1.0e4
NEG = -1.0e30
S5_WIDTH = 512
S5_GROUP = 16
S5_GROUPS = 32
S5_STATE = 64
RMS_EPS = 1.0e-6

LANES = 128
SUBLANES = 8
VMEM_LIMIT = 56 * 1024 * 1024

_OFF_GL = 1280
_OFF_GN = 1304
_W_GL, _W_GN, _W_U, _W_GS, _W_MG, _W_END = 1280, 1408, 1920, 2432, 2944, 4992

TM_PROJ = 512
TM_FINAL = 1024
FINAL_SUB = 4
TQ = 128
TK = 512
NSA_NB = 2
V_ROWS = 80
GATE_ROWS = 32
S5_L = 8
S5_CPB = 256 // S5_L
S5_CT = 64
S5W_BLOCKS_PER_STEP = 4
S5_SCAN_SLABS = 4


def _gelu_tanh(x):
    c = math.sqrt(2.0 / math.pi)
    return 0.5 * x * (1.0 + jnp.tanh(c * (x + 0.044715 * (x * x * x))))


def _sigmoid(x):
    return 1.0 / (1.0 + jnp.exp(-x))


def _rms_scale(xv):
    ms = jnp.mean(xv * xv, axis=-1, keepdims=True)
    return xv * lax.rsqrt(ms + RMS_EPS)


def _nt_dot(a, b):
    return lax.dot_general(a, b, (((1,), (1,)), ((), ())), preferred_element_type=F32)


def _inproj_kernel(x_ref, g_ref, w_ref, cos_ref, sin_ref,
                   qq_ref, kc_ref, vc_ref, ks_ref, vs_ref, kw_ref, vw_ref, gl_ref, u_ref, us_scr):
    h = (_rms_scale(x_ref[0]) * g_ref[...]).astype(BF16)
    cos2 = cos_ref[...]
    sin2 = sin_ref[...]
    lane = lax.broadcasted_iota(jnp.int32, cos2.shape, 1)
    first_half = (lane & (HEAD_DIM - 1)) < (HEAD_DIM // 2)
    low = lane < HEAD_DIM

    wide = {}

    def proj(a, b):
        for (s0, s1) in ((0, 512), (512, _W_GN), (_W_U, _W_GS)):
            if s0 <= a and b <= s1:
                if s0 not in wide:
                    wide[s0] = jnp.dot(h, w_ref[:, s0:s1], preferred_element_type=F32)
                return wide[s0][:, a - s0:b - s0]
        raise ValueError((a, b))

    def rope(xs):
        partner = jnp.where(first_half, pltpu.roll(xs, 96, 1), pltpu.roll(xs, 32, 1))
        return xs * cos2 + partner * sin2

    scale = HEAD_DIM ** -0.5 * math.log2(math.e)
    for i in range(NSA_HEADS // 2):
        xs = proj(LANES * i, LANES * (i + 1)) * scale
        xr = rope(xs)
        qq_ref[0, 2 * i] = jnp.where(low, xr, pltpu.roll(xs, 64, 1)).astype(BF16)
        qq_ref[0, 2 * i + 1] = jnp.where(low, pltpu.roll(xr, 64, 1), xs).astype(BF16)

    kc_ref[0] = proj(512, 640)
    vc_ref[0] = proj(640, 768)
    tm = cos2.shape[0]
    t_row = pl.program_id(1) * tm + lax.broadcasted_iota(jnp.int32, cos2.shape, 0)
    blk_onehot = jnp.where(lane - HEAD_DIM == t_row // SEL_BLOCK, 1.0, 0.0)
    ones_rows = jnp.where(lax.broadcasted_iota(jnp.int32, (V_ROWS - HEAD_DIM, tm), 0) == 0, 1.0, 0.0)
    for (off, k_out, v_out, k_pad) in ((768, ks_ref, vs_ref, blk_onehot), (1024, kw_ref, vw_ref, 0.0)):
        kr = rope(proj(off, off + LANES))
        k_out[0, 0] = jnp.where(low, kr, k_pad).astype(BF16)
        k_out[0, 1] = jnp.where(low, pltpu.roll(kr, 64, 1), k_pad).astype(BF16)
        vt = proj(off + LANES, off + 2 * LANES).T
        for g in range(NSA_GROUPS):
            v_out[0, g] = jnp.concatenate([vt[HEAD_DIM * g:HEAD_DIM * (g + 1)], ones_rows], axis=0).astype(BF16)
    gl_ref[0] = _sigmoid(proj(_W_GL, _W_GN)).T[0:GATE_ROWS]
    uv = proj(_W_U, _W_GS)
    for s in range(S5_WIDTH // LANES):
        us_scr[s] = uv[:, LANES * s:LANES * (s + 1)]
    for i in range(S5_L):
        for s in range(S5_WIDTH // LANES):
            c0 = i * S5_WIDTH + LANES * s
            u_ref[0, :, c0:c0 + LANES] = us_scr[s, pl.ds(i, tm // S5_L, stride=S5_L), :]


def _inproj(x, norm_g, w_a, cos2, sin2):
    B, T, D = x.shape
    tm = TM_PROJ
    grid = (B, T // tm)
    row_blk = lambda w: pl.BlockSpec((1, tm, w), lambda b, i: (b, i, 0))
    kv_blk = pl.BlockSpec((1, NSA_GROUPS, tm, LANES), lambda b, i: (b, 0, i, 0))
    kv_shape = jax.ShapeDtypeStruct((B, NSA_GROUPS, T, LANES), BF16)
    vt_blk = pl.BlockSpec((1, NSA_GROUPS, V_ROWS, tm), lambda b, i: (b, 0, 0, i))
    vt_shape = jax.ShapeDtypeStruct((B, NSA_GROUPS, V_ROWS, T), BF16)
    return pl.pallas_call(
        _inproj_kernel,
        grid=grid,
        in_specs=[
            row_blk(D),
            pl.BlockSpec((1, D), lambda b, i: (0, 0)),
            pl.BlockSpec(w_a.shape, lambda b, i: (0, 0), pipeline_mode=pl.Buffered(1)),
            pl.BlockSpec((tm, LANES), lambda b, i: (i, 0)),
            pl.BlockSpec((tm, LANES), lambda b, i: (i, 0)),
        ],
        out_specs=[
            pl.BlockSpec((1, NSA_HEADS, tm, LANES), lambda b, i: (b, 0, i, 0)),
            row_blk(LANES), row_blk(LANES),
            kv_blk, vt_blk, kv_blk, vt_blk,
            pl.BlockSpec((1, GATE_ROWS, tm), lambda b, i: (b, 0, i)),
            pl.BlockSpec((1, tm // S5_L, S5_L * S5_WIDTH), lambda b, i: (b, i, 0)),
        ],
        out_shape=[
            jax.ShapeDtypeStruct((B, NSA_HEADS, T, LANES), BF16),
            jax.ShapeDtypeStruct((B, T, LANES), F32), jax.ShapeDtypeStruct((B, T, LANES), F32),
            kv_shape, vt_shape, kv_shape, vt_shape,
            jax.ShapeDtypeStruct((B, GATE_ROWS, T), F32),
            jax.ShapeDtypeStruct((B, T // S5_L, S5_L * S5_WIDTH), F32),
        ],
        scratch_shapes=[pltpu.VMEM((S5_WIDTH // LANES, tm, LANES), F32)],
        compiler_params=pltpu.CompilerParams(
            dimension_semantics=("parallel", "arbitrary"), vmem_limit_bytes=VMEM_LIMIT),
        name="inproj",
    )(x, norm_g, w_a, cos2, sin2)


def _compress_kernel(kc_ref, vc_ref, w1k_ref, w1v_ref, w2k_ref, w2vt_ref, pbk_ref, pbv_ref, ko_ref, vo_ref, pb_scr):
    nch = ko_ref.shape[2]
    H = CMP_HIDDEN

    @pl.when(pl.program_id(0) == 0)
    def _():
        for n, (p_ref, w1_ref) in enumerate(((pbk_ref, w1k_ref), (pbv_ref, w1v_ref))):
            pw = jnp.dot(p_ref[...].astype(BF16), w1_ref[...], preferred_element_type=F32)
            pb_scr[n] = pw[0:1, 0:H] + pw[1:2, H:]

    def hidden(c_ref, w1_ref, n):
        acc = [jnp.zeros((nch, 2 * H), F32) for _ in range(NSA_GROUPS)]
        for j in range(CMP_STRIDE):
            rows = c_ref[0, pl.ds(j, nch, stride=CMP_STRIDE), :].astype(BF16)
            wj = w1_ref[HEAD_DIM * j:HEAD_DIM * (j + 1), :]
            for g in range(NSA_GROUPS):
                acc[g] = acc[g] + jnp.dot(rows[:, HEAD_DIM * g:HEAD_DIM * (g + 1)], wj, preferred_element_type=F32)
        return [_gelu_tanh(a[:, 0:H] + pltpu.roll(a[:, H:], nch - 1, 0) + pb_scr[n]).astype(BF16) for a in acc]

    hk = hidden(kc_ref, w1k_ref, 0)
    hv = hidden(vc_ref, w1v_ref, 1)
    for g in range(NSA_GROUPS):
        ko_ref[0, g] = jnp.dot(hk[g], w2k_ref[...], preferred_element_type=F32).astype(BF16)
        vo_ref[0, g] = _nt_dot(w2vt_ref[...], hv[g]).astype(BF16)


def _compress(kc, vc, w1k, w1v, w2k, w2vt, pbk, pbv):
    B, T, _ = kc.shape
    G = NSA_GROUPS
    nch = T // CMP_STRIDE
    c_blk = pl.BlockSpec((1, T, LANES), lambda b: (b, 0, 0))
    full = lambda a: pl.BlockSpec(a.shape, lambda b: (0,) * a.ndim)
    return pl.pallas_call(
        _compress_kernel,
        grid=(B,),
        in_specs=[c_blk, c_blk, full(w1k), full(w1v), full(w2k), full(w2vt), full(pbk), full(pbv)],
        out_specs=[pl.BlockSpec((1, G, nch, LANES), lambda b: (b, 0, 0, 0)),
                   pl.BlockSpec((1, G, HEAD_DIM, nch), lambda b: (b, 0, 0, 0))],
        out_shape=[jax.ShapeDtypeStruct((B, G, nch, LANES), BF16),
                   jax.ShapeDtypeStruct((B, G, HEAD_DIM, nch), BF16)],
        scratch_shapes=[pltpu.VMEM((2, 1, CMP_HIDDEN), F32)],
        compiler_params=pltpu.CompilerParams(dimension_semantics=("arbitrary",), vmem_limit_bytes=VMEM_LIMIT),
        name="compress",
    )(kc, vc, w1k, w1v, w2k, w2vt, pbk, pbv)


class _QTile:
    def __init__(self, x, t0, t_begin, t_end, q_ref, g_ref, o_ref, cols, n_wc):
        self.x, self.t0, self.q_ref, self.g_ref, self.o_ref = x, t0, q_ref, g_ref, o_ref
        self.window_inside = t_begin >= WINDOW
        self.ncp = t_end // CMP_STRIDE
        self.ns = t_end // SEL_BLOCK
        self.t_lane = t0 + (lax.broadcasted_iota(jnp.int32, (1, cols), 1) & (TQ - 1))
        c_end = lax.broadcasted_iota(jnp.int32, (self.ncp, cols), 0) * CMP_STRIDE + (CMP_BLOCK - 1)
        self.cmp_valid = c_end <= self.t_lane
        self.w_pos = [t0 - WINDOW + TQ * c for c in range(n_wc)]
        self.w_start = [pl.multiple_of(jnp.maximum(p, 0), TQ) for p in self.w_pos]


def _nsa_kernel(qa_ref, qb_ref, kc_ref, vct_ref, ksa_ref, vst_ref, kw_ref, vwt_ref, ga_ref, gb_ref, ovt_ref,
                oa_ref, ob_ref, qsel_scr, acc_scr, m_scr):
    units = [(bb, g) for bb in range(qa_ref.shape[0]) for g in range(NSA_GROUPS)]
    uidx = {u: i for i, u in enumerate(units)}
    n_qt = kw_ref.shape[2] // TQ
    R = HEADS_PER_GROUP
    cols = R * TQ
    NS = ovt_ref.shape[0]
    n_wc = (WINDOW + TQ) // TQ
    step = pl.program_id(1)
    t_mid = n_qt // 2 * TQ
    tiles = [_QTile(0, step * TQ, 0, t_mid, qa_ref, ga_ref, oa_ref, cols, n_wc),
             _QTile(1, (n_qt - 1 - step) * TQ, t_mid, n_qt * TQ, qb_ref, gb_ref, ob_ref, cols, n_wc)]
    sub8 = lax.broadcasted_iota(jnp.int32, (SUBLANES, TQ), 0)
    own_keys_visible = (lax.broadcasted_iota(jnp.int32, (TQ, cols), 0)
                        <= (lax.broadcasted_iota(jnp.int32, (1, cols), 1) & (TQ - 1)))

    q_t_cache = {}

    def group_q_t(c, u):
        if (c.x, u) not in q_t_cache:
            bb, g = u
            q = c.q_ref[bb, R * g:R * (g + 1)].reshape(cols, LANES)
            q_t_cache[c.x, u] = q.astype(F32).T.astype(BF16)
        return q_t_cache[c.x, u]

    def cmp_scores(c, u):
        return jnp.dot(kc_ref[u[0], u[1], 0:c.ncp, :], group_q_t(c, u), preferred_element_type=F32)

    def cmp_probs(c, s):
        s = jnp.where(c.cmp_valid, s, NEG)
        e = jnp.exp2(s - jnp.max(s, axis=0, keepdims=True))
        inv = 1.0 / jnp.maximum(jnp.sum(e, axis=0, keepdims=True), 1.0e-30)
        return e * jnp.where(c.t_lane >= CMP_BLOCK - 1, inv, 0.0)

    def win_scores(c, u):
        kw = jnp.concatenate([kw_ref[u[0], u[1], pl.ds(c.w_start[n], TQ), :] for n in range(n_wc)], axis=0)
        return jnp.dot(kw, group_q_t(c, u), preferred_element_type=F32)

    def win_probs(c, sw):
        parts = []
        for n in range(n_wc):
            sc = sw[TQ * n:TQ * (n + 1)]
            if n == 0:
                sc = jnp.where(own_keys_visible, NEG, sc)
            if n == n_wc - 1:
                sc = jnp.where(own_keys_visible, sc, NEG)
            elif not c.window_inside:
                sc = jnp.where(c.w_pos[n] >= 0, sc, NEG)
            parts.append(sc.astype(BF16))
        sw = jnp.concatenate(parts, axis=0)
        return jnp.exp2(sw - jnp.max(sw, axis=0, keepdims=True))

    def win_out(c, u, ew):
        vw = jnp.concatenate([vwt_ref[u[0], u[1], :, pl.ds(c.w_start[n], TQ)] for n in range(n_wc)], axis=1)
        ow = jnp.dot(vw, ew, preferred_element_type=F32)
        return ow[0:HEAD_DIM] * (1.0 / ow[HEAD_DIM:HEAD_DIM + 1])

    def select_blocks(c, u, p):
        ns = c.ns
        psum = p[:, 0:TQ] + p[:, TQ:2 * TQ] + p[:, 2 * TQ:3 * TQ] + p[:, 3 * TQ:4 * TQ]
        p_hi = psum.astype(BF16)
        p_lo = (psum - p_hi.astype(F32)).astype(BF16)
        ov = ovt_ref[0:ns, 0:c.ncp]
        imp = (jnp.dot(ov, p_hi, preferred_element_type=F32) + jnp.dot(ov, p_lo, preferred_element_type=F32))
        blk = lax.broadcasted_iota(jnp.int32, (ns, TQ), 0)
        t_l = c.t0 + lax.broadcasted_iota(jnp.int32, (ns, TQ), 1)
        cur = t_l // SEL_BLOCK
        imp = jnp.where(blk * SEL_BLOCK <= t_l, imp, -1.0)
        imp = jnp.where(blk == 0, FORCED_SCORE, imp)
        imp = jnp.where(blk == cur, FORCED_SCORE, imp)
        imp = jnp.where(blk == cur - 1, FORCED_SCORE, imp)
        nv = ns // SUBLANES
        imp8 = [imp[SUBLANES * j:SUBLANES * (j + 1)] for j in range(nv)]
        rank8 = [jnp.zeros((SUBLANES, TQ), F32) for _ in range(nv)]
        for mm in range(ns):
            row = imp[mm:mm + 1, :]
            jm = mm // SUBLANES
            for j in range(nv):
                if j < jm:
                    ahead = jnp.where(row > imp8[j], 1.0, 0.0)
                elif j > jm:
                    ahead = jnp.where(row >= imp8[j], 1.0, 0.0)
                else:
                    tie = jnp.where(sub8 > (mm % SUBLANES), 1.0, 0.0)
                    ahead = jnp.where(row > imp8[j], 1.0, 0.0) + jnp.where(row == imp8[j], tie, 0.0)
                rank8[j] = rank8[j] + ahead
        pen = jnp.where(jnp.concatenate(rank8, axis=0) < float(SEL_TOPK), 0.0, NEG)
        if ns < NS:
            pen = jnp.concatenate([pen, jnp.zeros((NS - ns, TQ), F32)], axis=0)
        qsel_scr[c.x, uidx[u]] = jnp.concatenate(
            [group_q_t(c, u)[0:HEAD_DIM], jnp.concatenate([pen.astype(BF16)] * R, axis=1)], axis=0)

    p_c, o_cmp, o_win = {}, {}, {}

    def cmp_job(c, u):
        def finish(p, _):
            p_c[c.x, u] = p
            o_cmp[c.x, u] = jnp.dot(vct_ref[u[0], u[1], :, 0:c.ncp], p.astype(BF16),
                                    preferred_element_type=F32)
        return (lambda: cmp_scores(c, u)), (lambda s: (cmp_probs(c, s), None)), finish

    def win_job(c, u):
        def finish(e_w, _):
            o_win[c.x, u] = win_out(c, u, e_w)
        return (lambda: win_scores(c, u)), (lambda s: (win_probs(c, s), None)), finish

    def sel_job(x, k0, nk, u, own):
        k0 = k0 if isinstance(k0, int) else pl.multiple_of(k0, TQ)

        def probs(sc):
            if own:
                sc = jnp.where(own_keys_visible, sc, NEG)
            sc = sc.astype(BF16)
            m_old = m_scr[x, uidx[u]]
            m_new = jnp.maximum(m_old, jnp.max(sc, axis=0, keepdims=True).astype(F32))
            m_scr[x, uidx[u]] = m_new
            return jnp.exp2(sc - m_new.astype(BF16)), jnp.exp2(m_old - m_new)

        def finish(pp, alpha):
            acc_scr[x, uidx[u]] = acc_scr[x, uidx[u]] * alpha + jnp.dot(
                vst_ref[u[0], u[1], :, pl.ds(k0, nk)], pp, preferred_element_type=F32)

        return (lambda: jnp.dot(ksa_ref[u[0], u[1], pl.ds(k0, nk), :], qsel_scr[x, uidx[u]],
                                preferred_element_type=F32)), probs, finish

    def fuse(js):
        return ((lambda: [j[0]() for j in js]),
                (lambda ss: ([j[1](s) for j, s in zip(js, ss)], None)),
                (lambda outs, _: [j[2](*o) for j, o in zip(js, outs)]))

    def sel_keys(x, k0, nk, own=False):
        return [fuse([sel_job(x, k0, nk, (bb, g), own) for g in range(NSA_GROUPS)]) for bb in range(qa_ref.shape[0])]

    early, late = tiles
    n_slots = (n_qt - 1) * TQ // TK
    n_static = n_slots - n_slots // 2
    n_late = late.t0 // TK
    c_late = late.t0 % TK // TQ
    plan = [cmp_job(c, u) for c in (late, early) for u in units]
    for u in units:
        plan += [lambda u=u: select_blocks(late, u, p_c[late.x, u]), win_job(late, u)]
    plan += sel_keys(late.x, late.t0, TQ, own=True)
    for s in range(n_static):
        plan += sel_keys(late.x, s * TK, TK)
        for u in units[s::n_static]:
            plan += [lambda u=u: select_blocks(early, u, p_c[early.x, u]), win_job(early, u)]
    plan += sel_keys(early.x, early.t0, TQ, own=True)
    for s in range(n_static, n_slots):
        is_late = s < n_late
        plan += sel_keys(jnp.where(is_late, late.x, early.x), jnp.where(is_late, s, s - n_late) * TK, TK)
    for s in range(TK // TQ - 1):
        is_late = s < c_late
        base = jnp.where(is_late, n_late, early.t0 // TK) * TK
        plan += sel_keys(jnp.where(is_late, late.x, early.x), base + jnp.where(is_late, s, s - c_late) * TQ, TQ)

    acc_scr[...] = jnp.zeros_like(acc_scr)
    m_scr[...] = jnp.full(m_scr.shape, NEG, F32)
    job_pos = [k for k, e in enumerate(plan) if isinstance(e, tuple)]
    following = dict(zip(job_pos, job_pos[1:]))
    issued = {job_pos[0]: plan[job_pos[0]][0]()}
    for k, entry in enumerate(plan):
        if not isinstance(entry, tuple):
            entry()
            continue
        if k in following:
            issued[following[k]] = plan[following[k]][0]()
        _, probs, finish = entry
        finish(*probs(issued.pop(k)))

    for c in tiles:
        for bb in range(qa_ref.shape[0]):
            glt = c.g_ref[bb]
            heads = []
            for g in range(NSA_GROUPS):
                acc = acc_scr[c.x, uidx[bb, g]]
                o_sel = acc[0:HEAD_DIM] * (1.0 / acc[HEAD_DIM:HEAD_DIM + 1])
                for r in range(R):
                    hh = R * g + r
                    sl = slice(r * TQ, (r + 1) * TQ)
                    heads.append(glt[3 * hh:3 * hh + 1] * o_cmp[c.x, (bb, g)][:, sl]
                                 + glt[3 * hh + 1:3 * hh + 2] * o_sel[:, sl]
                                 + glt[3 * hh + 2:3 * hh + 3] * o_win[c.x, (bb, g)][:, sl])
            c.o_ref[bb] = jnp.concatenate(heads, axis=0).T


def _nsa(qq, kcmp, vcmpt, ksa, vst, kw, vwt, glt, ovt):
    B, H, T, _ = qq.shape
    G = NSA_GROUPS
    NB = NSA_NB
    NCP = kcmp.shape[2]
    n_qt = T // TQ
    grid = (B // NB, n_qt // 2)
    k_blk = lambda n: pl.BlockSpec((NB, G, n, LANES), lambda b, i: (b, 0, 0, 0))
    vt_blk = lambda r, n: pl.BlockSpec((NB, G, r, n), lambda b, i: (b, 0, 0, 0))
    lo_tile = lambda b, i: i
    hi_tile = lambda b, i: n_qt - 1 - i
    q_blk = lambda tile: pl.BlockSpec((NB, H, TQ, LANES), lambda b, i: (b, 0, tile(b, i), 0))
    g_blk = lambda tile: pl.BlockSpec((NB, GATE_ROWS, TQ), lambda b, i: (b, 0, tile(b, i)))
    half = jax.ShapeDtypeStruct((B, T // 2, NSA_WIDTH), F32)
    return pl.pallas_call(
        _nsa_kernel,
        grid=grid,
        in_specs=[
            q_blk(lo_tile), q_blk(hi_tile),
            k_blk(NCP), vt_blk(HEAD_DIM, NCP), k_blk(T), vt_blk(V_ROWS, T), k_blk(T), vt_blk(V_ROWS, T),
            g_blk(lo_tile), g_blk(hi_tile),
            pl.BlockSpec(ovt.shape, lambda b, i: (0, 0)),
        ],
        out_specs=[pl.BlockSpec((NB, TQ, NSA_WIDTH), lambda b, i: (b, i, 0)),
                   pl.BlockSpec((NB, TQ, NSA_WIDTH), lambda b, i: (b, n_qt // 2 - 1 - i, 0))],
        out_shape=[half, half],
        scratch_shapes=[pltpu.VMEM((2, NB * G, LANES, HEADS_PER_GROUP * TQ), BF16),
                        pltpu.VMEM((2, NB * G, V_ROWS, HEADS_PER_GROUP * TQ), F32),
                        pltpu.VMEM((2, NB * G, 1, HEADS_PER_GROUP * TQ), F32)],
        compiler_params=pltpu.CompilerParams(
            dimension_semantics=("parallel", "arbitrary"), vmem_limit_bytes=VMEM_LIMIT),
        name="nsa",
    )(qq, qq, kcmp, vcmpt, ksa, vst, kw, vwt, glt, glt, ovt)


def _s5_prep_kernel(lre_ref, lim_ref, ldt_ref, bre_ref, bim_ref, are_ref, aim_ref, bbre_ref, bbim_ref):
    lre, lim = lre_ref[...], lim_ref[...]
    dt = jnp.exp(ldt_ref[...])
    mag = jnp.exp(lre * dt)
    a_re = mag * jnp.cos(lim * dt)
    a_im = mag * jnp.sin(lim * dt)
    den = lre * lre + lim * lim
    z_re = ((a_re - 1.0) * lre + a_im * lim) / den
    z_im = (a_im * lre - (a_re - 1.0) * lim) / den
    are_ref[...] = a_re
    aim_ref[...] = a_im
    bbre_ref[...] = z_re * bre_ref[...] - z_im * bim_ref[...]
    bbim_ref[...] = z_re * bim_ref[...] + z_im * bre_ref[...]


def _s5_prep(lre, lim, ldt, bre, bim):
    shp = jax.ShapeDtypeStruct(lre.shape, F32)
    return pl.pallas_call(_s5_prep_kernel, out_shape=[shp, shp, shp, shp], name="s5_prep")(lre, lim, ldt, bre, bim)


def _s5_weights_kernel(bbre_ref, bbim_ref, are_ref, aim_ref, crt_ref, cit_ref,
                       we_ref, wct_ref, tp_ref, alre_ref, alim_ref, bd_scr, arow_scr, tp_scr):
    gpb = S5_CPB // S5_GROUP
    for q in range(we_ref.shape[0]):
        r0 = S5_CPB * q
        bd_scr[...] = jnp.zeros_like(bd_scr)
        tp_scr[...] = jnp.zeros_like(tp_scr)
        for n, ref in enumerate((bbre_ref, bbim_ref, crt_ref, cit_ref)):
            for gl in range(gpb):
                bd_scr[n, S5_GROUP * gl:S5_GROUP * (gl + 1), S5_STATE * gl:S5_STATE * (gl + 1)] = (
                    ref[r0 + S5_GROUP * gl:r0 + S5_GROUP * (gl + 1), :])
        for n, ref in enumerate((are_ref, aim_ref)):
            for gl in range(gpb):
                arow_scr[n, :, S5_STATE * gl:S5_STATE * (gl + 1)] = ref[r0 + S5_GROUP * gl:r0 + S5_GROUP * gl + 1, :]
        bbre, bbim, crt, cit = bd_scr[0], bd_scr[1], bd_scr[2], bd_scr[3]
        are, aim = arow_scr[0], arow_scr[1]
        pre, pim = jnp.ones_like(are), jnp.zeros_like(are)
        for k in range(S5_L):
            bpr = bbre * pre - bbim * pim
            bpi = bbre * pim + bbim * pre
            i = S5_L - 1 - k
            we_ref[q, S5_CPB * i:S5_CPB * (i + 1), :] = jnp.concatenate([bpr, bpi], axis=1).astype(BF16)
            tap = _nt_dot(bpr.astype(BF16), crt.astype(BF16)) - _nt_dot(bpi.astype(BF16), cit.astype(BF16))
            for i in range(S5_L - k):
                j = i + k
                tp_scr[S5_CPB * i:S5_CPB * (i + 1), S5_CPB * j:S5_CPB * (j + 1)] = tap
            pre, pim = pre * are - pim * aim, pre * aim + pim * are
            wct_ref[q, S5_CPB * k:S5_CPB * (k + 1), :] = jnp.concatenate(
                [crt * pre - cit * pim, -(crt * pim + cit * pre)], axis=1).astype(BF16)
        tp_ref[q] = tp_scr[...].astype(BF16)
        alre_ref[q] = pre
        alim_ref[q] = pim


def _s5_weights(bbre, bbim, are, aim, crt, cit):
    nblk = S5_WIDTH // S5_CPB
    sb = S5_CPB // S5_GROUP * S5_STATE
    lc = S5_L * S5_CPB
    bps = S5W_BLOCKS_PER_STEP
    rows = pl.BlockSpec((bps * S5_CPB, S5_STATE), lambda q: (q, 0))
    blk = lambda r, c: pl.BlockSpec((bps, r, c), lambda q: (q, 0, 0))
    return pl.pallas_call(
        _s5_weights_kernel,
        grid=(nblk // bps,),
        in_specs=[rows] * 6,
        out_specs=[blk(lc, 2 * sb), blk(lc, 2 * sb), blk(lc, lc), blk(1, sb), blk(1, sb)],
        scratch_shapes=[pltpu.VMEM((4, S5_CPB, sb), F32), pltpu.VMEM((2, 1, sb), F32), pltpu.VMEM((lc, lc), F32)],
        out_shape=[jax.ShapeDtypeStruct((nblk, lc, 2 * sb), BF16),
                   jax.ShapeDtypeStruct((nblk, lc, 2 * sb), BF16),
                   jax.ShapeDtypeStruct((nblk, lc, lc), BF16),
                   jax.ShapeDtypeStruct((nblk, 1, sb), F32), jax.ShapeDtypeStruct((nblk, 1, sb), F32)],
        compiler_params=pltpu.CompilerParams(dimension_semantics=("parallel",), vmem_limit_bytes=VMEM_LIMIT),
        name="s5_weights",
    )(bbre, bbim, are, aim, crt, cit)


def _s5_kernel(u_ref, we_ref, wct_ref, tp_ref, alre_ref, alim_ref, d_ref, wg_ref, bg_ref, o_ref, e_scr, st_scr):
    nb, ct, _ = u_ref.shape
    L = S5_L
    nblk = S5_WIDTH // S5_CPB
    sb = S5_CPB // S5_GROUP * S5_STATE
    spb = 2 * sb // LANES
    W = S5_SCAN_SLABS * LANES

    @pl.when(pl.program_id(0) == 0)
    def _():
        st_scr[...] = jnp.zeros_like(st_scr)

    uf = u_ref[...].reshape(nb * ct, L * S5_WIDTH)
    ub = uf.astype(BF16)
    u_blk = [jnp.concatenate([ub[:, i * S5_WIDTH + S5_CPB * q:i * S5_WIDTH + S5_CPB * (q + 1)] for i in range(L)],
                             axis=1) for q in range(nblk)]

    for q in range(nblk):
        e = jnp.dot(u_blk[q], we_ref[q], preferred_element_type=F32)
        for s in range(spb):
            for b in range(nb):
                e_scr[spb * q + s, pl.ds(b, ct, stride=nb), :] = e[b * ct:(b + 1) * ct, LANES * s:LANES * (s + 1)]

    def slabs_of(cs):
        q, r = divmod(cs * LANES, sb)
        return spb * q + r // LANES, spb * q + (sb + r) // LANES

    for j in range(S5_GROUPS * S5_STATE // W):
        sl = [slabs_of(S5_SCAN_SLABS * j + n) for n in range(S5_SCAN_SLABS)]
        sl_re, sl_im = [s[0] for s in sl], [s[1] for s in sl]
        ar = alre_ref[:, W * j:W * (j + 1)]
        ai = alim_ref[:, W * j:W * (j + 1)]
        load = lambda r0, rows, slabs: jnp.concatenate([e_scr[s, pl.ds(r0, rows), :] for s in slabs], axis=1)
        state = lambda slabs: jnp.concatenate([st_scr[:, LANES * s:LANES * (s + 1)] for s in slabs], axis=1)
        sr, si = state(sl_re), state(sl_im)
        for k in range(ct // 2):
            r0 = k * 2 * nb
            er, ei = load(r0, 2 * nb, sl_re), load(r0, 2 * nb, sl_im)
            tr = ar * sr - ai * si + er[0:nb]
            ti = ar * si + ai * sr + ei[0:nb]
            xr = jnp.concatenate([sr, tr], axis=0)
            xi = jnp.concatenate([si, ti], axis=0)
            for n in range(S5_SCAN_SLABS):
                e_scr[sl_re[n], pl.ds(r0, 2 * nb), :] = xr[:, LANES * n:LANES * (n + 1)]
                e_scr[sl_im[n], pl.ds(r0, 2 * nb), :] = xi[:, LANES * n:LANES * (n + 1)]
            sr, si = ar * tr - ai * ti + er[nb:], ar * ti + ai * tr + ei[nb:]
        for n in range(S5_SCAN_SLABS):
            st_scr[:, LANES * sl_re[n]:LANES * (sl_re[n] + 1)] = sr[:, LANES * n:LANES * (n + 1)]
            st_scr[:, LANES * sl_im[n]:LANES * (sl_im[n] + 1)] = si[:, LANES * n:LANES * (n + 1)]

    ys = []
    for q in range(nblk):
        per_b = [jnp.concatenate([e_scr[spb * q + s, pl.ds(b, ct, stride=nb), :] for s in range(spb)], axis=1)
                 for b in range(nb)]
        x_in = jnp.concatenate(per_b, axis=0).astype(BF16)
        ys.append((_nt_dot(x_in, wct_ref[q]) + jnp.dot(u_blk[q], tp_ref[q], preferred_element_type=F32)).astype(BF16))
    outs = []
    for j in range(L):
        yj = jnp.concatenate([y[:, S5_CPB * j:S5_CPB * (j + 1)] for y in ys], axis=1).astype(F32)
        yj = yj + d_ref[...] * uf[:, j * S5_WIDTH:(j + 1) * S5_WIDTH]
        z = _gelu_tanh(yj)
        gate = jnp.dot(z.astype(BF16), wg_ref[...], preferred_element_type=F32) + bg_ref[...]
        outs.append(z * _sigmoid(gate))
    o_ref[...] = jnp.concatenate(outs, axis=1).reshape(nb, ct, L * S5_WIDTH)


def _s5(u4, we, wct, tp, alre, alim, d, w_glu, b_glu):
    nb, nchunks, w4 = u4.shape
    ct = S5_CT
    n_slabs = 2 * S5_GROUPS * S5_STATE // LANES
    resident = lambda a: pl.BlockSpec(a.shape, lambda i: (0,) * a.ndim, pipeline_mode=pl.Buffered(1))
    u_blk = pl.BlockSpec((nb, ct, w4), lambda i: (0, i, 0))
    return pl.pallas_call(
        _s5_kernel,
        grid=(nchunks // ct,),
        in_specs=[u_blk, resident(we), resident(wct), resident(tp), resident(alre), resident(alim),
                  resident(d), resident(w_glu), resident(b_glu)],
        out_specs=u_blk,
        out_shape=jax.ShapeDtypeStruct(u4.shape, F32),
        scratch_shapes=[pltpu.VMEM((n_slabs, nb * ct, LANES), F32), pltpu.VMEM((nb, n_slabs * LANES), F32)],
        compiler_params=pltpu.CompilerParams(dimension_semantics=("arbitrary",), vmem_limit_bytes=VMEM_LIMIT),
        name="s5",
    )(u4, we, wct, tp, alre, alim, d, w_glu, b_glu)


def _final_kernel(x_ref, g_ref, wb_ref, onl_ref, onh_ref, os_ref, wpn_ref, wps_ref, wo_ref, fg_ref, o_ref, os_scr):
    tm = x_ref.shape[1]
    o4 = os_ref[0]
    for i in range(S5_L):
        for s in range(S5_WIDTH // LANES):
            c0 = i * S5_WIDTH + LANES * s
            os_scr[s, pl.ds(i, tm // S5_L, stride=S5_L), :] = o4[:, c0:c0 + LANES]
    first_half = pl.program_id(1) < pl.num_programs(1) // 2

    def silu(v):
        return v * _sigmoid(v)

    rows = [slice(r, r + tm // FINAL_SUB) for r in range(0, tm, tm // FINAL_SUB)]
    hs = [(_rms_scale(x_ref[0, r]) * g_ref[...]).astype(BF16) for r in rows]
    for r, h in zip(rows, hs):
        proj = lambda a, b, h=h: jnp.dot(h, wb_ref[:, a:b], preferred_element_type=F32)
        o_nsa = jnp.where(first_half, onl_ref[0, r], onh_ref[0, r])
        o_s5 = jnp.concatenate([os_scr[s, r] for s in range(S5_WIDTH // LANES)], axis=1)
        a_in = (o_nsa * silu(proj(_W_GN, _W_U))).astype(BF16)
        b_in = (o_s5 * silu(proj(_W_GS, _W_MG))).astype(BF16)
        branch_a = jnp.dot(a_in, wpn_ref[...], preferred_element_type=F32)
        branch_b = jnp.dot(b_in, wps_ref[...], preferred_element_type=F32)
        merged = (_sigmoid(proj(_W_MG, _W_MG + D_MODEL)) * branch_a
                  + _sigmoid(proj(_W_MG + D_MODEL, _W_END)) * branch_b)
        y = x_ref[0, r] + jnp.dot(merged.astype(BF16), wo_ref[...], preferred_element_type=F32)
        o_ref[0, r] = _rms_scale(y) * fg_ref[...]


def _final(x, norm_g, w_b, o_nsa_lo, o_nsa_hi, o_s5, wpn, wps, wo, final_g):
    B, T, D = x.shape
    tm = TM_FINAL
    nh = T // tm // 2
    row_blk = lambda w: pl.BlockSpec((1, tm, w), lambda b, i: (b, i, 0))
    full = lambda a: pl.BlockSpec(a.shape, lambda b, i: (0,) * a.ndim, pipeline_mode=pl.Buffered(1))
    return pl.pallas_call(
        _final_kernel,
        grid=(B, T // tm),
        in_specs=[row_blk(D), full(norm_g), pl.BlockSpec(w_b.shape, lambda b, i: (0, 0), pipeline_mode=pl.Buffered(1)),
                  pl.BlockSpec((1, tm, NSA_WIDTH), lambda b, i: (b, jnp.minimum(i, nh - 1), 0)),
                  pl.BlockSpec((1, tm, NSA_WIDTH), lambda b, i: (b, jnp.maximum(i - nh, 0), 0)),
                  pl.BlockSpec((1, tm // S5_L, S5_L * S5_WIDTH), lambda b, i: (b, i, 0)),
                  full(wpn), full(wps), full(wo), full(final_g)],
        out_specs=row_blk(D),
        out_shape=jax.ShapeDtypeStruct((B, T, D), F32),
        scratch_shapes=[pltpu.VMEM((S5_WIDTH // LANES, tm, LANES), F32)],
        compiler_params=pltpu.CompilerParams(
            dimension_semantics=("parallel", "arbitrary"), vmem_limit_bytes=VMEM_LIMIT),
        name="final",
    )(x, norm_g, w_b, o_nsa_lo, o_nsa_hi, o_s5, wpn, wps, wo, final_g)


def _rope_tables(T):
    half = HEAD_DIM // 2
    inv_freq = np.float32(ROPE_THETA) ** (-np.arange(half, dtype=np.float32) / np.float32(half))
    ang = np.arange(T, dtype=np.float32)[:, None] * inv_freq[None, :].astype(np.float32)
    cos, sin = np.cos(ang).astype(np.float32), np.sin(ang).astype(np.float32)
    cos2 = np.concatenate([cos, cos, cos, cos], axis=1)
    sin2 = np.concatenate([-sin, sin, -sin, sin], axis=1)
    return jnp.asarray(cos2), jnp.asarray(sin2)


def _compress_w1(w1):
    half_rows = CMP_STRIDE * HEAD_DIM
    return jnp.concatenate([w1[:half_rows], w1[half_rows:]], axis=1).astype(BF16)


def kernel(x, norm_g, w_in, cmp_pos_k, cmp_pos_v, cmp_w1_k, cmp_w2_k, cmp_w1_v, cmp_w2_v, s5_lam_re, s5_lam_im, s5_log_dt, s5_b_re, s5_b_im, s5_c_re, s5_c_im, s5_d, w_glu, b_glu, w_proj_nsa, w_proj_s5, w_out, final_g):
    B, T, D = x.shape
    assert w_in.shape[0] == 1, "single-layer block"
    NCH = T // CMP_STRIDE
    NS = T // SEL_BLOCK

    w = w_in[0]
    w_all = jnp.concatenate([w[:, :_OFF_GL], jnp.pad(w[:, _OFF_GL:_OFF_GN], ((0, 0), (0, LANES - 24))),
                             w[:, _OFF_GN:]], axis=1).astype(BF16)
    g2 = norm_g[0][None, :]
    cos2, sin2 = _rope_tables(T)

    qq, kc, vc, ksa, vst, kw, vwt, glt, u4 = _inproj(x, g2, w_all, cos2, sin2)

    w2k = jnp.concatenate([jnp.zeros_like(cmp_w2_k[0]), cmp_w2_k[0]], axis=1).astype(BF16)
    w2vt = cmp_w2_v[0].T.astype(BF16)
    pos_rows = lambda p: jnp.pad(p.reshape(2, CMP_STRIDE * HEAD_DIM), ((0, 2 * SUBLANES - 2), (0, 0)))
    kcmp, vcmpt = _compress(kc, vc, _compress_w1(cmp_w1_k[0]), _compress_w1(cmp_w1_v[0]), w2k, w2vt,
                            pos_rows(cmp_pos_k[0]), pos_rows(cmp_pos_v[0]))

    c_start = jnp.arange(NCH) * CMP_STRIDE
    s_start = jnp.arange(NS) * SEL_BLOCK
    ovt = ((c_start[None, :] < s_start[:, None] + SEL_BLOCK) & (c_start[None, :] + CMP_BLOCK > s_start[:, None])
           & (jnp.arange(NCH)[None, :] < NCH - 1)).astype(BF16)
    o_nsa_lo, o_nsa_hi = _nsa(qq, kcmp, vcmpt, ksa, vst, kw, vwt, glt, ovt)

    rep = lambda a: jnp.repeat(a, S5_GROUP, axis=0)
    tr = lambda b: b.transpose(0, 2, 1).reshape(S5_GROUPS * S5_GROUP, S5_STATE)
    a_re, a_im, bb_re, bb_im = _s5_prep(
        rep(s5_lam_re[0]), rep(s5_lam_im[0]),
        rep(jnp.broadcast_to(s5_log_dt[0][:, None], (S5_GROUPS, S5_STATE))),
        tr(s5_b_re[0]), tr(s5_b_im[0]))
    flat = lambda c: c.reshape(S5_GROUPS * S5_GROUP, S5_STATE)
    we, wct, tp, alre, alim = _s5_weights(bb_re, bb_im, a_re, a_im, flat(s5_c_re[0]), flat(s5_c_im[0]))
    o_s5 = _s5(u4, we, wct, tp, alre.reshape(1, -1), alim.reshape(1, -1), s5_d[0][None, :],
               w_glu[0].astype(BF16), b_glu[0][None, :])

    return _final(x, g2, w_all, o_nsa_lo, o_nsa_hi, o_s5, w_proj_nsa[0].astype(BF16), w_proj_s5[0].astype(BF16),
                  w_out[0].astype(BF16), final_g[None, :])
```

```python
import math

import jax
import jax.numpy as jnp
import numpy as np
from jax import lax
from jax.experimental import pallas as pl
from jax.experimental.pallas import tpu as pltpu

F32 = jnp.float32
BF16 = jnp.bfloat16

D_MODEL = 1024
NSA_HEADS = 8
NSA_GROUPS = 2
HEADS_PER_GROUP = 4
HEAD_DIM = 64
NSA_WIDTH = 512
CMP_BLOCK = 32
CMP_STRIDE = 16
CMP_HIDDEN = 256
SEL_BLOCK = 64
SEL_TOPK = 16
WINDOW = 512
ROPE_THETA = 10000.0
FORCED_SCORE = 1.0e4
NEG = -1.0e30
S5_WIDTH = 512
S5_GROUP = 16
S5_GROUPS = 32
S5_STATE = 64
RMS_EPS = 1.0e-6

LANES = 128
SUBLANES = 8
VMEM_MIB = {"inproj": 36, "compress": 24, "nsa": 48, "s5_weights": 16, "s5": 40, "final": 44}


def _vmem_limit(call):
    return VMEM_MIB[call] * 1024 * 1024

_OFF_GL = 1280
_OFF_GN = 1304
_W_GL, _W_GN, _W_U, _W_GS, _W_MG, _W_END = 1280, 1408, 1920, 2432, 2944, 4992

TM_PROJ = 512
FINAL_SUB = 2
TQ = 128
TK = 512
NSA_NB = 2
V_ROWS = 80
GATE_ROWS = 32
S5_L = 8
S5_CPB = 256 // S5_L
S5_CT = 64
S5W_BLOCKS_PER_STEP = 4
S5_SCAN_SLABS = 4


def _gelu_tanh(x):
    c = math.sqrt(2.0 / math.pi)
    return 0.5 * x * (1.0 + jnp.tanh(c * (x + 0.044715 * (x * x * x))))


def _sigmoid(x):
    return 1.0 / (1.0 + jnp.exp(-x))


def _rms_scale(xv):
    ms = jnp.mean(xv * xv, axis=-1, keepdims=True)
    return xv * lax.rsqrt(ms + RMS_EPS)


def _nt_dot(a, b):
    return lax.dot_general(a, b, (((1,), (1,)), ((), ())), preferred_element_type=F32)


def _inproj_kernel(x_ref, g_ref, w_ref, cos_ref, sin_ref,
                   qq_ref, kc_ref, vc_ref, ks_ref, vs_ref, kw_ref, vw_ref, gl_ref, u_ref, us_scr):
    h = (_rms_scale(x_ref[0]) * g_ref[...]).astype(BF16)
    cos2 = cos_ref[...]
    sin2 = sin_ref[...]
    lane = lax.broadcasted_iota(jnp.int32, cos2.shape, 1)
    first_half = (lane & (HEAD_DIM - 1)) < (HEAD_DIM // 2)
    low = lane < HEAD_DIM

    wide = {}

    def proj(a, b):
        for (s0, s1) in ((0, 512), (512, _W_GN), (_W_U, _W_GS)):
            if s0 <= a and b <= s1:
                if s0 not in wide:
                    wide[s0] = jnp.dot(h, w_ref[:, s0:s1], preferred_element_type=F32)
                return wide[s0][:, a - s0:b - s0]
        raise ValueError((a, b))

    def rope(xs):
        partner = jnp.where(first_half, pltpu.roll(xs, 96, 1), pltpu.roll(xs, 32, 1))
        return xs * cos2 + partner * sin2

    scale = HEAD_DIM ** -0.5 * math.log2(math.e)
    for i in range(NSA_HEADS // 2):
        xs = proj(LANES * i, LANES * (i + 1)) * scale
        xr = rope(xs)
        qq_ref[0, 2 * i] = jnp.where(low, xr, pltpu.roll(xs, 64, 1)).astype(BF16)
        qq_ref[0, 2 * i + 1] = jnp.where(low, pltpu.roll(xr, 64, 1), xs).astype(BF16)

    kc_ref[0] = proj(512, 640)
    vc_ref[0] = proj(640, 768)
    tm = cos2.shape[0]
    t_row = pl.program_id(1) * tm + lax.broadcasted_iota(jnp.int32, cos2.shape, 0)
    blk_onehot = jnp.where(lane - HEAD_DIM == t_row // SEL_BLOCK, 1.0, 0.0)
    ones_rows = jnp.where(lax.broadcasted_iota(jnp.int32, (V_ROWS - HEAD_DIM, tm), 0) == 0, 1.0, 0.0)
    for (off, k_out, v_out, k_pad) in ((768, ks_ref, vs_ref, blk_onehot), (1024, kw_ref, vw_ref, 0.0)):
        kr = rope(proj(off, off + LANES))
        k_out[0, 0] = jnp.where(low, kr, k_pad).astype(BF16)
        k_out[0, 1] = jnp.where(low, pltpu.roll(kr, 64, 1), k_pad).astype(BF16)
        vt = proj(off + LANES, off + 2 * LANES).T
        for g in range(NSA_GROUPS):
            v_out[0, g] = jnp.concatenate([vt[HEAD_DIM * g:HEAD_DIM * (g + 1)], ones_rows], axis=0).astype(BF16)
    gl_ref[0] = _sigmoid(proj(_W_GL, _W_GN)).T[0:GATE_ROWS]
    uv = proj(_W_U, _W_GS)
    for s in range(S5_WIDTH // LANES):
        us_scr[s] = uv[:, LANES * s:LANES * (s + 1)]
    for i in range(S5_L):
        for s in range(S5_WIDTH // LANES):
            c0 = i * S5_WIDTH + LANES * s
            u_ref[0, :, c0:c0 + LANES] = us_scr[s, pl.ds(i, tm // S5_L, stride=S5_L), :]


def _inproj(x, norm_g, w_a, cos2, sin2):
    B, T, D = x.shape
    tm = TM_PROJ
    grid = (B, T // tm)
    row_blk = lambda w: pl.BlockSpec((1, tm, w), lambda b, i: (b, i, 0))
    kv_blk = pl.BlockSpec((1, NSA_GROUPS, tm, LANES), lambda b, i: (b, 0, i, 0))
    kv_shape = jax.ShapeDtypeStruct((B, NSA_GROUPS, T, LANES), BF16)
    vt_blk = pl.BlockSpec((1, NSA_GROUPS, V_ROWS, tm), lambda b, i: (b, 0, 0, i))
    vt_shape = jax.ShapeDtypeStruct((B, NSA_GROUPS, V_ROWS, T), BF16)
    return pl.pallas_call(
        _inproj_kernel,
        grid=grid,
        in_specs=[
            row_blk(D),
            pl.BlockSpec((1, D), lambda b, i: (0, 0)),
            pl.BlockSpec(w_a.shape, lambda b, i: (0, 0), pipeline_mode=pl.Buffered(1)),
            pl.BlockSpec((tm, LANES), lambda b, i: (i, 0)),
            pl.BlockSpec((tm, LANES), lambda b, i: (i, 0)),
        ],
        out_specs=[
            pl.BlockSpec((1, NSA_HEADS, tm, LANES), lambda b, i: (b, 0, i, 0)),
            row_blk(LANES), row_blk(LANES),
            kv_blk, vt_blk, kv_blk, vt_blk,
            pl.BlockSpec((1, GATE_ROWS, tm), lambda b, i: (b, 0, i)),
            pl.BlockSpec((1, tm // S5_L, S5_L * S5_WIDTH), lambda b, i: (b, i, 0)),
        ],
        out_shape=[
            jax.ShapeDtypeStruct((B, NSA_HEADS, T, LANES), BF16),
            jax.ShapeDtypeStruct((B, T, LANES), F32), jax.ShapeDtypeStruct((B, T, LANES), F32),
            kv_shape, vt_shape, kv_shape, vt_shape,
            jax.ShapeDtypeStruct((B, GATE_ROWS, T), F32),
            jax.ShapeDtypeStruct((B, T // S5_L, S5_L * S5_WIDTH), F32),
        ],
        scratch_shapes=[pltpu.VMEM((S5_WIDTH // LANES, tm, LANES), F32)],
        compiler_params=pltpu.CompilerParams(
            dimension_semantics=("parallel", "arbitrary"), vmem_limit_bytes=_vmem_limit("inproj")),
        name="inproj",
    )(x, norm_g, w_a, cos2, sin2)


def _compress_kernel(kc_ref, vc_ref, w1k_ref, w1v_ref, w2k_ref, w2vt_ref, pbk_ref, pbv_ref, ko_ref, vo_ref, pb_scr):
    nch = ko_ref.shape[2]
    H = CMP_HIDDEN

    @pl.when(pl.program_id(0) == 0)
    def _():
        for n, (p_ref, w1_ref) in enumerate(((pbk_ref, w1k_ref), (pbv_ref, w1v_ref))):
            pw = jnp.dot(p_ref[...].astype(BF16), w1_ref[...], preferred_element_type=F32)
            pb_scr[n] = pw[0:1, 0:H] + pw[1:2, H:]

    def hidden(c_ref, w1_ref, n):
        acc = [jnp.zeros((nch, 2 * H), F32) for _ in range(NSA_GROUPS)]
        for j in range(CMP_STRIDE):
            rows = c_ref[0, pl.ds(j, nch, stride=CMP_STRIDE), :].astype(BF16)
            wj = w1_ref[HEAD_DIM * j:HEAD_DIM * (j + 1), :]
            for g in range(NSA_GROUPS):
                acc[g] = acc[g] + jnp.dot(rows[:, HEAD_DIM * g:HEAD_DIM * (g + 1)], wj, preferred_element_type=F32)
        return [_gelu_tanh(a[:, 0:H] + pltpu.roll(a[:, H:], nch - 1, 0) + pb_scr[n]).astype(BF16) for a in acc]

    hk = hidden(kc_ref, w1k_ref, 0)
    hv = hidden(vc_ref, w1v_ref, 1)
    for g in range(NSA_GROUPS):
        ko_ref[0, g] = jnp.dot(hk[g], w2k_ref[...], preferred_element_type=F32).astype(BF16)
        vo_ref[0, g] = _nt_dot(w2vt_ref[...], hv[g]).astype(BF16)


def _compress(kc, vc, w1k, w1v, w2k, w2vt, pbk, pbv):
    B, T, _ = kc.shape
    G = NSA_GROUPS
    nch = T // CMP_STRIDE
    c_blk = pl.BlockSpec((1, T, LANES), lambda b: (b, 0, 0))
    full = lambda a: pl.BlockSpec(a.shape, lambda b: (0,) * a.ndim)
    return pl.pallas_call(
        _compress_kernel,
        grid=(B,),
        in_specs=[c_blk, c_blk, full(w1k), full(w1v), full(w2k), full(w2vt), full(pbk), full(pbv)],
        out_specs=[pl.BlockSpec((1, G, nch, LANES), lambda b: (b, 0, 0, 0)),
                   pl.BlockSpec((1, G, HEAD_DIM, nch), lambda b: (b, 0, 0, 0))],
        out_shape=[jax.ShapeDtypeStruct((B, G, nch, LANES), BF16),
                   jax.ShapeDtypeStruct((B, G, HEAD_DIM, nch), BF16)],
        scratch_shapes=[pltpu.VMEM((2, 1, CMP_HIDDEN), F32)],
        compiler_params=pltpu.CompilerParams(dimension_semantics=("arbitrary",),
                                             vmem_limit_bytes=_vmem_limit("compress")),
        name="compress",
    )(kc, vc, w1k, w1v, w2k, w2vt, pbk, pbv)


class _QTile:
    def __init__(self, x, t0, t_begin, t_end, q_ref, g_ref, o_ref, cols, n_wc):
        self.x, self.t0, self.q_ref, self.g_ref, self.o_ref = x, t0, q_ref, g_ref, o_ref
        self.window_inside = t_begin >= WINDOW
        self.ncp = t_end // CMP_STRIDE
        self.ns = t_end // SEL_BLOCK
        self.t_lane = t0 + (lax.broadcasted_iota(jnp.int32, (1, cols), 1) & (TQ - 1))
        c_end = lax.broadcasted_iota(jnp.int32, (self.ncp, cols), 0) * CMP_STRIDE + (CMP_BLOCK - 1)
        self.cmp_valid = c_end <= self.t_lane
        self.w_pos = [t0 - WINDOW + TQ * c for c in range(n_wc)]
        self.w_start = [pl.multiple_of(jnp.maximum(p, 0), TQ) for p in self.w_pos]


def _nsa_kernel(qa_ref, qb_ref, kc_ref, vct_ref, ksa_ref, vst_ref, kw_ref, vwt_ref, ga_ref, gb_ref, ovt_ref,
                oa_ref, ob_ref, qsel_scr, acc_scr, m_scr):
    units = [(bb, g) for bb in range(qa_ref.shape[0]) for g in range(NSA_GROUPS)]
    uidx = {u: i for i, u in enumerate(units)}
    n_qt = kw_ref.shape[2] // TQ
    R = HEADS_PER_GROUP
    cols = R * TQ
    NS = ovt_ref.shape[0]
    n_wc = (WINDOW + TQ) // TQ
    step = pl.program_id(1)
    t_mid = n_qt // 2 * TQ
    tiles = [_QTile(0, step * TQ, 0, t_mid, qa_ref, ga_ref, oa_ref, cols, n_wc),
             _QTile(1, (n_qt - 1 - step) * TQ, t_mid, n_qt * TQ, qb_ref, gb_ref, ob_ref, cols, n_wc)]
    sub8 = lax.broadcasted_iota(jnp.int32, (SUBLANES, TQ), 0)
    own_keys_visible = (lax.broadcasted_iota(jnp.int32, (TQ, cols), 0)
                        <= (lax.broadcasted_iota(jnp.int32, (1, cols), 1) & (TQ - 1)))

    q_t_cache = {}

    def group_q_t(c, u):
        if (c.x, u) not in q_t_cache:
            bb, g = u
            q = c.q_ref[bb, R * g:R * (g + 1)].reshape(cols, LANES)
            q_t_cache[c.x, u] = q.astype(F32).T.astype(BF16)
        return q_t_cache[c.x, u]

    def cmp_scores(c, u):
        return jnp.dot(kc_ref[u[0], u[1], 0:c.ncp, :], group_q_t(c, u), preferred_element_type=F32)

    def cmp_probs(c, s):
        s = jnp.where(c.cmp_valid, s, NEG)
        e = jnp.exp2(s - jnp.max(s, axis=0, keepdims=True))
        inv = 1.0 / jnp.maximum(jnp.sum(e, axis=0, keepdims=True), 1.0e-30)
        return e * jnp.where(c.t_lane >= CMP_BLOCK - 1, inv, 0.0)

    def win_scores(c, u):
        kw = jnp.concatenate([kw_ref[u[0], u[1], pl.ds(c.w_start[n], TQ), :] for n in range(n_wc)], axis=0)
        return jnp.dot(kw, group_q_t(c, u), preferred_element_type=F32)

    def win_probs(c, sw):
        parts = []
        for n in range(n_wc):
            sc = sw[TQ * n:TQ * (n + 1)]
            if n == 0:
                sc = jnp.where(own_keys_visible, NEG, sc)
            if n == n_wc - 1:
                sc = jnp.where(own_keys_visible, sc, NEG)
            elif not c.window_inside:
                sc = jnp.where(c.w_pos[n] >= 0, sc, NEG)
            parts.append(sc.astype(BF16))
        sw = jnp.concatenate(parts, axis=0)
        return jnp.exp2(sw - jnp.max(sw, axis=0, keepdims=True))

    def win_out(c, u, ew):
        vw = jnp.concatenate([vwt_ref[u[0], u[1], :, pl.ds(c.w_start[n], TQ)] for n in range(n_wc)], axis=1)
        ow = jnp.dot(vw, ew, preferred_element_type=F32)
        return ow[0:HEAD_DIM] * (1.0 / ow[HEAD_DIM:HEAD_DIM + 1])

    def select_blocks(c, u, p):
        ns = c.ns
        psum = p[:, 0:TQ] + p[:, TQ:2 * TQ] + p[:, 2 * TQ:3 * TQ] + p[:, 3 * TQ:4 * TQ]
        p_hi = psum.astype(BF16)
        p_lo = (psum - p_hi.astype(F32)).astype(BF16)
        ov = ovt_ref[0:ns, 0:c.ncp]
        imp = (jnp.dot(ov, p_hi, preferred_element_type=F32) + jnp.dot(ov, p_lo, preferred_element_type=F32))
        blk = lax.broadcasted_iota(jnp.int32, (ns, TQ), 0)
        t_l = c.t0 + lax.broadcasted_iota(jnp.int32, (ns, TQ), 1)
        cur = t_l // SEL_BLOCK
        imp = jnp.where(blk * SEL_BLOCK <= t_l, imp, -1.0)
        imp = jnp.where(blk == 0, FORCED_SCORE, imp)
        imp = jnp.where(blk == cur, FORCED_SCORE, imp)
        imp = jnp.where(blk == cur - 1, FORCED_SCORE, imp)
        nv = ns // SUBLANES
        imp8 = [imp[SUBLANES * j:SUBLANES * (j + 1)] for j in range(nv)]
        rank8 = [jnp.zeros((SUBLANES, TQ), F32) for _ in range(nv)]
        for mm in range(ns):
            row = imp[mm:mm + 1, :]
            jm = mm // SUBLANES
            for j in range(nv):
                if j < jm:
                    ahead = jnp.where(row > imp8[j], 1.0, 0.0)
                elif j > jm:
                    ahead = jnp.where(row >= imp8[j], 1.0, 0.0)
                else:
                    tie = jnp.where(sub8 > (mm % SUBLANES), 1.0, 0.0)
                    ahead = jnp.where(row > imp8[j], 1.0, 0.0) + jnp.where(row == imp8[j], tie, 0.0)
                rank8[j] = rank8[j] + ahead
        pen = jnp.where(jnp.concatenate(rank8, axis=0) < float(SEL_TOPK), 0.0, NEG)
        if ns < NS:
            pen = jnp.concatenate([pen, jnp.zeros((NS - ns, TQ), F32)], axis=0)
        qsel_scr[c.x, uidx[u]] = jnp.concatenate(
            [group_q_t(c, u)[0:HEAD_DIM], jnp.concatenate([pen.astype(BF16)] * R, axis=1)], axis=0)

    p_c, o_cmp, o_win = {}, {}, {}

    def cmp_job(c, u):
        def finish(p, _):
            p_c[c.x, u] = p
            o_cmp[c.x, u] = jnp.dot(vct_ref[u[0], u[1], :, 0:c.ncp], p.astype(BF16),
                                    preferred_element_type=F32)
        return (lambda: cmp_scores(c, u)), (lambda s: (cmp_probs(c, s), None)), finish

    def win_job(c, u):
        def finish(e_w, _):
            o_win[c.x, u] = win_out(c, u, e_w)
        return (lambda: win_scores(c, u)), (lambda s: (win_probs(c, s), None)), finish

    def sel_job(x, k0, nk, u, own):
        k0 = k0 if isinstance(k0, int) else pl.multiple_of(k0, TQ)

        def probs(sc):
            if own:
                sc = jnp.where(own_keys_visible, sc, NEG)
            sc = sc.astype(BF16)
            m_old = m_scr[x, uidx[u]]
            m_new = jnp.maximum(m_old, jnp.max(sc, axis=0, keepdims=True).astype(F32))
            m_scr[x, uidx[u]] = m_new
            return jnp.exp2(sc - m_new.astype(BF16)), jnp.exp2(m_old - m_new)

        def finish(pp, alpha):
            acc_scr[x, uidx[u]] = acc_scr[x, uidx[u]] * alpha + jnp.dot(
                vst_ref[u[0], u[1], :, pl.ds(k0, nk)], pp, preferred_element_type=F32)

        return (lambda: jnp.dot(ksa_ref[u[0], u[1], pl.ds(k0, nk), :], qsel_scr[x, uidx[u]],
                                preferred_element_type=F32)), probs, finish

    def fuse(js):
        return ((lambda: [j[0]() for j in js]),
                (lambda ss: ([j[1](s) for j, s in zip(js, ss)], None)),
                (lambda outs, _: [j[2](*o) for j, o in zip(js, outs)]))

    def sel_keys(x, k0, nk, own=False):
        return [fuse([sel_job(x, k0, nk, (bb, g), own) for g in range(NSA_GROUPS)]) for bb in range(qa_ref.shape[0])]

    early, late = tiles
    n_slots = (n_qt - 1) * TQ // TK
    n_static = n_slots - n_slots // 2
    n_late = late.t0 // TK
    c_late = late.t0 % TK // TQ
    plan = [cmp_job(c, u) for c in (late, early) for u in units]
    for u in units:
        plan += [lambda u=u: select_blocks(late, u, p_c[late.x, u]), win_job(late, u)]
    plan += sel_keys(late.x, late.t0, TQ, own=True)
    for s in range(n_static):
        plan += sel_keys(late.x, s * TK, TK)
        for u in units[s::n_static]:
            plan += [lambda u=u: select_blocks(early, u, p_c[early.x, u]), win_job(early, u)]
    plan += sel_keys(early.x, early.t0, TQ, own=True)
    for s in range(n_static, n_slots):
        is_late = s < n_late
        plan += sel_keys(jnp.where(is_late, late.x, early.x), jnp.where(is_late, s, s - n_late) * TK, TK)
    for s in range(TK // TQ - 1):
        is_late = s < c_late
        base = jnp.where(is_late, n_late, early.t0 // TK) * TK
        plan += sel_keys(jnp.where(is_late, late.x, early.x), base + jnp.where(is_late, s, s - c_late) * TQ, TQ)

    acc_scr[...] = jnp.zeros_like(acc_scr)
    m_scr[...] = jnp.full(m_scr.shape, NEG, F32)
    job_pos = [k for k, e in enumerate(plan) if isinstance(e, tuple)]
    following = dict(zip(job_pos, job_pos[1:]))
    issued = {job_pos[0]: plan[job_pos[0]][0]()}
    for k, entry in enumerate(plan):
        if not isinstance(entry, tuple):
            entry()
            continue
        if k in following:
            issued[following[k]] = plan[following[k]][0]()
        _, probs, finish = entry
        finish(*probs(issued.pop(k)))

    for c in tiles:
        for bb in range(qa_ref.shape[0]):
            glt = c.g_ref[bb]
            heads = []
            for g in range(NSA_GROUPS):
                acc = acc_scr[c.x, uidx[bb, g]]
                o_sel = acc[0:HEAD_DIM] * (1.0 / acc[HEAD_DIM:HEAD_DIM + 1])
                for r in range(R):
                    hh = R * g + r
                    sl = slice(r * TQ, (r + 1) * TQ)
                    heads.append(glt[3 * hh:3 * hh + 1] * o_cmp[c.x, (bb, g)][:, sl]
                                 + glt[3 * hh + 1:3 * hh + 2] * o_sel[:, sl]
                                 + glt[3 * hh + 2:3 * hh + 3] * o_win[c.x, (bb, g)][:, sl])
            c.o_ref[bb] = jnp.concatenate(heads, axis=0).T


def _nsa(qq, kcmp, vcmpt, ksa, vst, kw, vwt, glt, ovt):
    B, H, T, _ = qq.shape
    G = NSA_GROUPS
    NB = NSA_NB
    NCP = kcmp.shape[2]
    n_qt = T // TQ
    grid = (B // NB, n_qt // 2)
    k_blk = lambda n: pl.BlockSpec((NB, G, n, LANES), lambda b, i: (b, 0, 0, 0))
    vt_blk = lambda r, n: pl.BlockSpec((NB, G, r, n), lambda b, i: (b, 0, 0, 0))
    lo_tile = lambda b, i: i
    hi_tile = lambda b, i: n_qt - 1 - i
    q_blk = lambda tile: pl.BlockSpec((NB, H, TQ, LANES), lambda b, i: (b, 0, tile(b, i), 0))
    g_blk = lambda tile: pl.BlockSpec((NB, GATE_ROWS, TQ), lambda b, i: (b, 0, tile(b, i)))
    half = jax.ShapeDtypeStruct((B, T // 2, NSA_WIDTH), F32)
    return pl.pallas_call(
        _nsa_kernel,
        grid=grid,
        in_specs=[
            q_blk(lo_tile), q_blk(hi_tile),
            k_blk(NCP), vt_blk(HEAD_DIM, NCP), k_blk(T), vt_blk(V_ROWS, T), k_blk(T), vt_blk(V_ROWS, T),
            g_blk(lo_tile), g_blk(hi_tile),
            pl.BlockSpec(ovt.shape, lambda b, i: (0, 0)),
        ],
        out_specs=[pl.BlockSpec((NB, TQ, NSA_WIDTH), lambda b, i: (b, i, 0)),
                   pl.BlockSpec((NB, TQ, NSA_WIDTH), lambda b, i: (b, n_qt // 2 - 1 - i, 0))],
        out_shape=[half, half],
        scratch_shapes=[pltpu.VMEM((2, NB * G, LANES, HEADS_PER_GROUP * TQ), BF16),
                        pltpu.VMEM((2, NB * G, V_ROWS, HEADS_PER_GROUP * TQ), F32),
                        pltpu.VMEM((2, NB * G, 1, HEADS_PER_GROUP * TQ), F32)],
        compiler_params=pltpu.CompilerParams(
            dimension_semantics=("parallel", "arbitrary"), vmem_limit_bytes=_vmem_limit("nsa")),
        name="nsa",
    )(qq, qq, kcmp, vcmpt, ksa, vst, kw, vwt, glt, glt, ovt)


def _s5_prep_kernel(lre_ref, lim_ref, ldt_ref, bre_ref, bim_ref, are_ref, aim_ref, bbre_ref, bbim_ref):
    lre, lim = lre_ref[...], lim_ref[...]
    dt = jnp.exp(ldt_ref[...])
    mag = jnp.exp(lre * dt)
    a_re = mag * jnp.cos(lim * dt)
    a_im = mag * jnp.sin(lim * dt)
    den = lre * lre + lim * lim
    z_re = ((a_re - 1.0) * lre + a_im * lim) / den
    z_im = (a_im * lre - (a_re - 1.0) * lim) / den
    are_ref[...] = a_re
    aim_ref[...] = a_im
    bbre_ref[...] = z_re * bre_ref[...] - z_im * bim_ref[...]
    bbim_ref[...] = z_re * bim_ref[...] + z_im * bre_ref[...]


def _s5_prep(lre, lim, ldt, bre, bim):
    shp = jax.ShapeDtypeStruct(lre.shape, F32)
    return pl.pallas_call(_s5_prep_kernel, out_shape=[shp, shp, shp, shp], name="s5_prep")(lre, lim, ldt, bre, bim)


def _s5_weights_kernel(bbre_ref, bbim_ref, are_ref, aim_ref, crt_ref, cit_ref,
                       we_ref, wct_ref, tp_ref, alre_ref, alim_ref, bd_scr, arow_scr, tp_scr):
    gpb = S5_CPB // S5_GROUP
    for q in range(we_ref.shape[0]):
        r0 = S5_CPB * q
        bd_scr[...] = jnp.zeros_like(bd_scr)
        tp_scr[...] = jnp.zeros_like(tp_scr)
        for n, ref in enumerate((bbre_ref, bbim_ref, crt_ref, cit_ref)):
            for gl in range(gpb):
                bd_scr[n, S5_GROUP * gl:S5_GROUP * (gl + 1), S5_STATE * gl:S5_STATE * (gl + 1)] = (
                    ref[r0 + S5_GROUP * gl:r0 + S5_GROUP * (gl + 1), :])
        for n, ref in enumerate((are_ref, aim_ref)):
            for gl in range(gpb):
                arow_scr[n, :, S5_STATE * gl:S5_STATE * (gl + 1)] = ref[r0 + S5_GROUP * gl:r0 + S5_GROUP * gl + 1, :]
        bbre, bbim, crt, cit = bd_scr[0], bd_scr[1], bd_scr[2], bd_scr[3]
        are, aim = arow_scr[0], arow_scr[1]
        pre, pim = jnp.ones_like(are), jnp.zeros_like(are)
        for k in range(S5_L):
            bpr = bbre * pre - bbim * pim
            bpi = bbre * pim + bbim * pre
            i = S5_L - 1 - k
            we_ref[q, S5_CPB * i:S5_CPB * (i + 1), :] = jnp.concatenate([bpr, bpi], axis=1).astype(BF16)
            tap = _nt_dot(bpr.astype(BF16), crt.astype(BF16)) - _nt_dot(bpi.astype(BF16), cit.astype(BF16))
            for i in range(S5_L - k):
                j = i + k
                tp_scr[S5_CPB * i:S5_CPB * (i + 1), S5_CPB * j:S5_CPB * (j + 1)] = tap
            pre, pim = pre * are - pim * aim, pre * aim + pim * are
            wct_ref[q, S5_CPB * k:S5_CPB * (k + 1), :] = jnp.concatenate(
                [crt * pre - cit * pim, -(crt * pim + cit * pre)], axis=1).astype(BF16)
        tp_ref[q] = tp_scr[...].astype(BF16)
        alre_ref[q] = pre
        alim_ref[q] = pim


def _s5_weights(bbre, bbim, are, aim, crt, cit):
    nblk = S5_WIDTH // S5_CPB
    sb = S5_CPB // S5_GROUP * S5_STATE
    lc = S5_L * S5_CPB
    bps = S5W_BLOCKS_PER_STEP
    rows = pl.BlockSpec((bps * S5_CPB, S5_STATE), lambda q: (q, 0))
    blk = lambda r, c: pl.BlockSpec((bps, r, c), lambda q: (q, 0, 0))
    return pl.pallas_call(
        _s5_weights_kernel,
        grid=(nblk // bps,),
        in_specs=[rows] * 6,
        out_specs=[blk(lc, 2 * sb), blk(lc, 2 * sb), blk(lc, lc), blk(1, sb), blk(1, sb)],
        scratch_shapes=[pltpu.VMEM((4, S5_CPB, sb), F32), pltpu.VMEM((2, 1, sb), F32), pltpu.VMEM((lc, lc), F32)],
        out_shape=[jax.ShapeDtypeStruct((nblk, lc, 2 * sb), BF16),
                   jax.ShapeDtypeStruct((nblk, lc, 2 * sb), BF16),
                   jax.ShapeDtypeStruct((nblk, lc, lc), BF16),
                   jax.ShapeDtypeStruct((nblk, 1, sb), F32), jax.ShapeDtypeStruct((nblk, 1, sb), F32)],
        compiler_params=pltpu.CompilerParams(dimension_semantics=("parallel",),
                                             vmem_limit_bytes=_vmem_limit("s5_weights")),
        name="s5_weights",
    )(bbre, bbim, are, aim, crt, cit)


def _s5_kernel(u_ref, we_ref, wct_ref, tp_ref, alre_ref, alim_ref, d_ref, wg_ref, bg_ref, o_ref, e_scr, st_scr):
    nb, ct, _ = u_ref.shape
    L = S5_L
    nblk = S5_WIDTH // S5_CPB
    sb = S5_CPB // S5_GROUP * S5_STATE
    spb = 2 * sb // LANES
    W = S5_SCAN_SLABS * LANES

    @pl.when(pl.program_id(0) == 0)
    def _():
        st_scr[...] = jnp.zeros_like(st_scr)

    uf = u_ref[...].reshape(nb * ct, L * S5_WIDTH)
    ub = uf.astype(BF16)
    u_blk = [jnp.concatenate([ub[:, i * S5_WIDTH + S5_CPB * q:i * S5_WIDTH + S5_CPB * (q + 1)] for i in range(L)],
                             axis=1) for q in range(nblk)]

    for q in range(nblk):
        e = jnp.dot(u_blk[q], we_ref[q], preferred_element_type=F32)
        for s in range(spb):
            for b in range(nb):
                e_scr[spb * q + s, pl.ds(b, ct, stride=nb), :] = e[b * ct:(b + 1) * ct, LANES * s:LANES * (s + 1)]

    def slabs_of(cs):
        q, r = divmod(cs * LANES, sb)
        return spb * q + r // LANES, spb * q + (sb + r) // LANES

    for j in range(S5_GROUPS * S5_STATE // W):
        sl = [slabs_of(S5_SCAN_SLABS * j + n) for n in range(S5_SCAN_SLABS)]
        sl_re, sl_im = [s[0] for s in sl], [s[1] for s in sl]
        ar = alre_ref[:, W * j:W * (j + 1)]
        ai = alim_ref[:, W * j:W * (j + 1)]
        load = lambda r0, rows, slabs: jnp.concatenate([e_scr[s, pl.ds(r0, rows), :] for s in slabs], axis=1)
        state = lambda slabs: jnp.concatenate([st_scr[:, LANES * s:LANES * (s + 1)] for s in slabs], axis=1)
        sr, si = state(sl_re), state(sl_im)
        for k in range(ct // 2):
            r0 = k * 2 * nb
            er, ei = load(r0, 2 * nb, sl_re), load(r0, 2 * nb, sl_im)
            tr = ar * sr - ai * si + er[0:nb]
            ti = ar * si + ai * sr + ei[0:nb]
            xr = jnp.concatenate([sr, tr], axis=0)
            xi = jnp.concatenate([si, ti], axis=0)
            for n in range(S5_SCAN_SLABS):
                e_scr[sl_re[n], pl.ds(r0, 2 * nb), :] = xr[:, LANES * n:LANES * (n + 1)]
                e_scr[sl_im[n], pl.ds(r0, 2 * nb), :] = xi[:, LANES * n:LANES * (n + 1)]
            sr, si = ar * tr - ai * ti + er[nb:], ar * ti + ai * tr + ei[nb:]
        for n in range(S5_SCAN_SLABS):
            st_scr[:, LANES * sl_re[n]:LANES * (sl_re[n] + 1)] = sr[:, LANES * n:LANES * (n + 1)]
            st_scr[:, LANES * sl_im[n]:LANES * (sl_im[n] + 1)] = si[:, LANES * n:LANES * (n + 1)]

    ys = []
    for q in range(nblk):
        per_b = [jnp.concatenate([e_scr[spb * q + s, pl.ds(b, ct, stride=nb), :] for s in range(spb)], axis=1)
                 for b in range(nb)]
        x_in = jnp.concatenate(per_b, axis=0).astype(BF16)
        ys.append((_nt_dot(x_in, wct_ref[q]) + jnp.dot(u_blk[q], tp_ref[q], preferred_element_type=F32)).astype(BF16))
    outs = []
    for j in range(L):
        yj = jnp.concatenate([y[:, S5_CPB * j:S5_CPB * (j + 1)] for y in ys], axis=1).astype(F32)
        yj = yj + d_ref[...] * uf[:, j * S5_WIDTH:(j + 1) * S5_WIDTH]
        z = _gelu_tanh(yj)
        gate = jnp.dot(z.astype(BF16), wg_ref[...], preferred_element_type=F32) + bg_ref[...]
        outs.append(z * _sigmoid(gate))
    o_ref[...] = jnp.concatenate(outs, axis=1).reshape(nb, ct, L * S5_WIDTH)


def _s5(u4, we, wct, tp, alre, alim, d, w_glu, b_glu):
    nb, nchunks, w4 = u4.shape
    ct = S5_CT
    n_slabs = 2 * S5_GROUPS * S5_STATE // LANES
    resident = lambda a: pl.BlockSpec(a.shape, lambda i: (0,) * a.ndim, pipeline_mode=pl.Buffered(1))
    u_blk = pl.BlockSpec((nb, ct, w4), lambda i: (0, i, 0))
    return pl.pallas_call(
        _s5_kernel,
        grid=(nchunks // ct,),
        in_specs=[u_blk, resident(we), resident(wct), resident(tp), resident(alre), resident(alim),
                  resident(d), resident(w_glu), resident(b_glu)],
        out_specs=u_blk,
        out_shape=jax.ShapeDtypeStruct(u4.shape, F32),
        scratch_shapes=[pltpu.VMEM((n_slabs, nb * ct, LANES), F32), pltpu.VMEM((nb, n_slabs * LANES), F32)],
        compiler_params=pltpu.CompilerParams(dimension_semantics=("arbitrary",), vmem_limit_bytes=_vmem_limit("s5")),
        name="s5",
    )(u4, we, wct, tp, alre, alim, d, w_glu, b_glu)


def _final_kernel(x_ref, g_ref, wb_ref, onl_ref, onh_ref, os_ref, wpn_ref, wps_ref, wo_ref, fg_ref, o_ref, os_scr):
    tm = x_ref.shape[1]
    o4 = os_ref[0]
    for i in range(S5_L):
        for s in range(S5_WIDTH // LANES):
            c0 = i * S5_WIDTH + LANES * s
            os_scr[s, pl.ds(i, tm // S5_L, stride=S5_L), :] = o4[:, c0:c0 + LANES]
    first_half = pl.program_id(1) < pl.num_programs(1) // 2

    def silu(v):
        return v * _sigmoid(v)

    rows = [slice(r, r + tm // FINAL_SUB) for r in range(0, tm, tm // FINAL_SUB)]
    hs = [(_rms_scale(x_ref[0, r]) * g_ref[...]).astype(BF16) for r in rows]
    for r, h in zip(rows, hs):
        proj = lambda a, b, h=h: jnp.dot(h, wb_ref[:, a:b], preferred_element_type=F32)
        o_nsa = jnp.where(first_half, onl_ref[0, r], onh_ref[0, r])
        o_s5 = jnp.concatenate([os_scr[s, r] for s in range(S5_WIDTH // LANES)], axis=1)
        a_in = (o_nsa * silu(proj(_W_GN, _W_U))).astype(BF16)
        b_in = (o_s5 * silu(proj(_W_GS, _W_MG))).astype(BF16)
        branch_a = jnp.dot(a_in, wpn_ref[...], preferred_element_type=F32)
        branch_b = jnp.dot(b_in, wps_ref[...], preferred_element_type=F32)
        merged = (_sigmoid(proj(_W_MG, _W_MG + D_MODEL)) * branch_a
                  + _sigmoid(proj(_W_MG + D_MODEL, _W_END)) * branch_b)
        y = x_ref[0, r] + jnp.dot(merged.astype(BF16), wo_ref[...], preferred_element_type=F32)
        o_ref[0, r] = _rms_scale(y) * fg_ref[...]


def _final(x, norm_g, w_b, o_nsa_lo, o_nsa_hi, o_s5, wpn, wps, wo, final_g):
    B, T, D = x.shape
    tm = TM_PROJ
    nh = T // tm // 2
    row_blk = lambda w: pl.BlockSpec((1, tm, w), lambda b, i: (b, i, 0))
    full = lambda a: pl.BlockSpec(a.shape, lambda b, i: (0,) * a.ndim)
    return pl.pallas_call(
        _final_kernel,
        grid=(B, T // tm),
        in_specs=[row_blk(D), full(norm_g), pl.BlockSpec(w_b.shape, lambda b, i: (0, 0), pipeline_mode=pl.Buffered(1)),
                  pl.BlockSpec((1, tm, NSA_WIDTH), lambda b, i: (b, jnp.minimum(i, nh - 1), 0)),
                  pl.BlockSpec((1, tm, NSA_WIDTH), lambda b, i: (b, jnp.maximum(i - nh, 0), 0)),
                  pl.BlockSpec((1, tm // S5_L, S5_L * S5_WIDTH), lambda b, i: (b, i, 0)),
                  full(wpn), full(wps), full(wo), full(final_g)],
        out_specs=row_blk(D),
        out_shape=jax.ShapeDtypeStruct((B, T, D), F32),
        scratch_shapes=[pltpu.VMEM((S5_WIDTH // LANES, tm, LANES), F32)],
        compiler_params=pltpu.CompilerParams(
            dimension_semantics=("parallel", "arbitrary"), vmem_limit_bytes=_vmem_limit("final")),
        name="final",
    )(x, norm_g, w_b, o_nsa_lo, o_nsa_hi, o_s5, wpn, wps, wo, final_g)


def _rope_tables(T):
    half = HEAD_DIM // 2
    inv_freq = np.float32(ROPE_THETA) ** (-np.arange(half, dtype=np.float32) / np.float32(half))
    ang = np.arange(T, dtype=np.float32)[:, None] * inv_freq[None, :].astype(np.float32)
    cos, sin = np.cos(ang).astype(np.float32), np.sin(ang).astype(np.float32)
    cos2 = np.concatenate([cos, cos, cos, cos], axis=1)
    sin2 = np.concatenate([-sin, sin, -sin, sin], axis=1)
    return jnp.asarray(cos2), jnp.asarray(sin2)


def _compress_w1(w1):
    half_rows = CMP_STRIDE * HEAD_DIM
    return jnp.concatenate([w1[:half_rows], w1[half_rows:]], axis=1).astype(BF16)


def kernel(x, norm_g, w_in, cmp_pos_k, cmp_pos_v, cmp_w1_k, cmp_w2_k, cmp_w1_v, cmp_w2_v, s5_lam_re, s5_lam_im, s5_log_dt, s5_b_re, s5_b_im, s5_c_re, s5_c_im, s5_d, w_glu, b_glu, w_proj_nsa, w_proj_s5, w_out, final_g):
    B, T, D = x.shape
    assert w_in.shape[0] == 1, "single-layer block"
    NCH = T // CMP_STRIDE
    NS = T // SEL_BLOCK

    w = w_in[0]
    w_all = jnp.concatenate([w[:, :_OFF_GL], jnp.pad(w[:, _OFF_GL:_OFF_GN], ((0, 0), (0, LANES - 24))),
                             w[:, _OFF_GN:]], axis=1).astype(BF16)
    g2 = norm_g[0][None, :]
    cos2, sin2 = _rope_tables(T)

    qq, kc, vc, ksa, vst, kw, vwt, glt, u4 = _inproj(x, g2, w_all, cos2, sin2)

    w2k = jnp.concatenate([jnp.zeros_like(cmp_w2_k[0]), cmp_w2_k[0]], axis=1).astype(BF16)
    w2vt = cmp_w2_v[0].T.astype(BF16)
    pos_rows = lambda p: jnp.pad(p.reshape(2, CMP_STRIDE * HEAD_DIM), ((0, 2 * SUBLANES - 2), (0, 0)))
    kcmp, vcmpt = _compress(kc, vc, _compress_w1(cmp_w1_k[0]), _compress_w1(cmp_w1_v[0]), w2k, w2vt,
                            pos_rows(cmp_pos_k[0]), pos_rows(cmp_pos_v[0]))

    c_start = jnp.arange(NCH) * CMP_STRIDE
    s_start = jnp.arange(NS) * SEL_BLOCK
    ovt = ((c_start[None, :] < s_start[:, None] + SEL_BLOCK) & (c_start[None, :] + CMP_BLOCK > s_start[:, None])
           & (jnp.arange(NCH)[None, :] < NCH - 1)).astype(BF16)
    o_nsa_lo, o_nsa_hi = _nsa(qq, kcmp, vcmpt, ksa, vst, kw, vwt, glt, ovt)

    rep = lambda a: jnp.repeat(a, S5_GROUP, axis=0)
    tr = lambda b: b.transpose(0, 2, 1).reshape(S5_GROUPS * S5_GROUP, S5_STATE)
    a_re, a_im, bb_re, bb_im = _s5_prep(
        rep(s5_lam_re[0]), rep(s5_lam_im[0]),
        rep(jnp.broadcast_to(s5_log_dt[0][:, None], (S5_GROUPS, S5_STATE))),
        tr(s5_b_re[0]), tr(s5_b_im[0]))
    flat = lambda c: c.reshape(S5_GROUPS * S5_GROUP, S5_STATE)
    we, wct, tp, alre, alim = _s5_weights(bb_re, bb_im, a_re, a_im, flat(s5_c_re[0]), flat(s5_c_im[0]))
    o_s5 = _s5(u4, we, wct, tp, alre.reshape(1, -1), alim.reshape(1, -1), s5_d[0][None, :],
               w_glu[0].astype(BF16), b_glu[0][None, :])

    return _final(x, g2, w_all, o_nsa_lo, o_nsa_hi, o_s5, w_proj_nsa[0].astype(BF16), w_proj_s5[0].astype(BF16),
                  w_out[0].astype(BF16), final_g[None, :])
```

```python
import math

import jax
import jax.numpy as jnp
import numpy as np
from jax import lax
from jax.experimental import pallas as pl
from jax.experimental.pallas import tpu as pltpu

F32 = jnp.float32
BF16 = jnp.bfloat16

D_MODEL = 1024
NSA_HEADS = 8
NSA_GROUPS = 2
HEADS_PER_GROUP = 4
HEAD_DIM = 64
NSA_WIDTH = 512
CMP_BLOCK = 32
CMP_STRIDE = 16
CMP_HIDDEN = 256
SEL_BLOCK = 64
SEL_TOPK = 16
WINDOW = 512
ROPE_THETA = 10000.0
FORCED_SCORE = 1.0e4
NEG = -1.0e30
S5_WIDTH = 512
S5_GROUP = 16
S5_GROUPS = 32
S5_STATE = 64
RMS_EPS = 1.0e-6

LANES = 128
SUBLANES = 8
VMEM_LIMIT = 56 * 1024 * 1024
VMEM_LIMIT_MID = 40 * 1024 * 1024
VMEM_LIMIT_SMALL = 24 * 1024 * 1024

_OFF_GL = 1280
_OFF_GN = 1304
_W_GL, _W_GN, _W_U, _W_GS, _W_MG, _W_END = 1280, 1408, 1920, 2432, 2944, 4992

TM_PROJ = 512
FINAL_SUB = 2
TQ = 128
TK = 512
NSA_NB = 2
V_ROWS = 80
GATE_ROWS = 32
S5_L = 8
S5_CPB = 256 // S5_L
S5_CT = 64
S5W_BLOCKS_PER_STEP = 4
S5_SCAN_SLABS = 4


def _gelu_tanh(x):
    c = math.sqrt(2.0 / math.pi)
    return 0.5 * x * (1.0 + jnp.tanh(c * (x + 0.044715 * (x * x * x))))


def _sigmoid(x):
    return 1.0 / (1.0 + jnp.exp(-x))


def _rms_scale(xv):
    ms = jnp.mean(xv * xv, axis=-1, keepdims=True)
    return xv * lax.rsqrt(ms + RMS_EPS)


def _nt_dot(a, b):
    return lax.dot_general(a, b, (((1,), (1,)), ((), ())), preferred_element_type=F32)


def _inproj_kernel(x_ref, g_ref, w_ref, cos_ref, sin_ref,
                   qq_ref, kc_ref, vc_ref, ks_ref, vs_ref, kw_ref, vw_ref, gl_ref, u_ref, us_scr):
    h = (_rms_scale(x_ref[0]) * g_ref[...]).astype(BF16)
    cos2 = cos_ref[...]
    sin2 = sin_ref[...]
    lane = lax.broadcasted_iota(jnp.int32, cos2.shape, 1)
    first_half = (lane & (HEAD_DIM - 1)) < (HEAD_DIM // 2)
    low = lane < HEAD_DIM

    wide = {}

    def proj(a, b):
        for (s0, s1) in ((0, 512), (512, _W_GN), (_W_U, _W_GS)):
            if s0 <= a and b <= s1:
                if s0 not in wide:
                    wide[s0] = jnp.dot(h, w_ref[:, s0:s1], preferred_element_type=F32)
                return wide[s0][:, a - s0:b - s0]
        raise ValueError((a, b))

    def rope(xs):
        partner = jnp.where(first_half, pltpu.roll(xs, 96, 1), pltpu.roll(xs, 32, 1))
        return xs * cos2 + partner * sin2

    scale = HEAD_DIM ** -0.5 * math.log2(math.e)
    for i in range(NSA_HEADS // 2):
        xs = proj(LANES * i, LANES * (i + 1)) * scale
        xr = rope(xs)
        qq_ref[0, 2 * i] = jnp.where(low, xr, pltpu.roll(xs, 64, 1)).astype(BF16)
        qq_ref[0, 2 * i + 1] = jnp.where(low, pltpu.roll(xr, 64, 1), xs).astype(BF16)

    kc_ref[0] = proj(512, 640)
    vc_ref[0] = proj(640, 768)
    tm = cos2.shape[0]
    t_row = pl.program_id(1) * tm + lax.broadcasted_iota(jnp.int32, cos2.shape, 0)
    blk_onehot = jnp.where(lane - HEAD_DIM == t_row // SEL_BLOCK, 1.0, 0.0)
    ones_rows = jnp.where(lax.broadcasted_iota(jnp.int32, (V_ROWS - HEAD_DIM, tm), 0) == 0, 1.0, 0.0)
    for (off, k_out, v_out, k_pad) in ((768, ks_ref, vs_ref, blk_onehot), (1024, kw_ref, vw_ref, 0.0)):
        kr = rope(proj(off, off + LANES))
        k_out[0, 0] = jnp.where(low, kr, k_pad).astype(BF16)
        k_out[0, 1] = jnp.where(low, pltpu.roll(kr, 64, 1), k_pad).astype(BF16)
        vt = proj(off + LANES, off + 2 * LANES).T
        for g in range(NSA_GROUPS):
            v_out[0, g] = jnp.concatenate([vt[HEAD_DIM * g:HEAD_DIM * (g + 1)], ones_rows], axis=0).astype(BF16)
    gl_ref[0] = _sigmoid(proj(_W_GL, _W_GN)).T[0:GATE_ROWS]
    uv = proj(_W_U, _W_GS)
    for s in range(S5_WIDTH // LANES):
        us_scr[s] = uv[:, LANES * s:LANES * (s + 1)]
    for i in range(S5_L):
        for s in range(S5_WIDTH // LANES):
            c0 = i * S5_WIDTH + LANES * s
            u_ref[0, :, c0:c0 + LANES] = us_scr[s, pl.ds(i, tm // S5_L, stride=S5_L), :]


def _inproj(x, norm_g, w_a, cos2, sin2):
    B, T, D = x.shape
    tm = TM_PROJ
    grid = (B, T // tm)
    row_blk = lambda w: pl.BlockSpec((1, tm, w), lambda b, i: (b, i, 0))
    kv_blk = pl.BlockSpec((1, NSA_GROUPS, tm, LANES), lambda b, i: (b, 0, i, 0))
    kv_shape = jax.ShapeDtypeStruct((B, NSA_GROUPS, T, LANES), BF16)
    vt_blk = pl.BlockSpec((1, NSA_GROUPS, V_ROWS, tm), lambda b, i: (b, 0, 0, i))
    vt_shape = jax.ShapeDtypeStruct((B, NSA_GROUPS, V_ROWS, T), BF16)
    return pl.pallas_call(
        _inproj_kernel,
        grid=grid,
        in_specs=[
            row_blk(D),
            pl.BlockSpec((1, D), lambda b, i: (0, 0)),
            pl.BlockSpec(w_a.shape, lambda b, i: (0, 0), pipeline_mode=pl.Buffered(1)),
            pl.BlockSpec((tm, LANES), lambda b, i: (i, 0)),
            pl.BlockSpec((tm, LANES), lambda b, i: (i, 0)),
        ],
        out_specs=[
            pl.BlockSpec((1, NSA_HEADS, tm, LANES), lambda b, i: (b, 0, i, 0)),
            row_blk(LANES), row_blk(LANES),
            kv_blk, vt_blk, kv_blk, vt_blk,
            pl.BlockSpec((1, GATE_ROWS, tm), lambda b, i: (b, 0, i)),
            pl.BlockSpec((1, tm // S5_L, S5_L * S5_WIDTH), lambda b, i: (b, i, 0)),
        ],
        out_shape=[
            jax.ShapeDtypeStruct((B, NSA_HEADS, T, LANES), BF16),
            jax.ShapeDtypeStruct((B, T, LANES), F32), jax.ShapeDtypeStruct((B, T, LANES), F32),
            kv_shape, vt_shape, kv_shape, vt_shape,
            jax.ShapeDtypeStruct((B, GATE_ROWS, T), F32),
            jax.ShapeDtypeStruct((B, T // S5_L, S5_L * S5_WIDTH), F32),
        ],
        scratch_shapes=[pltpu.VMEM((S5_WIDTH // LANES, tm, LANES), F32)],
        compiler_params=pltpu.CompilerParams(
            dimension_semantics=("parallel", "arbitrary"), vmem_limit_bytes=VMEM_LIMIT),
        name="inproj",
    )(x, norm_g, w_a, cos2, sin2)


def _compress_kernel(kc_ref, vc_ref, w1k_ref, w1v_ref, w2k_ref, w2vt_ref, pbk_ref, pbv_ref, ko_ref, vo_ref, pb_scr):
    nch = ko_ref.shape[2]
    H = CMP_HIDDEN

    @pl.when(pl.program_id(0) == 0)
    def _():
        for n, (p_ref, w1_ref) in enumerate(((pbk_ref, w1k_ref), (pbv_ref, w1v_ref))):
            pw = jnp.dot(p_ref[...].astype(BF16), w1_ref[...], preferred_element_type=F32)
            pb_scr[n] = pw[0:1, 0:H] + pw[1:2, H:]

    def hidden(c_ref, w1_ref, n):
        acc = [jnp.zeros((nch, 2 * H), F32) for _ in range(NSA_GROUPS)]
        for j in range(CMP_STRIDE):
            rows = c_ref[0, pl.ds(j, nch, stride=CMP_STRIDE), :].astype(BF16)
            wj = w1_ref[HEAD_DIM * j:HEAD_DIM * (j + 1), :]
            for g in range(NSA_GROUPS):
                acc[g] = acc[g] + jnp.dot(rows[:, HEAD_DIM * g:HEAD_DIM * (g + 1)], wj, preferred_element_type=F32)
        return [_gelu_tanh(a[:, 0:H] + pltpu.roll(a[:, H:], nch - 1, 0) + pb_scr[n]).astype(BF16) for a in acc]

    hk = hidden(kc_ref, w1k_ref, 0)
    hv = hidden(vc_ref, w1v_ref, 1)
    for g in range(NSA_GROUPS):
        ko_ref[0, g] = jnp.dot(hk[g], w2k_ref[...], preferred_element_type=F32).astype(BF16)
        vo_ref[0, g] = _nt_dot(w2vt_ref[...], hv[g]).astype(BF16)


def _compress(kc, vc, w1k, w1v, w2k, w2vt, pbk, pbv):
    B, T, _ = kc.shape
    G = NSA_GROUPS
    nch = T // CMP_STRIDE
    c_blk = pl.BlockSpec((1, T, LANES), lambda b: (b, 0, 0))
    full = lambda a: pl.BlockSpec(a.shape, lambda b: (0,) * a.ndim)
    return pl.pallas_call(
        _compress_kernel,
        grid=(B,),
        in_specs=[c_blk, c_blk, full(w1k), full(w1v), full(w2k), full(w2vt), full(pbk), full(pbv)],
        out_specs=[pl.BlockSpec((1, G, nch, LANES), lambda b: (b, 0, 0, 0)),
                   pl.BlockSpec((1, G, HEAD_DIM, nch), lambda b: (b, 0, 0, 0))],
        out_shape=[jax.ShapeDtypeStruct((B, G, nch, LANES), BF16),
                   jax.ShapeDtypeStruct((B, G, HEAD_DIM, nch), BF16)],
        scratch_shapes=[pltpu.VMEM((2, 1, CMP_HIDDEN), F32)],
        compiler_params=pltpu.CompilerParams(dimension_semantics=("arbitrary",), vmem_limit_bytes=VMEM_LIMIT_SMALL),
        name="compress",
    )(kc, vc, w1k, w1v, w2k, w2vt, pbk, pbv)


class _QTile:
    def __init__(self, x, t0, t_begin, t_end, q_ref, g_ref, o_ref, cols, n_wc):
        self.x, self.t0, self.q_ref, self.g_ref, self.o_ref = x, t0, q_ref, g_ref, o_ref
        self.window_inside = t_begin >= WINDOW
        self.ncp = t_end // CMP_STRIDE
        self.ns = t_end // SEL_BLOCK
        self.t_lane = t0 + (lax.broadcasted_iota(jnp.int32, (1, cols), 1) & (TQ - 1))
        c_end = lax.broadcasted_iota(jnp.int32, (self.ncp, cols), 0) * CMP_STRIDE + (CMP_BLOCK - 1)
        self.cmp_valid = c_end <= self.t_lane
        self.w_pos = [t0 - WINDOW + TQ * c for c in range(n_wc)]
        self.w_start = [pl.multiple_of(jnp.maximum(p, 0), TQ) for p in self.w_pos]


def _nsa_kernel(qa_ref, qb_ref, kc_ref, vct_ref, ksa_ref, vst_ref, kw_ref, vwt_ref, ga_ref, gb_ref, ovt_ref,
                oa_ref, ob_ref, qsel_scr, acc_scr, m_scr):
    units = [(bb, g) for bb in range(qa_ref.shape[0]) for g in range(NSA_GROUPS)]
    uidx = {u: i for i, u in enumerate(units)}
    n_qt = kw_ref.shape[2] // TQ
    R = HEADS_PER_GROUP
    cols = R * TQ
    NS = ovt_ref.shape[0]
    n_wc = (WINDOW + TQ) // TQ
    step = pl.program_id(1)
    t_mid = n_qt // 2 * TQ
    tiles = [_QTile(0, step * TQ, 0, t_mid, qa_ref, ga_ref, oa_ref, cols, n_wc),
             _QTile(1, (n_qt - 1 - step) * TQ, t_mid, n_qt * TQ, qb_ref, gb_ref, ob_ref, cols, n_wc)]
    sub8 = lax.broadcasted_iota(jnp.int32, (SUBLANES, TQ), 0)
    own_keys_visible = (lax.broadcasted_iota(jnp.int32, (TQ, cols), 0)
                        <= (lax.broadcasted_iota(jnp.int32, (1, cols), 1) & (TQ - 1)))

    q_t_cache = {}

    def group_q_t(c, u):
        if (c.x, u) not in q_t_cache:
            bb, g = u
            q = c.q_ref[bb, R * g:R * (g + 1)].reshape(cols, LANES)
            q_t_cache[c.x, u] = q.astype(F32).T.astype(BF16)
        return q_t_cache[c.x, u]

    def cmp_scores(c, u):
        return jnp.dot(kc_ref[u[0], u[1], 0:c.ncp, :], group_q_t(c, u), preferred_element_type=F32)

    def cmp_probs(c, s):
        s = jnp.where(c.cmp_valid, s, NEG)
        e = jnp.exp2(s - jnp.max(s, axis=0, keepdims=True))
        inv = 1.0 / jnp.maximum(jnp.sum(e, axis=0, keepdims=True), 1.0e-30)
        return e * jnp.where(c.t_lane >= CMP_BLOCK - 1, inv, 0.0)

    def win_scores(c, u):
        kw = jnp.concatenate([kw_ref[u[0], u[1], pl.ds(c.w_start[n], TQ), :] for n in range(n_wc)], axis=0)
        return jnp.dot(kw, group_q_t(c, u), preferred_element_type=F32)

    def win_probs(c, sw):
        parts = []
        for n in range(n_wc):
            sc = sw[TQ * n:TQ * (n + 1)]
            if n == 0:
                sc = jnp.where(own_keys_visible, NEG, sc)
            if n == n_wc - 1:
                sc = jnp.where(own_keys_visible, sc, NEG)
            elif not c.window_inside:
                sc = jnp.where(c.w_pos[n] >= 0, sc, NEG)
            parts.append(sc.astype(BF16))
        sw = jnp.concatenate(parts, axis=0)
        return jnp.exp2(sw - jnp.max(sw, axis=0, keepdims=True))

    def win_out(c, u, ew):
        vw = jnp.concatenate([vwt_ref[u[0], u[1], :, pl.ds(c.w_start[n], TQ)] for n in range(n_wc)], axis=1)
        ow = jnp.dot(vw, ew, preferred_element_type=F32)
        return ow[0:HEAD_DIM] * (1.0 / ow[HEAD_DIM:HEAD_DIM + 1])

    def select_blocks(c, u, p):
        ns = c.ns
        psum = p[:, 0:TQ] + p[:, TQ:2 * TQ] + p[:, 2 * TQ:3 * TQ] + p[:, 3 * TQ:4 * TQ]
        p_hi = psum.astype(BF16)
        p_lo = (psum - p_hi.astype(F32)).astype(BF16)
        ov = ovt_ref[0:ns, 0:c.ncp]
        imp = (jnp.dot(ov, p_hi, preferred_element_type=F32) + jnp.dot(ov, p_lo, preferred_element_type=F32))
        blk = lax.broadcasted_iota(jnp.int32, (ns, TQ), 0)
        t_l = c.t0 + lax.broadcasted_iota(jnp.int32, (ns, TQ), 1)
        cur = t_l // SEL_BLOCK
        imp = jnp.where(blk * SEL_BLOCK <= t_l, imp, -1.0)
        imp = jnp.where(blk == 0, FORCED_SCORE, imp)
        imp = jnp.where(blk == cur, FORCED_SCORE, imp)
        imp = jnp.where(blk == cur - 1, FORCED_SCORE, imp)
        nv = ns // SUBLANES
        imp8 = [imp[SUBLANES * j:SUBLANES * (j + 1)] for j in range(nv)]
        rank8 = [jnp.zeros((SUBLANES, TQ), F32) for _ in range(nv)]
        for mm in range(ns):
            row = imp[mm:mm + 1, :]
            jm = mm // SUBLANES
            for j in range(nv):
                if j < jm:
                    ahead = jnp.where(row > imp8[j], 1.0, 0.0)
                elif j > jm:
                    ahead = jnp.where(row >= imp8[j], 1.0, 0.0)
                else:
                    tie = jnp.where(sub8 > (mm % SUBLANES), 1.0, 0.0)
                    ahead = jnp.where(row > imp8[j], 1.0, 0.0) + jnp.where(row == imp8[j], tie, 0.0)
                rank8[j] = rank8[j] + ahead
        pen = jnp.where(jnp.concatenate(rank8, axis=0) < float(SEL_TOPK), 0.0, NEG)
        if ns < NS:
            pen = jnp.concatenate([pen, jnp.zeros((NS - ns, TQ), F32)], axis=0)
        qsel_scr[c.x, uidx[u]] = jnp.concatenate(
            [group_q_t(c, u)[0:HEAD_DIM], jnp.concatenate([pen.astype(BF16)] * R, axis=1)], axis=0)

    p_c, o_cmp, o_win = {}, {}, {}

    def cmp_job(c, u):
        def finish(p, _):
            p_c[c.x, u] = p
            o_cmp[c.x, u] = jnp.dot(vct_ref[u[0], u[1], :, 0:c.ncp], p.astype(BF16),
                                    preferred_element_type=F32)
        return (lambda: cmp_scores(c, u)), (lambda s: (cmp_probs(c, s), None)), finish

    def win_job(c, u):
        def finish(e_w, _):
            o_win[c.x, u] = win_out(c, u, e_w)
        return (lambda: win_scores(c, u)), (lambda s: (win_probs(c, s), None)), finish

    def sel_job(x, k0, nk, u, own):
        k0 = k0 if isinstance(k0, int) else pl.multiple_of(k0, TQ)

        def probs(sc):
            if own:
                sc = jnp.where(own_keys_visible, sc, NEG)
            sc = sc.astype(BF16)
            m_old = m_scr[x, uidx[u]]
            m_new = jnp.maximum(m_old, jnp.max(sc, axis=0, keepdims=True).astype(F32))
            m_scr[x, uidx[u]] = m_new
            return jnp.exp2(sc - m_new.astype(BF16)), jnp.exp2(m_old - m_new)

        def finish(pp, alpha):
            acc_scr[x, uidx[u]] = acc_scr[x, uidx[u]] * alpha + jnp.dot(
                vst_ref[u[0], u[1], :, pl.ds(k0, nk)], pp, preferred_element_type=F32)

        return (lambda: jnp.dot(ksa_ref[u[0], u[1], pl.ds(k0, nk), :], qsel_scr[x, uidx[u]],
                                preferred_element_type=F32)), probs, finish

    def fuse(js):
        return ((lambda: [j[0]() for j in js]),
                (lambda ss: ([j[1](s) for j, s in zip(js, ss)], None)),
                (lambda outs, _: [j[2](*o) for j, o in zip(js, outs)]))

    def sel_keys(x, k0, nk, own=False):
        return [fuse([sel_job(x, k0, nk, (bb, g), own) for g in range(NSA_GROUPS)]) for bb in range(qa_ref.shape[0])]

    early, late = tiles
    n_slots = (n_qt - 1) * TQ // TK
    n_static = n_slots - n_slots // 2
    n_late = late.t0 // TK
    c_late = late.t0 % TK // TQ
    plan = [cmp_job(c, u) for c in (late, early) for u in units]
    for u in units:
        plan += [lambda u=u: select_blocks(late, u, p_c[late.x, u]), win_job(late, u)]
    plan += sel_keys(late.x, late.t0, TQ, own=True)
    for s in range(n_static):
        plan += sel_keys(late.x, s * TK, TK)
        for u in units[s::n_static]:
            plan += [lambda u=u: select_blocks(early, u, p_c[early.x, u]), win_job(early, u)]
    plan += sel_keys(early.x, early.t0, TQ, own=True)
    for s in range(n_static, n_slots):
        is_late = s < n_late
        plan += sel_keys(jnp.where(is_late, late.x, early.x), jnp.where(is_late, s, s - n_late) * TK, TK)
    for s in range(TK // TQ - 1):
        is_late = s < c_late
        base = jnp.where(is_late, n_late, early.t0 // TK) * TK
        plan += sel_keys(jnp.where(is_late, late.x, early.x), base + jnp.where(is_late, s, s - c_late) * TQ, TQ)

    acc_scr[...] = jnp.zeros_like(acc_scr)
    m_scr[...] = jnp.full(m_scr.shape, NEG, F32)
    job_pos = [k for k, e in enumerate(plan) if isinstance(e, tuple)]
    following = dict(zip(job_pos, job_pos[1:]))
    issued = {job_pos[0]: plan[job_pos[0]][0]()}
    for k, entry in enumerate(plan):
        if not isinstance(entry, tuple):
            entry()
            continue
        if k in following:
            issued[following[k]] = plan[following[k]][0]()
        _, probs, finish = entry
        finish(*probs(issued.pop(k)))

    for c in tiles:
        for bb in range(qa_ref.shape[0]):
            glt = c.g_ref[bb]
            heads = []
            for g in range(NSA_GROUPS):
                acc = acc_scr[c.x, uidx[bb, g]]
                o_sel = acc[0:HEAD_DIM] * (1.0 / acc[HEAD_DIM:HEAD_DIM + 1])
                for r in range(R):
                    hh = R * g + r
                    sl = slice(r * TQ, (r + 1) * TQ)
                    heads.append(glt[3 * hh:3 * hh + 1] * o_cmp[c.x, (bb, g)][:, sl]
                                 + glt[3 * hh + 1:3 * hh + 2] * o_sel[:, sl]
                                 + glt[3 * hh + 2:3 * hh + 3] * o_win[c.x, (bb, g)][:, sl])
            c.o_ref[bb] = jnp.concatenate(heads, axis=0).T


def _nsa(qq, kcmp, vcmpt, ksa, vst, kw, vwt, glt, ovt):
    B, H, T, _ = qq.shape
    G = NSA_GROUPS
    NB = NSA_NB
    NCP = kcmp.shape[2]
    n_qt = T // TQ
    grid = (B // NB, n_qt // 2)
    k_blk = lambda n: pl.BlockSpec((NB, G, n, LANES), lambda b, i: (b, 0, 0, 0))
    vt_blk = lambda r, n: pl.BlockSpec((NB, G, r, n), lambda b, i: (b, 0, 0, 0))
    lo_tile = lambda b, i: i
    hi_tile = lambda b, i: n_qt - 1 - i
    q_blk = lambda tile: pl.BlockSpec((NB, H, TQ, LANES), lambda b, i: (b, 0, tile(b, i), 0))
    g_blk = lambda tile: pl.BlockSpec((NB, GATE_ROWS, TQ), lambda b, i: (b, 0, tile(b, i)))
    half = jax.ShapeDtypeStruct((B, T // 2, NSA_WIDTH), F32)
    return pl.pallas_call(
        _nsa_kernel,
        grid=grid,
        in_specs=[
            q_blk(lo_tile), q_blk(hi_tile),
            k_blk(NCP), vt_blk(HEAD_DIM, NCP), k_blk(T), vt_blk(V_ROWS, T), k_blk(T), vt_blk(V_ROWS, T),
            g_blk(lo_tile), g_blk(hi_tile),
            pl.BlockSpec(ovt.shape, lambda b, i: (0, 0)),
        ],
        out_specs=[pl.BlockSpec((NB, TQ, NSA_WIDTH), lambda b, i: (b, i, 0)),
                   pl.BlockSpec((NB, TQ, NSA_WIDTH), lambda b, i: (b, n_qt // 2 - 1 - i, 0))],
        out_shape=[half, half],
        scratch_shapes=[pltpu.VMEM((2, NB * G, LANES, HEADS_PER_GROUP * TQ), BF16),
                        pltpu.VMEM((2, NB * G, V_ROWS, HEADS_PER_GROUP * TQ), F32),
                        pltpu.VMEM((2, NB * G, 1, HEADS_PER_GROUP * TQ), F32)],
        compiler_params=pltpu.CompilerParams(
            dimension_semantics=("parallel", "arbitrary"), vmem_limit_bytes=VMEM_LIMIT),
        name="nsa",
    )(qq, qq, kcmp, vcmpt, ksa, vst, kw, vwt, glt, glt, ovt)


def _s5_prep_kernel(lre_ref, lim_ref, ldt_ref, bre_ref, bim_ref, are_ref, aim_ref, bbre_ref, bbim_ref):
    lre, lim = lre_ref[...], lim_ref[...]
    dt = jnp.exp(ldt_ref[...])
    mag = jnp.exp(lre * dt)
    a_re = mag * jnp.cos(lim * dt)
    a_im = mag * jnp.sin(lim * dt)
    den = lre * lre + lim * lim
    z_re = ((a_re - 1.0) * lre + a_im * lim) / den
    z_im = (a_im * lre - (a_re - 1.0) * lim) / den
    are_ref[...] = a_re
    aim_ref[...] = a_im
    bbre_ref[...] = z_re * bre_ref[...] - z_im * bim_ref[...]
    bbim_ref[...] = z_re * bim_ref[...] + z_im * bre_ref[...]


def _s5_prep(lre, lim, ldt, bre, bim):
    shp = jax.ShapeDtypeStruct(lre.shape, F32)
    return pl.pallas_call(_s5_prep_kernel, out_shape=[shp, shp, shp, shp], name="s5_prep")(lre, lim, ldt, bre, bim)


def _s5_weights_kernel(bbre_ref, bbim_ref, are_ref, aim_ref, crt_ref, cit_ref,
                       we_ref, wct_ref, tp_ref, alre_ref, alim_ref, bd_scr, arow_scr, tp_scr):
    gpb = S5_CPB // S5_GROUP
    for q in range(we_ref.shape[0]):
        r0 = S5_CPB * q
        bd_scr[...] = jnp.zeros_like(bd_scr)
        tp_scr[...] = jnp.zeros_like(tp_scr)
        for n, ref in enumerate((bbre_ref, bbim_ref, crt_ref, cit_ref)):
            for gl in range(gpb):
                bd_scr[n, S5_GROUP * gl:S5_GROUP * (gl + 1), S5_STATE * gl:S5_STATE * (gl + 1)] = (
                    ref[r0 + S5_GROUP * gl:r0 + S5_GROUP * (gl + 1), :])
        for n, ref in enumerate((are_ref, aim_ref)):
            for gl in range(gpb):
                arow_scr[n, :, S5_STATE * gl:S5_STATE * (gl + 1)] = ref[r0 + S5_GROUP * gl:r0 + S5_GROUP * gl + 1, :]
        bbre, bbim, crt, cit = bd_scr[0], bd_scr[1], bd_scr[2], bd_scr[3]
        are, aim = arow_scr[0], arow_scr[1]
        pre, pim = jnp.ones_like(are), jnp.zeros_like(are)
        for k in range(S5_L):
            bpr = bbre * pre - bbim * pim
            bpi = bbre * pim + bbim * pre
            i = S5_L - 1 - k
            we_ref[q, S5_CPB * i:S5_CPB * (i + 1), :] = jnp.concatenate([bpr, bpi], axis=1).astype(BF16)
            tap = _nt_dot(bpr.astype(BF16), crt.astype(BF16)) - _nt_dot(bpi.astype(BF16), cit.astype(BF16))
            for i in range(S5_L - k):
                j = i + k
                tp_scr[S5_CPB * i:S5_CPB * (i + 1), S5_CPB * j:S5_CPB * (j + 1)] = tap
            pre, pim = pre * are - pim * aim, pre * aim + pim * are
            wct_ref[q, S5_CPB * k:S5_CPB * (k + 1), :] = jnp.concatenate(
                [crt * pre - cit * pim, -(crt * pim + cit * pre)], axis=1).astype(BF16)
        tp_ref[q] = tp_scr[...].astype(BF16)
        alre_ref[q] = pre
        alim_ref[q] = pim


def _s5_weights(bbre, bbim, are, aim, crt, cit):
    nblk = S5_WIDTH // S5_CPB
    sb = S5_CPB // S5_GROUP * S5_STATE
    lc = S5_L * S5_CPB
    bps = S5W_BLOCKS_PER_STEP
    rows = pl.BlockSpec((bps * S5_CPB, S5_STATE), lambda q: (q, 0))
    blk = lambda r, c: pl.BlockSpec((bps, r, c), lambda q: (q, 0, 0))
    return pl.pallas_call(
        _s5_weights_kernel,
        grid=(nblk // bps,),
        in_specs=[rows] * 6,
        out_specs=[blk(lc, 2 * sb), blk(lc, 2 * sb), blk(lc, lc), blk(1, sb), blk(1, sb)],
        scratch_shapes=[pltpu.VMEM((4, S5_CPB, sb), F32), pltpu.VMEM((2, 1, sb), F32), pltpu.VMEM((lc, lc), F32)],
        out_shape=[jax.ShapeDtypeStruct((nblk, lc, 2 * sb), BF16),
                   jax.ShapeDtypeStruct((nblk, lc, 2 * sb), BF16),
                   jax.ShapeDtypeStruct((nblk, lc, lc), BF16),
                   jax.ShapeDtypeStruct((nblk, 1, sb), F32), jax.ShapeDtypeStruct((nblk, 1, sb), F32)],
        compiler_params=pltpu.CompilerParams(dimension_semantics=("parallel",), vmem_limit_bytes=VMEM_LIMIT),
        name="s5_weights",
    )(bbre, bbim, are, aim, crt, cit)


def _s5_kernel(u_ref, we_ref, wct_ref, tp_ref, alre_ref, alim_ref, d_ref, wg_ref, bg_ref, o_ref, e_scr, st_scr):
    nb, ct, _ = u_ref.shape
    L = S5_L
    nblk = S5_WIDTH // S5_CPB
    sb = S5_CPB // S5_GROUP * S5_STATE
    spb = 2 * sb // LANES
    W = S5_SCAN_SLABS * LANES

    @pl.when(pl.program_id(0) == 0)
    def _():
        st_scr[...] = jnp.zeros_like(st_scr)

    uf = u_ref[...].reshape(nb * ct, L * S5_WIDTH)
    ub = uf.astype(BF16)
    u_blk = [jnp.concatenate([ub[:, i * S5_WIDTH + S5_CPB * q:i * S5_WIDTH + S5_CPB * (q + 1)] for i in range(L)],
                             axis=1) for q in range(nblk)]

    for q in range(nblk):
        e = jnp.dot(u_blk[q], we_ref[q], preferred_element_type=F32)
        for s in range(spb):
            for b in range(nb):
                e_scr[spb * q + s, pl.ds(b, ct, stride=nb), :] = e[b * ct:(b + 1) * ct, LANES * s:LANES * (s + 1)]

    def slabs_of(cs):
        q, r = divmod(cs * LANES, sb)
        return spb * q + r // LANES, spb * q + (sb + r) // LANES

    for j in range(S5_GROUPS * S5_STATE // W):
        sl = [slabs_of(S5_SCAN_SLABS * j + n) for n in range(S5_SCAN_SLABS)]
        sl_re, sl_im = [s[0] for s in sl], [s[1] for s in sl]
        ar = alre_ref[:, W * j:W * (j + 1)]
        ai = alim_ref[:, W * j:W * (j + 1)]
        load = lambda r0, rows, slabs: jnp.concatenate([e_scr[s, pl.ds(r0, rows), :] for s in slabs], axis=1)
        state = lambda slabs: jnp.concatenate([st_scr[:, LANES * s:LANES * (s + 1)] for s in slabs], axis=1)
        sr, si = state(sl_re), state(sl_im)
        for k in range(ct // 2):
            r0 = k * 2 * nb
            er, ei = load(r0, 2 * nb, sl_re), load(r0, 2 * nb, sl_im)
            tr = ar * sr - ai * si + er[0:nb]
            ti = ar * si + ai * sr + ei[0:nb]
            xr = jnp.concatenate([sr, tr], axis=0)
            xi = jnp.concatenate([si, ti], axis=0)
            for n in range(S5_SCAN_SLABS):
                e_scr[sl_re[n], pl.ds(r0, 2 * nb), :] = xr[:, LANES * n:LANES * (n + 1)]
                e_scr[sl_im[n], pl.ds(r0, 2 * nb), :] = xi[:, LANES * n:LANES * (n + 1)]
            sr, si = ar * tr - ai * ti + er[nb:], ar * ti + ai * tr + ei[nb:]
        for n in range(S5_SCAN_SLABS):
            st_scr[:, LANES * sl_re[n]:LANES * (sl_re[n] + 1)] = sr[:, LANES * n:LANES * (n + 1)]
            st_scr[:, LANES * sl_im[n]:LANES * (sl_im[n] + 1)] = si[:, LANES * n:LANES * (n + 1)]

    ys = []
    for q in range(nblk):
        per_b = [jnp.concatenate([e_scr[spb * q + s, pl.ds(b, ct, stride=nb), :] for s in range(spb)], axis=1)
                 for b in range(nb)]
        x_in = jnp.concatenate(per_b, axis=0).astype(BF16)
        ys.append((_nt_dot(x_in, wct_ref[q]) + jnp.dot(u_blk[q], tp_ref[q], preferred_element_type=F32)).astype(BF16))
    outs = []
    for j in range(L):
        yj = jnp.concatenate([y[:, S5_CPB * j:S5_CPB * (j + 1)] for y in ys], axis=1).astype(F32)
        yj = yj + d_ref[...] * uf[:, j * S5_WIDTH:(j + 1) * S5_WIDTH]
        z = _gelu_tanh(yj)
        gate = jnp.dot(z.astype(BF16), wg_ref[...], preferred_element_type=F32) + bg_ref[...]
        outs.append(z * _sigmoid(gate))
    o_ref[...] = jnp.concatenate(outs, axis=1).reshape(nb, ct, L * S5_WIDTH)


def _s5(u4, we, wct, tp, alre, alim, d, w_glu, b_glu):
    nb, nchunks, w4 = u4.shape
    ct = S5_CT
    n_slabs = 2 * S5_GROUPS * S5_STATE // LANES
    resident = lambda a: pl.BlockSpec(a.shape, lambda i: (0,) * a.ndim, pipeline_mode=pl.Buffered(1))
    u_blk = pl.BlockSpec((nb, ct, w4), lambda i: (0, i, 0))
    return pl.pallas_call(
        _s5_kernel,
        grid=(nchunks // ct,),
        in_specs=[u_blk, resident(we), resident(wct), resident(tp), resident(alre), resident(alim),
                  resident(d), resident(w_glu), resident(b_glu)],
        out_specs=u_blk,
        out_shape=jax.ShapeDtypeStruct(u4.shape, F32),
        scratch_shapes=[pltpu.VMEM((n_slabs, nb * ct, LANES), F32), pltpu.VMEM((nb, n_slabs * LANES), F32)],
        compiler_params=pltpu.CompilerParams(dimension_semantics=("arbitrary",), vmem_limit_bytes=VMEM_LIMIT_MID),
        name="s5",
    )(u4, we, wct, tp, alre, alim, d, w_glu, b_glu)


def _final_kernel(x_ref, g_ref, wb_ref, onl_ref, onh_ref, os_ref, wpn_ref, wps_ref, wo_ref, fg_ref, o_ref, os_scr):
    tm = x_ref.shape[1]
    o4 = os_ref[0]
    for i in range(S5_L):
        for s in range(S5_WIDTH // LANES):
            c0 = i * S5_WIDTH + LANES * s
            os_scr[s, pl.ds(i, tm // S5_L, stride=S5_L), :] = o4[:, c0:c0 + LANES]
    first_half = pl.program_id(1) < pl.num_programs(1) // 2

    def silu(v):
        return v * _sigmoid(v)

    rows = [slice(r, r + tm // FINAL_SUB) for r in range(0, tm, tm // FINAL_SUB)]
    hs = [(_rms_scale(x_ref[0, r]) * g_ref[...]).astype(BF16) for r in rows]
    for r, h in zip(rows, hs):
        proj = lambda a, b, h=h: jnp.dot(h, wb_ref[:, a:b], preferred_element_type=F32)
        o_nsa = jnp.where(first_half, onl_ref[0, r], onh_ref[0, r])
        o_s5 = jnp.concatenate([os_scr[s, r] for s in range(S5_WIDTH // LANES)], axis=1)
        a_in = (o_nsa * silu(proj(_W_GN, _W_U))).astype(BF16)
        b_in = (o_s5 * silu(proj(_W_GS, _W_MG))).astype(BF16)
        branch_a = jnp.dot(a_in, wpn_ref[...], preferred_element_type=F32)
        branch_b = jnp.dot(b_in, wps_ref[...], preferred_element_type=F32)
        merged = (_sigmoid(proj(_W_MG, _W_MG + D_MODEL)) * branch_a
                  + _sigmoid(proj(_W_MG + D_MODEL, _W_END)) * branch_b)
        y = x_ref[0, r] + jnp.dot(merged.astype(BF16), wo_ref[...], preferred_element_type=F32)
        o_ref[0, r] = _rms_scale(y) * fg_ref[...]


def _final(x, norm_g, w_b, o_nsa_lo, o_nsa_hi, o_s5, wpn, wps, wo, final_g):
    B, T, D = x.shape
    tm = TM_PROJ
    nh = T // tm // 2
    row_blk = lambda w: pl.BlockSpec((1, tm, w), lambda b, i: (b, i, 0))
    full = lambda a: pl.BlockSpec(a.shape, lambda b, i: (0,) * a.ndim)
    return pl.pallas_call(
        _final_kernel,
        grid=(B, T // tm),
        in_specs=[row_blk(D), full(norm_g), pl.BlockSpec(w_b.shape, lambda b, i: (0, 0), pipeline_mode=pl.Buffered(1)),
                  pl.BlockSpec((1, tm, NSA_WIDTH), lambda b, i: (b, jnp.minimum(i, nh - 1), 0)),
                  pl.BlockSpec((1, tm, NSA_WIDTH), lambda b, i: (b, jnp.maximum(i - nh, 0), 0)),
                  pl.BlockSpec((1, tm // S5_L, S5_L * S5_WIDTH), lambda b, i: (b, i, 0)),
                  full(wpn), full(wps), full(wo), full(final_g)],
        out_specs=row_blk(D),
        out_shape=jax.ShapeDtypeStruct((B, T, D), F32),
        scratch_shapes=[pltpu.VMEM((S5_WIDTH // LANES, tm, LANES), F32)],
        compiler_params=pltpu.CompilerParams(
            dimension_semantics=("parallel", "arbitrary"), vmem_limit_bytes=VMEM_LIMIT),
        name="final",
    )(x, norm_g, w_b, o_nsa_lo, o_nsa_hi, o_s5, wpn, wps, wo, final_g)


def _rope_tables(T):
    half = HEAD_DIM // 2
    inv_freq = np.float32(ROPE_THETA) ** (-np.arange(half, dtype=np.float32) / np.float32(half))
    ang = np.arange(T, dtype=np.float32)[:, None] * inv_freq[None, :].astype(np.float32)
    cos, sin = np.cos(ang).astype(np.float32), np.sin(ang).astype(np.float32)
    cos2 = np.concatenate([cos, cos, cos, cos], axis=1)
    sin2 = np.concatenate([-sin, sin, -sin, sin], axis=1)
    return jnp.asarray(cos2), jnp.asarray(sin2)


def _compress_w1(w1):
    half_rows = CMP_STRIDE * HEAD_DIM
    return jnp.concatenate([w1[:half_rows], w1[half_rows:]], axis=1).astype(BF16)


def kernel(x, norm_g, w_in, cmp_pos_k, cmp_pos_v, cmp_w1_k, cmp_w2_k, cmp_w1_v, cmp_w2_v, s5_lam_re, s5_lam_im, s5_log_dt, s5_b_re, s5_b_im, s5_c_re, s5_c_im, s5_d, w_glu, b_glu, w_proj_nsa, w_proj_s5, w_out, final_g):
    B, T, D = x.shape
    assert w_in.shape[0] == 1, "single-layer block"
    NCH = T // CMP_STRIDE
    NS = T // SEL_BLOCK

    w = w_in[0]
    w_all = jnp.concatenate([w[:, :_OFF_GL], jnp.pad(w[:, _OFF_GL:_OFF_GN], ((0, 0), (0, LANES - 24))),
                             w[:, _OFF_GN:]], axis=1).astype(BF16)
    g2 = norm_g[0][None, :]
    cos2, sin2 = _rope_tables(T)

    qq, kc, vc, ksa, vst, kw, vwt, glt, u4 = _inproj(x, g2, w_all, cos2, sin2)

    w2k = jnp.concatenate([jnp.zeros_like(cmp_w2_k[0]), cmp_w2_k[0]], axis=1).astype(BF16)
    w2vt = cmp_w2_v[0].T.astype(BF16)
    pos_rows = lambda p: jnp.pad(p.reshape(2, CMP_STRIDE * HEAD_DIM), ((0, 2 * SUBLANES - 2), (0, 0)))
    kcmp, vcmpt = _compress(kc, vc, _compress_w1(cmp_w1_k[0]), _compress_w1(cmp_w1_v[0]), w2k, w2vt,
                            pos_rows(cmp_pos_k[0]), pos_rows(cmp_pos_v[0]))

    c_start = jnp.arange(NCH) * CMP_STRIDE
    s_start = jnp.arange(NS) * SEL_BLOCK
    ovt = ((c_start[None, :] < s_start[:, None] + SEL_BLOCK) & (c_start[None, :] + CMP_BLOCK > s_start[:, None])
           & (jnp.arange(NCH)[None, :] < NCH - 1)).astype(BF16)
    o_nsa_lo, o_nsa_hi = _nsa(qq, kcmp, vcmpt, ksa, vst, kw, vwt, glt, ovt)

    rep = lambda a: jnp.repeat(a, S5_GROUP, axis=0)
    tr = lambda b: b.transpose(0, 2, 1).reshape(S5_GROUPS * S5_GROUP, S5_STATE)
    a_re, a_im, bb_re, bb_im = _s5_prep(
        rep(s5_lam_re[0]), rep(s5_lam_im[0]),
        rep(jnp.broadcast_to(s5_log_dt[0][:, None], (S5_GROUPS, S5_STATE))),
        tr(s5_b_re[0]), tr(s5_b_im[0]))
    flat = lambda c: c.reshape(S5_GROUPS * S5_GROUP, S5_STATE)
    we, wct, tp, alre, alim = _s5_weights(bb_re, bb_im, a_re, a_im, flat(s5_c_re[0]), flat(s5_c_im[0]))
    o_s5 = _s5(u4, we, wct, tp, alre.reshape(1, -1), alim.reshape(1, -1), s5_d[0][None, :],
               w_glu[0].astype(BF16), b_glu[0][None, :])

    return _final(x, g2, w_all, o_nsa_lo, o_nsa_hi, o_s5, w_proj_nsa[0].astype(BF16), w_proj_s5[0].astype(BF16),
                  w_out[0].astype(BF16), final_g[None, :])
```
